```python
import jax
import jax.numpy as jnp
from jax import lax
import numpy as np

D_MODEL = 1024
BATCH = 8
SEQ = 2048
DEPTH = 4

N_META = 16
CHUNK = 64
PAD_FRONT = (-N_META) % CHUNK
BRANCH_WIDTH = 512
N_BRANCH = 4
D_FF = 4 * D_MODEL
NORM_EPS = 1e-6

RET_HEADS = 4
RET_DK = 64
RET_DV = 128
RET_ROPE_BASE = 10000.0

S5_GROUP = 16
S5_GROUPS = BRANCH_WIDTH // S5_GROUP
S5_STATE = 64

SSD_HEADDIM = 64
SSD_HEADS = BRANCH_WIDTH // SSD_HEADDIM
SSD_GROUPS = 2
SSD_STATE = 128
SSD_CONV = 4
SSD_CONV_DIM = BRANCH_WIDTH + 2 * SSD_GROUPS * SSD_STATE

HG_HEADS = 4
HG_DK = BRANCH_WIDTH // HG_HEADS
HG_DV = BRANCH_WIDTH // HG_HEADS

IN_SIZES = (RET_HEADS * RET_DK, RET_HEADS * RET_DK, RET_HEADS * RET_DV, RET_HEADS * RET_DV,
            BRANCH_WIDTH,
            BRANCH_WIDTH, SSD_CONV_DIM, SSD_HEADS,
            HG_HEADS * HG_DK, HG_HEADS * HG_DK, HG_HEADS * HG_DV, HG_HEADS * HG_DV,
            N_BRANCH * D_MODEL)
IN_DIM = sum(IN_SIZES)

kernel_name = 'hybrid_ret_s5_ssd_hgrn2_trunk'


def rms_norm(x, w):
    xf = x.astype(jnp.float32)
    y = xf * lax.rsqrt(jnp.mean(xf * xf, axis=-1, keepdims=True) + NORM_EPS)
    return (y * w.astype(jnp.float32)).astype(x.dtype)


def to_chunks(t):
    pad = [(0, 0)] * t.ndim
    pad[1] = (PAD_FRONT, 0)
    t = jnp.pad(t, pad)
    b, tt = t.shape[0], t.shape[1]
    t = t.reshape((b, tt // CHUNK, CHUNK) + t.shape[2:])
    return jnp.moveaxis(t, 1, 0)


def from_chunks(y):
    y = jnp.moveaxis(y, 0, 1)
    b, n = y.shape[0], y.shape[1]
    y = y.reshape((b, n * CHUNK) + y.shape[3:])
    return y[:, PAD_FRONT:]


def chunked_scalar_decay_attention(q, k, v, log_a):
    b, _, g, dk = q.shape
    hg, dv = v.shape[3], v.shape[4]
    causal = jnp.tril(jnp.ones((CHUNK, CHUNK), dtype=bool))[None, :, :, None, None]

    def step(state, inp):
        qc, kc, vc, lac = inp
        cum = jnp.cumsum(lac.astype(jnp.float32), axis=1)
        seg = cum[:, :, None] - cum[:, None, :]
        decay = jnp.exp(jnp.where(causal, seg, -jnp.inf)).astype(qc.dtype)
        scores = jnp.einsum('btgd,bsgd->btsg', qc, kc)
        y_in = jnp.einsum('btsgh,bsghe->btghe', scores[..., None] * decay, vc)
        y_x = jnp.einsum('btgd,bghde->btghe', qc, state) * jnp.exp(cum)[..., None]
        last = cum[:, -1]
        w = jnp.exp(last[:, None] - cum).astype(kc.dtype)
        state = state * jnp.exp(last)[..., None, None] + jnp.einsum('bsgd,bsgh,bsghe->bghde', kc, w, vc)
        return state, y_in + y_x

    state0 = jnp.zeros((b, g, hg, dk, dv), jnp.float32)
    _, y = lax.scan(step, state0, (to_chunks(q), to_chunks(k), to_chunks(v), to_chunks(log_a)))
    return from_chunks(y).astype(v.dtype)


def chunked_vector_decay_attention(q, k, v, log_f):
    b, _, h, dk = q.shape
    dv = v.shape[-1]
    causal = jnp.tril(jnp.ones((CHUNK, CHUNK), dtype=bool))[None, :, :, None, None]

    def step(state, inp):
        qc, kc, vc, lfc = inp
        cum = jnp.cumsum(lfc.astype(jnp.float32), axis=1)
        decay = jnp.exp(jnp.where(causal, cum[:, :, None] - cum[:, None, :], -jnp.inf)).astype(qc.dtype)
        scores = jnp.einsum('bthd,bshd,btshd->btsh', qc, kc, decay)
        y_in = jnp.einsum('btsh,bshe->bthe', scores, vc)
        y_x = jnp.einsum('bthd,bhde->bthe', qc * jnp.exp(cum).astype(qc.dtype), state)
        last = cum[:, -1]
        kw = kc * jnp.exp(last[:, None] - cum).astype(kc.dtype)
        state = state * jnp.exp(last)[..., None] + jnp.einsum('bshd,bshe->bhde', kw, vc)
        return state, y_in + y_x

    state0 = jnp.zeros((b, h, dk, dv), jnp.float32)
    _, y = lax.scan(step, state0, (to_chunks(q), to_chunks(k), to_chunks(v), to_chunks(log_f)))
    return from_chunks(y).astype(v.dtype)


def rotary(x):
    t, dk = x.shape[1], x.shape[-1]
    half = dk // 2
    inv_freq = RET_ROPE_BASE ** (-jnp.arange(half, dtype=jnp.float32) / half)
    ang = jnp.arange(t, dtype=jnp.float32)[:, None] * inv_freq[None, :]
    cos = jnp.cos(ang)[None, :, None, :].astype(x.dtype)
    sin = jnp.sin(ang)[None, :, None, :].astype(x.dtype)
    x1, x2 = x[..., :half], x[..., half:]
    return jnp.concatenate([x1 * cos - x2 * sin, x1 * sin + x2 * cos], axis=-1)


def retention_mixer(q, k, v, g, gn_w):
    bsz, t, _ = q.shape
    q = rotary(q.reshape(bsz, t, RET_HEADS, RET_DK))
    k = rotary(k.reshape(bsz, t, RET_HEADS, RET_DK)) * RET_DK ** -0.5
    v = v.reshape(bsz, t, RET_HEADS, 1, RET_DV)
    gamma = 1.0 - jnp.exp2(-5.0 - jnp.arange(RET_HEADS, dtype=jnp.float32))
    log_a = jnp.broadcast_to(jnp.log(gamma)[:, None], (bsz, t, RET_HEADS, 1))
    y = chunked_scalar_decay_attention(q, k, v, log_a).reshape(bsz, t, RET_HEADS, RET_DV)
    yf = y.astype(jnp.float32)
    mu = jnp.mean(yf, axis=-1, keepdims=True)
    var = jnp.mean(jnp.square(yf - mu), axis=-1, keepdims=True)
    yn = ((yf - mu) * lax.rsqrt(var + NORM_EPS)).reshape(bsz, t, BRANCH_WIDTH) * gn_w.astype(jnp.float32)
    return jax.nn.silu(g) * yn.astype(g.dtype)


def s5_mixer(u, lam_re, lam_im, b_re, b_im, c_re, c_im, d_skip, log_step, glu_w, glu_b):
    bsz, t, _ = u.shape
    uf = u.astype(jnp.float32).reshape(bsz, t, S5_GROUPS, S5_GROUP)
    step = jnp.exp(log_step.astype(jnp.float32))[:, None]
    lr, li = lam_re.astype(jnp.float32), lam_im.astype(jnp.float32)
    mag = jnp.exp(lr * step)
    ab_re, ab_im = mag * jnp.cos(li * step), mag * jnp.sin(li * step)
    inv = 1.0 / (lr * lr + li * li)
    co_re = ((ab_re - 1.0) * lr + ab_im * li) * inv
    co_im = (ab_im * lr - (ab_re - 1.0) * li) * inv
    br, bi = b_re.astype(jnp.float32), b_im.astype(jnp.float32)
    bb_re = co_re[..., None] * br - co_im[..., None] * bi
    bb_im = co_re[..., None] * bi + co_im[..., None] * br
    bu_re = jnp.einsum('btgj,gpj->btgp', uf, bb_re)
    bu_im = jnp.einsum('btgj,gpj->btgp', uf, bb_im)
    a_re = jnp.broadcast_to(ab_re, bu_re.shape)
    a_im = jnp.broadcast_to(ab_im, bu_im.shape)

    def combine(e1, e2):
        a1r, a1i, b1r, b1i = e1
        a2r, a2i, b2r, b2i = e2
        return (a2r * a1r - a2i * a1i, a2r * a1i + a2i * a1r,
                a2r * b1r - a2i * b1i + b2r, a2r * b1i + a2i * b1r + b2i)

    _, _, xr, xi = lax.associative_scan(combine, (a_re, a_im, bu_re, bu_im), axis=1)
    y = (jnp.einsum('btgp,gjp->btgj', xr, c_re.astype(jnp.float32))
         - jnp.einsum('btgp,gjp->btgj', xi, c_im.astype(jnp.float32)))
    y = y.reshape(bsz, t, BRANCH_WIDTH) + d_skip.astype(jnp.float32) * u.astype(jnp.float32)
    y = jax.nn.gelu(y).astype(u.dtype)
    a, gt = jnp.split(y @ glu_w + glu_b, 2, axis=-1)
    return a * jax.nn.sigmoid(gt)


def ssd_mixer(z, xbc, dt_raw, conv_w, conv_b, dt_bias, a_log, d_skip, norm_w):
    bsz, t, _ = z.shape
    hg = SSD_HEADS // SSD_GROUPS
    xbc = lax.conv_general_dilated(xbc, conv_w[:, None, :], window_strides=(1,), padding=[(SSD_CONV - 1, 0)],
                                   dimension_numbers=('NWC', 'WIO', 'NWC'),
                                   feature_group_count=SSD_CONV_DIM) + conv_b
    xbc = jax.nn.silu(xbc)
    xs, bm, cm = jnp.split(xbc, [BRANCH_WIDTH, BRANCH_WIDTH + SSD_GROUPS * SSD_STATE], axis=-1)
    xs = xs.reshape(bsz, t, SSD_GROUPS, hg, SSD_HEADDIM)
    bm = bm.reshape(bsz, t, SSD_GROUPS, SSD_STATE)
    cm = cm.reshape(bsz, t, SSD_GROUPS, SSD_STATE)
    dt = jax.nn.softplus((dt_raw + dt_bias).astype(jnp.float32)).reshape(bsz, t, SSD_GROUPS, hg)
    a = -jnp.exp(a_log.astype(jnp.float32)).reshape(SSD_GROUPS, hg)
    y = chunked_scalar_decay_attention(cm, bm, xs * dt[..., None].astype(xs.dtype), dt * a)
    y = y + d_skip.reshape(SSD_GROUPS, hg)[..., None] * xs
    y = y.reshape(bsz, t, BRANCH_WIDTH) * jax.nn.silu(z)
    y = rms_norm(y.reshape(bsz, t, SSD_GROUPS, -1), norm_w.reshape(SSD_GROUPS, -1))
    return y.reshape(bsz, t, BRANCH_WIDTH)


def hgrn2_mixer(q, f_raw, i, g, lb, norm_w):
    bsz, t, _ = q.shape
    q = jax.nn.silu(q).reshape(bsz, t, HG_HEADS, HG_DK)
    f = lb + (1.0 - lb) * jax.nn.sigmoid(f_raw.astype(jnp.float32))
    log_f = jnp.log(f).reshape(bsz, t, HG_HEADS, HG_DK)
    k = (1.0 - f).astype(q.dtype).reshape(bsz, t, HG_HEADS, HG_DK)
    v = i.reshape(bsz, t, HG_HEADS, HG_DV)
    o = chunked_vector_decay_attention(q, k, v, log_f)
    o = rms_norm(o, norm_w.reshape(HG_HEADS, HG_DV)).reshape(bsz, t, BRANCH_WIDTH)
    return o * jax.nn.silu(g)


def _fwd_setup_inputs(seed: int = 0) -> dict:
    key = jax.random.key(seed)
    ks = jax.random.split(key, 32)
    f32 = jnp.float32

    def nrm(k, shape, scale):
        return jax.random.normal(k, shape, f32) * scale

    def gain(k, shape):
        return 1.0 + 0.05 * jax.random.normal(k, shape, f32)

    n = jnp.arange(S5_STATE, dtype=f32)
    dt0 = jnp.exp(jax.random.uniform(ks[24], (DEPTH, SSD_HEADS), f32, np.log(1e-3), np.log(1e-1)))
    return {
        'x': nrm(ks[0], (BATCH, SEQ, D_MODEL), 1.0),
        'meta_tokens': nrm(ks[1], (N_META, D_MODEL), 1.0),
        'w_in': nrm(ks[2], (DEPTH, D_MODEL, IN_DIM), D_MODEL ** -0.5),
        'w_branch': nrm(ks[3], (DEPTH, N_BRANCH, BRANCH_WIDTH, D_MODEL), BRANCH_WIDTH ** -0.5),
        'w_out': nrm(ks[4], (DEPTH, D_MODEL, D_MODEL), D_MODEL ** -0.5),
        'norm_pre_mix': gain(ks[5], (DEPTH, D_MODEL)),
        'norm_post_mix': gain(ks[6], (DEPTH, D_MODEL)),
        'norm_pre_mlp': gain(ks[7], (DEPTH, D_MODEL)),
        'norm_post_mlp': gain(ks[8], (DEPTH, D_MODEL)),
        'w_up': nrm(ks[9], (DEPTH, D_MODEL, D_FF), D_MODEL ** -0.5),
        'w_down': nrm(ks[10], (DEPTH, D_FF, D_MODEL), D_FF ** -0.5),
        'ret_gn_w': gain(ks[11], (DEPTH, BRANCH_WIDTH)),
        's5_lam_re': -0.5 + 0.01 * jax.random.normal(ks[12], (DEPTH, S5_GROUPS, S5_STATE), f32),
        's5_lam_im': jnp.pi * n + 0.01 * jax.random.normal(ks[13], (DEPTH, S5_GROUPS, S5_STATE), f32),
        's5_b_re': nrm(ks[14], (DEPTH, S5_GROUPS, S5_STATE, S5_GROUP), 1.0),
        's5_b_im': nrm(ks[15], (DEPTH, S5_GROUPS, S5_STATE, S5_GROUP), 1.0),
        's5_c_re': nrm(ks[16], (DEPTH, S5_GROUPS, S5_GROUP, S5_STATE), S5_STATE ** -0.5),
        's5_c_im': nrm(ks[17], (DEPTH, S5_GROUPS, S5_GROUP, S5_STATE), S5_STATE ** -0.5),
        's5_d': nrm(ks[18], (DEPTH, BRANCH_WIDTH), 1.0),
        's5_log_step': jax.random.uniform(ks[19], (DEPTH, S5_GROUPS), f32, np.log(1e-3), np.log(1e-1)),
        's5_glu_w': nrm(ks[20], (DEPTH, BRANCH_WIDTH, 2 * BRANCH_WIDTH), BRANCH_WIDTH ** -0.5),
        's5_glu_b': nrm(ks[21], (DEPTH, 2 * BRANCH_WIDTH), 0.02),
        'ssd_conv_w': nrm(ks[22], (DEPTH, SSD_CONV, SSD_CONV_DIM), SSD_CONV ** -0.5),
        'ssd_conv_b': nrm(ks[23], (DEPTH, SSD_CONV_DIM), 0.02),
        'ssd_dt_bias': dt0 + jnp.log(-jnp.expm1(-dt0)),
        'ssd_a_log': jnp.log(jax.random.uniform(ks[25], (DEPTH, SSD_HEADS), f32, 1.0, 16.0)),
        'ssd_d': gain(ks[26], (DEPTH, SSD_HEADS)),
        'ssd_norm_w': gain(ks[27], (DEPTH, BRANCH_WIDTH)),
        'hgrn_lb': nrm(ks[28], (DEPTH, HG_HEADS * HG_DK), 0.1),
        'hgrn_norm_w': gain(ks[29], (DEPTH, BRANCH_WIDTH)),
    }


def _fwd_reference(x, meta_tokens, w_in, w_branch, w_out, norm_pre_mix, norm_post_mix, norm_pre_mlp, norm_post_mlp,
              w_up, w_down, ret_gn_w, s5_lam_re, s5_lam_im, s5_b_re, s5_b_im, s5_c_re, s5_c_im, s5_d,
              s5_log_step, s5_glu_w, s5_glu_b, ssd_conv_w, ssd_conv_b, ssd_dt_bias, ssd_a_log, ssd_d,
              ssd_norm_w, hgrn_lb, hgrn_norm_w):
    bsz = x.shape[0]
    meta = jnp.broadcast_to(meta_tokens[None].astype(x.dtype), (bsz, N_META, D_MODEL))
    h = jnp.concatenate([meta, x], axis=1)
    t = h.shape[1]
    lb_all = jnp.cumsum(jax.nn.softmax(hgrn_lb.astype(jnp.float32), axis=0), axis=0)
    lb_all = lb_all - lb_all[0]
    split_at = [int(s) for s in np.cumsum(IN_SIZES)[:-1]]
    for l in range(DEPTH):
        u = rms_norm(h, norm_pre_mix[l])
        proj = u @ w_in[l]
        (rq, rk, rv, rg, s5u, sz, sxbc, sdt, hq, hf, hi, hgate, gates) = jnp.split(proj, split_at, axis=-1)
        y_ret = retention_mixer(rq, rk, rv, rg, ret_gn_w[l])
        y_s5 = s5_mixer(s5u, s5_lam_re[l], s5_lam_im[l], s5_b_re[l], s5_b_im[l], s5_c_re[l], s5_c_im[l],
                        s5_d[l], s5_log_step[l], s5_glu_w[l], s5_glu_b[l])
        y_ssd = ssd_mixer(sz, sxbc, sdt, ssd_conv_w[l], ssd_conv_b[l], ssd_dt_bias[l], ssd_a_log[l],
                          ssd_d[l], ssd_norm_w[l])
        y_hg = hgrn2_mixer(hq, hf, hi, hgate, lb_all[l], hgrn_norm_w[l])
        branches = jnp.stack([y_ret, y_s5, y_ssd, y_hg], axis=2)
        proj_b = jnp.einsum('btnc,ncd->btnd', branches, w_branch[l])
        gate = jax.nn.sigmoid(gates.reshape(bsz, t, N_BRANCH, D_MODEL))
        mixed = jnp.sum(gate * proj_b, axis=2) @ w_out[l]
        h = h + rms_norm(mixed, norm_post_mix[l])
        m = rms_norm(h, norm_pre_mlp[l])
        m = jnp.square(jax.nn.relu(m @ w_up[l])) @ w_down[l]
        h = h + rms_norm(m, norm_post_mlp[l])
    return h[:, N_META:]


import jax as _jax
import jax.numpy as _jnp

TWIN_FORMAT = 'train_step'
FWD_PARAMS = ['x', 'meta_tokens', 'w_in', 'w_branch', 'w_out', 'norm_pre_mix', 'norm_post_mix', 'norm_pre_mlp', 'norm_post_mlp', 'w_up', 'w_down', 'ret_gn_w', 's5_lam_re', 's5_lam_im', 's5_b_re', 's5_b_im', 's5_c_re', 's5_c_im', 's5_d', 's5_log_step', 's5_glu_w', 's5_glu_b', 'ssd_conv_w', 'ssd_conv_b', 'ssd_dt_bias', 'ssd_a_log', 'ssd_d', 'ssd_norm_w', 'hgrn_lb', 'hgrn_norm_w']
TWIN_WEIGHTS = ['meta_tokens', 'w_in', 'w_branch', 'w_out', 'norm_pre_mix', 'norm_post_mix', 'norm_pre_mlp', 'norm_post_mlp', 'w_up', 'w_down', 'ret_gn_w', 's5_lam_re', 's5_lam_im', 's5_b_re', 's5_b_im', 's5_c_re', 's5_c_im', 's5_d', 's5_log_step', 's5_glu_w', 's5_glu_b', 'ssd_conv_w', 'ssd_conv_b', 'ssd_dt_bias', 'ssd_a_log', 'ssd_d', 'ssd_norm_w', 'hgrn_lb', 'hgrn_norm_w']
TWIN_DIFF_INPUT = 'x'
TWIN_INPUTS = ['x', 'meta_tokens', 'w_in', 'w_branch', 'w_out', 'norm_pre_mix', 'norm_post_mix', 'norm_pre_mlp', 'norm_post_mlp', 'w_up', 'w_down', 'ret_gn_w', 's5_lam_re', 's5_lam_im', 's5_b_re', 's5_b_im', 's5_c_re', 's5_c_im', 's5_d', 's5_log_step', 's5_glu_w', 's5_glu_b', 'ssd_conv_w', 'ssd_conv_b', 'ssd_dt_bias', 'ssd_a_log', 'ssd_d', 'ssd_norm_w', 'hgrn_lb', 'hgrn_norm_w', 'loss_target', 'm_meta_tokens', 'm_w_in', 'm_w_branch', 'm_w_out', 'm_norm_pre_mix', 'm_norm_post_mix', 'm_norm_pre_mlp', 'm_norm_post_mlp', 'm_w_up', 'm_w_down', 'm_ret_gn_w', 'm_s5_lam_re', 'm_s5_lam_im', 'm_s5_b_re', 'm_s5_b_im', 'm_s5_c_re', 'm_s5_c_im', 'm_s5_d', 'm_s5_log_step', 'm_s5_glu_w', 'm_s5_glu_b', 'm_ssd_conv_w', 'm_ssd_conv_b', 'm_ssd_dt_bias', 'm_ssd_a_log', 'm_ssd_d', 'm_ssd_norm_w', 'm_hgrn_lb', 'm_hgrn_norm_w', 'v_meta_tokens', 'v_w_in', 'v_w_branch', 'v_w_out', 'v_norm_pre_mix', 'v_norm_post_mix', 'v_norm_pre_mlp', 'v_norm_post_mlp', 'v_w_up', 'v_w_down', 'v_ret_gn_w', 'v_s5_lam_re', 'v_s5_lam_im', 'v_s5_b_re', 'v_s5_b_im', 'v_s5_c_re', 'v_s5_c_im', 'v_s5_d', 'v_s5_log_step', 'v_s5_glu_w', 'v_s5_glu_b', 'v_ssd_conv_w', 'v_ssd_conv_b', 'v_ssd_dt_bias', 'v_ssd_a_log', 'v_ssd_d', 'v_ssd_norm_w', 'v_hgrn_lb', 'v_hgrn_norm_w']
TWIN_OUTPUTS = ['loss', 'grad_x', 'grad_meta_tokens', 'grad_w_in', 'grad_w_branch', 'grad_w_out', 'grad_norm_pre_mix', 'grad_norm_post_mix', 'grad_norm_pre_mlp', 'grad_norm_post_mlp', 'grad_w_up', 'grad_w_down', 'grad_ret_gn_w', 'grad_s5_lam_re', 'grad_s5_lam_im', 'grad_s5_b_re', 'grad_s5_b_im', 'grad_s5_c_re', 'grad_s5_c_im', 'grad_s5_d', 'grad_s5_log_step', 'grad_s5_glu_w', 'grad_s5_glu_b', 'grad_ssd_conv_w', 'grad_ssd_conv_b', 'grad_ssd_dt_bias', 'grad_ssd_a_log', 'grad_ssd_d', 'grad_ssd_norm_w', 'grad_hgrn_lb', 'grad_hgrn_norm_w', 'delta_meta_tokens', 'delta_w_in', 'delta_w_branch', 'delta_w_out', 'delta_norm_pre_mix', 'delta_norm_post_mix', 'delta_norm_pre_mlp', 'delta_norm_post_mlp', 'delta_w_up', 'delta_w_down', 'delta_ret_gn_w', 'delta_s5_lam_re', 'delta_s5_lam_im', 'delta_s5_b_re', 'delta_s5_b_im', 'delta_s5_c_re', 'delta_s5_c_im', 'delta_s5_d', 'delta_s5_log_step', 'delta_s5_glu_w', 'delta_s5_glu_b', 'delta_ssd_conv_w', 'delta_ssd_conv_b', 'delta_ssd_dt_bias', 'delta_ssd_a_log', 'delta_ssd_d', 'delta_ssd_norm_w', 'delta_hgrn_lb', 'delta_hgrn_norm_w', 'new_m_meta_tokens', 'new_m_w_in', 'new_m_w_branch', 'new_m_w_out', 'new_m_norm_pre_mix', 'new_m_norm_post_mix', 'new_m_norm_pre_mlp', 'new_m_norm_post_mlp', 'new_m_w_up', 'new_m_w_down', 'new_m_ret_gn_w', 'new_m_s5_lam_re', 'new_m_s5_lam_im', 'new_m_s5_b_re', 'new_m_s5_b_im', 'new_m_s5_c_re', 'new_m_s5_c_im', 'new_m_s5_d', 'new_m_s5_log_step', 'new_m_s5_glu_w', 'new_m_s5_glu_b', 'new_m_ssd_conv_w', 'new_m_ssd_conv_b', 'new_m_ssd_dt_bias', 'new_m_ssd_a_log', 'new_m_ssd_d', 'new_m_ssd_norm_w', 'new_m_hgrn_lb', 'new_m_hgrn_norm_w', 'new_v_meta_tokens', 'new_v_w_in', 'new_v_w_branch', 'new_v_w_out', 'new_v_norm_pre_mix', 'new_v_norm_post_mix', 'new_v_norm_pre_mlp', 'new_v_norm_post_mlp', 'new_v_w_up', 'new_v_w_down', 'new_v_ret_gn_w', 'new_v_s5_lam_re', 'new_v_s5_lam_im', 'new_v_s5_b_re', 'new_v_s5_b_im', 'new_v_s5_c_re', 'new_v_s5_c_im', 'new_v_s5_d', 'new_v_s5_log_step', 'new_v_s5_glu_w', 'new_v_s5_glu_b', 'new_v_ssd_conv_w', 'new_v_ssd_conv_b', 'new_v_ssd_dt_bias', 'new_v_ssd_a_log', 'new_v_ssd_d', 'new_v_ssd_norm_w', 'new_v_hgrn_lb', 'new_v_hgrn_norm_w']
TWIN_LEAF_KINDS = {'loss': 'loss', 'grad_x': 'grad_x', 'grad_meta_tokens': 'grad_w', 'grad_w_in': 'grad_w', 'grad_w_branch': 'grad_w', 'grad_w_out': 'grad_w', 'grad_norm_pre_mix': 'grad_w', 'grad_norm_post_mix': 'grad_w', 'grad_norm_pre_mlp': 'grad_w', 'grad_norm_post_mlp': 'grad_w', 'grad_w_up': 'grad_w', 'grad_w_down': 'grad_w', 'grad_ret_gn_w': 'grad_w', 'grad_s5_lam_re': 'grad_w', 'grad_s5_lam_im': 'grad_w', 'grad_s5_b_re': 'grad_w', 'grad_s5_b_im': 'grad_w', 'grad_s5_c_re': 'grad_w', 'grad_s5_c_im': 'grad_w', 'grad_s5_d': 'grad_w', 'grad_s5_log_step': 'grad_w', 'grad_s5_glu_w': 'grad_w', 'grad_s5_glu_b': 'grad_w', 'grad_ssd_conv_w': 'grad_w', 'grad_ssd_conv_b': 'grad_w', 'grad_ssd_dt_bias': 'grad_w', 'grad_ssd_a_log': 'grad_w', 'grad_ssd_d': 'grad_w', 'grad_ssd_norm_w': 'grad_w', 'grad_hgrn_lb': 'grad_w', 'grad_hgrn_norm_w': 'grad_w', 'delta_meta_tokens': 'delta_w', 'delta_w_in': 'delta_w', 'delta_w_branch': 'delta_w', 'delta_w_out': 'delta_w', 'delta_norm_pre_mix': 'delta_w', 'delta_norm_post_mix': 'delta_w', 'delta_norm_pre_mlp': 'delta_w', 'delta_norm_post_mlp': 'delta_w', 'delta_w_up': 'delta_w', 'delta_w_down': 'delta_w', 'delta_ret_gn_w': 'delta_w', 'delta_s5_lam_re': 'delta_w', 'delta_s5_lam_im': 'delta_w', 'delta_s5_b_re': 'delta_w', 'delta_s5_b_im': 'delta_w', 'delta_s5_c_re': 'delta_w', 'delta_s5_c_im': 'delta_w', 'delta_s5_d': 'delta_w', 'delta_s5_log_step': 'delta_w', 'delta_s5_glu_w': 'delta_w', 'delta_s5_glu_b': 'delta_w', 'delta_ssd_conv_w': 'delta_w', 'delta_ssd_conv_b': 'delta_w', 'delta_ssd_dt_bias': 'delta_w', 'delta_ssd_a_log': 'delta_w', 'delta_ssd_d': 'delta_w', 'delta_ssd_norm_w': 'delta_w', 'delta_hgrn_lb': 'delta_w', 'delta_hgrn_norm_w': 'delta_w', 'new_m_meta_tokens': 'new_m', 'new_m_w_in': 'new_m', 'new_m_w_branch': 'new_m', 'new_m_w_out': 'new_m', 'new_m_norm_pre_mix': 'new_m', 'new_m_norm_post_mix': 'new_m', 'new_m_norm_pre_mlp': 'new_m', 'new_m_norm_post_mlp': 'new_m', 'new_m_w_up': 'new_m', 'new_m_w_down': 'new_m', 'new_m_ret_gn_w': 'new_m', 'new_m_s5_lam_re': 'new_m', 'new_m_s5_lam_im': 'new_m', 'new_m_s5_b_re': 'new_m', 'new_m_s5_b_im': 'new_m', 'new_m_s5_c_re': 'new_m', 'new_m_s5_c_im': 'new_m', 'new_m_s5_d': 'new_m', 'new_m_s5_log_step': 'new_m', 'new_m_s5_glu_w': 'new_m', 'new_m_s5_glu_b': 'new_m', 'new_m_ssd_conv_w': 'new_m', 'new_m_ssd_conv_b': 'new_m', 'new_m_ssd_dt_bias': 'new_m', 'new_m_ssd_a_log': 'new_m', 'new_m_ssd_d': 'new_m', 'new_m_ssd_norm_w': 'new_m', 'new_m_hgrn_lb': 'new_m', 'new_m_hgrn_norm_w': 'new_m', 'new_v_meta_tokens': 'new_v', 'new_v_w_in': 'new_v', 'new_v_w_branch': 'new_v', 'new_v_w_out': 'new_v', 'new_v_norm_pre_mix': 'new_v', 'new_v_norm_post_mix': 'new_v', 'new_v_norm_pre_mlp': 'new_v', 'new_v_norm_post_mlp': 'new_v', 'new_v_w_up': 'new_v', 'new_v_w_down': 'new_v', 'new_v_ret_gn_w': 'new_v', 'new_v_s5_lam_re': 'new_v', 'new_v_s5_lam_im': 'new_v', 'new_v_s5_b_re': 'new_v', 'new_v_s5_b_im': 'new_v', 'new_v_s5_c_re': 'new_v', 'new_v_s5_c_im': 'new_v', 'new_v_s5_d': 'new_v', 'new_v_s5_log_step': 'new_v', 'new_v_s5_glu_w': 'new_v', 'new_v_s5_glu_b': 'new_v', 'new_v_ssd_conv_w': 'new_v', 'new_v_ssd_conv_b': 'new_v', 'new_v_ssd_dt_bias': 'new_v', 'new_v_ssd_a_log': 'new_v', 'new_v_ssd_d': 'new_v', 'new_v_ssd_norm_w': 'new_v', 'new_v_hgrn_lb': 'new_v', 'new_v_hgrn_norm_w': 'new_v'}


def _forward(args):
    return _fwd_reference(*[args[k] for k in FWD_PARAMS])


def _output_shape():
    out = _jax.eval_shape(lambda: _forward(_fwd_setup_inputs(0)))
    return out.shape, out.dtype

N_MICROBATCH = 1
ADAM_LR = 0.001
ADAM_B1 = 0.9
ADAM_B2 = 0.999
ADAM_EPS = 1e-08
ADAM_WD = 0.01
ADAM_STEP = 10
PER_EXAMPLE_BATCH_AXIS = {'x': 0, 'loss_target': 0}
SHARED_INPUTS = []
_WEIGHT_DTYPES = {'meta_tokens': _jnp.float32, 'w_in': _jnp.float32, 'w_branch': _jnp.float32, 'w_out': _jnp.float32, 'norm_pre_mix': _jnp.float32, 'norm_post_mix': _jnp.float32, 'norm_pre_mlp': _jnp.float32, 'norm_post_mlp': _jnp.float32, 'w_up': _jnp.float32, 'w_down': _jnp.float32, 'ret_gn_w': _jnp.float32, 's5_lam_re': _jnp.float32, 's5_lam_im': _jnp.float32, 's5_b_re': _jnp.float32, 's5_b_im': _jnp.float32, 's5_c_re': _jnp.float32, 's5_c_im': _jnp.float32, 's5_d': _jnp.float32, 's5_log_step': _jnp.float32, 's5_glu_w': _jnp.float32, 's5_glu_b': _jnp.float32, 'ssd_conv_w': _jnp.float32, 'ssd_conv_b': _jnp.float32, 'ssd_dt_bias': _jnp.float32, 'ssd_a_log': _jnp.float32, 'ssd_d': _jnp.float32, 'ssd_norm_w': _jnp.float32, 'hgrn_lb': _jnp.float32, 'hgrn_norm_w': _jnp.float32}
MOMENT_SCALE = {'meta_tokens': 1.180341e-01, 'w_in': 9.503560e-01, 'w_branch': 2.019252e+00, 'w_out': 4.199408e+00, 'norm_pre_mix': 3.058573e+00, 'norm_post_mix': 1.646400e+01, 'norm_pre_mlp': 2.648872e+00, 'norm_post_mlp': 1.832950e+01, 'w_up': 1.367246e+00, 'w_down': 7.459062e+00, 'ret_gn_w': 8.560246e-01, 's5_lam_re': 1.168109e+00, 's5_lam_im': 8.356951e-01, 's5_b_re': 1.126916e-01, 's5_b_im': 1.236513e-01, 's5_c_re': 8.285834e-01, 's5_c_im': 9.152432e-01, 's5_d': 2.947871e+00, 's5_log_step': 7.828412e+01, 's5_glu_w': 2.948263e+00, 's5_glu_b': 6.840624e+00, 'ssd_conv_w': 2.121396e+00, 'ssd_conv_b': 5.921760e+00, 'ssd_dt_bias': 1.723584e+00, 'ssd_a_log': 9.164878e+00, 'ssd_d': 1.163842e+01, 'ssd_norm_w': 3.428115e+00, 'hgrn_lb': 5.995761e-02, 'hgrn_norm_w': 2.343515e+00}


def _to_microbatches(a, axis):
    t = _jnp.moveaxis(a, axis, 0)
    t = t.reshape((N_MICROBATCH, t.shape[0] // N_MICROBATCH) + t.shape[1:])
    return _jnp.moveaxis(t, 1, axis + 1)


def setup_inputs(seed: int = 0) -> dict:
    inp = _fwd_setup_inputs(seed)
    key = _jax.random.fold_in(_jax.random.key(seed), 7919)
    shape, _ = _output_shape()
    out = dict(inp)
    out["loss_target"] = _jax.random.normal(_jax.random.fold_in(key, 0), shape, _jnp.float32)
    for i, name in enumerate(TWIN_WEIGHTS):
        w = inp[name].astype(_jnp.float32)
        if MOMENT_SCALE is None:
            s = _jnp.sqrt(_jnp.mean(_jnp.square(w)) + 1e-30)
        else:
            s = MOMENT_SCALE[name]
        km, kv = _jax.random.split(_jax.random.fold_in(key, i + 1))
        out[name] = w
        out["m_" + name] = s * _jax.random.normal(km, w.shape, _jnp.float32)
        out["v_" + name] = (s * s) * _jax.random.uniform(kv, w.shape, _jnp.float32, 0.5, 1.5)
    if N_MICROBATCH > 1:
        for name, axis in PER_EXAMPLE_BATCH_AXIS.items():
            out[name] = _to_microbatches(out[name], axis)
    return {'x': out['x'], 'meta_tokens': out['meta_tokens'], 'w_in': out['w_in'], 'w_branch': out['w_branch'], 'w_out': out['w_out'], 'norm_pre_mix': out['norm_pre_mix'], 'norm_post_mix': out['norm_post_mix'], 'norm_pre_mlp': out['norm_pre_mlp'], 'norm_post_mlp': out['norm_post_mlp'], 'w_up': out['w_up'], 'w_down': out['w_down'], 'ret_gn_w': out['ret_gn_w'], 's5_lam_re': out['s5_lam_re'], 's5_lam_im': out['s5_lam_im'], 's5_b_re': out['s5_b_re'], 's5_b_im': out['s5_b_im'], 's5_c_re': out['s5_c_re'], 's5_c_im': out['s5_c_im'], 's5_d': out['s5_d'], 's5_log_step': out['s5_log_step'], 's5_glu_w': out['s5_glu_w'], 's5_glu_b': out['s5_glu_b'], 'ssd_conv_w': out['ssd_conv_w'], 'ssd_conv_b': out['ssd_conv_b'], 'ssd_dt_bias': out['ssd_dt_bias'], 'ssd_a_log': out['ssd_a_log'], 'ssd_d': out['ssd_d'], 'ssd_norm_w': out['ssd_norm_w'], 'hgrn_lb': out['hgrn_lb'], 'hgrn_norm_w': out['hgrn_norm_w'], 'loss_target': out['loss_target'], 'm_meta_tokens': out['m_meta_tokens'], 'm_w_in': out['m_w_in'], 'm_w_branch': out['m_w_branch'], 'm_w_out': out['m_w_out'], 'm_norm_pre_mix': out['m_norm_pre_mix'], 'm_norm_post_mix': out['m_norm_post_mix'], 'm_norm_pre_mlp': out['m_norm_pre_mlp'], 'm_norm_post_mlp': out['m_norm_post_mlp'], 'm_w_up': out['m_w_up'], 'm_w_down': out['m_w_down'], 'm_ret_gn_w': out['m_ret_gn_w'], 'm_s5_lam_re': out['m_s5_lam_re'], 'm_s5_lam_im': out['m_s5_lam_im'], 'm_s5_b_re': out['m_s5_b_re'], 'm_s5_b_im': out['m_s5_b_im'], 'm_s5_c_re': out['m_s5_c_re'], 'm_s5_c_im': out['m_s5_c_im'], 'm_s5_d': out['m_s5_d'], 'm_s5_log_step': out['m_s5_log_step'], 'm_s5_glu_w': out['m_s5_glu_w'], 'm_s5_glu_b': out['m_s5_glu_b'], 'm_ssd_conv_w': out['m_ssd_conv_w'], 'm_ssd_conv_b': out['m_ssd_conv_b'], 'm_ssd_dt_bias': out['m_ssd_dt_bias'], 'm_ssd_a_log': out['m_ssd_a_log'], 'm_ssd_d': out['m_ssd_d'], 'm_ssd_norm_w': out['m_ssd_norm_w'], 'm_hgrn_lb': out['m_hgrn_lb'], 'm_hgrn_norm_w': out['m_hgrn_norm_w'], 'v_meta_tokens': out['v_meta_tokens'], 'v_w_in': out['v_w_in'], 'v_w_branch': out['v_w_branch'], 'v_w_out': out['v_w_out'], 'v_norm_pre_mix': out['v_norm_pre_mix'], 'v_norm_post_mix': out['v_norm_post_mix'], 'v_norm_pre_mlp': out['v_norm_pre_mlp'], 'v_norm_post_mlp': out['v_norm_post_mlp'], 'v_w_up': out['v_w_up'], 'v_w_down': out['v_w_down'], 'v_ret_gn_w': out['v_ret_gn_w'], 'v_s5_lam_re': out['v_s5_lam_re'], 'v_s5_lam_im': out['v_s5_lam_im'], 'v_s5_b_re': out['v_s5_b_re'], 'v_s5_b_im': out['v_s5_b_im'], 'v_s5_c_re': out['v_s5_c_re'], 'v_s5_c_im': out['v_s5_c_im'], 'v_s5_d': out['v_s5_d'], 'v_s5_log_step': out['v_s5_log_step'], 'v_s5_glu_w': out['v_s5_glu_w'], 'v_s5_glu_b': out['v_s5_glu_b'], 'v_ssd_conv_w': out['v_ssd_conv_w'], 'v_ssd_conv_b': out['v_ssd_conv_b'], 'v_ssd_dt_bias': out['v_ssd_dt_bias'], 'v_ssd_a_log': out['v_ssd_a_log'], 'v_ssd_d': out['v_ssd_d'], 'v_ssd_norm_w': out['v_ssd_norm_w'], 'v_hgrn_lb': out['v_hgrn_lb'], 'v_hgrn_norm_w': out['v_hgrn_norm_w']}


def _loss(weights, diff, rest, loss_target):
    with _jax.named_scope("forward"):
        args = {**rest, TWIN_DIFF_INPUT: diff, **{k: w.astype(_WEIGHT_DTYPES[k]) for k, w in weights.items()}}
        y = _forward(args)
    with _jax.named_scope("loss_head"):
        err = _jnp.square(y.astype(_jnp.float32) - loss_target)
        return 0.5 * _jnp.sum(_jnp.mean(err, axis=-1)) if err.ndim else 0.5 * err


def _adamw(w, g, m, v):
    m = ADAM_B1 * m + (1.0 - ADAM_B1) * g
    v = ADAM_B2 * v + (1.0 - ADAM_B2) * _jnp.square(g)
    m_hat = m / (1.0 - ADAM_B1 ** ADAM_STEP)
    v_hat = v / (1.0 - ADAM_B2 ** ADAM_STEP)
    delta = -ADAM_LR * (m_hat / (_jnp.sqrt(v_hat) + ADAM_EPS) + ADAM_WD * w)
    return delta, m, v


def reference(x, meta_tokens, w_in, w_branch, w_out, norm_pre_mix, norm_post_mix, norm_pre_mlp, norm_post_mlp, w_up, w_down, ret_gn_w, s5_lam_re, s5_lam_im, s5_b_re, s5_b_im, s5_c_re, s5_c_im, s5_d, s5_log_step, s5_glu_w, s5_glu_b, ssd_conv_w, ssd_conv_b, ssd_dt_bias, ssd_a_log, ssd_d, ssd_norm_w, hgrn_lb, hgrn_norm_w, loss_target, m_meta_tokens, m_w_in, m_w_branch, m_w_out, m_norm_pre_mix, m_norm_post_mix, m_norm_pre_mlp, m_norm_post_mlp, m_w_up, m_w_down, m_ret_gn_w, m_s5_lam_re, m_s5_lam_im, m_s5_b_re, m_s5_b_im, m_s5_c_re, m_s5_c_im, m_s5_d, m_s5_log_step, m_s5_glu_w, m_s5_glu_b, m_ssd_conv_w, m_ssd_conv_b, m_ssd_dt_bias, m_ssd_a_log, m_ssd_d, m_ssd_norm_w, m_hgrn_lb, m_hgrn_norm_w, v_meta_tokens, v_w_in, v_w_branch, v_w_out, v_norm_pre_mix, v_norm_post_mix, v_norm_pre_mlp, v_norm_post_mlp, v_w_up, v_w_down, v_ret_gn_w, v_s5_lam_re, v_s5_lam_im, v_s5_b_re, v_s5_b_im, v_s5_c_re, v_s5_c_im, v_s5_d, v_s5_log_step, v_s5_glu_w, v_s5_glu_b, v_ssd_conv_w, v_ssd_conv_b, v_ssd_dt_bias, v_ssd_a_log, v_ssd_d, v_ssd_norm_w, v_hgrn_lb, v_hgrn_norm_w):
    given = dict(x=x, meta_tokens=meta_tokens, w_in=w_in, w_branch=w_branch, w_out=w_out, norm_pre_mix=norm_pre_mix, norm_post_mix=norm_post_mix, norm_pre_mlp=norm_pre_mlp, norm_post_mlp=norm_post_mlp, w_up=w_up, w_down=w_down, ret_gn_w=ret_gn_w, s5_lam_re=s5_lam_re, s5_lam_im=s5_lam_im, s5_b_re=s5_b_re, s5_b_im=s5_b_im, s5_c_re=s5_c_re, s5_c_im=s5_c_im, s5_d=s5_d, s5_log_step=s5_log_step, s5_glu_w=s5_glu_w, s5_glu_b=s5_glu_b, ssd_conv_w=ssd_conv_w, ssd_conv_b=ssd_conv_b, ssd_dt_bias=ssd_dt_bias, ssd_a_log=ssd_a_log, ssd_d=ssd_d, ssd_norm_w=ssd_norm_w, hgrn_lb=hgrn_lb, hgrn_norm_w=hgrn_norm_w, loss_target=loss_target, m_meta_tokens=m_meta_tokens, m_w_in=m_w_in, m_w_branch=m_w_branch, m_w_out=m_w_out, m_norm_pre_mix=m_norm_pre_mix, m_norm_post_mix=m_norm_post_mix, m_norm_pre_mlp=m_norm_pre_mlp, m_norm_post_mlp=m_norm_post_mlp, m_w_up=m_w_up, m_w_down=m_w_down, m_ret_gn_w=m_ret_gn_w, m_s5_lam_re=m_s5_lam_re, m_s5_lam_im=m_s5_lam_im, m_s5_b_re=m_s5_b_re, m_s5_b_im=m_s5_b_im, m_s5_c_re=m_s5_c_re, m_s5_c_im=m_s5_c_im, m_s5_d=m_s5_d, m_s5_log_step=m_s5_log_step, m_s5_glu_w=m_s5_glu_w, m_s5_glu_b=m_s5_glu_b, m_ssd_conv_w=m_ssd_conv_w, m_ssd_conv_b=m_ssd_conv_b, m_ssd_dt_bias=m_ssd_dt_bias, m_ssd_a_log=m_ssd_a_log, m_ssd_d=m_ssd_d, m_ssd_norm_w=m_ssd_norm_w, m_hgrn_lb=m_hgrn_lb, m_hgrn_norm_w=m_hgrn_norm_w, v_meta_tokens=v_meta_tokens, v_w_in=v_w_in, v_w_branch=v_w_branch, v_w_out=v_w_out, v_norm_pre_mix=v_norm_pre_mix, v_norm_post_mix=v_norm_post_mix, v_norm_pre_mlp=v_norm_pre_mlp, v_norm_post_mlp=v_norm_post_mlp, v_w_up=v_w_up, v_w_down=v_w_down, v_ret_gn_w=v_ret_gn_w, v_s5_lam_re=v_s5_lam_re, v_s5_lam_im=v_s5_lam_im, v_s5_b_re=v_s5_b_re, v_s5_b_im=v_s5_b_im, v_s5_c_re=v_s5_c_re, v_s5_c_im=v_s5_c_im, v_s5_d=v_s5_d, v_s5_log_step=v_s5_log_step, v_s5_glu_w=v_s5_glu_w, v_s5_glu_b=v_s5_glu_b, v_ssd_conv_w=v_ssd_conv_w, v_ssd_conv_b=v_ssd_conv_b, v_ssd_dt_bias=v_ssd_dt_bias, v_ssd_a_log=v_ssd_a_log, v_ssd_d=v_ssd_d, v_ssd_norm_w=v_ssd_norm_w, v_hgrn_lb=v_hgrn_lb, v_hgrn_norm_w=v_hgrn_norm_w)
    weights = {n: given[n] for n in TWIN_WEIGHTS}
    shared = {n: given[n] for n in SHARED_INPUTS}
    per_example = {n: given[n] for n in ['x']}
    grad_fn = _jax.value_and_grad(_loss, argnums=(0, 1))

    def one_microbatch(ex, loss_target):
        ex = dict(ex)
        diff = ex.pop(TWIN_DIFF_INPUT)
        return grad_fn(weights, diff, {**shared, **ex}, loss_target)

    if N_MICROBATCH == 1:
        loss, (grad_w, grad_x) = one_microbatch(per_example, given["loss_target"])
    else:
        def body(carry, xs):
            loss_sum, grad_sum = carry
            l_k, (gw_k, gx_k) = one_microbatch(xs[0], xs[1])
            with _jax.named_scope("update"):
                return (loss_sum + l_k, _jax.tree.map(_jnp.add, grad_sum, gw_k)), gx_k

        init = (_jnp.zeros((), _jnp.float32), _jax.tree.map(_jnp.zeros_like, weights))
        (loss, grad_w), grad_x = _jax.lax.scan(body, init, (per_example, given["loss_target"]))
    with _jax.named_scope("update"):
        delta_w, new_m, new_v = {}, {}, {}
        for n in TWIN_WEIGHTS:
            delta_w[n], new_m[n], new_v[n] = _adamw(weights[n], grad_w[n], given["m_" + n], given["v_" + n])
    return (loss, grad_x, *[grad_w[n] for n in TWIN_WEIGHTS], *[delta_w[n] for n in TWIN_WEIGHTS],
            *[new_m[n] for n in TWIN_WEIGHTS], *[new_v[n] for n in TWIN_WEIGHTS])
```

```python
import numpy as np
import jax
import jax.numpy as jnp
from jax import lax
from jax.experimental import pallas as pl
from jax.experimental.pallas import tpu as pltpu

f32 = jnp.float32
bf16 = jnp.bfloat16
HI = lax.Precision.HIGHEST

D_MODEL = 1024
N_META = 16
DEPTH = 4
BW = 512
D_FF = 4096
EPS = 1e-6
N_DEV = 8
RET_HEADS = 4
SSD_HEADS = 8
SSD_GROUPS = 2
HG_HEADS = 4
S5_G, S5_J, S5_P = 32, 16, 64

ADAM_LR, ADAM_B1, ADAM_B2, ADAM_EPS, ADAM_WD, ADAM_STEP = 0.001, 0.9, 0.999, 1e-08, 0.01, 10

CHUNK = 64
HG_SUB = 16
VMEM_LIMIT = 56 * 1024 * 1024

OFF = dict(gates=0, rq=4096, rk=4352, rv=4608, rg=5120, s5u=5632, sxbc=6144, sz=7168, hq=7680, hf=8192,
           hi=8704, hg=9216, dt=9728)
NP = 9856
IN_DIM = 9736


def _orig_col_slices():
    sl = [(5640, 9736)]
    for base in (0, 256):
        for half in (0, 32):
            for h in range(4):
                sl.append((base + 64 * h + half, base + 64 * h + half + 32))
    sl.append((512, 1024))
    sl.append((1024, 1536))
    sl.append((1536, 2048))
    sl.append((2560, 3584))
    sl.append((2048, 2560))
    sl.append((3592, 5640))
    return sl


def _to_my_cols(w):
    parts = [w[..., a:b] for a, b in _orig_col_slices()]
    parts.append(w[..., 3584:3592])
    parts.append(jnp.zeros(w.shape[:-1] + (120,), w.dtype))
    return jnp.concatenate(parts, axis=-1)


def _from_my_cols(g):
    pos, pieces = 0, {}
    for a, b in _orig_col_slices():
        pieces[a] = g[..., pos:pos + (b - a)]
        pos += b - a
    pieces[3584] = g[..., 9728:9736]
    return jnp.concatenate([pieces[k] for k in sorted(pieces)], axis=-1)


def _bdot(a, b):
    return jnp.dot(a.astype(bf16), b.astype(bf16), preferred_element_type=f32)


def _bdot_nt(a, b):
    return lax.dot_general(a.astype(bf16), b.astype(bf16), (((1,), (1,)), ((), ())), preferred_element_type=f32)


def _bdot_tn(a, b):
    return lax.dot_general(a.astype(bf16), b.astype(bf16), (((0,), (0,)), ((), ())), preferred_element_type=f32)


def _hdot(a, b):
    return jnp.dot(a, b, precision=HI, preferred_element_type=f32)


def _sig(x):
    return jax.nn.sigmoid(x)


def _rms(x, w):
    return x * lax.rsqrt(jnp.mean(x * x, axis=-1, keepdims=True) + EPS) * w


def _softplus(x):
    return jnp.maximum(x, 0.0) + jnp.log(1.0 + jnp.exp(-jnp.abs(x)))


def _r(arr, rb, w, jb=0):
    return (arr, (rb, w), lambda n, jb=jb: (n, jb))


def _full(arr):
    nd = arr.ndim
    return (arr, arr.shape, lambda n, nd=nd: (0,) * nd)


def _seq_fwd(name, step, rows, consts, nds, states, outs, n_chunks, save_states=False):
    nr, nc, nn, ns, no = len(rows), len(consts), len(nds), len(states), len(outs)
    whole = list(consts) + list(nds)

    def body(*refs):
        row_refs = refs[:nr]
        whole_hbm = refs[nr:nr + nc + nn]
        out_refs = refs[nr + nc + nn:nr + nc + nn + no]
        k = nr + nc + nn + no
        saved_refs = refs[k:k + (ns if save_states else 0)]
        k += ns if save_states else 0
        whole_vmem = refs[k:k + nc + nn]
        state_refs = refs[k + nc + nn:]
        n = pl.program_id(0)

        @pl.when(n == 0)
        def _():
            for src, dst in zip(whole_hbm, whole_vmem):
                pltpu.sync_copy(src, dst)
            for s in state_refs:
                s[...] = jnp.zeros_like(s)

        st = tuple(s[...] for s in state_refs)
        if save_states:
            for sv, v in zip(saved_refs, st):
                sv[0] = v
        o, new = step(tuple(r[...] for r in row_refs), tuple(c[...] for c in whole_vmem[:nc]),
                      tuple(c[...] for c in whole_vmem[nc:]), st, n)
        for ref, v in zip(out_refs, o):
            ref[...] = v.astype(ref.dtype)
        for s, v in zip(state_refs, new):
            s[...] = v

    in_specs = [pl.BlockSpec(bs, im) for _, bs, im in rows] + [pl.BlockSpec(memory_space=pl.ANY)] * (nc + nn)
    out_shape = [jax.ShapeDtypeStruct(s, d) for s, d, _, _ in outs]
    out_specs = [pl.BlockSpec(bs, im) for _, _, bs, im in outs]
    if save_states:
        for s in states:
            out_shape.append(jax.ShapeDtypeStruct((n_chunks,) + tuple(s), f32))
            out_specs.append(pl.BlockSpec((1,) + tuple(s), lambda n, z=len(s): (n,) + (0,) * z))
    scratch = [pltpu.VMEM(a.shape, a.dtype) for a in whole] + [pltpu.VMEM(tuple(s), f32) for s in states]
    res = pl.pallas_call(
        body, name=name, grid=(n_chunks,), in_specs=in_specs, out_specs=out_specs, out_shape=out_shape,
        scratch_shapes=scratch,
        compiler_params=pltpu.CompilerParams(dimension_semantics=("arbitrary",), vmem_limit_bytes=VMEM_LIMIT),
    )(*[a for a, _, _ in rows], *whole)
    return list(res)


def _seq_bwd(name, step, rows, row_diff, consts, nds, saved, couts, n_chunks):
    nr, nc, nn, ns, no = len(rows), len(consts), len(nds), len(saved), len(couts)
    whole = list(consts) + list(nds)
    didx = [i for i in range(nr) if row_diff[i]]

    def rev(im):
        return lambda n: im(n_chunks - 1 - n)

    def body(*refs):
        row_refs = refs[:nr]
        whole_hbm = refs[nr:nr + nc + nn]
        k = nr + nc + nn
        saved_refs = refs[k:k + ns]
        k += ns
        cout_refs = refs[k:k + no]
        k += no
        drow_refs = refs[k:k + len(didx)]
        k += len(didx)
        dconst_hbm = refs[k:k + nc]
        k += nc
        whole_vmem = refs[k:k + nc + nn]
        k += nc + nn
        dconst_acc = refs[k:k + nc]
        k += nc
        dstate_refs = refs[k:]
        n = pl.program_id(0)

        @pl.when(n == 0)
        def _():
            for src, dst in zip(whole_hbm, whole_vmem):
                pltpu.sync_copy(src, dst)
            for a in dconst_acc:
                a[...] = jnp.zeros_like(a)
            for s in dstate_refs:
                s[...] = jnp.zeros_like(s)

        rvals = tuple(r[...] for r in row_refs)
        cvals = tuple(c[...] for c in whole_vmem[:nc])
        nvals = tuple(c[...] for c in whole_vmem[nc:])
        svals = tuple(s[0] for s in saved_refs)
        cidx = n_chunks - 1 - n

        def f(dr, cv, sv):
            full = list(rvals)
            for i, v in zip(didx, dr):
                full[i] = v
            return step(tuple(full), cv, nvals, sv, cidx)

        (o, _), vf = jax.vjp(f, tuple(rvals[i] for i in didx), cvals, svals)
        ct_o = tuple(c[...].astype(v.dtype) for c, v in zip(cout_refs, o))
        ct_s = tuple(s[...] for s in dstate_refs)
        d_rows, d_consts, d_states = vf((ct_o, ct_s))
        for ref, v in zip(drow_refs, d_rows):
            ref[...] = v.astype(ref.dtype)
        for a, v in zip(dconst_acc, d_consts):
            a[...] += v.astype(f32)
        for s, v in zip(dstate_refs, d_states):
            s[...] = v

        @pl.when(n == n_chunks - 1)
        def _():
            for a, dst in zip(dconst_acc, dconst_hbm):
                pltpu.sync_copy(a, dst)

    in_specs = ([pl.BlockSpec(bs, rev(im)) for _, bs, im in rows]
                + [pl.BlockSpec(memory_space=pl.ANY)] * (nc + nn)
                + [pl.BlockSpec((1,) + a.shape[1:], lambda n, z=a.ndim - 1: (n_chunks - 1 - n,) + (0,) * z) for a in saved]
                + [pl.BlockSpec(bs, rev(im)) for _, bs, im in couts])
    out_shape, out_specs = [], []
    for i in didx:
        a, bs, im = rows[i]
        nrows = a.shape[0]
        out_shape.append(jax.ShapeDtypeStruct((nrows,) + tuple(bs[1:]), f32))
        out_specs.append(pl.BlockSpec(bs, (lambda im: lambda n: (im(n_chunks - 1 - n)[0],) + (0,) * (len(bs) - 1))(im)))
    for c in consts:
        out_shape.append(jax.ShapeDtypeStruct(c.shape, f32))
        out_specs.append(pl.BlockSpec(memory_space=pl.ANY))
    scratch = ([pltpu.VMEM(a.shape, a.dtype) for a in whole] + [pltpu.VMEM(c.shape, f32) for c in consts]
               + [pltpu.VMEM(a.shape[1:], f32) for a in saved])
    res = pl.pallas_call(
        body, name=name, grid=(n_chunks,), in_specs=in_specs, out_specs=out_specs, out_shape=out_shape,
        scratch_shapes=scratch,
        compiler_params=pltpu.CompilerParams(dimension_semantics=("arbitrary",), vmem_limit_bytes=VMEM_LIMIT),
    )(*[a for a, _, _ in rows], *whole, *saved, *[a for a, _, _ in couts])
    res = list(res)
    return res[:len(didx)], res[len(didx):]


def _mm(name, a, b, mode, tm, tn, tk, precision=None):
    if mode == "nn":
        (m, kd), nn_ = a.shape, b.shape[1]
        a_spec = pl.BlockSpec((tm, tk), lambda i, j, k: (i, k))
        b_spec = pl.BlockSpec((tk, tn), lambda i, j, k: (k, j))
        dims = (((1,), (0,)), ((), ()))
    elif mode == "tn":
        (kd, m), nn_ = a.shape, b.shape[1]
        a_spec = pl.BlockSpec((tk, tm), lambda i, j, k: (k, i))
        b_spec = pl.BlockSpec((tk, tn), lambda i, j, k: (k, j))
        dims = (((0,), (0,)), ((), ()))
    else:
        (m, kd), nn_ = a.shape, b.shape[0]
        a_spec = pl.BlockSpec((tm, tk), lambda i, j, k: (i, k))
        b_spec = pl.BlockSpec((tn, tk), lambda i, j, k: (j, k))
        dims = (((1,), (1,)), ((), ()))
    assert m % tm == 0 and nn_ % tn == 0 and kd % tk == 0, (name, a.shape, b.shape, tm, tn, tk)
    nk = kd // tk

    def body(a_ref, b_ref, o_ref, acc):
        k = pl.program_id(2)

        @pl.when(k == 0)
        def _():
            acc[...] = jnp.zeros_like(acc)

        if precision is None:
            acc[...] += lax.dot_general(a_ref[...].astype(bf16), b_ref[...].astype(bf16), dims, preferred_element_type=f32)
        else:
            acc[...] += lax.dot_general(a_ref[...], b_ref[...], dims, precision=precision, preferred_element_type=f32)

        @pl.when(k == nk - 1)
        def _():
            o_ref[...] = acc[...]

    return pl.pallas_call(
        body, name=name, grid=(m // tm, nn_ // tn, nk), in_specs=[a_spec, b_spec],
        out_specs=pl.BlockSpec((tm, tn), lambda i, j, k: (i, j)),
        out_shape=jax.ShapeDtypeStruct((m, nn_), f32), scratch_shapes=[pltpu.VMEM((tm, tn), f32)],
        compiler_params=pltpu.CompilerParams(dimension_semantics=("parallel", "parallel", "arbitrary"),
                                             vmem_limit_bytes=VMEM_LIMIT),
    )(a, b)


def _div_tile(n, want, mult):
    best = None
    for t in range(mult, min(n, want) + 1, mult):
        if n % t == 0:
            best = t
    return best if best is not None else n


def _exchange(name, srcs, same):
    n_arr = len(srcs)
    blks = [s.shape if same else s.shape[1:] for s in srcs]

    def body(*refs):
        src_refs, dst_refs = refs[:n_arr], refs[n_arr:2 * n_arr]
        send_sems, recv_sems, loc_sems = refs[2 * n_arr:]
        me = 4 * lax.axis_index("x") + 2 * lax.axis_index("y") + lax.axis_index("c")

        def mine(i, j):
            return src_refs[i] if same else src_refs[i].at[j]

        def copy(i, j):
            return pltpu.make_async_remote_copy(
                src_ref=mine(i, j), dst_ref=dst_refs[i].at[me], send_sem=send_sems.at[i, j], recv_sem=recv_sems.at[i, me],
                device_id=(j // 4, (j // 2) % 2, j % 2), device_id_type=pl.DeviceIdType.MESH)

        def arrival(i, j):
            return pltpu.make_async_remote_copy(
                src_ref=mine(i, j), dst_ref=dst_refs[i].at[j], send_sem=send_sems.at[i, j], recv_sem=recv_sems.at[i, j],
                device_id=(j // 4, (j // 2) % 2, j % 2), device_id_type=pl.DeviceIdType.MESH)

        local = [pltpu.make_async_copy(src_refs[i] if same else src_refs[i].at[me], dst_refs[i].at[me], loc_sems.at[i])
                 for i in range(n_arr)]
        for cp in local:
            cp.start()
        for j in range(N_DEV):
            @pl.when(j != me)
            def _(j=j):
                for i in range(n_arr):
                    copy(i, j).start()
        for j in range(N_DEV):
            @pl.when(j != me)
            def _(j=j):
                for i in range(n_arr):
                    arrival(i, j).wait_recv()
                    copy(i, j).wait_send()
        for cp in local:
            cp.wait()

    hbm = pl.BlockSpec(memory_space=pltpu.HBM)
    return pl.pallas_call(
        body, name=name, in_specs=[hbm] * n_arr, out_specs=[hbm] * n_arr,
        out_shape=[jax.ShapeDtypeStruct((N_DEV,) + tuple(b), s.dtype) for b, s in zip(blks, srcs)],
        scratch_shapes=[pltpu.SemaphoreType.DMA((n_arr, N_DEV)), pltpu.SemaphoreType.DMA((n_arr, N_DEV)),
                        pltpu.SemaphoreType.DMA((n_arr,))],
    )(*srcs)


def _ret_tables(c):
    gam = 1.0 - 2.0 ** (-5.0 - np.arange(RET_HEADS))
    lg = np.log(gam)
    t = np.arange(c)
    dmat = np.where(t[:, None] >= t[None, :], np.exp((t[:, None] - t[None, :])[None] * lg[:, None, None]), 0.0)
    head_v = np.arange(512) // 128
    ysc = np.exp((t[:, None] + 1) * lg[head_v][None, :])
    wtab = np.exp((c - 1 - t)[:, None] * lg[head_v][None, :])
    gtab = np.exp(c * lg[head_v])[None, :]
    head_k = (np.arange(256) % 128) // 32
    mask = (head_k[:, None] == head_v[None, :]).astype(np.float32)
    hm = (head_k[None, None, :] == np.arange(4)[:, None, None]).astype(np.float32)
    return [jnp.asarray(x, f32) for x in (dmat, ysc, wtab, gtab, mask, hm)]


def _rope_tables(tp):
    inv = 10000.0 ** (-np.arange(32, dtype=np.float32) / 32)
    ang = np.arange(tp, dtype=np.float32)[:, None] * inv[None, :]
    cos = np.tile(np.cos(ang), (1, 4)).astype(np.float32)
    sin = np.tile(np.sin(ang), (1, 4)).astype(np.float32)
    return jnp.asarray(cos), jnp.asarray(sin)


def _tri(c):
    t = np.arange(c)
    return jnp.asarray((t[:, None] >= t[None, :]).astype(np.float32))


def _ssd_tables(c):
    sh = np.zeros((3 * c, c + 8), np.float32)
    for d in (3, 2, 1):
        for t in range(c):
            sh[(3 - d) * c + t, 8 + t - d] = 1.0
    e = np.zeros((128, 512), np.float32)
    for h in range(SSD_HEADS):
        e[h, 64 * h:64 * h + 64] = 1.0
    mg = ((np.arange(256) // 128)[:, None] == (np.arange(512) // 256)[None, :]).astype(np.float32)
    cm4 = ((np.arange(256) // 64)[None, None, :] == np.arange(4)[:, None, None]).astype(np.float32)
    return [jnp.asarray(x) for x in (sh, e, mg, cm4)]


def _ret_step(rows, consts, nds, states, _):
    q, k, v, g, cos, sin = rows
    (gnw,) = consts
    dmat, ysc, wtab, gtab, mask, hm = nds
    (s,) = states
    q1, q2, k1, k2 = q[:, :128], q[:, 128:], k[:, :128], k[:, 128:]
    qr = jnp.concatenate([q1 * cos - q2 * sin, q1 * sin + q2 * cos], axis=1)
    kr = jnp.concatenate([k1 * cos - k2 * sin, k1 * sin + k2 * cos], axis=1) * 0.125
    y = _bdot(qr, s) * ysc
    parts = []
    for h in range(RET_HEADS):
        a = _bdot_nt(qr * hm[h], kr) * dmat[h]
        parts.append(_bdot(a, v[:, 128 * h:128 * h + 128]))
    y = y + jnp.concatenate(parts, axis=1)
    s_new = s * gtab + _bdot_tn(kr, v * wtab) * mask
    outs = []
    for h in range(RET_HEADS):
        yh = y[:, 128 * h:128 * h + 128]
        d = yh - jnp.mean(yh, axis=-1, keepdims=True)
        outs.append(d * lax.rsqrt(jnp.mean(d * d, axis=-1, keepdims=True) + EPS))
    yn = jnp.concatenate(outs, axis=1) * gnw
    return (g * _sig(g) * yn,), (s_new,)


def _ssd_step(rows, consts, nds, states, _):
    z, xbc, dt128 = rows
    cw, cb, dtb, alog, dsk, nw = consts
    tri, sh, e, mg, cm4 = nds
    s, tail = states
    c = xbc.shape[0]
    shifted = _hdot(sh, jnp.concatenate([tail, xbc], axis=0))
    xc = (cw[0:1] * shifted[0:c] + cw[1:2] * shifted[c:2 * c] + cw[2:3] * shifted[2 * c:3 * c] + cw[3:4] * xbc + cb)
    xc = xc * _sig(xc)
    xs, bm, cm = xc[:, :512], xc[:, 512:768], xc[:, 768:]
    dt = _softplus(dt128 + dtb)
    la = dt * (-jnp.exp(alog))
    cum = _hdot(tri, la)
    cum_t = cum.T
    dtx, cumx = _hdot(dt, e), _hdot(cum, e)
    lastx = cumx[c - 1:c]
    dskx = _hdot(jnp.broadcast_to(dsk, (8, 128)), e)[0:1]
    v = xs * dtx
    y = _bdot(cm, s) * jnp.exp(cumx)
    lane = lax.broadcasted_iota(jnp.int32, cum.shape, 1)
    sub = lax.broadcasted_iota(jnp.int32, cum_t.shape, 0)
    parts = []
    for grp in range(SSD_GROUPS):
        sg = _bdot_nt(cm[:, 128 * grp:128 * grp + 128], bm[:, 128 * grp:128 * grp + 128])
        vg = v[:, 256 * grp:256 * grp + 256]
        acc = jnp.zeros((c, 256), f32)
        for hh in range(4):
            h = 4 * grp + hh
            col = jnp.sum(jnp.where(lane == h, cum, 0.0), axis=1, keepdims=True)
            row = jnp.sum(jnp.where(sub == h, cum_t, 0.0), axis=0, keepdims=True)
            dec = jnp.exp(jnp.where(tri > 0.5, col - row, -1e30))
            acc = acc + _bdot(sg * dec, vg * cm4[hh])
        parts.append(acc)
    y = y + jnp.concatenate(parts, axis=1) + dskx * xs
    s_new = s * jnp.exp(lastx) + _bdot_tn(bm, v * jnp.exp(lastx - cumx)) * mg
    y = y * (z * _sig(z))
    outs = []
    for grp in range(SSD_GROUPS):
        yg = y[:, 256 * grp:256 * grp + 256]
        outs.append(yg * lax.rsqrt(jnp.mean(yg * yg, axis=-1, keepdims=True) + EPS))
    return (jnp.concatenate(outs, axis=1) * nw,), (s_new, xbc[c - 8:c])


def _hg_step(rows, consts, nds, states, _):
    hq, hf, hi, hgate = rows
    lb, nw = consts
    (tri,) = nds
    (st,) = states
    c = hq.shape[0]
    q = hq * _sig(hq)
    f = lb + (1.0 - lb) * _sig(hf)
    lf = jnp.log(f)
    k = 1.0 - f
    v = hi
    cum = _hdot(tri, lf)
    last = jnp.sum(lf, axis=0, keepdims=True)
    qd = q * jnp.exp(cum)
    kw = k * jnp.exp(last - cum)
    heads = [slice(128 * h, 128 * h + 128) for h in range(HG_HEADS)]
    y_rows = []
    rowid = lax.broadcasted_iota(jnp.int32, (HG_SUB, 512), 0)
    for i in range(c // HG_SUB):
        r0 = HG_SUB * i
        qi, ci, ki, vi = q[r0:r0 + HG_SUB], cum[r0:r0 + HG_SUB], k[r0:r0 + HG_SUB], v[r0:r0 + HG_SUB]
        yi = [jnp.zeros((HG_SUB, 128), f32) for _ in heads]
        for s_ in range(HG_SUB):
            e = qi * ki[s_:s_ + 1] * jnp.exp(jnp.where(rowid >= s_, ci - ci[s_:s_ + 1], -1e30))
            for h, sl in enumerate(heads):
                yi[h] = yi[h] + jnp.sum(e[:, sl], axis=1, keepdims=True) * vi[s_:s_ + 1, sl]
        if i > 0:
            b = cum[r0 - 1:r0]
            qs = qi * jnp.exp(ci - b)
            ks = k[:r0] * jnp.exp(b - cum[:r0])
            for h, sl in enumerate(heads):
                yi[h] = yi[h] + _bdot(_bdot_nt(qs[:, sl], ks[:, sl]), v[:r0, sl])
        y_rows.append(jnp.concatenate(yi, axis=1))
    y_in = jnp.concatenate(y_rows, axis=0)
    y = y_in + jnp.concatenate([_bdot_nt(qd[:, sl], st[:, sl]) for sl in heads], axis=1)
    st_new = jnp.concatenate([st[:, sl] * jnp.exp(last[:, sl]) + _bdot_tn(v[:, sl], kw[:, sl]) for sl in heads], axis=1)
    outs = []
    for sl in heads:
        yh = y[:, sl]
        outs.append(yh * lax.rsqrt(jnp.mean(yh * yh, axis=-1, keepdims=True) + EPS))
    o = jnp.concatenate(outs, axis=1) * nw
    return (o * (hgate * _sig(hgate)),), (st_new,)


def _s5post_step(rows, consts, nds, states, _):
    ycore, u = rows
    dsk, gw, gb = consts
    y = jax.nn.gelu(ycore + dsk * u)
    zz = _bdot(y, gw) + gb
    return (zz[:, :512] * _sig(zz[:, 512:]),), ()


def _s5prep_step(rows, consts, nds, states, _):
    lr, li, lstep, btr, bti = rows
    (rep,) = nds
    step = jnp.exp(lstep)
    mag = jnp.exp(lr * step)
    ab_re, ab_im = mag * jnp.cos(li * step), mag * jnp.sin(li * step)
    inv = 1.0 / (lr * lr + li * li)
    co_re = ((ab_re - 1.0) * lr + ab_im * li) * inv
    co_im = (ab_im * lr - (ab_re - 1.0) * li) * inv
    cre, cim = _hdot(rep, co_re), _hdot(rep, co_im)
    return (ab_re, ab_im, cre * btr - cim * bti, cre * bti + cim * btr), ()


def _lb_step(rows, consts, nds, states, _):
    (x,) = rows
    (lmat,) = nds
    valid = lax.broadcasted_iota(jnp.int32, x.shape, 0) < DEPTH
    xm = jnp.where(valid, x, -1e30)
    ex = jnp.where(valid, jnp.exp(xm - jnp.max(xm, axis=0, keepdims=True)), 0.0)
    sm = ex / jnp.sum(ex, axis=0, keepdims=True)
    return (_hdot(lmat, sm),), ()


def _rmsn_step(rows, consts, nds, states, _):
    (h,) = rows
    (w,) = consts
    return (_rms(h, w),), ()


def _merge_step(rows, consts, nds, states, _):
    yr, ys5, yssd, yhg, gates, h = rows
    wb, wout, npost = consts
    mixed = jnp.zeros(h.shape, f32)
    for n, yb in enumerate((yr, ys5, yssd, yhg)):
        mixed = mixed + _sig(gates[:, 1024 * n:1024 * n + 1024]) * _bdot(yb, wb[n])
    return (h + _rms(_bdot(mixed, wout), npost),), ()


def _mlp_head_step(rows, consts, nds, states, _):
    (h,) = rows
    npre, wup = consts
    return (_bdot(_rms(h, npre), wup),), ()


def _mlp_tail_step(rows, consts, nds, states, _):
    a, h = rows
    wdown, npost = consts
    act = jnp.square(jnp.maximum(a, 0.0))
    return (h + _rms(_bdot(act, wdown), npost),), ()


SCAN_RB = 16


def _scan_inplace(xr, xi, ar0, ai0, tp, reverse):
    rb = SCAN_RB
    w = xr.shape[1]
    nblk = tp // rb
    ar, ai = ar0, ai0
    d = 1
    while d < tp:
        arb, aib = jnp.broadcast_to(ar, (rb, w)), jnp.broadcast_to(ai, (rb, w))

        def upd(r0, sr, si, arb=arb, aib=aib):
            cr, ci = xr[pl.ds(r0, rb), :], xi[pl.ds(r0, rb), :]
            xr[pl.ds(r0, rb), :] = cr + arb * sr - aib * si
            xi[pl.ds(r0, rb), :] = ci + arb * si + aib * sr

        rowid = lax.broadcasted_iota(jnp.int32, (rb, w), 0)
        if d < 8:
            if not reverse:
                def body(i, carry, d=d, upd=upd):
                    r0 = pl.multiple_of(tp - (i + 1) * rb, 8)
                    lo = pl.multiple_of(r0 - 8, 8)
                    sr = pltpu.roll(xr[pl.ds(lo, rb + 8), :], d, 0)[8:, :]
                    si = pltpu.roll(xi[pl.ds(lo, rb + 8), :], d, 0)[8:, :]
                    upd(r0, sr, si)
                    return carry
                lax.fori_loop(0, nblk - 1, body, 0)
                sr = jnp.where(rowid >= d, pltpu.roll(xr[pl.ds(0, rb), :], d, 0), 0.0)
                si = jnp.where(rowid >= d, pltpu.roll(xi[pl.ds(0, rb), :], d, 0), 0.0)
                upd(0, sr, si)
            else:
                def body(i, carry, d=d, upd=upd):
                    r0 = pl.multiple_of(i * rb, 8)
                    sr = pltpu.roll(xr[pl.ds(r0, rb + 8), :], rb + 8 - d, 0)[:rb, :]
                    si = pltpu.roll(xi[pl.ds(r0, rb + 8), :], rb + 8 - d, 0)[:rb, :]
                    upd(r0, sr, si)
                    return carry
                lax.fori_loop(0, nblk - 1, body, 0)
                sr = jnp.where(rowid < rb - d, pltpu.roll(xr[pl.ds(tp - rb, rb), :], rb - d, 0), 0.0)
                si = jnp.where(rowid < rb - d, pltpu.roll(xi[pl.ds(tp - rb, rb), :], rb - d, 0), 0.0)
                upd(tp - rb, sr, si)
        else:
            nfull = (tp - d) // rb
            rem = (tp - d) - nfull * rb
            if not reverse:
                def body(i, carry, d=d, upd=upd):
                    r0 = pl.multiple_of(tp - (i + 1) * rb, 8)
                    lo = pl.multiple_of(r0 - d, 8)
                    upd(r0, xr[pl.ds(lo, rb), :], xi[pl.ds(lo, rb), :])
                    return carry
                lax.fori_loop(0, nfull, body, 0)
                if rem:
                    cr, ci = xr[pl.ds(d, rem), :], xi[pl.ds(d, rem), :]
                    sr, si = xr[pl.ds(0, rem), :], xi[pl.ds(0, rem), :]
                    xr[pl.ds(d, rem), :] = cr + ar * sr - ai * si
                    xi[pl.ds(d, rem), :] = ci + ar * si + ai * sr
            else:
                def body(i, carry, d=d, upd=upd):
                    r0 = pl.multiple_of(i * rb, 8)
                    hi = pl.multiple_of(r0 + d, 8)
                    upd(r0, xr[pl.ds(hi, rb), :], xi[pl.ds(hi, rb), :])
                    return carry
                lax.fori_loop(0, nfull, body, 0)
                if rem:
                    lo = nfull * rb
                    cr, ci = xr[pl.ds(lo, rem), :], xi[pl.ds(lo, rem), :]
                    sr, si = xr[pl.ds(lo + d, rem), :], xi[pl.ds(lo + d, rem), :]
                    xr[pl.ds(lo, rem), :] = cr + ar * sr - ai * si
                    xi[pl.ds(lo, rem), :] = ci + ar * si + ai * sr
        ar, ai = ar * ar - ai * ai, 2.0 * ar * ai
        d *= 2


def _s5_blocks(tp):
    rbm = tp // 8 if (tp // 8) % 8 == 0 and tp % 8 == 0 else tp
    return rbm, tp // rbm


def _s5_in_specs(proj, tp):
    ucol = OFF["s5u"] // 128
    return [
        pl.BlockSpec((tp, 128), lambda cb: (0, ucol + cb)),
        pl.BlockSpec((128, 512), lambda cb: (cb, 0)),
        pl.BlockSpec((128, 512), lambda cb: (cb, 0)),
        pl.BlockSpec((1, 512), lambda cb: (0, cb)),
        pl.BlockSpec((1, 512), lambda cb: (0, cb)),
        pl.BlockSpec((512, 128), lambda cb: (cb, 0)),
        pl.BlockSpec((512, 128), lambda cb: (cb, 0)),
    ]


def _s5_core_fwd(tag, proj, bbr, bbi, ar, ai, ccr, cci):
    tp = proj.shape[0]
    rbm, nb = _s5_blocks(tp)

    def body(u_ref, bbr_ref, bbi_ref, ar_ref, ai_ref, ccr_ref, cci_ref, y_ref, xr, xi):
        def fill(i, carry):
            r = pl.multiple_of(i * rbm, 8)
            ub = u_ref[pl.ds(r, rbm), :]
            xr[pl.ds(r, rbm), :] = _bdot(ub, bbr_ref[...])
            xi[pl.ds(r, rbm), :] = _bdot(ub, bbi_ref[...])
            return carry
        lax.fori_loop(0, nb, fill, 0)
        _scan_inplace(xr, xi, ar_ref[...], ai_ref[...], tp, False)

        def out(i, carry):
            r = pl.multiple_of(i * rbm, 8)
            y_ref[pl.ds(r, rbm), :] = (_bdot(xr[pl.ds(r, rbm), :], ccr_ref[...]) - _bdot(xi[pl.ds(r, rbm), :], cci_ref[...]))
            return carry
        lax.fori_loop(0, nb, out, 0)

    return pl.pallas_call(
        body, name="s5_core_fwd" + tag, grid=(4,), in_specs=_s5_in_specs(proj, tp),
        out_specs=pl.BlockSpec((tp, 128), lambda cb: (0, cb)),
        out_shape=jax.ShapeDtypeStruct((tp, 512), f32),
        scratch_shapes=[pltpu.VMEM((tp, 512), f32), pltpu.VMEM((tp, 512), f32)],
        compiler_params=pltpu.CompilerParams(dimension_semantics=("arbitrary",), vmem_limit_bytes=VMEM_LIMIT),
    )(proj, bbr, bbi, ar, ai, ccr, cci)


def _s5_core_bwd(tag, proj, bbr, bbi, ar, ai, ccr, cci, dy, du_post):
    tp = proj.shape[0]
    rbm, nb = _s5_blocks(tp)
    rb = SCAN_RB

    def body(u_ref, bbr_ref, bbi_ref, ar_ref, ai_ref, ccr_ref, cci_ref, dy_ref, dup_ref,
             du_ref, dbbr_ref, dbbi_ref, dar_ref, dai_ref, dccr_ref, dcci_ref, xr, xi, gr, gi):
        dccr_ref[...] = jnp.zeros_like(dccr_ref)
        dcci_ref[...] = jnp.zeros_like(dcci_ref)
        dbbr_ref[...] = jnp.zeros_like(dbbr_ref)
        dbbi_ref[...] = jnp.zeros_like(dbbi_ref)

        def fill(i, carry):
            r = pl.multiple_of(i * rbm, 8)
            ub = u_ref[pl.ds(r, rbm), :]
            xr[pl.ds(r, rbm), :] = _bdot(ub, bbr_ref[...])
            xi[pl.ds(r, rbm), :] = _bdot(ub, bbi_ref[...])
            return carry
        lax.fori_loop(0, nb, fill, 0)
        _scan_inplace(xr, xi, ar_ref[...], ai_ref[...], tp, False)

        def seed(i, carry):
            r = pl.multiple_of(i * rbm, 8)
            dyb = dy_ref[pl.ds(r, rbm), :]
            dccr_ref[...] += _bdot_tn(xr[pl.ds(r, rbm), :], dyb)
            dcci_ref[...] -= _bdot_tn(xi[pl.ds(r, rbm), :], dyb)
            gr[pl.ds(r, rbm), :] = _bdot_nt(dyb, ccr_ref[...])
            gi[pl.ds(r, rbm), :] = -_bdot_nt(dyb, cci_ref[...])
            return carry
        lax.fori_loop(0, nb, seed, 0)
        _scan_inplace(gr, gi, ar_ref[...], -ai_ref[...], tp, True)

        w = xr.shape[1]
        rowid = lax.broadcasted_iota(jnp.int32, (rb, w), 0)

        def prods(pr, pi, r0):
            g_r, g_i = gr[pl.ds(r0, rb), :], gi[pl.ds(r0, rb), :]
            return pr * g_r + pi * g_i, pr * g_i - pi * g_r

        def dacc(i, carry):
            r0 = pl.multiple_of((i + 1) * rb, 8)
            lo = pl.multiple_of(r0 - 8, 8)
            pr = pltpu.roll(xr[pl.ds(lo, rb + 8), :], 1, 0)[8:, :]
            pi = pltpu.roll(xi[pl.ds(lo, rb + 8), :], 1, 0)[8:, :]
            a, b = prods(pr, pi, r0)
            return carry[0] + a, carry[1] + b
        pr0 = jnp.where(rowid >= 1, pltpu.roll(xr[pl.ds(0, rb), :], 1, 0), 0.0)
        pi0 = jnp.where(rowid >= 1, pltpu.roll(xi[pl.ds(0, rb), :], 1, 0), 0.0)
        acc_r, acc_i = lax.fori_loop(0, tp // rb - 1, dacc, prods(pr0, pi0, 0))
        dar_ref[...] = jnp.sum(acc_r, axis=0, keepdims=True)
        dai_ref[...] = jnp.sum(acc_i, axis=0, keepdims=True)

        def tail(i, carry):
            r = pl.multiple_of(i * rbm, 8)
            g_r, g_i = gr[pl.ds(r, rbm), :], gi[pl.ds(r, rbm), :]
            ub = u_ref[pl.ds(r, rbm), :]
            du_ref[pl.ds(r, rbm), :] = _bdot_nt(g_r, bbr_ref[...]) + _bdot_nt(g_i, bbi_ref[...]) + dup_ref[pl.ds(r, rbm), :]
            dbbr_ref[...] += _bdot_tn(ub, g_r)
            dbbi_ref[...] += _bdot_tn(ub, g_i)
            return carry
        lax.fori_loop(0, nb, tail, 0)

    col = lambda cb: (0, cb)
    blk = lambda cb: (cb, 0)
    return pl.pallas_call(
        body, name="s5_core_bwd" + tag, grid=(4,),
        in_specs=_s5_in_specs(proj, tp) + [pl.BlockSpec((tp, 128), col), pl.BlockSpec((tp, 128), col)],
        out_specs=[pl.BlockSpec((tp, 128), col), pl.BlockSpec((128, 512), blk), pl.BlockSpec((128, 512), blk),
                   pl.BlockSpec((1, 512), col), pl.BlockSpec((1, 512), col),
                   pl.BlockSpec((512, 128), blk), pl.BlockSpec((512, 128), blk)],
        out_shape=[jax.ShapeDtypeStruct((tp, 512), f32), jax.ShapeDtypeStruct((512, 512), f32),
                   jax.ShapeDtypeStruct((512, 512), f32), jax.ShapeDtypeStruct((1, 2048), f32),
                   jax.ShapeDtypeStruct((1, 2048), f32), jax.ShapeDtypeStruct((2048, 128), f32),
                   jax.ShapeDtypeStruct((2048, 128), f32)],
        scratch_shapes=[pltpu.VMEM((tp, 512), f32)] * 4,
        compiler_params=pltpu.CompilerParams(dimension_semantics=("arbitrary",), vmem_limit_bytes=VMEM_LIMIT),
    )(proj, bbr, bbi, ar, ai, ccr, cci, dy, du_post)


_EYE8 = np.eye(8, dtype=np.float32)


def _bb_dense(bb):
    return jnp.einsum("cgjp,gh->cgjhp", bb.reshape(4, 8, 16, 64), _EYE8).reshape(512, 512)


def _bb_diag(d):
    return jnp.einsum("cgjhp,gh->cgjp", d.reshape(4, 8, 16, 8, 64), _EYE8).reshape(512, 64)


def _cc_dense(cmat):
    return jnp.einsum("cgjp,gh->cgphj", cmat.reshape(4, 8, 16, 64), _EYE8).reshape(2048, 128)


def _cc_diag(d):
    return jnp.einsum("cgphj,gh->cgjp", d.reshape(4, 8, 64, 8, 16), _EYE8).reshape(32, 16, 64)


def _tables(tp):
    cos, sin = _rope_tables(tp)
    rep = np.zeros((512, 32), np.float32)
    rep[np.arange(512), np.arange(512) // 16] = 1.0
    lmat = np.zeros((8, 8), np.float32)
    for l in range(DEPTH):
        lmat[l, 1:l + 1] = 1.0
    return dict(cos=cos, sin=sin, ret=_ret_tables(CHUNK), tri=_tri(CHUNK), ssd=_ssd_tables(CHUNK),
                rep=jnp.asarray(rep), lmat=jnp.asarray(lmat))


def _tiles(tp):
    return dict(tr=_div_tile(tp, 352, 16), tmg=_div_tile(tp, 192, 16), tmlp=_div_tile(tp, 96, 16))


def _mixer_rows(proj, c):
    ret = [_r(proj, c, 256, OFF["rq"] // 256), _r(proj, c, 256, OFF["rk"] // 256), _r(proj, c, 512, OFF["rv"] // 512),
           _r(proj, c, 512, OFF["rg"] // 512)]
    ssd = [_r(proj, c, 512, OFF["sz"] // 512), _r(proj, c, 1024, OFF["sxbc"] // 1024), _r(proj, c, 128, OFF["dt"] // 128)]
    hg = [_r(proj, c, 512, OFF[k] // 512) for k in ("hq", "hf", "hi", "hg")]
    return ret, ssd, hg


def _out2(rows, w, rb, dtype=f32):
    return (rows, w), dtype, (rb, w), lambda n: (n, 0)


def _s5_prep_rows(p):
    return [_full(p["lam_re"]), _full(p["lam_im"]), _full(p["lstep"]), _full(p["bt_re"]), _full(p["bt_im"])]


def _s5_consts(p, tabs, tag):
    whole = lambda s: (s, f32, s, lambda n: (0, 0))
    ab_re, ab_im, bb_re, bb_im = _seq_fwd("s5_prep" + tag, _s5prep_step, _s5_prep_rows(p), [], [tabs["rep"]], [],
                                          [whole((32, 64)), whole((32, 64)), whole((512, 64)), whole((512, 64))], 1)
    return (_bb_dense(bb_re).astype(bf16), _bb_dense(bb_im).astype(bf16), ab_re.reshape(1, 2048), ab_im.reshape(1, 2048),
            _cc_dense(p["c_re"]).astype(bf16), _cc_dense(p["c_im"]).astype(bf16))


def _rmsn_step_b(rows, consts, nds, states, n):
    (o,), _ = _rmsn_step(rows, consts, nds, states, n)
    return (o, rows[0]), ()


def _mlp_head_step_b(rows, consts, nds, states, n):
    (o,), _ = _mlp_head_step(rows, consts, nds, states, n)
    return (o, rows[0]), ()


def _layer_fwd(h, p, tabs, tag):
    tp = h.shape[0]
    c = CHUNK
    nch = tp // c
    tl = _tiles(tp)
    tr, tmg, tmlp = tl["tr"], tl["tmg"], tl["tmlp"]
    (u,) = _seq_fwd("rms_premix" + tag, _rmsn_step, [_r(h, tr, 1024)], [p["npm"]], [], [], [_out2(tp, 1024, tr)], tp // tr)
    proj = _mm("in_proj" + tag, u, p["w_in"], "nn", _div_tile(tp, 1056, 16), 1408, 1024)
    ret_rows, ssd_rows, hg_rows = _mixer_rows(proj, c)
    y_ret, ret_s = _seq_fwd("ret_fwd" + tag, _ret_step, ret_rows + [_r(tabs["cos"], c, 128), _r(tabs["sin"], c, 128)],
                            [p["ret_gn"]], tabs["ret"], [(256, 512)], [_out2(tp, 512, c)], nch, save_states=True)
    ycore = _s5_core_fwd(tag, proj, *_s5_consts(p, tabs, tag))
    (y_s5,) = _seq_fwd("s5_post" + tag, _s5post_step, [_r(ycore, tr, 512), _r(proj, tr, 512, OFF["s5u"] // 512)],
                       [p["s5_d"], p["glu_w"], p["glu_b"]], [], [], [_out2(tp, 512, tr)], tp // tr)
    y_ssd, ssd_s, ssd_tail = _seq_fwd(
        "ssd_fwd" + tag, _ssd_step, ssd_rows, [p["conv_w"], p["conv_b"], p["dt_bias"], p["a_log"], p["ssd_d"], p["ssd_nw"]],
        [tabs["tri"]] + tabs["ssd"], [(256, 512), (8, 1024)], [_out2(tp, 512, c)], nch, save_states=True)
    y_hg, hg_s = _seq_fwd("hg_fwd" + tag, _hg_step, hg_rows, [p["lb"], p["hg_nw"]], [tabs["tri"]], [(128, 512)],
                          [_out2(tp, 512, c)], nch, save_states=True)
    (h_mid,) = _seq_fwd(
        "merge_fwd" + tag, _merge_step,
        [_r(y_ret, tmg, 512), _r(y_s5, tmg, 512), _r(y_ssd, tmg, 512), _r(y_hg, tmg, 512), _r(proj, tmg, 4096, 0),
         _r(h, tmg, 1024)],
        [p["w_branch"], p["w_out"], p["npostmix"]], [], [], [_out2(tp, 1024, tmg)], tp // tmg)
    (a,) = _seq_fwd("mlp_head_fwd" + tag, _mlp_head_step, [_r(h_mid, tmlp, 1024)], [p["npremlp"], p["w_up"]], [], [],
                    [_out2(tp, 4096, tmlp)], tp // tmlp)
    (h_new,) = _seq_fwd("mlp_tail_fwd" + tag, _mlp_tail_step, [_r(a, tmlp, 4096), _r(h_mid, tmlp, 1024)],
                        [p["w_down"], p["npostmlp"]], [], [], [_out2(tp, 1024, tmlp)], tp // tmlp)
    saved = dict(h=h, u=u, proj=proj, ret_s=ret_s, ycore=ycore, ssd_s=ssd_s, ssd_tail=ssd_tail, hg_s=hg_s,
                 y_ret=y_ret, y_s5=y_s5, y_ssd=y_ssd, y_hg=y_hg, h_mid=h_mid, a=a)
    return h_new, saved


def _layer_bwd(dh, p, sv, tabs, tag):
    tp = dh.shape[0]
    c = CHUNK
    nch = tp // c
    tl = _tiles(tp)
    tr, tmg, tmlp = tl["tr"], tl["tmg"], tl["tmlp"]
    proj = sv["proj"]
    g = {}
    (d_a, d_hmid), (g["w_down"], g["npostmlp"]) = _seq_bwd(
        "mlp_tail_bwd" + tag, _mlp_tail_step, [_r(sv["a"], tmlp, 4096), _r(sv["h_mid"], tmlp, 1024)], [True, True],
        [p["w_down"], p["npostmlp"]], [], [], [_r(dh, tmlp, 1024)], tp // tmlp)
    (d_hmid,), (g["npremlp"], g["w_up"]) = _seq_bwd(
        "mlp_head_bwd" + tag, _mlp_head_step_b, [_r(sv["h_mid"], tmlp, 1024)], [True], [p["npremlp"], p["w_up"]], [], [],
        [_r(d_a, tmlp, 4096), _r(d_hmid, tmlp, 1024)], tp // tmlp)
    (dy_ret, dy_s5, dy_ssd, dy_hg, d_gates, d_h1), (g["w_branch"], g["w_out"], g["npostmix"]) = _seq_bwd(
        "merge_bwd" + tag, _merge_step,
        [_r(sv["y_ret"], tmg, 512), _r(sv["y_s5"], tmg, 512), _r(sv["y_ssd"], tmg, 512), _r(sv["y_hg"], tmg, 512),
         _r(proj, tmg, 4096, 0), _r(sv["h"], tmg, 1024)], [True] * 6,
        [p["w_branch"], p["w_out"], p["npostmix"]], [], [], [_r(d_hmid, tmg, 1024)], tp // tmg)
    ret_rows, ssd_rows, hg_rows = _mixer_rows(proj, c)
    (d_hq, d_hf, d_hi, d_hg), (g["lb"], g["hg_nw"]) = _seq_bwd(
        "hg_bwd" + tag, _hg_step, hg_rows, [True] * 4, [p["lb"], p["hg_nw"]], [tabs["tri"]], [sv["hg_s"]],
        [_r(dy_hg, c, 512)], nch)
    (d_z, d_xbc, d_dt), (g["conv_w"], g["conv_b"], g["dt_bias"], g["a_log"], g["ssd_d"], g["ssd_nw"]) = _seq_bwd(
        "ssd_bwd" + tag, _ssd_step, ssd_rows, [True] * 3,
        [p["conv_w"], p["conv_b"], p["dt_bias"], p["a_log"], p["ssd_d"], p["ssd_nw"]], [tabs["tri"]] + tabs["ssd"],
        [sv["ssd_s"], sv["ssd_tail"]], [_r(dy_ssd, c, 512)], nch)
    (d_ycore, du_post), (g["s5_d"], g["glu_w"], g["glu_b"]) = _seq_bwd(
        "s5_post_bwd" + tag, _s5post_step, [_r(sv["ycore"], tr, 512), _r(proj, tr, 512, OFF["s5u"] // 512)], [True, True],
        [p["s5_d"], p["glu_w"], p["glu_b"]], [], [], [_r(dy_s5, tr, 512)], tp // tr)
    du_s5, dbbr, dbbi, dar, dai, dccr, dcci = _s5_core_bwd(tag, proj, *_s5_consts(p, tabs, tag + "b"), d_ycore, du_post)
    g["c_re"], g["c_im"] = _cc_diag(dccr), _cc_diag(dcci)
    (g["lam_re"], g["lam_im"], g["lstep"], g["bt_re"], g["bt_im"]), _ = _seq_bwd(
        "s5_prep_bwd" + tag, _s5prep_step, _s5_prep_rows(p), [True] * 5, [], [tabs["rep"]], [],
        [_full(dar.reshape(32, 64)), _full(dai.reshape(32, 64)), _full(_bb_diag(dbbr)), _full(_bb_diag(dbbi))], 1)
    (d_q, d_k, d_v, d_g), (g["ret_gn"],) = _seq_bwd(
        "ret_bwd" + tag, _ret_step, ret_rows + [_r(tabs["cos"], c, 128), _r(tabs["sin"], c, 128)], [True] * 4 + [False] * 2,
        [p["ret_gn"]], tabs["ret"], [sv["ret_s"]], [_r(dy_ret, c, 512)], nch)
    dproj = jnp.concatenate([d_gates, d_q, d_k, d_v, d_g, du_s5, d_xbc, d_z, d_hq, d_hf, d_hi, d_hg, d_dt], axis=1)
    g["w_in"] = _mm("in_proj_dw" + tag, sv["u"], dproj, "tn", 512, 1408, _div_tile(tp, 704, 16))
    du = _mm("in_proj_dx" + tag, dproj, p["w_in"], "nt", _div_tile(tp, 1056, 16), 1024, 1408)
    (dh_prev,), (g["npm"],) = _seq_bwd("rms_premix_bwd" + tag, _rmsn_step_b, [_r(sv["h"], tr, 1024)], [True], [p["npm"]], [], [],
                                       [_r(du, tr, 1024), _r(d_h1, tr, 1024)], tp // tr)
    return dh_prev, g


def _loss_call(h, tgt, lo, hi):
    tp = h.shape[0]
    tr = _div_tile(tp, 352, 16)

    def step(rows, consts, nds, states, n):
        hh, tt = rows
        row = n * tr + lax.broadcasted_iota(jnp.int32, hh.shape, 0)
        err = jnp.where((row >= lo) & (row < hi), hh - tt, 0.0)
        part = 0.5 * jnp.sum(err * err) * (1.0 / D_MODEL)
        return (err * (1.0 / D_MODEL), jnp.zeros((8, 128), f32) + part), ()

    dh, parts = _seq_fwd("loss_head", step, [_r(h, tr, 1024), _r(tgt, tr, 1024)], [], [], [],
                         [_out2(tp, 1024, tr), ((8 * (tp // tr), 128), f32, (8, 128), lambda n: (n, 0))], tp // tr)
    return dh, jnp.sum(parts[::8, 0])


def _adam_step(rows, consts, nds, states, _):
    g8, w, m, v = rows
    g = g8[0].astype(f32)
    for d in range(1, N_DEV):
        g = g + g8[d].astype(f32)
    m2 = ADAM_B1 * m + (1.0 - ADAM_B1) * g
    v2 = ADAM_B2 * v + (1.0 - ADAM_B2) * jnp.square(g)
    m_hat = m2 / (1.0 - ADAM_B1 ** ADAM_STEP)
    v_hat = v2 / (1.0 - ADAM_B2 ** ADAM_STEP)
    delta = -ADAM_LR * (m_hat / (jnp.sqrt(v_hat) + ADAM_EPS) + ADAM_WD * w)
    return (g, delta, m2, v2), ()


def _adam_call(name, g8, w, m, v):
    r, wd = w.shape
    tb = r
    for cand in range(16, r + 1, 16):
        if r % cand == 0 and cand * wd <= 256 * 1024:
            tb = cand
    o = ((r, wd), f32, (tb, wd), lambda n: (n, 0))
    return _seq_fwd(name, _adam_step, [(g8, (N_DEV, tb, wd), lambda n: (0, n, 0)), _r(w, tb, wd), _r(m, tb, wd), _r(v, tb, wd)],
                    [], [], [], [o, o, o, o], r // tb)


WEIGHTS = ['meta_tokens', 'w_in', 'w_branch', 'w_out', 'norm_pre_mix', 'norm_post_mix', 'norm_pre_mlp', 'norm_post_mlp',
           'w_up', 'w_down', 'ret_gn_w', 's5_lam_re', 's5_lam_im', 's5_b_re', 's5_b_im', 's5_c_re', 's5_c_im', 's5_d',
           's5_log_step', 's5_glu_w', 's5_glu_b', 'ssd_conv_w', 'ssd_conv_b', 'ssd_dt_bias', 'ssd_a_log', 'ssd_d',
           'ssd_norm_w', 'hgrn_lb', 'hgrn_norm_w']
SHARDED = [("w_in", 2), ("w_branch", 3), ("w_out", 1), ("w_up", 2), ("w_down", 1), ("s5_glu_w", 2), ("meta_tokens", 1),
           ("ssd_conv_w", 2)]
N_BF16 = 6
REPL = [n for n in WEIGHTS if n not in dict(SHARDED)]
SMALL_PAD = 512
WIRE = bf16
SHARD_COLS = IN_DIM // N_DEV


def _col_pieces():
    out, pos = [], 0
    for a, b in _orig_col_slices() + [(3584, 3592)]:
        if a == 3584:
            pos = OFF["dt"]
        while a < b:
            e = min(b, (a // SHARD_COLS + 1) * SHARD_COLS)
            out.append((a, e, pos))
            pos += e - a
            a = e
    return out


def _w_in_from_shards(got_l):
    parts = [got_l[a // SHARD_COLS][:, a % SHARD_COLS:a % SHARD_COLS + (b - a)] for a, b, _ in _col_pieces()]
    parts.append(jnp.zeros((got_l.shape[1], NP - IN_DIM - (OFF["dt"] - 9728)), got_l.dtype))
    return jnp.concatenate(parts, axis=1)


def _w_in_to_shards(g):
    pieces = sorted(_col_pieces())
    blocks = []
    for d in range(N_DEV):
        blocks.append(jnp.concatenate([g[:, m:m + (b - a)] for a, b, m in pieces if a // SHARD_COLS == d], axis=1))
    return jnp.stack(blocks, axis=0)


def _to8(full, axis):
    sh = full.shape
    return jnp.moveaxis(full.reshape(sh[:axis] + (N_DEV, sh[axis] // N_DEV) + sh[axis + 1:]), axis, 0)


def _from8(g8, axis):
    r = jnp.moveaxis(g8, 0, axis)
    sh = r.shape
    return r.reshape(sh[:axis] + (sh[axis] * sh[axis + 1],) + sh[axis + 2:])


def _pack(arrs, pad_rows):
    flat = jnp.concatenate([a.reshape(-1) for a in arrs])
    n = flat.shape[0]
    total = -(-n // (128 * pad_rows)) * (128 * pad_rows)
    if total != n:
        flat = jnp.concatenate([flat, jnp.zeros((total - n,), flat.dtype)])
    return flat.reshape(total // 128, 128)


def _unpack(flat, shapes):
    v = flat.reshape(-1)
    out, pos = [], 0
    for s in shapes:
        n = int(np.prod(s))
        out.append(v[pos:pos + n].reshape(tuple(s)))
        pos += n
    return out


def _rows2d(a, lead=0):
    return a.reshape(a.shape[:lead] + (-1, a.shape[-1]))


def _local_step(x0, tgt0, wf, layer_w):
    seq = x0.shape[0]
    t = N_META + seq
    tp = -(-t // CHUNK) * CHUNK
    tabs = _tables(tp)
    lb_in = jnp.concatenate([wf["hgrn_lb"], jnp.zeros((8 - DEPTH, BW), f32)], axis=0)
    (lb_all,) = _seq_fwd("lb_prep", _lb_step, [_full(lb_in)], [], [tabs["lmat"]], [], [((8, BW), f32, (8, BW), lambda n: (0, 0))], 1)

    def pad128(a):
        return jnp.concatenate([a, jnp.zeros((128 - a.shape[0],), f32)]).reshape(1, 128)

    def layer_params(l):
        return dict(
            layer_w[l],
            npm=wf["norm_pre_mix"][l].reshape(1, D_MODEL), npostmix=wf["norm_post_mix"][l].reshape(1, D_MODEL),
            npremlp=wf["norm_pre_mlp"][l].reshape(1, D_MODEL), npostmlp=wf["norm_post_mlp"][l].reshape(1, D_MODEL),
            ret_gn=wf["ret_gn_w"][l].reshape(1, BW), lam_re=wf["s5_lam_re"][l], lam_im=wf["s5_lam_im"][l],
            lstep=wf["s5_log_step"][l].reshape(S5_G, 1),
            bt_re=wf["s5_b_re"][l].transpose(0, 2, 1).reshape(S5_G * S5_J, S5_P),
            bt_im=wf["s5_b_im"][l].transpose(0, 2, 1).reshape(S5_G * S5_J, S5_P),
            c_re=wf["s5_c_re"][l], c_im=wf["s5_c_im"][l], s5_d=wf["s5_d"][l].reshape(1, BW),
            glu_b=wf["s5_glu_b"][l].reshape(1, 2 * BW), conv_w=wf["ssd_conv_w"][l],
            conv_b=wf["ssd_conv_b"][l].reshape(1, 1024), dt_bias=pad128(wf["ssd_dt_bias"][l]),
            a_log=pad128(wf["ssd_a_log"][l]), ssd_d=pad128(wf["ssd_d"][l]), ssd_nw=wf["ssd_norm_w"][l].reshape(1, BW),
            lb=lb_all[l].reshape(1, BW), hg_nw=wf["hgrn_norm_w"][l].reshape(1, BW))

    params = [layer_params(l) for l in range(DEPTH)]
    zpad = jnp.zeros((tp - t, D_MODEL), f32)
    h = jnp.concatenate([wf["meta_tokens"], x0, zpad], axis=0)
    tgt = jnp.concatenate([jnp.zeros((N_META, D_MODEL), f32), tgt0, zpad], axis=0)
    saved = []
    for l in range(DEPTH):
        h, sv = _layer_fwd(h, params[l], tabs, "_l%d" % l)
        saved.append(sv)
    dh, loss_local = _loss_call(h, tgt, N_META, t)
    g = [None] * DEPTH
    for l in reversed(range(DEPTH)):
        dh, g[l] = _layer_bwd(dh, params[l], saved[l], tabs, "_l%d" % l)
    d_lb = jnp.concatenate([jnp.concatenate([gl["lb"] for gl in g], axis=0), jnp.zeros((8 - DEPTH, BW), f32)], axis=0)
    (d_hgrn_lb,), _ = _seq_bwd("lb_prep_bwd", _lb_step, [_full(lb_in)], [True], [], [tabs["lmat"]], [], [_full(d_lb)], 1)
    return loss_local, dh, g, d_hgrn_lb[:DEPTH]


def kernel(x, meta_tokens, w_in, w_branch, w_out, norm_pre_mix, norm_post_mix, norm_pre_mlp, norm_post_mlp, w_up, w_down, ret_gn_w, s5_lam_re, s5_lam_im, s5_b_re, s5_b_im, s5_c_re, s5_c_im, s5_d, s5_log_step, s5_glu_w, s5_glu_b, ssd_conv_w, ssd_conv_b, ssd_dt_bias, ssd_a_log, ssd_d, ssd_norm_w, hgrn_lb, hgrn_norm_w, loss_target, m_meta_tokens, m_w_in, m_w_branch, m_w_out, m_norm_pre_mix, m_norm_post_mix, m_norm_pre_mlp, m_norm_post_mlp, m_w_up, m_w_down, m_ret_gn_w, m_s5_lam_re, m_s5_lam_im, m_s5_b_re, m_s5_b_im, m_s5_c_re, m_s5_c_im, m_s5_d, m_s5_log_step, m_s5_glu_w, m_s5_glu_b, m_ssd_conv_w, m_ssd_conv_b, m_ssd_dt_bias, m_ssd_a_log, m_ssd_d, m_ssd_norm_w, m_hgrn_lb, m_hgrn_norm_w, v_meta_tokens, v_w_in, v_w_branch, v_w_out, v_norm_pre_mix, v_norm_post_mix, v_norm_pre_mlp, v_norm_post_mlp, v_w_up, v_w_down, v_ret_gn_w, v_s5_lam_re, v_s5_lam_im, v_s5_b_re, v_s5_b_im, v_s5_c_re, v_s5_c_im, v_s5_d, v_s5_log_step, v_s5_glu_w, v_s5_glu_b, v_ssd_conv_w, v_ssd_conv_b, v_ssd_dt_bias, v_ssd_a_log, v_ssd_d, v_ssd_norm_w, v_hgrn_lb, v_hgrn_norm_w):
    args = dict(locals())
    w = {n: args[n] for n in WEIGHTS}
    mom = {n: args["m_" + n] for n in WEIGHTS}
    var = {n: args["v_" + n] for n in WEIGHTS}
    names = [n for n, _ in SHARDED]
    axis = dict(SHARDED)

    got = dict(zip(names, _exchange("gather_weights", [w[n].astype(bf16) if i < N_BF16 else w[n]
                                                      for i, n in enumerate(names)], True)))
    layer_w = [dict(w_in=_w_in_from_shards(got["w_in"][:, l]), w_branch=_from8(got["w_branch"][:, l], 2),
                    w_out=_from8(got["w_out"][:, l], 0), w_up=_from8(got["w_up"][:, l], 1),
                    w_down=_from8(got["w_down"][:, l], 0), glu_w=_from8(got["s5_glu_w"][:, l], 1)) for l in range(DEPTH)]
    wf = {n: w[n] for n in REPL}
    wf["meta_tokens"] = _from8(got["meta_tokens"], 1)
    wf["ssd_conv_w"] = _from8(got["ssd_conv_w"], 2)

    loss_local, dh0, g, d_hgrn_lb = _local_step(x[0], loss_target[0], wf, layer_w)
    seq = x.shape[1]
    t = N_META + seq

    def stacked(key, ax, fn=None):
        return jnp.stack([(fn(gl[key]) if fn else _to8(gl[key], ax)).astype(WIRE) for gl in g], axis=1)

    send = dict(w_in=stacked("w_in", None, _w_in_to_shards), w_branch=stacked("w_branch", 2), w_out=stacked("w_out", 0),
                w_up=stacked("w_up", 1), w_down=stacked("w_down", 0), s5_glu_w=stacked("glu_w", 1),
                meta_tokens=_to8(dh0[:N_META], 1).astype(WIRE), ssd_conv_w=stacked("conv_w", 1))
    parts = dict(zip(names, _exchange("scatter_grads", [send[n] for n in names], False)))
    out = {k: {} for k in ("grad", "delta", "m", "v")}
    for n in names:
        res = _adam_call("adamw_" + n, _rows2d(parts[n], 1), _rows2d(w[n]), _rows2d(mom[n]), _rows2d(var[n]))
        for k, r in zip(("grad", "delta", "m", "v"), res):
            out[k][n] = r.reshape(w[n].shape)

    def stack_l(key, shape):
        return jnp.stack([gl[key].reshape(shape) for gl in g], axis=0)

    small = dict(
        norm_pre_mix=stack_l("npm", (D_MODEL,)), norm_post_mix=stack_l("npostmix", (D_MODEL,)),
        norm_pre_mlp=stack_l("npremlp", (D_MODEL,)), norm_post_mlp=stack_l("npostmlp", (D_MODEL,)),
        ret_gn_w=stack_l("ret_gn", (BW,)), s5_lam_re=stack_l("lam_re", (S5_G, S5_P)), s5_lam_im=stack_l("lam_im", (S5_G, S5_P)),
        s5_b_re=stack_l("bt_re", (S5_G, S5_J, S5_P)).transpose(0, 1, 3, 2),
        s5_b_im=stack_l("bt_im", (S5_G, S5_J, S5_P)).transpose(0, 1, 3, 2),
        s5_c_re=stack_l("c_re", (S5_G, S5_J, S5_P)), s5_c_im=stack_l("c_im", (S5_G, S5_J, S5_P)),
        s5_d=stack_l("s5_d", (BW,)), s5_log_step=stack_l("lstep", (S5_G,)), s5_glu_b=stack_l("glu_b", (2 * BW,)),
        ssd_conv_b=stack_l("conv_b", (1024,)), ssd_dt_bias=stack_l("dt_bias", (128,))[:, :SSD_HEADS],
        ssd_a_log=stack_l("a_log", (128,))[:, :SSD_HEADS], ssd_d=stack_l("ssd_d", (128,))[:, :SSD_HEADS],
        ssd_norm_w=stack_l("ssd_nw", (BW,)), hgrn_lb=d_hgrn_lb, hgrn_norm_w=stack_l("hg_nw", (BW,)))
    (parts_small,) = _exchange("gather_small_grads", [_pack([small[n] for n in REPL], SMALL_PAD)], True)
    res = _adam_call("adamw_replicated", parts_small, _pack([w[n] for n in REPL], SMALL_PAD),
                     _pack([mom[n] for n in REPL], SMALL_PAD), _pack([var[n] for n in REPL], SMALL_PAD))
    for k, r in zip(("grad", "delta", "m", "v"), res):
        out[k].update(zip(REPL, _unpack(r, [w[n].shape for n in REPL])))

    loss = lax.psum(loss_local, ("x", "y", "c"))
    return (loss, dh0[N_META:t][None], *[out["grad"][n] for n in WEIGHTS], *[out["delta"][n] for n in WEIGHTS],
            *[out["m"][n] for n in WEIGHTS], *[out["v"][n] for n in WEIGHTS])
```

```python
import numpy as np
import jax
import jax.numpy as jnp
from jax import lax
from jax.experimental import pallas as pl
from jax.experimental.pallas import tpu as pltpu

f32 = jnp.float32
bf16 = jnp.bfloat16
HI = lax.Precision.HIGHEST

D_MODEL = 1024
N_META = 16
DEPTH = 4
BW = 512
D_FF = 4096
EPS = 1e-6
N_DEV = 8
RET_HEADS = 4
SSD_HEADS = 8
SSD_GROUPS = 2
HG_HEADS = 4
S5_G, S5_J, S5_P = 32, 16, 64

ADAM_LR, ADAM_B1, ADAM_B2, ADAM_EPS, ADAM_WD, ADAM_STEP = 0.001, 0.9, 0.999, 1e-08, 0.01, 10

CHUNK = 64
HG_SUB = 16
VMEM_LIMIT = 56 * 1024 * 1024

OFF = dict(gates=0, rq=4096, rk=4352, rv=4608, rg=5120, s5u=5632, sxbc=6144, sz=7168, hq=7680, hf=8192,
           hi=8704, hg=9216, dt=9728)
NP = 9856
IN_DIM = 9736


def _orig_col_slices():
    sl = [(5640, 9736)]
    for base in (0, 256):
        for half in (0, 32):
            for h in range(4):
                sl.append((base + 64 * h + half, base + 64 * h + half + 32))
    sl.append((512, 1024))
    sl.append((1024, 1536))
    sl.append((1536, 2048))
    sl.append((2560, 3584))
    sl.append((2048, 2560))
    sl.append((3592, 5640))
    return sl


def _bdot(a, b):
    return jnp.dot(a.astype(bf16), b.astype(bf16), preferred_element_type=f32)


def _bdot_nt(a, b):
    return lax.dot_general(a.astype(bf16), b.astype(bf16), (((1,), (1,)), ((), ())), preferred_element_type=f32)


def _bdot_tn(a, b):
    return lax.dot_general(a.astype(bf16), b.astype(bf16), (((0,), (0,)), ((), ())), preferred_element_type=f32)


def _hdot(a, b):
    return jnp.dot(a, b, precision=HI, preferred_element_type=f32)


def _sig(x):
    return jax.nn.sigmoid(x)


def _rms(x, w):
    return x * lax.rsqrt(jnp.mean(x * x, axis=-1, keepdims=True) + EPS) * w


def _softplus(x):
    return jnp.maximum(x, 0.0) + jnp.log(1.0 + jnp.exp(-jnp.abs(x)))


def _r(arr, rb, w, jb=0):
    return (arr, (rb, w), lambda n, jb=jb: (n, jb))


def _full(arr):
    nd = arr.ndim
    return (arr, arr.shape, lambda n, nd=nd: (0,) * nd)


def _seq_fwd(name, step, rows, consts, nds, states, outs, n_chunks, save_states=False):
    nr, nc, nn, ns, no = len(rows), len(consts), len(nds), len(states), len(outs)
    whole = list(consts) + list(nds)

    def body(*refs):
        row_refs = refs[:nr]
        whole_hbm = refs[nr:nr + nc + nn]
        out_refs = refs[nr + nc + nn:nr + nc + nn + no]
        k = nr + nc + nn + no
        saved_refs = refs[k:k + (ns if save_states else 0)]
        k += ns if save_states else 0
        whole_vmem = refs[k:k + nc + nn]
        state_refs = refs[k + nc + nn:]
        n = pl.program_id(0)

        @pl.when(n == 0)
        def _():
            for src, dst in zip(whole_hbm, whole_vmem):
                pltpu.sync_copy(src, dst)
            for s in state_refs:
                s[...] = jnp.zeros_like(s)

        st = tuple(s[...] for s in state_refs)
        if save_states:
            for sv, v in zip(saved_refs, st):
                sv[0] = v
        o, new = step(tuple(r[...] for r in row_refs), tuple(c[...] for c in whole_vmem[:nc]),
                      tuple(c[...] for c in whole_vmem[nc:]), st, n)
        for ref, v in zip(out_refs, o):
            ref[...] = v.astype(ref.dtype)
        for s, v in zip(state_refs, new):
            s[...] = v

    in_specs = [pl.BlockSpec(bs, im) for _, bs, im in rows] + [pl.BlockSpec(memory_space=pl.ANY)] * (nc + nn)
    out_shape = [jax.ShapeDtypeStruct(s, d) for s, d, _, _ in outs]
    out_specs = [pl.BlockSpec(bs, im) for _, _, bs, im in outs]
    if save_states:
        for s in states:
            out_shape.append(jax.ShapeDtypeStruct((n_chunks,) + tuple(s), f32))
            out_specs.append(pl.BlockSpec((1,) + tuple(s), lambda n, z=len(s): (n,) + (0,) * z))
    scratch = [pltpu.VMEM(a.shape, a.dtype) for a in whole] + [pltpu.VMEM(tuple(s), f32) for s in states]
    res = pl.pallas_call(
        body, name=name, grid=(n_chunks,), in_specs=in_specs, out_specs=out_specs, out_shape=out_shape,
        scratch_shapes=scratch,
        compiler_params=pltpu.CompilerParams(dimension_semantics=("arbitrary",), vmem_limit_bytes=VMEM_LIMIT),
    )(*[a for a, _, _ in rows], *whole)
    return list(res)


def _seq_bwd(name, step, rows, row_diff, consts, nds, saved, couts, n_chunks):
    nr, nc, nn, ns, no = len(rows), len(consts), len(nds), len(saved), len(couts)
    whole = list(consts) + list(nds)
    didx = [i for i in range(nr) if row_diff[i]]

    def rev(im):
        return lambda n: im(n_chunks - 1 - n)

    def body(*refs):
        row_refs = refs[:nr]
        whole_hbm = refs[nr:nr + nc + nn]
        k = nr + nc + nn
        saved_refs = refs[k:k + ns]
        k += ns
        cout_refs = refs[k:k + no]
        k += no
        drow_refs = refs[k:k + len(didx)]
        k += len(didx)
        dconst_hbm = refs[k:k + nc]
        k += nc
        whole_vmem = refs[k:k + nc + nn]
        k += nc + nn
        dconst_acc = refs[k:k + nc]
        k += nc
        dstate_refs = refs[k:]
        n = pl.program_id(0)

        @pl.when(n == 0)
        def _():
            for src, dst in zip(whole_hbm, whole_vmem):
                pltpu.sync_copy(src, dst)
            for a in dconst_acc:
                a[...] = jnp.zeros_like(a)
            for s in dstate_refs:
                s[...] = jnp.zeros_like(s)

        rvals = tuple(r[...] for r in row_refs)
        cvals = tuple(c[...] for c in whole_vmem[:nc])
        nvals = tuple(c[...] for c in whole_vmem[nc:])
        svals = tuple(s[0] for s in saved_refs)
        cidx = n_chunks - 1 - n

        def f(dr, cv, sv):
            full = list(rvals)
            for i, v in zip(didx, dr):
                full[i] = v
            return step(tuple(full), cv, nvals, sv, cidx)

        (o, _), vf = jax.vjp(f, tuple(rvals[i] for i in didx), cvals, svals)
        ct_o = tuple(c[...].astype(v.dtype) for c, v in zip(cout_refs, o))
        ct_s = tuple(s[...] for s in dstate_refs)
        d_rows, d_consts, d_states = vf((ct_o, ct_s))
        for ref, v in zip(drow_refs, d_rows):
            ref[...] = v.astype(ref.dtype)
        for a, v in zip(dconst_acc, d_consts):
            a[...] += v.astype(f32)
        for s, v in zip(dstate_refs, d_states):
            s[...] = v

        @pl.when(n == n_chunks - 1)
        def _():
            for a, dst in zip(dconst_acc, dconst_hbm):
                pltpu.sync_copy(a, dst)

    in_specs = ([pl.BlockSpec(bs, rev(im)) for _, bs, im in rows]
                + [pl.BlockSpec(memory_space=pl.ANY)] * (nc + nn)
                + [pl.BlockSpec((1,) + a.shape[1:], lambda n, z=a.ndim - 1: (n_chunks - 1 - n,) + (0,) * z) for a in saved]
                + [pl.BlockSpec(bs, rev(im)) for _, bs, im in couts])
    out_shape, out_specs = [], []
    for i in didx:
        a, bs, im = rows[i]
        nrows = a.shape[0]
        out_shape.append(jax.ShapeDtypeStruct((nrows,) + tuple(bs[1:]), f32))
        out_specs.append(pl.BlockSpec(bs, (lambda im: lambda n: (im(n_chunks - 1 - n)[0],) + (0,) * (len(bs) - 1))(im)))
    for c in consts:
        out_shape.append(jax.ShapeDtypeStruct(c.shape, f32))
        out_specs.append(pl.BlockSpec(memory_space=pl.ANY))
    scratch = ([pltpu.VMEM(a.shape, a.dtype) for a in whole] + [pltpu.VMEM(c.shape, f32) for c in consts]
               + [pltpu.VMEM(a.shape[1:], f32) for a in saved])
    res = pl.pallas_call(
        body, name=name, grid=(n_chunks,), in_specs=in_specs, out_specs=out_specs, out_shape=out_shape,
        scratch_shapes=scratch,
        compiler_params=pltpu.CompilerParams(dimension_semantics=("arbitrary",), vmem_limit_bytes=VMEM_LIMIT),
    )(*[a for a, _, _ in rows], *whole, *saved, *[a for a, _, _ in couts])
    res = list(res)
    return res[:len(didx)], res[len(didx):]


def _mm(name, a, b, mode, tm, tn, tk, precision=None):
    if mode == "nn":
        (m, kd), nn_ = a.shape, b.shape[1]
        a_spec = pl.BlockSpec((tm, tk), lambda i, j, k: (i, k))
        b_spec = pl.BlockSpec((tk, tn), lambda i, j, k: (k, j))
        dims = (((1,), (0,)), ((), ()))
    elif mode == "tn":
        (kd, m), nn_ = a.shape, b.shape[1]
        a_spec = pl.BlockSpec((tk, tm), lambda i, j, k: (k, i))
        b_spec = pl.BlockSpec((tk, tn), lambda i, j, k: (k, j))
        dims = (((0,), (0,)), ((), ()))
    else:
        (m, kd), nn_ = a.shape, b.shape[0]
        a_spec = pl.BlockSpec((tm, tk), lambda i, j, k: (i, k))
        b_spec = pl.BlockSpec((tn, tk), lambda i, j, k: (j, k))
        dims = (((1,), (1,)), ((), ()))
    assert m % tm == 0 and nn_ % tn == 0 and kd % tk == 0, (name, a.shape, b.shape, tm, tn, tk)
    nk = kd // tk

    def body(a_ref, b_ref, o_ref, acc):
        k = pl.program_id(2)

        @pl.when(k == 0)
        def _():
            acc[...] = jnp.zeros_like(acc)

        if precision is None:
            acc[...] += lax.dot_general(a_ref[...].astype(bf16), b_ref[...].astype(bf16), dims, preferred_element_type=f32)
        else:
            acc[...] += lax.dot_general(a_ref[...], b_ref[...], dims, precision=precision, preferred_element_type=f32)

        @pl.when(k == nk - 1)
        def _():
            o_ref[...] = acc[...]

    return pl.pallas_call(
        body, name=name, grid=(m // tm, nn_ // tn, nk), in_specs=[a_spec, b_spec],
        out_specs=pl.BlockSpec((tm, tn), lambda i, j, k: (i, j)),
        out_shape=jax.ShapeDtypeStruct((m, nn_), f32), scratch_shapes=[pltpu.VMEM((tm, tn), f32)],
        compiler_params=pltpu.CompilerParams(dimension_semantics=("parallel", "parallel", "arbitrary"),
                                             vmem_limit_bytes=VMEM_LIMIT),
    )(a, b)


def _div_tile(n, want, mult):
    best = None
    for t in range(mult, min(n, want) + 1, mult):
        if n % t == 0:
            best = t
    return best if best is not None else n


def _exchange(name, srcs):
    n_arr = len(srcs)

    def body(*refs):
        src_refs, dst_refs = refs[:n_arr], refs[n_arr:2 * n_arr]
        send_sems, recv_sems, loc_sems = refs[2 * n_arr:]
        me = 4 * lax.axis_index("x") + 2 * lax.axis_index("y") + lax.axis_index("c")

        def copy(i, j, slot):
            return pltpu.make_async_remote_copy(
                src_ref=src_refs[i].at[j], dst_ref=dst_refs[i].at[slot], send_sem=send_sems.at[i, j], recv_sem=recv_sems.at[i, slot],
                device_id=(j // 4, (j // 2) % 2, j % 2), device_id_type=pl.DeviceIdType.MESH)

        local = [pltpu.make_async_copy(src_refs[i].at[me], dst_refs[i].at[me], loc_sems.at[i]) for i in range(n_arr)]
        for cp in local:
            cp.start()
        for j in range(N_DEV):
            @pl.when(j != me)
            def _(j=j):
                for i in range(n_arr):
                    copy(i, j, me).start()
        for j in range(N_DEV):
            @pl.when(j != me)
            def _(j=j):
                for i in range(n_arr):
                    copy(i, j, j).wait_recv()
                    copy(i, j, me).wait_send()
        for cp in local:
            cp.wait()

    hbm = pl.BlockSpec(memory_space=pltpu.HBM)
    return pl.pallas_call(
        body, name=name, in_specs=[hbm] * n_arr, out_specs=[hbm] * n_arr,
        out_shape=[jax.ShapeDtypeStruct(s.shape, s.dtype) for s in srcs],
        scratch_shapes=[pltpu.SemaphoreType.DMA((n_arr, N_DEV)), pltpu.SemaphoreType.DMA((n_arr, N_DEV)),
                        pltpu.SemaphoreType.DMA((n_arr,))],
    )(*srcs)


def _all_gather(name, srcs):
    n_arr = len(srcs)

    def body(*refs):
        src_refs, dst_refs = refs[:n_arr], refs[n_arr:2 * n_arr]
        send_sems, recv_sems, loc_sems = refs[2 * n_arr:]
        x, y, c = lax.axis_index("x"), lax.axis_index("y"), lax.axis_index("c")
        me, sibling = (x, y, c), (x, y, 1 - c)
        chips = [(1 - x, y), (x, 1 - y), (1 - x, 1 - y)]

        def slot(i, dev):
            return dst_refs[i].at[4 * dev[0] + 2 * dev[1] + dev[2]]

        def copy(i, k, block, to, src=None):
            return pltpu.make_async_remote_copy(
                src_ref=slot(i, block) if src is None else src, dst_ref=slot(i, block),
                send_sem=send_sems.at[i, k], recv_sem=recv_sems.at[i, k], device_id=to, device_id_type=pl.DeviceIdType.MESH)

        local = [pltpu.make_async_copy(src_refs[i], slot(i, me), loc_sems.at[i]) for i in range(n_arr)]
        for cp in local:
            cp.start()
        first = []
        for i in range(n_arr):
            first.append(copy(i, 0, me, sibling, src=src_refs[i]))
            first += [copy(i, 1 + j, me, (*chip, c), src=src_refs[i]) for j, chip in enumerate(chips)]
        for cp in first:
            cp.start()
        passed = []
        for j, chip in enumerate(chips):
            for i in range(n_arr):
                copy(i, 1 + j, (*chip, c), me).wait_recv()
                fwd = copy(i, 4 + j, (*chip, c), sibling)
                fwd.start()
                passed.append(fwd)
        for i in range(n_arr):
            copy(i, 0, sibling, me).wait_recv()
            for j, chip in enumerate(chips):
                copy(i, 4 + j, (*chip, 1 - c), me).wait_recv()
        for cp in first + passed:
            cp.wait_send()
        for cp in local:
            cp.wait()

    hbm = pl.BlockSpec(memory_space=pltpu.HBM)
    return pl.pallas_call(
        body, name=name, in_specs=[hbm] * n_arr, out_specs=[hbm] * n_arr,
        out_shape=[jax.ShapeDtypeStruct((N_DEV,) + tuple(s.shape), s.dtype) for s in srcs],
        scratch_shapes=[pltpu.SemaphoreType.DMA((n_arr, 7)), pltpu.SemaphoreType.DMA((n_arr, 7)),
                        pltpu.SemaphoreType.DMA((n_arr,))],
    )(*srcs)


def _ret_tables(c):
    gam = 1.0 - 2.0 ** (-5.0 - np.arange(RET_HEADS))
    lg = np.log(gam)
    t = np.arange(c)
    dmat = np.where(t[:, None] >= t[None, :], np.exp((t[:, None] - t[None, :])[None] * lg[:, None, None]), 0.0)
    head_v = np.arange(512) // 128
    ysc = np.exp((t[:, None] + 1) * lg[head_v][None, :])
    wtab = np.exp((c - 1 - t)[:, None] * lg[head_v][None, :])
    gtab = np.exp(c * lg[head_v])[None, :]
    head_k = (np.arange(256) % 128) // 32
    mask = (head_k[:, None] == head_v[None, :]).astype(np.float32)
    hm = (head_k[None, None, :] == np.arange(4)[:, None, None]).astype(np.float32)
    return [jnp.asarray(x, f32) for x in (dmat, ysc, wtab, gtab, mask, hm)]


def _rope_tables(tp):
    inv = 10000.0 ** (-np.arange(32, dtype=np.float32) / 32)
    ang = np.arange(tp, dtype=np.float32)[:, None] * inv[None, :]
    cos = np.tile(np.cos(ang), (1, 4)).astype(np.float32)
    sin = np.tile(np.sin(ang), (1, 4)).astype(np.float32)
    return jnp.asarray(cos), jnp.asarray(sin)


def _tri(c):
    t = np.arange(c)
    return jnp.asarray((t[:, None] >= t[None, :]).astype(np.float32))


def _ssd_tables(c):
    sh = np.zeros((3 * c, c + 8), np.float32)
    for d in (3, 2, 1):
        for t in range(c):
            sh[(3 - d) * c + t, 8 + t - d] = 1.0
    e = np.zeros((128, 512), np.float32)
    for h in range(SSD_HEADS):
        e[h, 64 * h:64 * h + 64] = 1.0
    mg = ((np.arange(256) // 128)[:, None] == (np.arange(512) // 256)[None, :]).astype(np.float32)
    cm4 = ((np.arange(256) // 64)[None, None, :] == np.arange(4)[:, None, None]).astype(np.float32)
    return [jnp.asarray(x) for x in (sh, e, mg, cm4)]


def _ret_step(rows, consts, nds, states, _):
    q, k, v, g, cos, sin = rows
    (gnw,) = consts
    dmat, ysc, wtab, gtab, mask, hm = nds
    (s,) = states
    q1, q2, k1, k2 = q[:, :128], q[:, 128:], k[:, :128], k[:, 128:]
    qr = jnp.concatenate([q1 * cos - q2 * sin, q1 * sin + q2 * cos], axis=1)
    kr = jnp.concatenate([k1 * cos - k2 * sin, k1 * sin + k2 * cos], axis=1) * 0.125
    y = _bdot(qr, s) * ysc
    parts = []
    for h in range(RET_HEADS):
        a = _bdot_nt(qr * hm[h], kr) * dmat[h]
        parts.append(_bdot(a, v[:, 128 * h:128 * h + 128]))
    y = y + jnp.concatenate(parts, axis=1)
    s_new = s * gtab + _bdot_tn(kr, v * wtab) * mask
    outs = []
    for h in range(RET_HEADS):
        yh = y[:, 128 * h:128 * h + 128]
        d = yh - jnp.mean(yh, axis=-1, keepdims=True)
        outs.append(d * lax.rsqrt(jnp.mean(d * d, axis=-1, keepdims=True) + EPS))
    yn = jnp.concatenate(outs, axis=1) * gnw
    return (g * _sig(g) * yn,), (s_new,)


def _ssd_step(rows, consts, nds, states, _):
    z, xbc, dt128 = rows
    cw, cb, dtb, alog, dsk, nw = consts
    tri, sh, e, mg, cm4 = nds
    s, tail = states
    c = xbc.shape[0]
    shifted = _hdot(sh, jnp.concatenate([tail, xbc], axis=0))
    xc = (cw[0:1] * shifted[0:c] + cw[1:2] * shifted[c:2 * c] + cw[2:3] * shifted[2 * c:3 * c] + cw[3:4] * xbc + cb)
    xc = xc * _sig(xc)
    xs, bm, cm = xc[:, :512], xc[:, 512:768], xc[:, 768:]
    dt = _softplus(dt128 + dtb)
    la = dt * (-jnp.exp(alog))
    cum = _hdot(tri, la)
    cum_t = cum.T
    dtx, cumx = _hdot(dt, e), _hdot(cum, e)
    lastx = cumx[c - 1:c]
    dskx = _hdot(jnp.broadcast_to(dsk, (8, 128)), e)[0:1]
    v = xs * dtx
    y = _bdot(cm, s) * jnp.exp(cumx)
    lane = lax.broadcasted_iota(jnp.int32, cum.shape, 1)
    sub = lax.broadcasted_iota(jnp.int32, cum_t.shape, 0)
    parts = []
    for grp in range(SSD_GROUPS):
        sg = _bdot_nt(cm[:, 128 * grp:128 * grp + 128], bm[:, 128 * grp:128 * grp + 128])
        vg = v[:, 256 * grp:256 * grp + 256]
        acc = jnp.zeros((c, 256), f32)
        for hh in range(4):
            h = 4 * grp + hh
            col = jnp.sum(jnp.where(lane == h, cum, 0.0), axis=1, keepdims=True)
            row = jnp.sum(jnp.where(sub == h, cum_t, 0.0), axis=0, keepdims=True)
            dec = jnp.exp(jnp.where(tri > 0.5, col - row, -1e30))
            acc = acc + _bdot(sg * dec, vg * cm4[hh])
        parts.append(acc)
    y = y + jnp.concatenate(parts, axis=1) + dskx * xs
    s_new = s * jnp.exp(lastx) + _bdot_tn(bm, v * jnp.exp(lastx - cumx)) * mg
    y = y * (z * _sig(z))
    outs = []
    for grp in range(SSD_GROUPS):
        yg = y[:, 256 * grp:256 * grp + 256]
        outs.append(yg * lax.rsqrt(jnp.mean(yg * yg, axis=-1, keepdims=True) + EPS))
    return (jnp.concatenate(outs, axis=1) * nw,), (s_new, xbc[c - 8:c])


def _hg_step(rows, consts, nds, states, _):
    hq, hf, hi, hgate = rows
    lb, nw = consts
    (tri,) = nds
    (st,) = states
    c = hq.shape[0]
    q = hq * _sig(hq)
    f = lb + (1.0 - lb) * _sig(hf)
    lf = jnp.log(f)
    k = 1.0 - f
    v = hi
    cum = _hdot(tri, lf)
    last = jnp.sum(lf, axis=0, keepdims=True)
    qd = q * jnp.exp(cum)
    kw = k * jnp.exp(last - cum)
    heads = [slice(128 * h, 128 * h + 128) for h in range(HG_HEADS)]
    y_rows = []
    rowid = lax.broadcasted_iota(jnp.int32, (HG_SUB, 512), 0)
    for i in range(c // HG_SUB):
        r0 = HG_SUB * i
        qi, ci, ki, vi = q[r0:r0 + HG_SUB], cum[r0:r0 + HG_SUB], k[r0:r0 + HG_SUB], v[r0:r0 + HG_SUB]
        yi = [jnp.zeros((HG_SUB, 128), f32) for _ in heads]
        for s_ in range(HG_SUB):
            e = qi * ki[s_:s_ + 1] * jnp.exp(jnp.where(rowid >= s_, ci - ci[s_:s_ + 1], -1e30))
            for h, sl in enumerate(heads):
                yi[h] = yi[h] + jnp.sum(e[:, sl], axis=1, keepdims=True) * vi[s_:s_ + 1, sl]
        if i > 0:
            b = cum[r0 - 1:r0]
            qs = qi * jnp.exp(ci - b)
            ks = k[:r0] * jnp.exp(b - cum[:r0])
            for h, sl in enumerate(heads):
                yi[h] = yi[h] + _bdot(_bdot_nt(qs[:, sl], ks[:, sl]), v[:r0, sl])
        y_rows.append(jnp.concatenate(yi, axis=1))
    y_in = jnp.concatenate(y_rows, axis=0)
    y = y_in + jnp.concatenate([_bdot_nt(qd[:, sl], st[:, sl]) for sl in heads], axis=1)
    st_new = jnp.concatenate([st[:, sl] * jnp.exp(last[:, sl]) + _bdot_tn(v[:, sl], kw[:, sl]) for sl in heads], axis=1)
    outs = []
    for sl in heads:
        yh = y[:, sl]
        outs.append(yh * lax.rsqrt(jnp.mean(yh * yh, axis=-1, keepdims=True) + EPS))
    o = jnp.concatenate(outs, axis=1) * nw
    return (o * (hgate * _sig(hgate)),), (st_new,)


def _s5post_step(rows, consts, nds, states, _):
    ycore, u = rows
    dsk, gw, gb = consts
    y = jax.nn.gelu(ycore + dsk * u)
    zz = _bdot(y, gw) + gb
    return (zz[:, :512] * _sig(zz[:, 512:]),), ()


def _s5prep_step(rows, consts, nds, states, _):
    lr, li, lstep, btr, bti = rows
    (rep,) = nds
    step = jnp.exp(lstep)
    mag = jnp.exp(lr * step)
    ab_re, ab_im = mag * jnp.cos(li * step), mag * jnp.sin(li * step)
    inv = 1.0 / (lr * lr + li * li)
    co_re = ((ab_re - 1.0) * lr + ab_im * li) * inv
    co_im = (ab_im * lr - (ab_re - 1.0) * li) * inv
    cre, cim = _hdot(rep, co_re), _hdot(rep, co_im)
    return (ab_re, ab_im, cre * btr - cim * bti, cre * bti + cim * btr), ()


def _lb_step(rows, consts, nds, states, _):
    (x,) = rows
    (lmat,) = nds
    valid = lax.broadcasted_iota(jnp.int32, x.shape, 0) < DEPTH
    xm = jnp.where(valid, x, -1e30)
    ex = jnp.where(valid, jnp.exp(xm - jnp.max(xm, axis=0, keepdims=True)), 0.0)
    sm = ex / jnp.sum(ex, axis=0, keepdims=True)
    return (_hdot(lmat, sm),), ()


def _rmsn_step(rows, consts, nds, states, _):
    (h,) = rows
    (w,) = consts
    return (_rms(h, w),), ()


def _merge_step(rows, consts, nds, states, _):
    yr, ys5, yssd, yhg, gates, h = rows
    wb, wout, npost = consts
    mixed = jnp.zeros(h.shape, f32)
    for n, yb in enumerate((yr, ys5, yssd, yhg)):
        mixed = mixed + _sig(gates[:, 1024 * n:1024 * n + 1024]) * _bdot(yb, wb[n])
    return (h + _rms(_bdot(mixed, wout), npost),), ()


FF_BLK = 512


def _row_blocks(tp):
    rb = _div_tile(tp, 528, 16)
    return rb, tp // rb


def _mlp_core_fwd(tag, u, w_up, w_down):
    tp = u.shape[0]
    rb, nrb = _row_blocks(tp)
    nff = D_FF // FF_BLK

    def body(u_ref, wup_ref, wdown_ref, m_ref):
        j = pl.program_id(0)

        def rows(i, carry):
            r = pl.multiple_of(i * rb, 16)
            a = jnp.dot(u_ref[pl.ds(r, rb), :], wup_ref[...], preferred_element_type=f32)
            part = _bdot(jnp.square(jnp.maximum(a, 0.0)), wdown_ref[...])

            @pl.when(j == 0)
            def _():
                m_ref[pl.ds(r, rb), :] = part

            @pl.when(j > 0)
            def _():
                m_ref[pl.ds(r, rb), :] += part
            return carry
        lax.fori_loop(0, nrb, rows, 0)

    return pl.pallas_call(
        body, name="mlp_core_fwd" + tag, grid=(nff,),
        in_specs=[pl.BlockSpec((tp, D_MODEL), lambda j: (0, 0)), pl.BlockSpec((D_MODEL, FF_BLK), lambda j: (0, j)),
                  pl.BlockSpec((FF_BLK, D_MODEL), lambda j: (j, 0))],
        out_specs=pl.BlockSpec((tp, D_MODEL), lambda j: (0, 0)), out_shape=jax.ShapeDtypeStruct((tp, D_MODEL), f32),
        compiler_params=pltpu.CompilerParams(dimension_semantics=("arbitrary",), vmem_limit_bytes=VMEM_LIMIT),
    )(u, w_up, w_down)


def _mlp_core_bwd(tag, u, dm, w_up, w_down):
    tp = u.shape[0]
    rb, nrb = _row_blocks(tp)
    nff = D_FF // FF_BLK

    def body(u_hbm, dm_hbm, wup_ref, wdown_ref, du_hbm, dwup_ref, dwdown_ref, u_s, dm_s, du_s):
        j = pl.program_id(0)

        @pl.when(j == 0)
        def _():
            pltpu.sync_copy(u_hbm, u_s)
            pltpu.sync_copy(dm_hbm, dm_s)

        def rows(i, carry):
            r = pl.multiple_of(i * rb, 16)
            ub = u_s[pl.ds(r, rb), :]
            dmb = dm_s[pl.ds(r, rb), :].astype(bf16)
            a = jnp.dot(ub, wup_ref[...], preferred_element_type=f32)
            ra = jnp.maximum(a, 0.0)
            da = (_bdot_nt(dmb, wdown_ref[...]) * (2.0 * ra)).astype(bf16)
            dwd = _bdot_tn(ra * ra, dmb)
            dwu = _bdot_tn(ub, da)
            dub = _bdot_nt(da, wup_ref[...])

            @pl.when(i == 0)
            def _():
                dwdown_ref[...] = dwd
                dwup_ref[...] = dwu

            @pl.when(i > 0)
            def _():
                dwdown_ref[...] += dwd
                dwup_ref[...] += dwu

            @pl.when(j == 0)
            def _():
                du_s[pl.ds(r, rb), :] = dub

            @pl.when(j > 0)
            def _():
                du_s[pl.ds(r, rb), :] += dub
            return carry
        lax.fori_loop(0, nrb, rows, 0)

        @pl.when(j == nff - 1)
        def _():
            pltpu.sync_copy(du_s, du_hbm)

    anyspec = pl.BlockSpec(memory_space=pl.ANY)
    return pl.pallas_call(
        body, name="mlp_core_bwd" + tag, grid=(nff,),
        in_specs=[anyspec, anyspec, pl.BlockSpec((D_MODEL, FF_BLK), lambda j: (0, j)),
                  pl.BlockSpec((FF_BLK, D_MODEL), lambda j: (j, 0))],
        out_specs=[anyspec, pl.BlockSpec((D_MODEL, FF_BLK), lambda j: (0, j)), pl.BlockSpec((FF_BLK, D_MODEL), lambda j: (j, 0))],
        out_shape=[jax.ShapeDtypeStruct((tp, D_MODEL), f32), jax.ShapeDtypeStruct((D_MODEL, D_FF), f32),
                   jax.ShapeDtypeStruct((D_FF, D_MODEL), f32)],
        scratch_shapes=[pltpu.VMEM((tp, D_MODEL), bf16), pltpu.VMEM((tp, D_MODEL), f32), pltpu.VMEM((tp, D_MODEL), f32)],
        compiler_params=pltpu.CompilerParams(dimension_semantics=("arbitrary",), vmem_limit_bytes=VMEM_LIMIT),
    )(u, dm, w_up, w_down)


def _resid_rms_step(rows, consts, nds, states, _):
    m, h = rows
    (w,) = consts
    return (h + _rms(m, w),), ()


SCAN_RB = 16


def _scan_inplace(xr, xi, ar0, ai0, tp, reverse):
    rb = SCAN_RB
    w = xr.shape[1]
    nblk = tp // rb
    ar, ai = ar0, ai0
    d = 1
    while d < tp:
        arb, aib = jnp.broadcast_to(ar, (rb, w)), jnp.broadcast_to(ai, (rb, w))

        def upd(r0, sr, si, arb=arb, aib=aib):
            cr, ci = xr[pl.ds(r0, rb), :], xi[pl.ds(r0, rb), :]
            xr[pl.ds(r0, rb), :] = cr + arb * sr - aib * si
            xi[pl.ds(r0, rb), :] = ci + arb * si + aib * sr

        rowid = lax.broadcasted_iota(jnp.int32, (rb, w), 0)
        if d < 8:
            if not reverse:
                def body(i, carry, d=d, upd=upd):
                    r0 = pl.multiple_of(tp - (i + 1) * rb, 8)
                    lo = pl.multiple_of(r0 - 8, 8)
                    sr = pltpu.roll(xr[pl.ds(lo, rb + 8), :], d, 0)[8:, :]
                    si = pltpu.roll(xi[pl.ds(lo, rb + 8), :], d, 0)[8:, :]
                    upd(r0, sr, si)
                    return carry
                lax.fori_loop(0, nblk - 1, body, 0)
                sr = jnp.where(rowid >= d, pltpu.roll(xr[pl.ds(0, rb), :], d, 0), 0.0)
                si = jnp.where(rowid >= d, pltpu.roll(xi[pl.ds(0, rb), :], d, 0), 0.0)
                upd(0, sr, si)
            else:
                def body(i, carry, d=d, upd=upd):
                    r0 = pl.multiple_of(i * rb, 8)
                    sr = pltpu.roll(xr[pl.ds(r0, rb + 8), :], rb + 8 - d, 0)[:rb, :]
                    si = pltpu.roll(xi[pl.ds(r0, rb + 8), :], rb + 8 - d, 0)[:rb, :]
                    upd(r0, sr, si)
                    return carry
                lax.fori_loop(0, nblk - 1, body, 0)
                sr = jnp.where(rowid < rb - d, pltpu.roll(xr[pl.ds(tp - rb, rb), :], rb - d, 0), 0.0)
                si = jnp.where(rowid < rb - d, pltpu.roll(xi[pl.ds(tp - rb, rb), :], rb - d, 0), 0.0)
                upd(tp - rb, sr, si)
        else:
            nfull = (tp - d) // rb
            rem = (tp - d) - nfull * rb
            if not reverse:
                def body(i, carry, d=d, upd=upd):
                    r0 = pl.multiple_of(tp - (i + 1) * rb, 8)
                    lo = pl.multiple_of(r0 - d, 8)
                    upd(r0, xr[pl.ds(lo, rb), :], xi[pl.ds(lo, rb), :])
                    return carry
                lax.fori_loop(0, nfull, body, 0)
                if rem:
                    cr, ci = xr[pl.ds(d, rem), :], xi[pl.ds(d, rem), :]
                    sr, si = xr[pl.ds(0, rem), :], xi[pl.ds(0, rem), :]
                    xr[pl.ds(d, rem), :] = cr + ar * sr - ai * si
                    xi[pl.ds(d, rem), :] = ci + ar * si + ai * sr
            else:
                def body(i, carry, d=d, upd=upd):
                    r0 = pl.multiple_of(i * rb, 8)
                    hi = pl.multiple_of(r0 + d, 8)
                    upd(r0, xr[pl.ds(hi, rb), :], xi[pl.ds(hi, rb), :])
                    return carry
                lax.fori_loop(0, nfull, body, 0)
                if rem:
                    lo = nfull * rb
                    cr, ci = xr[pl.ds(lo, rem), :], xi[pl.ds(lo, rem), :]
                    sr, si = xr[pl.ds(lo + d, rem), :], xi[pl.ds(lo + d, rem), :]
                    xr[pl.ds(lo, rem), :] = cr + ar * sr - ai * si
                    xi[pl.ds(lo, rem), :] = ci + ar * si + ai * sr
        ar, ai = ar * ar - ai * ai, 2.0 * ar * ai
        d *= 2


def _s5_blocks(tp):
    rbm = tp // 8 if (tp // 8) % 8 == 0 and tp % 8 == 0 else tp
    return rbm, tp // rbm


def _s5_in_specs(proj, tp):
    ucol = OFF["s5u"] // 128
    return [
        pl.BlockSpec((tp, 128), lambda cb: (0, ucol + cb)),
        pl.BlockSpec((128, 512), lambda cb: (cb, 0)),
        pl.BlockSpec((128, 512), lambda cb: (cb, 0)),
        pl.BlockSpec((1, 512), lambda cb: (0, cb)),
        pl.BlockSpec((1, 512), lambda cb: (0, cb)),
        pl.BlockSpec((512, 128), lambda cb: (cb, 0)),
        pl.BlockSpec((512, 128), lambda cb: (cb, 0)),
    ]


def _s5_core_fwd(tag, proj, bbr, bbi, ar, ai, ccr, cci):
    tp = proj.shape[0]
    rbm, nb = _s5_blocks(tp)

    def body(u_ref, bbr_ref, bbi_ref, ar_ref, ai_ref, ccr_ref, cci_ref, y_ref, xr, xi):
        def fill(i, carry):
            r = pl.multiple_of(i * rbm, 8)
            ub = u_ref[pl.ds(r, rbm), :]
            xr[pl.ds(r, rbm), :] = _bdot(ub, bbr_ref[...])
            xi[pl.ds(r, rbm), :] = _bdot(ub, bbi_ref[...])
            return carry
        lax.fori_loop(0, nb, fill, 0)
        _scan_inplace(xr, xi, ar_ref[...], ai_ref[...], tp, False)

        def out(i, carry):
            r = pl.multiple_of(i * rbm, 8)
            y_ref[pl.ds(r, rbm), :] = (_bdot(xr[pl.ds(r, rbm), :], ccr_ref[...]) - _bdot(xi[pl.ds(r, rbm), :], cci_ref[...]))
            return carry
        lax.fori_loop(0, nb, out, 0)

    return pl.pallas_call(
        body, name="s5_core_fwd" + tag, grid=(4,), in_specs=_s5_in_specs(proj, tp),
        out_specs=pl.BlockSpec((tp, 128), lambda cb: (0, cb)),
        out_shape=jax.ShapeDtypeStruct((tp, 512), f32),
        scratch_shapes=[pltpu.VMEM((tp, 512), f32), pltpu.VMEM((tp, 512), f32)],
        compiler_params=pltpu.CompilerParams(dimension_semantics=("arbitrary",), vmem_limit_bytes=VMEM_LIMIT),
    )(proj, bbr, bbi, ar, ai, ccr, cci)


def _s5_core_bwd(tag, proj, bbr, bbi, ar, ai, ccr, cci, dy, du_post):
    tp = proj.shape[0]
    rbm, nb = _s5_blocks(tp)
    rb = SCAN_RB

    def body(u_ref, bbr_ref, bbi_ref, ar_ref, ai_ref, ccr_ref, cci_ref, dy_ref, dup_ref,
             du_ref, dbbr_ref, dbbi_ref, dar_ref, dai_ref, dccr_ref, dcci_ref, xr, xi, gr, gi):
        dccr_ref[...] = jnp.zeros_like(dccr_ref)
        dcci_ref[...] = jnp.zeros_like(dcci_ref)
        dbbr_ref[...] = jnp.zeros_like(dbbr_ref)
        dbbi_ref[...] = jnp.zeros_like(dbbi_ref)

        def fill(i, carry):
            r = pl.multiple_of(i * rbm, 8)
            ub = u_ref[pl.ds(r, rbm), :]
            xr[pl.ds(r, rbm), :] = _bdot(ub, bbr_ref[...])
            xi[pl.ds(r, rbm), :] = _bdot(ub, bbi_ref[...])
            return carry
        lax.fori_loop(0, nb, fill, 0)
        _scan_inplace(xr, xi, ar_ref[...], ai_ref[...], tp, False)

        def seed(i, carry):
            r = pl.multiple_of(i * rbm, 8)
            dyb = dy_ref[pl.ds(r, rbm), :]
            dccr_ref[...] += _bdot_tn(xr[pl.ds(r, rbm), :], dyb)
            dcci_ref[...] -= _bdot_tn(xi[pl.ds(r, rbm), :], dyb)
            gr[pl.ds(r, rbm), :] = _bdot_nt(dyb, ccr_ref[...])
            gi[pl.ds(r, rbm), :] = -_bdot_nt(dyb, cci_ref[...])
            return carry
        lax.fori_loop(0, nb, seed, 0)
        _scan_inplace(gr, gi, ar_ref[...], -ai_ref[...], tp, True)

        w = xr.shape[1]
        rowid = lax.broadcasted_iota(jnp.int32, (rb, w), 0)

        def prods(pr, pi, r0):
            g_r, g_i = gr[pl.ds(r0, rb), :], gi[pl.ds(r0, rb), :]
            return pr * g_r + pi * g_i, pr * g_i - pi * g_r

        def dacc(i, carry):
            r0 = pl.multiple_of((i + 1) * rb, 8)
            lo = pl.multiple_of(r0 - 8, 8)
            pr = pltpu.roll(xr[pl.ds(lo, rb + 8), :], 1, 0)[8:, :]
            pi = pltpu.roll(xi[pl.ds(lo, rb + 8), :], 1, 0)[8:, :]
            a, b = prods(pr, pi, r0)
            return carry[0] + a, carry[1] + b
        pr0 = jnp.where(rowid >= 1, pltpu.roll(xr[pl.ds(0, rb), :], 1, 0), 0.0)
        pi0 = jnp.where(rowid >= 1, pltpu.roll(xi[pl.ds(0, rb), :], 1, 0), 0.0)
        acc_r, acc_i = lax.fori_loop(0, tp // rb - 1, dacc, prods(pr0, pi0, 0))
        dar_ref[...] = jnp.sum(acc_r, axis=0, keepdims=True)
        dai_ref[...] = jnp.sum(acc_i, axis=0, keepdims=True)

        def tail(i, carry):
            r = pl.multiple_of(i * rbm, 8)
            g_r, g_i = gr[pl.ds(r, rbm), :], gi[pl.ds(r, rbm), :]
            ub = u_ref[pl.ds(r, rbm), :]
            du_ref[pl.ds(r, rbm), :] = _bdot_nt(g_r, bbr_ref[...]) + _bdot_nt(g_i, bbi_ref[...]) + dup_ref[pl.ds(r, rbm), :]
            dbbr_ref[...] += _bdot_tn(ub, g_r)
            dbbi_ref[...] += _bdot_tn(ub, g_i)
            return carry
        lax.fori_loop(0, nb, tail, 0)

    col = lambda cb: (0, cb)
    blk = lambda cb: (cb, 0)
    return pl.pallas_call(
        body, name="s5_core_bwd" + tag, grid=(4,),
        in_specs=_s5_in_specs(proj, tp) + [pl.BlockSpec((tp, 128), col), pl.BlockSpec((tp, 128), col)],
        out_specs=[pl.BlockSpec((tp, 128), col), pl.BlockSpec((128, 512), blk), pl.BlockSpec((128, 512), blk),
                   pl.BlockSpec((1, 512), col), pl.BlockSpec((1, 512), col),
                   pl.BlockSpec((512, 128), blk), pl.BlockSpec((512, 128), blk)],
        out_shape=[jax.ShapeDtypeStruct((tp, 512), f32), jax.ShapeDtypeStruct((512, 512), f32),
                   jax.ShapeDtypeStruct((512, 512), f32), jax.ShapeDtypeStruct((1, 2048), f32),
                   jax.ShapeDtypeStruct((1, 2048), f32), jax.ShapeDtypeStruct((2048, 128), f32),
                   jax.ShapeDtypeStruct((2048, 128), f32)],
        scratch_shapes=[pltpu.VMEM((tp, 512), f32)] * 4,
        compiler_params=pltpu.CompilerParams(dimension_semantics=("arbitrary",), vmem_limit_bytes=VMEM_LIMIT),
    )(proj, bbr, bbi, ar, ai, ccr, cci, dy, du_post)


_EYE8 = np.eye(8, dtype=np.float32)


def _bb_dense(bb):
    return jnp.einsum("cgjp,gh->cgjhp", bb.reshape(4, 8, 16, 64), _EYE8).reshape(512, 512)


def _bb_diag(d):
    return jnp.einsum("cgjhp,gh->cgjp", d.reshape(4, 8, 16, 8, 64), _EYE8).reshape(512, 64)


def _cc_dense(cmat):
    return jnp.einsum("cgjp,gh->cgphj", cmat.reshape(4, 8, 16, 64), _EYE8).reshape(2048, 128)


def _cc_diag(d):
    return jnp.einsum("cgphj,gh->cgjp", d.reshape(4, 8, 64, 8, 16), _EYE8).reshape(32, 16, 64)


def _tables(tp):
    cos, sin = _rope_tables(tp)
    rep = np.zeros((512, 32), np.float32)
    rep[np.arange(512), np.arange(512) // 16] = 1.0
    lmat = np.zeros((8, 8), np.float32)
    for l in range(DEPTH):
        lmat[l, 1:l + 1] = 1.0
    return dict(cos=cos, sin=sin, ret=_ret_tables(CHUNK), tri=_tri(CHUNK), ssd=_ssd_tables(CHUNK),
                rep=jnp.asarray(rep), lmat=jnp.asarray(lmat))


def _tiles(tp):
    return dict(tr=_div_tile(tp, 352, 16), tmg=_div_tile(tp, 192, 16))


def _mixer_rows(proj, c):
    ret = [_r(proj, c, 256, OFF["rq"] // 256), _r(proj, c, 256, OFF["rk"] // 256), _r(proj, c, 512, OFF["rv"] // 512),
           _r(proj, c, 512, OFF["rg"] // 512)]
    ssd = [_r(proj, c, 512, OFF["sz"] // 512), _r(proj, c, 1024, OFF["sxbc"] // 1024), _r(proj, c, 128, OFF["dt"] // 128)]
    hg = [_r(proj, c, 512, OFF[k] // 512) for k in ("hq", "hf", "hi", "hg")]
    return ret, ssd, hg


def _out2(rows, w, rb, dtype=f32):
    return (rows, w), dtype, (rb, w), lambda n: (n, 0)


def _s5_prep_rows(p):
    return [_full(p["lam_re"]), _full(p["lam_im"]), _full(p["lstep"]), _full(p["bt_re"]), _full(p["bt_im"])]


def _s5_consts(p, tabs, tag):
    whole = lambda s: (s, f32, s, lambda n: (0, 0))
    ab_re, ab_im, bb_re, bb_im = _seq_fwd("s5_prep" + tag, _s5prep_step, _s5_prep_rows(p), [], [tabs["rep"]], [],
                                          [whole((32, 64)), whole((32, 64)), whole((512, 64)), whole((512, 64))], 1)
    return (_bb_dense(bb_re).astype(bf16), _bb_dense(bb_im).astype(bf16), ab_re.reshape(1, 2048), ab_im.reshape(1, 2048),
            _cc_dense(p["c_re"]).astype(bf16), _cc_dense(p["c_im"]).astype(bf16))


def _rmsn_step_b(rows, consts, nds, states, n):
    (o,), _ = _rmsn_step(rows, consts, nds, states, n)
    return (o, rows[0]), ()


def _layer_fwd(h, p, tabs, tag):
    tp = h.shape[0]
    c = CHUNK
    nch = tp // c
    tl = _tiles(tp)
    tr, tmg = tl["tr"], tl["tmg"]
    (u,) = _seq_fwd("rms_premix" + tag, _rmsn_step, [_r(h, tr, 1024)], [p["npm"]], [], [], [_out2(tp, 1024, tr)], tp // tr)
    proj = _mm("in_proj" + tag, u, p["w_in"], "nn", _div_tile(tp, 1056, 16), 1408, 1024)
    ret_rows, ssd_rows, hg_rows = _mixer_rows(proj, c)
    y_ret, ret_s = _seq_fwd("ret_fwd" + tag, _ret_step, ret_rows + [_r(tabs["cos"], c, 128), _r(tabs["sin"], c, 128)],
                            [p["ret_gn"]], tabs["ret"], [(256, 512)], [_out2(tp, 512, c)], nch, save_states=True)
    ycore = _s5_core_fwd(tag, proj, *_s5_consts(p, tabs, tag))
    (y_s5,) = _seq_fwd("s5_post" + tag, _s5post_step, [_r(ycore, tr, 512), _r(proj, tr, 512, OFF["s5u"] // 512)],
                       [p["s5_d"], p["glu_w"], p["glu_b"]], [], [], [_out2(tp, 512, tr)], tp // tr)
    y_ssd, ssd_s, ssd_tail = _seq_fwd(
        "ssd_fwd" + tag, _ssd_step, ssd_rows, [p["conv_w"], p["conv_b"], p["dt_bias"], p["a_log"], p["ssd_d"], p["ssd_nw"]],
        [tabs["tri"]] + tabs["ssd"], [(256, 512), (8, 1024)], [_out2(tp, 512, c)], nch, save_states=True)
    y_hg, hg_s = _seq_fwd("hg_fwd" + tag, _hg_step, hg_rows, [p["lb"], p["hg_nw"]], [tabs["tri"]], [(128, 512)],
                          [_out2(tp, 512, c)], nch, save_states=True)
    (h_mid,) = _seq_fwd(
        "merge_fwd" + tag, _merge_step,
        [_r(y_ret, tmg, 512), _r(y_s5, tmg, 512), _r(y_ssd, tmg, 512), _r(y_hg, tmg, 512), _r(proj, tmg, 4096, 0),
         _r(h, tmg, 1024)],
        [p["w_branch"], p["w_out"], p["npostmix"]], [], [], [_out2(tp, 1024, tmg)], tp // tmg)
    (u2,) = _seq_fwd("rms_premlp" + tag, _rmsn_step, [_r(h_mid, tr, 1024)], [p["npremlp"]], [], [],
                     [_out2(tp, 1024, tr, bf16)], tp // tr)
    m = _mlp_core_fwd(tag, u2, p["w_up"], p["w_down"])
    (h_new,) = _seq_fwd("mlp_post" + tag, _resid_rms_step, [_r(m, tr, 1024), _r(h_mid, tr, 1024)], [p["npostmlp"]], [], [],
                        [_out2(tp, 1024, tr)], tp // tr)
    saved = dict(h=h, u=u, proj=proj, ret_s=ret_s, ycore=ycore, ssd_s=ssd_s, ssd_tail=ssd_tail, hg_s=hg_s,
                 y_ret=y_ret, y_s5=y_s5, y_ssd=y_ssd, y_hg=y_hg, h_mid=h_mid, u2=u2, m=m)
    return h_new, saved


def _layer_bwd(dh, p, sv, tabs, tag):
    tp = dh.shape[0]
    c = CHUNK
    nch = tp // c
    tl = _tiles(tp)
    tr, tmg = tl["tr"], tl["tmg"]
    proj = sv["proj"]
    g = {}
    (d_m, d_hmid), (g["npostmlp"],) = _seq_bwd(
        "mlp_post_bwd" + tag, _resid_rms_step, [_r(sv["m"], tr, 1024), _r(sv["h_mid"], tr, 1024)], [True, True],
        [p["npostmlp"]], [], [], [_r(dh, tr, 1024)], tp // tr)
    d_u2, g["w_up"], g["w_down"] = _mlp_core_bwd(tag, sv["u2"], d_m, p["w_up"], p["w_down"])
    (d_hmid,), (g["npremlp"],) = _seq_bwd(
        "rms_premlp_bwd" + tag, _rmsn_step_b, [_r(sv["h_mid"], tr, 1024)], [True], [p["npremlp"]], [], [],
        [_r(d_u2, tr, 1024), _r(d_hmid, tr, 1024)], tp // tr)
    (dy_ret, dy_s5, dy_ssd, dy_hg, d_gates, d_h1), (g["w_branch"], g["w_out"], g["npostmix"]) = _seq_bwd(
        "merge_bwd" + tag, _merge_step,
        [_r(sv["y_ret"], tmg, 512), _r(sv["y_s5"], tmg, 512), _r(sv["y_ssd"], tmg, 512), _r(sv["y_hg"], tmg, 512),
         _r(proj, tmg, 4096, 0), _r(sv["h"], tmg, 1024)], [True] * 6,
        [p["w_branch"], p["w_out"], p["npostmix"]], [], [], [_r(d_hmid, tmg, 1024)], tp // tmg)
    ret_rows, ssd_rows, hg_rows = _mixer_rows(proj, c)
    (d_hq, d_hf, d_hi, d_hg), (g["lb"], g["hg_nw"]) = _seq_bwd(
        "hg_bwd" + tag, _hg_step, hg_rows, [True] * 4, [p["lb"], p["hg_nw"]], [tabs["tri"]], [sv["hg_s"]],
        [_r(dy_hg, c, 512)], nch)
    (d_z, d_xbc, d_dt), (g["conv_w"], g["conv_b"], g["dt_bias"], g["a_log"], g["ssd_d"], g["ssd_nw"]) = _seq_bwd(
        "ssd_bwd" + tag, _ssd_step, ssd_rows, [True] * 3,
        [p["conv_w"], p["conv_b"], p["dt_bias"], p["a_log"], p["ssd_d"], p["ssd_nw"]], [tabs["tri"]] + tabs["ssd"],
        [sv["ssd_s"], sv["ssd_tail"]], [_r(dy_ssd, c, 512)], nch)
    (d_ycore, du_post), (g["s5_d"], g["glu_w"], g["glu_b"]) = _seq_bwd(
        "s5_post_bwd" + tag, _s5post_step, [_r(sv["ycore"], tr, 512), _r(proj, tr, 512, OFF["s5u"] // 512)], [True, True],
        [p["s5_d"], p["glu_w"], p["glu_b"]], [], [], [_r(dy_s5, tr, 512)], tp // tr)
    du_s5, dbbr, dbbi, dar, dai, dccr, dcci = _s5_core_bwd(tag, proj, *_s5_consts(p, tabs, tag + "b"), d_ycore, du_post)
    g["c_re"], g["c_im"] = _cc_diag(dccr), _cc_diag(dcci)
    (g["lam_re"], g["lam_im"], g["lstep"], g["bt_re"], g["bt_im"]), _ = _seq_bwd(
        "s5_prep_bwd" + tag, _s5prep_step, _s5_prep_rows(p), [True] * 5, [], [tabs["rep"]], [],
        [_full(dar.reshape(32, 64)), _full(dai.reshape(32, 64)), _full(_bb_diag(dbbr)), _full(_bb_diag(dbbi))], 1)
    (d_q, d_k, d_v, d_g), (g["ret_gn"],) = _seq_bwd(
        "ret_bwd" + tag, _ret_step, ret_rows + [_r(tabs["cos"], c, 128), _r(tabs["sin"], c, 128)], [True] * 4 + [False] * 2,
        [p["ret_gn"]], tabs["ret"], [sv["ret_s"]], [_r(dy_ret, c, 512)], nch)
    dproj = jnp.concatenate([d_gates, d_q, d_k, d_v, d_g, du_s5, d_xbc, d_z, d_hq, d_hf, d_hi, d_hg, d_dt], axis=1)
    g["w_in"] = _mm("in_proj_dw" + tag, sv["u"], dproj, "tn", 512, 1408, _div_tile(tp, 704, 16))
    du = _mm("in_proj_dx" + tag, dproj, p["w_in"], "nt", _div_tile(tp, 1056, 16), 1024, 1408)
    (dh_prev,), (g["npm"],) = _seq_bwd("rms_premix_bwd" + tag, _rmsn_step_b, [_r(sv["h"], tr, 1024)], [True], [p["npm"]], [], [],
                                       [_r(du, tr, 1024), _r(d_h1, tr, 1024)], tp // tr)
    return dh_prev, g


def _loss_call(h, tgt, lo, hi):
    tp = h.shape[0]
    tr = _div_tile(tp, 352, 16)

    def step(rows, consts, nds, states, n):
        hh, tt = rows
        row = n * tr + lax.broadcasted_iota(jnp.int32, hh.shape, 0)
        err = jnp.where((row >= lo) & (row < hi), hh - tt, 0.0)
        part = 0.5 * jnp.sum(err * err) * (1.0 / D_MODEL)
        return (err * (1.0 / D_MODEL), jnp.zeros((8, 128), f32) + part), ()

    dh, parts = _seq_fwd("loss_head", step, [_r(h, tr, 1024), _r(tgt, tr, 1024)], [], [], [],
                         [_out2(tp, 1024, tr), ((8 * (tp // tr), 128), f32, (8, 128), lambda n: (n, 0))], tp // tr)
    return dh, jnp.sum(parts[::8, 0])


def _adam_step(rows, consts, nds, states, _):
    g8, w, m, v = rows
    g = g8[0].astype(f32)
    for d in range(1, N_DEV):
        g = g + g8[d].astype(f32)
    m2 = ADAM_B1 * m + (1.0 - ADAM_B1) * g
    v2 = ADAM_B2 * v + (1.0 - ADAM_B2) * jnp.square(g)
    m_hat = m2 / (1.0 - ADAM_B1 ** ADAM_STEP)
    v_hat = v2 / (1.0 - ADAM_B2 ** ADAM_STEP)
    delta = -ADAM_LR * (m_hat / (jnp.sqrt(v_hat) + ADAM_EPS) + ADAM_WD * w)
    return (g, delta, m2, v2), ()


def _adam_call(name, g8, w, m, v):
    r, wd = w.shape
    tb = r
    for cand in range(16, r + 1, 16):
        if r % cand == 0 and cand * wd <= 256 * 1024:
            tb = cand
    o = ((r, wd), f32, (tb, wd), lambda n: (n, 0))
    return _seq_fwd(name, _adam_step, [(g8, (N_DEV, tb, wd), lambda n: (0, n, 0)), _r(w, tb, wd), _r(m, tb, wd), _r(v, tb, wd)],
                    [], [], [], [o, o, o, o], r // tb)


WEIGHTS = ['meta_tokens', 'w_in', 'w_branch', 'w_out', 'norm_pre_mix', 'norm_post_mix', 'norm_pre_mlp', 'norm_post_mlp',
           'w_up', 'w_down', 'ret_gn_w', 's5_lam_re', 's5_lam_im', 's5_b_re', 's5_b_im', 's5_c_re', 's5_c_im', 's5_d',
           's5_log_step', 's5_glu_w', 's5_glu_b', 'ssd_conv_w', 'ssd_conv_b', 'ssd_dt_bias', 'ssd_a_log', 'ssd_d',
           'ssd_norm_w', 'hgrn_lb', 'hgrn_norm_w']
SHARDED = [("w_in", 2), ("w_branch", 3), ("w_out", 1), ("w_up", 2), ("w_down", 1), ("s5_glu_w", 2), ("meta_tokens", 1),
           ("ssd_conv_w", 2)]
N_BF16 = 6
REPL = [n for n in WEIGHTS if n not in dict(SHARDED)]
SMALL_PAD = 512
WIRE = bf16
SHARD_COLS = IN_DIM // N_DEV


def _col_pieces():
    out, pos = [], 0
    for a, b in _orig_col_slices() + [(3584, 3592)]:
        if a == 3584:
            pos = OFF["dt"]
        while a < b:
            e = min(b, (a // SHARD_COLS + 1) * SHARD_COLS)
            out.append((a, e, pos))
            pos += e - a
            a = e
    return out


def _w_in_from_shards(got_l):
    parts = [got_l[a // SHARD_COLS][:, a % SHARD_COLS:a % SHARD_COLS + (b - a)] for a, b, _ in _col_pieces()]
    parts.append(jnp.zeros((got_l.shape[1], NP - IN_DIM - (OFF["dt"] - 9728)), got_l.dtype))
    return jnp.concatenate(parts, axis=1)


def _w_in_to_shards(g):
    pieces = sorted(_col_pieces())
    blocks = []
    for d in range(N_DEV):
        blocks.append(jnp.concatenate([g[:, m:m + (b - a)] for a, b, m in pieces if a // SHARD_COLS == d], axis=1))
    return jnp.stack(blocks, axis=0)


def _to8(full, axis):
    sh = full.shape
    return jnp.moveaxis(full.reshape(sh[:axis] + (N_DEV, sh[axis] // N_DEV) + sh[axis + 1:]), axis, 0)


def _from8(g8, axis):
    r = jnp.moveaxis(g8, 0, axis)
    sh = r.shape
    return r.reshape(sh[:axis] + (sh[axis] * sh[axis + 1],) + sh[axis + 2:])


def _pack(arrs, pad_rows):
    flat = jnp.concatenate([a.reshape(-1) for a in arrs])
    n = flat.shape[0]
    total = -(-n // (128 * pad_rows)) * (128 * pad_rows)
    if total != n:
        flat = jnp.concatenate([flat, jnp.zeros((total - n,), flat.dtype)])
    return flat.reshape(total // 128, 128)


def _unpack(flat, shapes):
    v = flat.reshape(-1)
    out, pos = [], 0
    for s in shapes:
        n = int(np.prod(s))
        out.append(v[pos:pos + n].reshape(tuple(s)))
        pos += n
    return out


def _rows2d(a, lead=0):
    return a.reshape(a.shape[:lead] + (-1, a.shape[-1]))


def _local_step(x0, tgt0, wf, layer_w):
    seq = x0.shape[0]
    t = N_META + seq
    tp = -(-t // CHUNK) * CHUNK
    tabs = _tables(tp)
    lb_in = jnp.concatenate([wf["hgrn_lb"], jnp.zeros((8 - DEPTH, BW), f32)], axis=0)
    (lb_all,) = _seq_fwd("lb_prep", _lb_step, [_full(lb_in)], [], [tabs["lmat"]], [], [((8, BW), f32, (8, BW), lambda n: (0, 0))], 1)

    def pad128(a):
        return jnp.concatenate([a, jnp.zeros((128 - a.shape[0],), f32)]).reshape(1, 128)

    def layer_params(l):
        return dict(
            layer_w[l],
            npm=wf["norm_pre_mix"][l].reshape(1, D_MODEL), npostmix=wf["norm_post_mix"][l].reshape(1, D_MODEL),
            npremlp=wf["norm_pre_mlp"][l].reshape(1, D_MODEL), npostmlp=wf["norm_post_mlp"][l].reshape(1, D_MODEL),
            ret_gn=wf["ret_gn_w"][l].reshape(1, BW), lam_re=wf["s5_lam_re"][l], lam_im=wf["s5_lam_im"][l],
            lstep=wf["s5_log_step"][l].reshape(S5_G, 1),
            bt_re=wf["s5_b_re"][l].transpose(0, 2, 1).reshape(S5_G * S5_J, S5_P),
            bt_im=wf["s5_b_im"][l].transpose(0, 2, 1).reshape(S5_G * S5_J, S5_P),
            c_re=wf["s5_c_re"][l], c_im=wf["s5_c_im"][l], s5_d=wf["s5_d"][l].reshape(1, BW),
            glu_b=wf["s5_glu_b"][l].reshape(1, 2 * BW), conv_w=wf["ssd_conv_w"][l],
            conv_b=wf["ssd_conv_b"][l].reshape(1, 1024), dt_bias=pad128(wf["ssd_dt_bias"][l]),
            a_log=pad128(wf["ssd_a_log"][l]), ssd_d=pad128(wf["ssd_d"][l]), ssd_nw=wf["ssd_norm_w"][l].reshape(1, BW),
            lb=lb_all[l].reshape(1, BW), hg_nw=wf["hgrn_norm_w"][l].reshape(1, BW))

    params = [layer_params(l) for l in range(DEPTH)]
    zpad = jnp.zeros((tp - t, D_MODEL), f32)
    h = jnp.concatenate([wf["meta_tokens"], x0, zpad], axis=0)
    tgt = jnp.concatenate([jnp.zeros((N_META, D_MODEL), f32), tgt0, zpad], axis=0)
    saved = []
    for l in range(DEPTH):
        h, sv = _layer_fwd(h, params[l], tabs, "_l%d" % l)
        saved.append(sv)
    dh, loss_local = _loss_call(h, tgt, N_META, t)
    g = [None] * DEPTH
    for l in reversed(range(DEPTH)):
        dh, g[l] = _layer_bwd(dh, params[l], saved[l], tabs, "_l%d" % l)
    d_lb = jnp.concatenate([jnp.concatenate([gl["lb"] for gl in g], axis=0), jnp.zeros((8 - DEPTH, BW), f32)], axis=0)
    (d_hgrn_lb,), _ = _seq_bwd("lb_prep_bwd", _lb_step, [_full(lb_in)], [True], [], [tabs["lmat"]], [], [_full(d_lb)], 1)
    return loss_local, dh, g, d_hgrn_lb[:DEPTH]


def kernel(x, meta_tokens, w_in, w_branch, w_out, norm_pre_mix, norm_post_mix, norm_pre_mlp, norm_post_mlp, w_up, w_down, ret_gn_w, s5_lam_re, s5_lam_im, s5_b_re, s5_b_im, s5_c_re, s5_c_im, s5_d, s5_log_step, s5_glu_w, s5_glu_b, ssd_conv_w, ssd_conv_b, ssd_dt_bias, ssd_a_log, ssd_d, ssd_norm_w, hgrn_lb, hgrn_norm_w, loss_target, m_meta_tokens, m_w_in, m_w_branch, m_w_out, m_norm_pre_mix, m_norm_post_mix, m_norm_pre_mlp, m_norm_post_mlp, m_w_up, m_w_down, m_ret_gn_w, m_s5_lam_re, m_s5_lam_im, m_s5_b_re, m_s5_b_im, m_s5_c_re, m_s5_c_im, m_s5_d, m_s5_log_step, m_s5_glu_w, m_s5_glu_b, m_ssd_conv_w, m_ssd_conv_b, m_ssd_dt_bias, m_ssd_a_log, m_ssd_d, m_ssd_norm_w, m_hgrn_lb, m_hgrn_norm_w, v_meta_tokens, v_w_in, v_w_branch, v_w_out, v_norm_pre_mix, v_norm_post_mix, v_norm_pre_mlp, v_norm_post_mlp, v_w_up, v_w_down, v_ret_gn_w, v_s5_lam_re, v_s5_lam_im, v_s5_b_re, v_s5_b_im, v_s5_c_re, v_s5_c_im, v_s5_d, v_s5_log_step, v_s5_glu_w, v_s5_glu_b, v_ssd_conv_w, v_ssd_conv_b, v_ssd_dt_bias, v_ssd_a_log, v_ssd_d, v_ssd_norm_w, v_hgrn_lb, v_hgrn_norm_w):
    args = dict(locals())
    w = {n: args[n] for n in WEIGHTS}
    mom = {n: args["m_" + n] for n in WEIGHTS}
    var = {n: args["v_" + n] for n in WEIGHTS}
    names = [n for n, _ in SHARDED]

    got = dict(zip(names, _all_gather("gather_weights", [w[n].astype(bf16) if i < N_BF16 else w[n] for i, n in enumerate(names)])))
    layer_w = [dict(w_in=_w_in_from_shards(got["w_in"][:, l]), w_branch=_from8(got["w_branch"][:, l], 2),
                    w_out=_from8(got["w_out"][:, l], 0), w_up=_from8(got["w_up"][:, l], 1),
                    w_down=_from8(got["w_down"][:, l], 0), glu_w=_from8(got["s5_glu_w"][:, l], 1)) for l in range(DEPTH)]
    wf = {n: w[n] for n in REPL}
    wf["meta_tokens"] = _from8(got["meta_tokens"], 1)
    wf["ssd_conv_w"] = _from8(got["ssd_conv_w"], 2)

    loss_local, dh0, g, d_hgrn_lb = _local_step(x[0], loss_target[0], wf, layer_w)
    seq = x.shape[1]
    t = N_META + seq

    def stacked(key, ax, fn=None):
        return jnp.stack([(fn(gl[key]) if fn else _to8(gl[key], ax)).astype(WIRE) for gl in g], axis=1)

    send = dict(w_in=stacked("w_in", None, _w_in_to_shards), w_branch=stacked("w_branch", 2), w_out=stacked("w_out", 0),
                w_up=stacked("w_up", 1), w_down=stacked("w_down", 0), s5_glu_w=stacked("glu_w", 1),
                meta_tokens=_to8(dh0[:N_META], 1).astype(WIRE), ssd_conv_w=stacked("conv_w", 1))
    parts = dict(zip(names, _exchange("scatter_grads", [send[n] for n in names])))
    out = {k: {} for k in ("grad", "delta", "m", "v")}
    for n in names:
        res = _adam_call("adamw_" + n, _rows2d(parts[n], 1), _rows2d(w[n]), _rows2d(mom[n]), _rows2d(var[n]))
        for k, r in zip(("grad", "delta", "m", "v"), res):
            out[k][n] = r.reshape(w[n].shape)

    def stack_l(key, shape):
        return jnp.stack([gl[key].reshape(shape) for gl in g], axis=0)

    small = dict(
        norm_pre_mix=stack_l("npm", (D_MODEL,)), norm_post_mix=stack_l("npostmix", (D_MODEL,)),
        norm_pre_mlp=stack_l("npremlp", (D_MODEL,)), norm_post_mlp=stack_l("npostmlp", (D_MODEL,)),
        ret_gn_w=stack_l("ret_gn", (BW,)), s5_lam_re=stack_l("lam_re", (S5_G, S5_P)), s5_lam_im=stack_l("lam_im", (S5_G, S5_P)),
        s5_b_re=stack_l("bt_re", (S5_G, S5_J, S5_P)).transpose(0, 1, 3, 2),
        s5_b_im=stack_l("bt_im", (S5_G, S5_J, S5_P)).transpose(0, 1, 3, 2),
        s5_c_re=stack_l("c_re", (S5_G, S5_J, S5_P)), s5_c_im=stack_l("c_im", (S5_G, S5_J, S5_P)),
        s5_d=stack_l("s5_d", (BW,)), s5_log_step=stack_l("lstep", (S5_G,)), s5_glu_b=stack_l("glu_b", (2 * BW,)),
        ssd_conv_b=stack_l("conv_b", (1024,)), ssd_dt_bias=stack_l("dt_bias", (128,))[:, :SSD_HEADS],
        ssd_a_log=stack_l("a_log", (128,))[:, :SSD_HEADS], ssd_d=stack_l("ssd_d", (128,))[:, :SSD_HEADS],
        ssd_norm_w=stack_l("ssd_nw", (BW,)), hgrn_lb=d_hgrn_lb, hgrn_norm_w=stack_l("hg_nw", (BW,)))
    (parts_small,) = _all_gather("gather_small_grads", [_pack([small[n] for n in REPL], SMALL_PAD)])
    res = _adam_call("adamw_replicated", parts_small, _pack([w[n] for n in REPL], SMALL_PAD),
                     _pack([mom[n] for n in REPL], SMALL_PAD), _pack([var[n] for n in REPL], SMALL_PAD))
    for k, r in zip(("grad", "delta", "m", "v"), res):
        out[k].update(zip(REPL, _unpack(r, [w[n].shape for n in REPL])))

    loss = lax.psum(loss_local, ("x", "y", "c"))
    return (loss, dh0[N_META:t][None], *[out["grad"][n] for n in WEIGHTS], *[out["delta"][n] for n in WEIGHTS],
            *[out["m"][n] for n in WEIGHTS], *[out["v"][n] for n in WEIGHTS])
```

```python
import numpy as np
import jax
import jax.numpy as jnp
from jax import lax
from jax.experimental import pallas as pl
from jax.experimental.pallas import tpu as pltpu

f32 = jnp.float32
bf16 = jnp.bfloat16
HI = lax.Precision.HIGHEST

D_MODEL = 1024
N_META = 16
DEPTH = 4
BW = 512
D_FF = 4096
EPS = 1e-6
N_DEV = 8
RET_HEADS = 4
SSD_HEADS = 8
SSD_GROUPS = 2
HG_HEADS = 4
S5_G, S5_J, S5_P = 32, 16, 64

ADAM_LR, ADAM_B1, ADAM_B2, ADAM_EPS, ADAM_WD, ADAM_STEP = 0.001, 0.9, 0.999, 1e-08, 0.01, 10

CHUNK = 64
HG_SUB = 16
VMEM_LIMIT = 56 * 1024 * 1024

OFF = dict(gates=0, rq=4096, rk=4352, rv=4608, rg=5120, s5u=5632, sxbc=6144, sz=7168, hq=7680, hf=8192,
           hi=8704, hg=9216, dt=9728)
NP = 9856
IN_DIM = 9736


def _orig_col_slices():
    sl = [(5640, 9736)]
    for base in (0, 256):
        for half in (0, 32):
            for h in range(4):
                sl.append((base + 64 * h + half, base + 64 * h + half + 32))
    sl.append((512, 1024))
    sl.append((1024, 1536))
    sl.append((1536, 2048))
    sl.append((2560, 3584))
    sl.append((2048, 2560))
    sl.append((3592, 5640))
    return sl


def _bdot(a, b):
    return jnp.dot(a.astype(bf16), b.astype(bf16), preferred_element_type=f32)


def _bdot_nt(a, b):
    return lax.dot_general(a.astype(bf16), b.astype(bf16), (((1,), (1,)), ((), ())), preferred_element_type=f32)


def _bdot_tn(a, b):
    return lax.dot_general(a.astype(bf16), b.astype(bf16), (((0,), (0,)), ((), ())), preferred_element_type=f32)


def _hdot(a, b):
    return jnp.dot(a, b, precision=HI, preferred_element_type=f32)


def _sig(x):
    return jax.nn.sigmoid(x)


def _rms(x, w):
    return x * lax.rsqrt(jnp.mean(x * x, axis=-1, keepdims=True) + EPS) * w


def _softplus(x):
    return jnp.maximum(x, 0.0) + jnp.log(1.0 + jnp.exp(-jnp.abs(x)))


def _r(arr, rb, w, jb=0):
    return (arr, (rb, w), lambda n, jb=jb: (n, jb))


def _full(arr):
    nd = arr.ndim
    return (arr, arr.shape, lambda n, nd=nd: (0,) * nd)


def _seq_fwd(name, step, rows, consts, nds, states, outs, n_chunks, save_states=False, carry=None):
    nr, nc, nn, ns, no = len(rows), len(consts), len(nds), len(states), len(outs)
    whole = list(consts) + list(nds)

    def body(*refs):
        row_refs = refs[:nr]
        whole_hbm = refs[nr:nr + nc + nn]
        out_refs = refs[nr + nc + nn:nr + nc + nn + no]
        k = nr + nc + nn + no
        saved_refs = refs[k:k + (ns if save_states else 0)]
        k += ns if save_states else 0
        whole_vmem = refs[k:k + nc + nn]
        state_refs = refs[k + nc + nn:]
        n = pl.program_id(0)

        @pl.when(n == 0)
        def _():
            for src, dst in zip(whole_hbm, whole_vmem):
                pltpu.sync_copy(src, dst)
            for s in state_refs:
                s[...] = jnp.zeros_like(s)

        st = tuple(s[...] for s in state_refs)
        if save_states:
            for sv, v in zip(saved_refs, st):
                sv[0] = v
        o, new = step(tuple(r[...] for r in row_refs), tuple(c[...] for c in whole_vmem[:nc]),
                      tuple(c[...] for c in whole_vmem[nc:]), st, n)
        for ref, v in zip(out_refs, o):
            ref[...] = v.astype(ref.dtype)
        for s, v in zip(state_refs, new):
            s[...] = v

    in_specs = [pl.BlockSpec(bs, im) for _, bs, im in rows] + [pl.BlockSpec(memory_space=pl.ANY)] * (nc + nn)
    out_shape = [jax.ShapeDtypeStruct(s, d) for s, d, _, _ in outs]
    out_specs = [pl.BlockSpec(bs, im) for _, _, bs, im in outs]
    if save_states:
        for s in states:
            out_shape.append(jax.ShapeDtypeStruct((n_chunks,) + tuple(s), f32))
            out_specs.append(pl.BlockSpec((1,) + tuple(s), lambda n, z=len(s): (n,) + (0,) * z))
    scratch = [pltpu.VMEM(a.shape, a.dtype) for a in whole] + [pltpu.VMEM(tuple(s), f32) for s in states]
    return _pcall(body, name, (n_chunks,), in_specs, out_specs, out_shape, scratch, [a for a, _, _ in rows] + whole,
                  ("arbitrary",), carry)


def _seq_bwd(name, step, rows, row_diff, consts, nds, saved, couts, n_chunks, carry=None):
    nr, nc, nn, ns, no = len(rows), len(consts), len(nds), len(saved), len(couts)
    whole = list(consts) + list(nds)
    didx = [i for i in range(nr) if row_diff[i]]

    def rev(im):
        return lambda n: im(n_chunks - 1 - n)

    def body(*refs):
        row_refs = refs[:nr]
        whole_hbm = refs[nr:nr + nc + nn]
        k = nr + nc + nn
        saved_refs = refs[k:k + ns]
        k += ns
        cout_refs = refs[k:k + no]
        k += no
        drow_refs = refs[k:k + len(didx)]
        k += len(didx)
        dconst_hbm = refs[k:k + nc]
        k += nc
        whole_vmem = refs[k:k + nc + nn]
        k += nc + nn
        dconst_acc = refs[k:k + nc]
        k += nc
        dstate_refs = refs[k:]
        n = pl.program_id(0)

        @pl.when(n == 0)
        def _():
            for src, dst in zip(whole_hbm, whole_vmem):
                pltpu.sync_copy(src, dst)
            for a in dconst_acc:
                a[...] = jnp.zeros_like(a)
            for s in dstate_refs:
                s[...] = jnp.zeros_like(s)

        rvals = tuple(r[...] for r in row_refs)
        cvals = tuple(c[...] for c in whole_vmem[:nc])
        nvals = tuple(c[...] for c in whole_vmem[nc:])
        svals = tuple(s[0] for s in saved_refs)
        cidx = n_chunks - 1 - n

        def f(dr, cv, sv):
            full = list(rvals)
            for i, v in zip(didx, dr):
                full[i] = v
            return step(tuple(full), cv, nvals, sv, cidx)

        (o, _), vf = jax.vjp(f, tuple(rvals[i] for i in didx), cvals, svals)
        ct_o = tuple(c[...].astype(v.dtype) for c, v in zip(cout_refs, o))
        ct_s = tuple(s[...] for s in dstate_refs)
        d_rows, d_consts, d_states = vf((ct_o, ct_s))
        for ref, v in zip(drow_refs, d_rows):
            ref[...] = v.astype(ref.dtype)
        for a, v in zip(dconst_acc, d_consts):
            a[...] += v.astype(f32)
        for s, v in zip(dstate_refs, d_states):
            s[...] = v

        @pl.when(n == n_chunks - 1)
        def _():
            for a, dst in zip(dconst_acc, dconst_hbm):
                pltpu.sync_copy(a, dst)

    in_specs = ([pl.BlockSpec(bs, rev(im)) for _, bs, im in rows]
                + [pl.BlockSpec(memory_space=pl.ANY)] * (nc + nn)
                + [pl.BlockSpec((1,) + a.shape[1:], lambda n, z=a.ndim - 1: (n_chunks - 1 - n,) + (0,) * z) for a in saved]
                + [pl.BlockSpec(bs, rev(im)) for _, bs, im in couts])
    out_shape, out_specs = [], []
    for i in didx:
        a, bs, im = rows[i]
        nrows = a.shape[0]
        out_shape.append(jax.ShapeDtypeStruct((nrows,) + tuple(bs[1:]), f32))
        out_specs.append(pl.BlockSpec(bs, (lambda im: lambda n: (im(n_chunks - 1 - n)[0],) + (0,) * (len(bs) - 1))(im)))
    for c in consts:
        out_shape.append(jax.ShapeDtypeStruct(c.shape, f32))
        out_specs.append(pl.BlockSpec(memory_space=pl.ANY))
    scratch = ([pltpu.VMEM(a.shape, a.dtype) for a in whole] + [pltpu.VMEM(c.shape, f32) for c in consts]
               + [pltpu.VMEM(a.shape[1:], f32) for a in saved])
    res = _pcall(body, name, (n_chunks,), in_specs, out_specs, out_shape, scratch,
                 [a for a, _, _ in rows] + whole + list(saved) + [a for a, _, _ in couts], ("arbitrary",), carry)
    return res[:len(didx)], res[len(didx):]


def _mm(name, a, b, mode, tm, tn, tk, precision=None):
    if mode == "nn":
        (m, kd), nn_ = a.shape, b.shape[1]
        a_spec = pl.BlockSpec((tm, tk), lambda i, j, k: (i, k))
        b_spec = pl.BlockSpec((tk, tn), lambda i, j, k: (k, j))
        dims = (((1,), (0,)), ((), ()))
    elif mode == "tn":
        (kd, m), nn_ = a.shape, b.shape[1]
        a_spec = pl.BlockSpec((tk, tm), lambda i, j, k: (k, i))
        b_spec = pl.BlockSpec((tk, tn), lambda i, j, k: (k, j))
        dims = (((0,), (0,)), ((), ()))
    else:
        (m, kd), nn_ = a.shape, b.shape[0]
        a_spec = pl.BlockSpec((tm, tk), lambda i, j, k: (i, k))
        b_spec = pl.BlockSpec((tn, tk), lambda i, j, k: (j, k))
        dims = (((1,), (1,)), ((), ()))
    assert m % tm == 0 and nn_ % tn == 0 and kd % tk == 0, (name, a.shape, b.shape, tm, tn, tk)
    nk = kd // tk

    def body(a_ref, b_ref, o_ref, acc):
        k = pl.program_id(2)

        @pl.when(k == 0)
        def _():
            acc[...] = jnp.zeros_like(acc)

        if precision is None:
            acc[...] += lax.dot_general(a_ref[...].astype(bf16), b_ref[...].astype(bf16), dims, preferred_element_type=f32)
        else:
            acc[...] += lax.dot_general(a_ref[...], b_ref[...], dims, precision=precision, preferred_element_type=f32)

        @pl.when(k == nk - 1)
        def _():
            o_ref[...] = acc[...]

    return pl.pallas_call(
        body, name=name, grid=(m // tm, nn_ // tn, nk), in_specs=[a_spec, b_spec],
        out_specs=pl.BlockSpec((tm, tn), lambda i, j, k: (i, j)),
        out_shape=jax.ShapeDtypeStruct((m, nn_), f32), scratch_shapes=[pltpu.VMEM((tm, tn), f32)],
        compiler_params=pltpu.CompilerParams(dimension_semantics=("parallel", "parallel", "arbitrary"),
                                             vmem_limit_bytes=VMEM_LIMIT),
    )(a, b)


def _div_tile(n, want, mult):
    best = None
    for t in range(mult, min(n, want) + 1, mult):
        if n % t == 0:
            best = t
    return best if best is not None else n


class _Carry:
    def __init__(self, kind, srcs):
        self.kind, self.srcs, self.n, self.result = kind, list(srcs), len(srcs), None

    def out_shapes(self):
        if self.kind == "scatter":
            return [jax.ShapeDtypeStruct(s.shape, s.dtype) for s in self.srcs]
        return [jax.ShapeDtypeStruct((N_DEV,) + tuple(s.shape), s.dtype) for s in self.srcs]

    def sems(self):
        k = N_DEV if self.kind == "scatter" else N_DEV - 1
        return [pltpu.SemaphoreType.DMA((self.n, k)), pltpu.SemaphoreType.DMA((self.n, k)), pltpu.SemaphoreType.DMA((self.n,))]

    def _sc(self, src, dst, sems, i, j, slot):
        return pltpu.make_async_remote_copy(
            src_ref=src[i].at[j], dst_ref=dst[i].at[slot], send_sem=sems[0].at[i, j], recv_sem=sems[1].at[i, slot],
            device_id=(j // 4, (j // 2) % 2, j % 2), device_id_type=pl.DeviceIdType.MESH)

    def _where(self):
        x, y, c = lax.axis_index("x"), lax.axis_index("y"), lax.axis_index("c")
        return (x, y, c), (x, y, 1 - c), [(1 - x, y), (x, 1 - y), (1 - x, 1 - y)], c

    def _gc(self, src, dst, sems, i, k, block, to, own=False):
        slot = dst[i].at[4 * block[0] + 2 * block[1] + block[2]]
        return pltpu.make_async_remote_copy(
            src_ref=src[i] if own else slot, dst_ref=slot, send_sem=sems[0].at[i, k], recv_sem=sems[1].at[i, k],
            device_id=to, device_id_type=pl.DeviceIdType.MESH)

    def _local(self, src, dst, sems, i):
        if self.kind == "scatter":
            me = 4 * lax.axis_index("x") + 2 * lax.axis_index("y") + lax.axis_index("c")
            return pltpu.make_async_copy(src[i].at[me], dst[i].at[me], sems[2].at[i])
        x, y, c = lax.axis_index("x"), lax.axis_index("y"), lax.axis_index("c")
        return pltpu.make_async_copy(src[i], dst[i].at[4 * x + 2 * y + c], sems[2].at[i])

    def _first(self, src, dst, sems):
        me, sibling, chips, c = self._where()
        out = []
        for i in range(self.n):
            out.append(self._gc(src, dst, sems, i, 0, me, sibling, own=True))
            out += [self._gc(src, dst, sems, i, 1 + j, me, (*chip, c), own=True) for j, chip in enumerate(chips)]
        return out

    def start(self, src, dst, sems):
        for i in range(self.n):
            self._local(src, dst, sems, i).start()
        if self.kind == "scatter":
            me = 4 * lax.axis_index("x") + 2 * lax.axis_index("y") + lax.axis_index("c")
            for j in range(N_DEV):
                @pl.when(j != me)
                def _(j=j):
                    for i in range(self.n):
                        self._sc(src, dst, sems, i, j, me).start()
        else:
            for cp in self._first(src, dst, sems):
                cp.start()

    def finish(self, src, dst, sems):
        if self.kind == "scatter":
            me = 4 * lax.axis_index("x") + 2 * lax.axis_index("y") + lax.axis_index("c")
            for j in range(N_DEV):
                @pl.when(j != me)
                def _(j=j):
                    for i in range(self.n):
                        self._sc(src, dst, sems, i, j, j).wait_recv()
                        self._sc(src, dst, sems, i, j, me).wait_send()
        else:
            me, sibling, chips, c = self._where()
            passed = []
            for j, chip in enumerate(chips):
                for i in range(self.n):
                    self._gc(src, dst, sems, i, 1 + j, (*chip, c), me).wait_recv()
                    fwd = self._gc(src, dst, sems, i, 4 + j, (*chip, c), sibling)
                    fwd.start()
                    passed.append(fwd)
            for i in range(self.n):
                self._gc(src, dst, sems, i, 0, sibling, me).wait_recv()
                for j, chip in enumerate(chips):
                    self._gc(src, dst, sems, i, 4 + j, (*chip, 1 - c), me).wait_recv()
            for cp in self._first(src, dst, sems) + passed:
                cp.wait_send()
        for i in range(self.n):
            self._local(src, dst, sems, i).wait()


def _pcall(body, name, grid, in_specs, out_specs, out_shape, scratch, operands, dims, carry=None):
    n_in, n_out = len(in_specs), len(out_shape)
    if carry is not None:
        n, inner, n_steps = carry.n, body, (grid[0] if grid else 1)

        def body(*refs):
            ins, csrc = refs[:n_in], refs[n_in:n_in + n]
            outs, cdst = refs[n_in + n:n_in + n + n_out], refs[n_in + n + n_out:n_in + 2 * n + n_out]
            rest = refs[n_in + 2 * n + n_out:]
            step = pl.program_id(0) if grid else 0

            @pl.when(step == 0)
            def _():
                carry.start(csrc, cdst, rest[-3:])

            inner(*ins, *outs, *rest[:-3])

            @pl.when(step == n_steps - 1)
            def _():
                carry.finish(csrc, cdst, rest[-3:])

        hbm = pl.BlockSpec(memory_space=pltpu.HBM)
        in_specs, out_specs = list(in_specs) + [hbm] * n, list(out_specs) + [hbm] * n
        out_shape, scratch = list(out_shape) + carry.out_shapes(), list(scratch) + carry.sems()
        operands = list(operands) + carry.srcs
    kw = dict(grid=grid) if grid else {}
    res = pl.pallas_call(
        body, name=name, in_specs=in_specs, out_specs=out_specs, out_shape=out_shape, scratch_shapes=scratch,
        compiler_params=pltpu.CompilerParams(dimension_semantics=dims, vmem_limit_bytes=VMEM_LIMIT) if grid else None, **kw,
    )(*operands)
    res = list(res)
    if carry is not None:
        carry.result = res[n_out:]
    return res[:n_out]


def _exchange(name, srcs):
    carry = _Carry("scatter", srcs)
    _pcall(lambda *refs: None, name, (), [], [], [], [], [], None, carry)
    return carry.result


def _all_gather(name, srcs):
    carry = _Carry("gather", srcs)
    _pcall(lambda *refs: None, name, (), [], [], [], [], [], None, carry)
    return carry.result


def _ret_tables(c):
    gam = 1.0 - 2.0 ** (-5.0 - np.arange(RET_HEADS))
    lg = np.log(gam)
    t = np.arange(c)
    dmat = np.where(t[:, None] >= t[None, :], np.exp((t[:, None] - t[None, :])[None] * lg[:, None, None]), 0.0)
    head_v = np.arange(512) // 128
    ysc = np.exp((t[:, None] + 1) * lg[head_v][None, :])
    wtab = np.exp((c - 1 - t)[:, None] * lg[head_v][None, :])
    gtab = np.exp(c * lg[head_v])[None, :]
    head_k = (np.arange(256) % 128) // 32
    mask = (head_k[:, None] == head_v[None, :]).astype(np.float32)
    hm = (head_k[None, None, :] == np.arange(4)[:, None, None]).astype(np.float32)
    return [jnp.asarray(x, f32) for x in (dmat, ysc, wtab, gtab, mask, hm)]


def _rope_tables(tp):
    inv = 10000.0 ** (-np.arange(32, dtype=np.float32) / 32)
    ang = np.arange(tp, dtype=np.float32)[:, None] * inv[None, :]
    cos = np.tile(np.cos(ang), (1, 4)).astype(np.float32)
    sin = np.tile(np.sin(ang), (1, 4)).astype(np.float32)
    return jnp.asarray(cos), jnp.asarray(sin)


def _tri(c):
    t = np.arange(c)
    return jnp.asarray((t[:, None] >= t[None, :]).astype(np.float32))


def _ssd_tables(c):
    sh = np.zeros((3 * c, c + 8), np.float32)
    for d in (3, 2, 1):
        for t in range(c):
            sh[(3 - d) * c + t, 8 + t - d] = 1.0
    e = np.zeros((128, 512), np.float32)
    for h in range(SSD_HEADS):
        e[h, 64 * h:64 * h + 64] = 1.0
    mg = ((np.arange(256) // 128)[:, None] == (np.arange(512) // 256)[None, :]).astype(np.float32)
    cm4 = ((np.arange(256) // 64)[None, None, :] == np.arange(4)[:, None, None]).astype(np.float32)
    return [jnp.asarray(x) for x in (sh, e, mg, cm4)]


def _ret_step(rows, consts, nds, states, _):
    q, k, v, g, cos, sin = rows
    (gnw,) = consts
    dmat, ysc, wtab, gtab, mask, hm = nds
    (s,) = states
    q1, q2, k1, k2 = q[:, :128], q[:, 128:], k[:, :128], k[:, 128:]
    qr = jnp.concatenate([q1 * cos - q2 * sin, q1 * sin + q2 * cos], axis=1)
    kr = jnp.concatenate([k1 * cos - k2 * sin, k1 * sin + k2 * cos], axis=1) * 0.125
    y = _bdot(qr, s) * ysc
    parts = []
    for h in range(RET_HEADS):
        a = _bdot_nt(qr * hm[h], kr) * dmat[h]
        parts.append(_bdot(a, v[:, 128 * h:128 * h + 128]))
    y = y + jnp.concatenate(parts, axis=1)
    s_new = s * gtab + _bdot_tn(kr, v * wtab) * mask
    outs = []
    for h in range(RET_HEADS):
        yh = y[:, 128 * h:128 * h + 128]
        d = yh - jnp.mean(yh, axis=-1, keepdims=True)
        outs.append(d * lax.rsqrt(jnp.mean(d * d, axis=-1, keepdims=True) + EPS))
    yn = jnp.concatenate(outs, axis=1) * gnw
    return (g * _sig(g) * yn,), (s_new,)


def _ssd_step(rows, consts, nds, states, _):
    z, xbc, dt128 = rows
    cw, cb, dtb, alog, dsk, nw = consts
    tri, sh, e, mg, cm4 = nds
    s, tail = states
    c = xbc.shape[0]
    shifted = _hdot(sh, jnp.concatenate([tail, xbc], axis=0))
    xc = (cw[0:1] * shifted[0:c] + cw[1:2] * shifted[c:2 * c] + cw[2:3] * shifted[2 * c:3 * c] + cw[3:4] * xbc + cb)
    xc = xc * _sig(xc)
    xs, bm, cm = xc[:, :512], xc[:, 512:768], xc[:, 768:]
    dt = _softplus(dt128 + dtb)
    la = dt * (-jnp.exp(alog))
    cum = _hdot(tri, la)
    cum_t = cum.T
    dtx, cumx = _hdot(dt, e), _hdot(cum, e)
    lastx = cumx[c - 1:c]
    dskx = _hdot(jnp.broadcast_to(dsk, (8, 128)), e)[0:1]
    v = xs * dtx
    y = _bdot(cm, s) * jnp.exp(cumx)
    lane = lax.broadcasted_iota(jnp.int32, cum.shape, 1)
    sub = lax.broadcasted_iota(jnp.int32, cum_t.shape, 0)
    parts = []
    for grp in range(SSD_GROUPS):
        sg = _bdot_nt(cm[:, 128 * grp:128 * grp + 128], bm[:, 128 * grp:128 * grp + 128])
        vg = v[:, 256 * grp:256 * grp + 256]
        acc = jnp.zeros((c, 256), f32)
        for hh in range(4):
            h = 4 * grp + hh
            col = jnp.sum(jnp.where(lane == h, cum, 0.0), axis=1, keepdims=True)
            row = jnp.sum(jnp.where(sub == h, cum_t, 0.0), axis=0, keepdims=True)
            dec = jnp.exp(jnp.where(tri > 0.5, col - row, -1e30))
            acc = acc + _bdot(sg * dec, vg * cm4[hh])
        parts.append(acc)
    y = y + jnp.concatenate(parts, axis=1) + dskx * xs
    s_new = s * jnp.exp(lastx) + _bdot_tn(bm, v * jnp.exp(lastx - cumx)) * mg
    y = y * (z * _sig(z))
    outs = []
    for grp in range(SSD_GROUPS):
        yg = y[:, 256 * grp:256 * grp + 256]
        outs.append(yg * lax.rsqrt(jnp.mean(yg * yg, axis=-1, keepdims=True) + EPS))
    return (jnp.concatenate(outs, axis=1) * nw,), (s_new, xbc[c - 8:c])


def _hg_step(rows, consts, nds, states, _):
    hq, hf, hi, hgate = rows
    lb, nw = consts
    (tri,) = nds
    (st,) = states
    c = hq.shape[0]
    q = hq * _sig(hq)
    f = lb + (1.0 - lb) * _sig(hf)
    lf = jnp.log(f)
    k = 1.0 - f
    v = hi
    cum = _hdot(tri, lf)
    last = jnp.sum(lf, axis=0, keepdims=True)
    qd = q * jnp.exp(cum)
    kw = k * jnp.exp(last - cum)
    heads = [slice(128 * h, 128 * h + 128) for h in range(HG_HEADS)]
    y_rows = []
    rowid = lax.broadcasted_iota(jnp.int32, (HG_SUB, 512), 0)
    for i in range(c // HG_SUB):
        r0 = HG_SUB * i
        qi, ci, ki, vi = q[r0:r0 + HG_SUB], cum[r0:r0 + HG_SUB], k[r0:r0 + HG_SUB], v[r0:r0 + HG_SUB]
        yi = [jnp.zeros((HG_SUB, 128), f32) for _ in heads]
        for s_ in range(HG_SUB):
            e = qi * ki[s_:s_ + 1] * jnp.exp(jnp.where(rowid >= s_, ci - ci[s_:s_ + 1], -1e30))
            for h, sl in enumerate(heads):
                yi[h] = yi[h] + jnp.sum(e[:, sl], axis=1, keepdims=True) * vi[s_:s_ + 1, sl]
        if i > 0:
            b = cum[r0 - 1:r0]
            qs = qi * jnp.exp(ci - b)
            ks = k[:r0] * jnp.exp(b - cum[:r0])
            for h, sl in enumerate(heads):
                yi[h] = yi[h] + _bdot(_bdot_nt(qs[:, sl], ks[:, sl]), v[:r0, sl])
        y_rows.append(jnp.concatenate(yi, axis=1))
    y_in = jnp.concatenate(y_rows, axis=0)
    y = y_in + jnp.concatenate([_bdot_nt(qd[:, sl], st[:, sl]) for sl in heads], axis=1)
    st_new = jnp.concatenate([st[:, sl] * jnp.exp(last[:, sl]) + _bdot_tn(v[:, sl], kw[:, sl]) for sl in heads], axis=1)
    outs = []
    for sl in heads:
        yh = y[:, sl]
        outs.append(yh * lax.rsqrt(jnp.mean(yh * yh, axis=-1, keepdims=True) + EPS))
    o = jnp.concatenate(outs, axis=1) * nw
    return (o * (hgate * _sig(hgate)),), (st_new,)


def _s5post_step(rows, consts, nds, states, _):
    ycore, u = rows
    dsk, gw, gb = consts
    y = jax.nn.gelu(ycore + dsk * u)
    zz = _bdot(y, gw) + gb
    return (zz[:, :512] * _sig(zz[:, 512:]),), ()


def _s5prep_step(rows, consts, nds, states, _):
    lr, li, lstep, btr, bti = rows
    (rep,) = nds
    step = jnp.exp(lstep)
    mag = jnp.exp(lr * step)
    ab_re, ab_im = mag * jnp.cos(li * step), mag * jnp.sin(li * step)
    inv = 1.0 / (lr * lr + li * li)
    co_re = ((ab_re - 1.0) * lr + ab_im * li) * inv
    co_im = (ab_im * lr - (ab_re - 1.0) * li) * inv
    cre, cim = _hdot(rep, co_re), _hdot(rep, co_im)
    return (ab_re, ab_im, cre * btr - cim * bti, cre * bti + cim * btr), ()


def _lb_step(rows, consts, nds, states, _):
    (x,) = rows
    (lmat,) = nds
    valid = lax.broadcasted_iota(jnp.int32, x.shape, 0) < DEPTH
    xm = jnp.where(valid, x, -1e30)
    ex = jnp.where(valid, jnp.exp(xm - jnp.max(xm, axis=0, keepdims=True)), 0.0)
    sm = ex / jnp.sum(ex, axis=0, keepdims=True)
    return (_hdot(lmat, sm),), ()


def _rmsn_step(rows, consts, nds, states, _):
    (h,) = rows
    (w,) = consts
    return (_rms(h, w),), ()


def _merge_step(rows, consts, nds, states, _):
    yr, ys5, yssd, yhg, gates, h = rows
    wb, wout, npost = consts
    mixed = jnp.zeros(h.shape, f32)
    for n, yb in enumerate((yr, ys5, yssd, yhg)):
        mixed = mixed + _sig(gates[:, 1024 * n:1024 * n + 1024]) * _bdot(yb, wb[n])
    return (h + _rms(_bdot(mixed, wout), npost),), ()


FF_BLK = 512


def _row_blocks(tp):
    rb = _div_tile(tp, 528, 16)
    return rb, tp // rb


def _mlp_core_fwd(tag, u, w_up, w_down):
    tp = u.shape[0]
    rb, nrb = _row_blocks(tp)
    nff = D_FF // FF_BLK

    def body(u_ref, wup_ref, wdown_ref, m_ref):
        j = pl.program_id(0)

        def rows(i, carry):
            r = pl.multiple_of(i * rb, 16)
            a = jnp.dot(u_ref[pl.ds(r, rb), :], wup_ref[...], preferred_element_type=f32)
            part = _bdot(jnp.square(jnp.maximum(a, 0.0)), wdown_ref[...])

            @pl.when(j == 0)
            def _():
                m_ref[pl.ds(r, rb), :] = part

            @pl.when(j > 0)
            def _():
                m_ref[pl.ds(r, rb), :] += part
            return carry
        lax.fori_loop(0, nrb, rows, 0)

    return pl.pallas_call(
        body, name="mlp_core_fwd" + tag, grid=(nff,),
        in_specs=[pl.BlockSpec((tp, D_MODEL), lambda j: (0, 0)), pl.BlockSpec((D_MODEL, FF_BLK), lambda j: (0, j)),
                  pl.BlockSpec((FF_BLK, D_MODEL), lambda j: (j, 0))],
        out_specs=pl.BlockSpec((tp, D_MODEL), lambda j: (0, 0)), out_shape=jax.ShapeDtypeStruct((tp, D_MODEL), f32),
        compiler_params=pltpu.CompilerParams(dimension_semantics=("arbitrary",), vmem_limit_bytes=VMEM_LIMIT),
    )(u, w_up, w_down)


def _mlp_core_bwd(tag, u, dm, w_up, w_down, carry=None):
    tp = u.shape[0]
    rb, nrb = _row_blocks(tp)
    nff = D_FF // FF_BLK

    def body(u_hbm, dm_hbm, wup_ref, wdown_ref, du_hbm, dwup_ref, dwdown_ref, u_s, dm_s, du_s):
        j = pl.program_id(0)

        @pl.when(j == 0)
        def _():
            pltpu.sync_copy(u_hbm, u_s)
            pltpu.sync_copy(dm_hbm, dm_s)

        def rows(i, carry):
            r = pl.multiple_of(i * rb, 16)
            ub = u_s[pl.ds(r, rb), :]
            dmb = dm_s[pl.ds(r, rb), :].astype(bf16)
            a = jnp.dot(ub, wup_ref[...], preferred_element_type=f32)
            ra = jnp.maximum(a, 0.0)
            da = (_bdot_nt(dmb, wdown_ref[...]) * (2.0 * ra)).astype(bf16)
            dwd = _bdot_tn(ra * ra, dmb)
            dwu = _bdot_tn(ub, da)
            dub = _bdot_nt(da, wup_ref[...])

            @pl.when(i == 0)
            def _():
                dwdown_ref[...] = dwd
                dwup_ref[...] = dwu

            @pl.when(i > 0)
            def _():
                dwdown_ref[...] += dwd
                dwup_ref[...] += dwu

            @pl.when(j == 0)
            def _():
                du_s[pl.ds(r, rb), :] = dub

            @pl.when(j > 0)
            def _():
                du_s[pl.ds(r, rb), :] += dub
            return carry
        lax.fori_loop(0, nrb, rows, 0)

        @pl.when(j == nff - 1)
        def _():
            pltpu.sync_copy(du_s, du_hbm)

    anyspec = pl.BlockSpec(memory_space=pl.ANY)
    return _pcall(
        body, "mlp_core_bwd" + tag, (nff,),
        [anyspec, anyspec, pl.BlockSpec((D_MODEL, FF_BLK), lambda j: (0, j)), pl.BlockSpec((FF_BLK, D_MODEL), lambda j: (j, 0))],
        [anyspec, pl.BlockSpec((D_MODEL, FF_BLK), lambda j: (0, j)), pl.BlockSpec((FF_BLK, D_MODEL), lambda j: (j, 0))],
        [jax.ShapeDtypeStruct((tp, D_MODEL), f32), jax.ShapeDtypeStruct((D_MODEL, D_FF), f32), jax.ShapeDtypeStruct((D_FF, D_MODEL), f32)],
        [pltpu.VMEM((tp, D_MODEL), bf16), pltpu.VMEM((tp, D_MODEL), f32), pltpu.VMEM((tp, D_MODEL), f32)],
        [u, dm, w_up, w_down], ("arbitrary",), carry)


def _resid_rms_step(rows, consts, nds, states, _):
    m, h = rows
    (w,) = consts
    return (h + _rms(m, w),), ()


SCAN_RB = 16


def _scan_inplace(xr, xi, ar0, ai0, tp, reverse):
    rb = SCAN_RB
    w = xr.shape[1]
    nblk = tp // rb
    ar, ai = ar0, ai0
    d = 1
    while d < tp:
        arb, aib = jnp.broadcast_to(ar, (rb, w)), jnp.broadcast_to(ai, (rb, w))

        def upd(r0, sr, si, arb=arb, aib=aib):
            cr, ci = xr[pl.ds(r0, rb), :], xi[pl.ds(r0, rb), :]
            xr[pl.ds(r0, rb), :] = cr + arb * sr - aib * si
            xi[pl.ds(r0, rb), :] = ci + arb * si + aib * sr

        rowid = lax.broadcasted_iota(jnp.int32, (rb, w), 0)
        if d < 8:
            if not reverse:
                def body(i, carry, d=d, upd=upd):
                    r0 = pl.multiple_of(tp - (i + 1) * rb, 8)
                    lo = pl.multiple_of(r0 - 8, 8)
                    sr = pltpu.roll(xr[pl.ds(lo, rb + 8), :], d, 0)[8:, :]
                    si = pltpu.roll(xi[pl.ds(lo, rb + 8), :], d, 0)[8:, :]
                    upd(r0, sr, si)
                    return carry
                lax.fori_loop(0, nblk - 1, body, 0)
                sr = jnp.where(rowid >= d, pltpu.roll(xr[pl.ds(0, rb), :], d, 0), 0.0)
                si = jnp.where(rowid >= d, pltpu.roll(xi[pl.ds(0, rb), :], d, 0), 0.0)
                upd(0, sr, si)
            else:
                def body(i, carry, d=d, upd=upd):
                    r0 = pl.multiple_of(i * rb, 8)
                    sr = pltpu.roll(xr[pl.ds(r0, rb + 8), :], rb + 8 - d, 0)[:rb, :]
                    si = pltpu.roll(xi[pl.ds(r0, rb + 8), :], rb + 8 - d, 0)[:rb, :]
                    upd(r0, sr, si)
                    return carry
                lax.fori_loop(0, nblk - 1, body, 0)
                sr = jnp.where(rowid < rb - d, pltpu.roll(xr[pl.ds(tp - rb, rb), :], rb - d, 0), 0.0)
                si = jnp.where(rowid < rb - d, pltpu.roll(xi[pl.ds(tp - rb, rb), :], rb - d, 0), 0.0)
                upd(tp - rb, sr, si)
        else:
            nfull = (tp - d) // rb
            rem = (tp - d) - nfull * rb
            if not reverse:
                def body(i, carry, d=d, upd=upd):
                    r0 = pl.multiple_of(tp - (i + 1) * rb, 8)
                    lo = pl.multiple_of(r0 - d, 8)
                    upd(r0, xr[pl.ds(lo, rb), :], xi[pl.ds(lo, rb), :])
                    return carry
                lax.fori_loop(0, nfull, body, 0)
                if rem:
                    cr, ci = xr[pl.ds(d, rem), :], xi[pl.ds(d, rem), :]
                    sr, si = xr[pl.ds(0, rem), :], xi[pl.ds(0, rem), :]
                    xr[pl.ds(d, rem), :] = cr + ar * sr - ai * si
                    xi[pl.ds(d, rem), :] = ci + ar * si + ai * sr
            else:
                def body(i, carry, d=d, upd=upd):
                    r0 = pl.multiple_of(i * rb, 8)
                    hi = pl.multiple_of(r0 + d, 8)
                    upd(r0, xr[pl.ds(hi, rb), :], xi[pl.ds(hi, rb), :])
                    return carry
                lax.fori_loop(0, nfull, body, 0)
                if rem:
                    lo = nfull * rb
                    cr, ci = xr[pl.ds(lo, rem), :], xi[pl.ds(lo, rem), :]
                    sr, si = xr[pl.ds(lo + d, rem), :], xi[pl.ds(lo + d, rem), :]
                    xr[pl.ds(lo, rem), :] = cr + ar * sr - ai * si
                    xi[pl.ds(lo, rem), :] = ci + ar * si + ai * sr
        ar, ai = ar * ar - ai * ai, 2.0 * ar * ai
        d *= 2


def _s5_blocks(tp):
    rbm = tp // 8 if (tp // 8) % 8 == 0 and tp % 8 == 0 else tp
    return rbm, tp // rbm


def _s5_in_specs(proj, tp):
    ucol = OFF["s5u"] // 128
    return [
        pl.BlockSpec((tp, 128), lambda cb: (0, ucol + cb)),
        pl.BlockSpec((128, 512), lambda cb: (cb, 0)),
        pl.BlockSpec((128, 512), lambda cb: (cb, 0)),
        pl.BlockSpec((1, 512), lambda cb: (0, cb)),
        pl.BlockSpec((1, 512), lambda cb: (0, cb)),
        pl.BlockSpec((512, 128), lambda cb: (cb, 0)),
        pl.BlockSpec((512, 128), lambda cb: (cb, 0)),
    ]


def _s5_core_fwd(tag, proj, bbr, bbi, ar, ai, ccr, cci, carry=None):
    tp = proj.shape[0]
    rbm, nb = _s5_blocks(tp)

    def body(u_ref, bbr_ref, bbi_ref, ar_ref, ai_ref, ccr_ref, cci_ref, y_ref, xr, xi):
        def fill(i, carry):
            r = pl.multiple_of(i * rbm, 8)
            ub = u_ref[pl.ds(r, rbm), :]
            xr[pl.ds(r, rbm), :] = _bdot(ub, bbr_ref[...])
            xi[pl.ds(r, rbm), :] = _bdot(ub, bbi_ref[...])
            return carry
        lax.fori_loop(0, nb, fill, 0)
        _scan_inplace(xr, xi, ar_ref[...], ai_ref[...], tp, False)

        def out(i, carry):
            r = pl.multiple_of(i * rbm, 8)
            y_ref[pl.ds(r, rbm), :] = (_bdot(xr[pl.ds(r, rbm), :], ccr_ref[...]) - _bdot(xi[pl.ds(r, rbm), :], cci_ref[...]))
            return carry
        lax.fori_loop(0, nb, out, 0)

    return _pcall(body, "s5_core_fwd" + tag, (4,), _s5_in_specs(proj, tp), [pl.BlockSpec((tp, 128), lambda cb: (0, cb))],
                  [jax.ShapeDtypeStruct((tp, 512), f32)], [pltpu.VMEM((tp, 512), f32), pltpu.VMEM((tp, 512), f32)],
                  [proj, bbr, bbi, ar, ai, ccr, cci], ("arbitrary",), carry)[0]


def _s5_core_bwd(tag, proj, bbr, bbi, ar, ai, ccr, cci, dy, du_post, carry=None):
    tp = proj.shape[0]
    rbm, nb = _s5_blocks(tp)
    rb = SCAN_RB

    def body(u_ref, bbr_ref, bbi_ref, ar_ref, ai_ref, ccr_ref, cci_ref, dy_ref, dup_ref,
             du_ref, dbbr_ref, dbbi_ref, dar_ref, dai_ref, dccr_ref, dcci_ref, xr, xi, gr, gi):
        dccr_ref[...] = jnp.zeros_like(dccr_ref)
        dcci_ref[...] = jnp.zeros_like(dcci_ref)
        dbbr_ref[...] = jnp.zeros_like(dbbr_ref)
        dbbi_ref[...] = jnp.zeros_like(dbbi_ref)

        def fill(i, carry):
            r = pl.multiple_of(i * rbm, 8)
            ub = u_ref[pl.ds(r, rbm), :]
            xr[pl.ds(r, rbm), :] = _bdot(ub, bbr_ref[...])
            xi[pl.ds(r, rbm), :] = _bdot(ub, bbi_ref[...])
            return carry
        lax.fori_loop(0, nb, fill, 0)
        _scan_inplace(xr, xi, ar_ref[...], ai_ref[...], tp, False)

        def seed(i, carry):
            r = pl.multiple_of(i * rbm, 8)
            dyb = dy_ref[pl.ds(r, rbm), :]
            dccr_ref[...] += _bdot_tn(xr[pl.ds(r, rbm), :], dyb)
            dcci_ref[...] -= _bdot_tn(xi[pl.ds(r, rbm), :], dyb)
            gr[pl.ds(r, rbm), :] = _bdot_nt(dyb, ccr_ref[...])
            gi[pl.ds(r, rbm), :] = -_bdot_nt(dyb, cci_ref[...])
            return carry
        lax.fori_loop(0, nb, seed, 0)
        _scan_inplace(gr, gi, ar_ref[...], -ai_ref[...], tp, True)

        w = xr.shape[1]
        rowid = lax.broadcasted_iota(jnp.int32, (rb, w), 0)

        def prods(pr, pi, r0):
            g_r, g_i = gr[pl.ds(r0, rb), :], gi[pl.ds(r0, rb), :]
            return pr * g_r + pi * g_i, pr * g_i - pi * g_r

        def dacc(i, carry):
            r0 = pl.multiple_of((i + 1) * rb, 8)
            lo = pl.multiple_of(r0 - 8, 8)
            pr = pltpu.roll(xr[pl.ds(lo, rb + 8), :], 1, 0)[8:, :]
            pi = pltpu.roll(xi[pl.ds(lo, rb + 8), :], 1, 0)[8:, :]
            a, b = prods(pr, pi, r0)
            return carry[0] + a, carry[1] + b
        pr0 = jnp.where(rowid >= 1, pltpu.roll(xr[pl.ds(0, rb), :], 1, 0), 0.0)
        pi0 = jnp.where(rowid >= 1, pltpu.roll(xi[pl.ds(0, rb), :], 1, 0), 0.0)
        acc_r, acc_i = lax.fori_loop(0, tp // rb - 1, dacc, prods(pr0, pi0, 0))
        dar_ref[...] = jnp.sum(acc_r, axis=0, keepdims=True)
        dai_ref[...] = jnp.sum(acc_i, axis=0, keepdims=True)

        def tail(i, carry):
            r = pl.multiple_of(i * rbm, 8)
            g_r, g_i = gr[pl.ds(r, rbm), :], gi[pl.ds(r, rbm), :]
            ub = u_ref[pl.ds(r, rbm), :]
            du_ref[pl.ds(r, rbm), :] = _bdot_nt(g_r, bbr_ref[...]) + _bdot_nt(g_i, bbi_ref[...]) + dup_ref[pl.ds(r, rbm), :]
            dbbr_ref[...] += _bdot_tn(ub, g_r)
            dbbi_ref[...] += _bdot_tn(ub, g_i)
            return carry
        lax.fori_loop(0, nb, tail, 0)

    col = lambda cb: (0, cb)
    blk = lambda cb: (cb, 0)
    return _pcall(
        body, "s5_core_bwd" + tag, (4,),
        _s5_in_specs(proj, tp) + [pl.BlockSpec((tp, 128), col), pl.BlockSpec((tp, 128), col)],
        [pl.BlockSpec((tp, 128), col), pl.BlockSpec((128, 512), blk), pl.BlockSpec((128, 512), blk),
         pl.BlockSpec((1, 512), col), pl.BlockSpec((1, 512), col), pl.BlockSpec((512, 128), blk), pl.BlockSpec((512, 128), blk)],
        [jax.ShapeDtypeStruct((tp, 512), f32), jax.ShapeDtypeStruct((512, 512), f32), jax.ShapeDtypeStruct((512, 512), f32),
         jax.ShapeDtypeStruct((1, 2048), f32), jax.ShapeDtypeStruct((1, 2048), f32), jax.ShapeDtypeStruct((2048, 128), f32),
         jax.ShapeDtypeStruct((2048, 128), f32)],
        [pltpu.VMEM((tp, 512), f32)] * 4, [proj, bbr, bbi, ar, ai, ccr, cci, dy, du_post], ("arbitrary",), carry)


_EYE8 = np.eye(8, dtype=np.float32)


def _bb_dense(bb):
    return jnp.einsum("cgjp,gh->cgjhp", bb.reshape(4, 8, 16, 64), _EYE8).reshape(512, 512)


def _bb_diag(d):
    return jnp.einsum("cgjhp,gh->cgjp", d.reshape(4, 8, 16, 8, 64), _EYE8).reshape(512, 64)


def _cc_dense(cmat):
    return jnp.einsum("cgjp,gh->cgphj", cmat.reshape(4, 8, 16, 64), _EYE8).reshape(2048, 128)


def _cc_diag(d):
    return jnp.einsum("cgphj,gh->cgjp", d.reshape(4, 8, 64, 8, 16), _EYE8).reshape(32, 16, 64)


def _tables(tp):
    cos, sin = _rope_tables(tp)
    rep = np.zeros((512, 32), np.float32)
    rep[np.arange(512), np.arange(512) // 16] = 1.0
    lmat = np.zeros((8, 8), np.float32)
    for l in range(DEPTH):
        lmat[l, 1:l + 1] = 1.0
    return dict(cos=cos, sin=sin, ret=_ret_tables(CHUNK), tri=_tri(CHUNK), ssd=_ssd_tables(CHUNK),
                rep=jnp.asarray(rep), lmat=jnp.asarray(lmat))


def _tiles(tp):
    return dict(tr=_div_tile(tp, 352, 16), tmg=_div_tile(tp, 192, 16))


def _mixer_rows(proj, c):
    ret = [_r(proj, c, 256, OFF["rq"] // 256), _r(proj, c, 256, OFF["rk"] // 256), _r(proj, c, 512, OFF["rv"] // 512),
           _r(proj, c, 512, OFF["rg"] // 512)]
    ssd = [_r(proj, c, 512, OFF["sz"] // 512), _r(proj, c, 1024, OFF["sxbc"] // 1024), _r(proj, c, 128, OFF["dt"] // 128)]
    hg = [_r(proj, c, 512, OFF[k] // 512) for k in ("hq", "hf", "hi", "hg")]
    return ret, ssd, hg


def _out2(rows, w, rb, dtype=f32):
    return (rows, w), dtype, (rb, w), lambda n: (n, 0)


def _s5_prep_rows(p):
    return [_full(p["lam_re"]), _full(p["lam_im"]), _full(p["lstep"]), _full(p["bt_re"]), _full(p["bt_im"])]


def _s5_consts(p, tabs, tag):
    whole = lambda s: (s, f32, s, lambda n: (0, 0))
    ab_re, ab_im, bb_re, bb_im = _seq_fwd("s5_prep" + tag, _s5prep_step, _s5_prep_rows(p), [], [tabs["rep"]], [],
                                          [whole((32, 64)), whole((32, 64)), whole((512, 64)), whole((512, 64))], 1)
    return (_bb_dense(bb_re).astype(bf16), _bb_dense(bb_im).astype(bf16), ab_re.reshape(1, 2048), ab_im.reshape(1, 2048),
            _cc_dense(p["c_re"]).astype(bf16), _cc_dense(p["c_im"]).astype(bf16))


def _rmsn_step_b(rows, consts, nds, states, n):
    (o,), _ = _rmsn_step(rows, consts, nds, states, n)
    return (o, rows[0]), ()


def _layer_fwd(h, p, tabs, tag, carry):
    tp = h.shape[0]
    c = CHUNK
    nch = tp // c
    tl = _tiles(tp)
    tr, tmg = tl["tr"], tl["tmg"]
    (u,) = _seq_fwd("rms_premix" + tag, _rmsn_step, [_r(h, tr, 1024)], [p["npm"]], [], [], [_out2(tp, 1024, tr)], tp // tr)
    proj = _mm("in_proj" + tag, u, p["w_in"], "nn", _div_tile(tp, 1056, 16), 1408, 1024)
    ret_rows, ssd_rows, hg_rows = _mixer_rows(proj, c)
    y_ret, ret_s = _seq_fwd("ret_fwd" + tag, _ret_step, ret_rows + [_r(tabs["cos"], c, 128), _r(tabs["sin"], c, 128)],
                            [p["ret_gn"]], tabs["ret"], [(256, 512)], [_out2(tp, 512, c)], nch, save_states=True,
                            carry=carry.get("ret"))
    ycore = _s5_core_fwd(tag, proj, *_s5_consts(p, tabs, tag), carry=carry.get("s5"))
    (y_s5,) = _seq_fwd("s5_post" + tag, _s5post_step, [_r(ycore, tr, 512), _r(proj, tr, 512, OFF["s5u"] // 512)],
                       [p["s5_d"], p["glu_w"], p["glu_b"]], [], [], [_out2(tp, 512, tr)], tp // tr)
    y_ssd, ssd_s, ssd_tail = _seq_fwd(
        "ssd_fwd" + tag, _ssd_step, ssd_rows, [p["conv_w"], p["conv_b"], p["dt_bias"], p["a_log"], p["ssd_d"], p["ssd_nw"]],
        [tabs["tri"]] + tabs["ssd"], [(256, 512), (8, 1024)], [_out2(tp, 512, c)], nch, save_states=True, carry=carry.get("ssd"))
    y_hg, hg_s = _seq_fwd("hg_fwd" + tag, _hg_step, hg_rows, [p["lb"], p["hg_nw"]], [tabs["tri"]], [(128, 512)],
                          [_out2(tp, 512, c)], nch, save_states=True, carry=carry.get("hg"))
    (h_mid,) = _seq_fwd(
        "merge_fwd" + tag, _merge_step,
        [_r(y_ret, tmg, 512), _r(y_s5, tmg, 512), _r(y_ssd, tmg, 512), _r(y_hg, tmg, 512), _r(proj, tmg, 4096, 0),
         _r(h, tmg, 1024)],
        [p["w_branch"], p["w_out"], p["npostmix"]], [], [], [_out2(tp, 1024, tmg)], tp // tmg)
    (u2,) = _seq_fwd("rms_premlp" + tag, _rmsn_step, [_r(h_mid, tr, 1024)], [p["npremlp"]], [], [],
                     [_out2(tp, 1024, tr, bf16)], tp // tr)
    m = _mlp_core_fwd(tag, u2, p["w_up"], p["w_down"])
    (h_new,) = _seq_fwd("mlp_post" + tag, _resid_rms_step, [_r(m, tr, 1024), _r(h_mid, tr, 1024)], [p["npostmlp"]], [], [],
                        [_out2(tp, 1024, tr)], tp // tr)
    saved = dict(h=h, u=u, proj=proj, ret_s=ret_s, ycore=ycore, ssd_s=ssd_s, ssd_tail=ssd_tail, hg_s=hg_s,
                 y_ret=y_ret, y_s5=y_s5, y_ssd=y_ssd, y_hg=y_hg, h_mid=h_mid, u2=u2, m=m)
    return h_new, saved


def _layer_bwd(dh, p, sv, tabs, tag, carry):
    tp = dh.shape[0]
    c = CHUNK
    nch = tp // c
    tl = _tiles(tp)
    tr, tmg = tl["tr"], tl["tmg"]
    proj = sv["proj"]
    g = {}
    (d_m, d_hmid), (g["npostmlp"],) = _seq_bwd(
        "mlp_post_bwd" + tag, _resid_rms_step, [_r(sv["m"], tr, 1024), _r(sv["h_mid"], tr, 1024)], [True, True],
        [p["npostmlp"]], [], [], [_r(dh, tr, 1024)], tp // tr)
    d_u2, g["w_up"], g["w_down"] = _mlp_core_bwd(tag, sv["u2"], d_m, p["w_up"], p["w_down"], carry=carry.get("mlp"))
    (d_hmid,), (g["npremlp"],) = _seq_bwd(
        "rms_premlp_bwd" + tag, _rmsn_step_b, [_r(sv["h_mid"], tr, 1024)], [True], [p["npremlp"]], [], [],
        [_r(d_u2, tr, 1024), _r(d_hmid, tr, 1024)], tp // tr)
    (dy_ret, dy_s5, dy_ssd, dy_hg, d_gates, d_h1), (g["w_branch"], g["w_out"], g["npostmix"]) = _seq_bwd(
        "merge_bwd" + tag, _merge_step,
        [_r(sv["y_ret"], tmg, 512), _r(sv["y_s5"], tmg, 512), _r(sv["y_ssd"], tmg, 512), _r(sv["y_hg"], tmg, 512),
         _r(proj, tmg, 4096, 0), _r(sv["h"], tmg, 1024)], [True] * 6,
        [p["w_branch"], p["w_out"], p["npostmix"]], [], [], [_r(d_hmid, tmg, 1024)], tp // tmg)
    ret_rows, ssd_rows, hg_rows = _mixer_rows(proj, c)
    (d_hq, d_hf, d_hi, d_hg), (g["lb"], g["hg_nw"]) = _seq_bwd(
        "hg_bwd" + tag, _hg_step, hg_rows, [True] * 4, [p["lb"], p["hg_nw"]], [tabs["tri"]], [sv["hg_s"]],
        [_r(dy_hg, c, 512)], nch, carry=carry.get("hg"))
    (d_z, d_xbc, d_dt), (g["conv_w"], g["conv_b"], g["dt_bias"], g["a_log"], g["ssd_d"], g["ssd_nw"]) = _seq_bwd(
        "ssd_bwd" + tag, _ssd_step, ssd_rows, [True] * 3,
        [p["conv_w"], p["conv_b"], p["dt_bias"], p["a_log"], p["ssd_d"], p["ssd_nw"]], [tabs["tri"]] + tabs["ssd"],
        [sv["ssd_s"], sv["ssd_tail"]], [_r(dy_ssd, c, 512)], nch, carry=carry.get("ssd"))
    (d_ycore, du_post), (g["s5_d"], g["glu_w"], g["glu_b"]) = _seq_bwd(
        "s5_post_bwd" + tag, _s5post_step, [_r(sv["ycore"], tr, 512), _r(proj, tr, 512, OFF["s5u"] // 512)], [True, True],
        [p["s5_d"], p["glu_w"], p["glu_b"]], [], [], [_r(dy_s5, tr, 512)], tp // tr)
    du_s5, dbbr, dbbi, dar, dai, dccr, dcci = _s5_core_bwd(tag, proj, *_s5_consts(p, tabs, tag + "b"), d_ycore, du_post,
                                                           carry=carry.get("s5"))
    g["c_re"], g["c_im"] = _cc_diag(dccr), _cc_diag(dcci)
    (g["lam_re"], g["lam_im"], g["lstep"], g["bt_re"], g["bt_im"]), _ = _seq_bwd(
        "s5_prep_bwd" + tag, _s5prep_step, _s5_prep_rows(p), [True] * 5, [], [tabs["rep"]], [],
        [_full(dar.reshape(32, 64)), _full(dai.reshape(32, 64)), _full(_bb_diag(dbbr)), _full(_bb_diag(dbbi))], 1)
    (d_q, d_k, d_v, d_g), (g["ret_gn"],) = _seq_bwd(
        "ret_bwd" + tag, _ret_step, ret_rows + [_r(tabs["cos"], c, 128), _r(tabs["sin"], c, 128)], [True] * 4 + [False] * 2,
        [p["ret_gn"]], tabs["ret"], [sv["ret_s"]], [_r(dy_ret, c, 512)], nch)
    dproj = jnp.concatenate([d_gates, d_q, d_k, d_v, d_g, du_s5, d_xbc, d_z, d_hq, d_hf, d_hi, d_hg, d_dt], axis=1)
    g["w_in"] = _mm("in_proj_dw" + tag, sv["u"], dproj, "tn", 512, 1408, _div_tile(tp, 704, 16))
    du = _mm("in_proj_dx" + tag, dproj, p["w_in"], "nt", _div_tile(tp, 1056, 16), 1024, 1408)
    (dh_prev,), (g["npm"],) = _seq_bwd("rms_premix_bwd" + tag, _rmsn_step_b, [_r(sv["h"], tr, 1024)], [True], [p["npm"]], [], [],
                                       [_r(du, tr, 1024), _r(d_h1, tr, 1024)], tp // tr)
    return dh_prev, g


def _loss_call(h, tgt, lo, hi):
    tp = h.shape[0]
    tr = _div_tile(tp, 352, 16)

    def step(rows, consts, nds, states, n):
        hh, tt = rows
        row = n * tr + lax.broadcasted_iota(jnp.int32, hh.shape, 0)
        err = jnp.where((row >= lo) & (row < hi), hh - tt, 0.0)
        part = 0.5 * jnp.sum(err * err) * (1.0 / D_MODEL)
        return (err * (1.0 / D_MODEL), jnp.zeros((8, 128), f32) + part), ()

    dh, parts = _seq_fwd("loss_head", step, [_r(h, tr, 1024), _r(tgt, tr, 1024)], [], [], [],
                         [_out2(tp, 1024, tr), ((8 * (tp // tr), 128), f32, (8, 128), lambda n: (n, 0))], tp // tr)
    return dh, jnp.sum(parts[::8, 0])


def _adam_step(rows, consts, nds, states, _):
    g8, w, m, v = rows
    g = g8[0].astype(f32)
    for d in range(1, N_DEV):
        g = g + g8[d].astype(f32)
    m2 = ADAM_B1 * m + (1.0 - ADAM_B1) * g
    v2 = ADAM_B2 * v + (1.0 - ADAM_B2) * jnp.square(g)
    m_hat = m2 / (1.0 - ADAM_B1 ** ADAM_STEP)
    v_hat = v2 / (1.0 - ADAM_B2 ** ADAM_STEP)
    delta = -ADAM_LR * (m_hat / (jnp.sqrt(v_hat) + ADAM_EPS) + ADAM_WD * w)
    return (g, delta, m2, v2), ()


def _adam_call(name, g8, w, m, v):
    r, wd = w.shape
    tb = r
    for cand in range(16, r + 1, 16):
        if r % cand == 0 and cand * wd <= 256 * 1024:
            tb = cand
    o = ((r, wd), f32, (tb, wd), lambda n: (n, 0))
    return _seq_fwd(name, _adam_step, [(g8, (N_DEV, tb, wd), lambda n: (0, n, 0)), _r(w, tb, wd), _r(m, tb, wd), _r(v, tb, wd)],
                    [], [], [], [o, o, o, o], r // tb)


WEIGHTS = ['meta_tokens', 'w_in', 'w_branch', 'w_out', 'norm_pre_mix', 'norm_post_mix', 'norm_pre_mlp', 'norm_post_mlp',
           'w_up', 'w_down', 'ret_gn_w', 's5_lam_re', 's5_lam_im', 's5_b_re', 's5_b_im', 's5_c_re', 's5_c_im', 's5_d',
           's5_log_step', 's5_glu_w', 's5_glu_b', 'ssd_conv_w', 'ssd_conv_b', 'ssd_dt_bias', 'ssd_a_log', 'ssd_d',
           'ssd_norm_w', 'hgrn_lb', 'hgrn_norm_w']
SHARDED = [("w_in", 2), ("w_branch", 3), ("w_out", 1), ("w_up", 2), ("w_down", 1), ("s5_glu_w", 2), ("meta_tokens", 1),
           ("ssd_conv_w", 2)]
N_BF16 = 6
REPL = [n for n in WEIGHTS if n not in dict(SHARDED)]
SMALL_PAD = 512
WIRE = bf16
SHARD_COLS = IN_DIM // N_DEV


def _col_pieces():
    out, pos = [], 0
    for a, b in _orig_col_slices() + [(3584, 3592)]:
        if a == 3584:
            pos = OFF["dt"]
        while a < b:
            e = min(b, (a // SHARD_COLS + 1) * SHARD_COLS)
            out.append((a, e, pos))
            pos += e - a
            a = e
    return out


def _w_in_from_shards(got_l):
    parts = [got_l[a // SHARD_COLS][:, a % SHARD_COLS:a % SHARD_COLS + (b - a)] for a, b, _ in _col_pieces()]
    parts.append(jnp.zeros((got_l.shape[1], NP - IN_DIM - (OFF["dt"] - 9728)), got_l.dtype))
    return jnp.concatenate(parts, axis=1)


def _w_in_to_shards(g):
    pieces = sorted(_col_pieces())
    blocks = []
    for d in range(N_DEV):
        blocks.append(jnp.concatenate([g[:, m:m + (b - a)] for a, b, m in pieces if a // SHARD_COLS == d], axis=1))
    return jnp.stack(blocks, axis=0)


def _to8(full, axis):
    sh = full.shape
    return jnp.moveaxis(full.reshape(sh[:axis] + (N_DEV, sh[axis] // N_DEV) + sh[axis + 1:]), axis, 0)


def _from8(g8, axis):
    r = jnp.moveaxis(g8, 0, axis)
    sh = r.shape
    return r.reshape(sh[:axis] + (sh[axis] * sh[axis + 1],) + sh[axis + 2:])


def _pack(arrs, pad_rows):
    flat = jnp.concatenate([a.reshape(-1) for a in arrs])
    n = flat.shape[0]
    total = -(-n // (128 * pad_rows)) * (128 * pad_rows)
    if total != n:
        flat = jnp.concatenate([flat, jnp.zeros((total - n,), flat.dtype)])
    return flat.reshape(total // 128, 128)


def _unpack(flat, shapes):
    v = flat.reshape(-1)
    out, pos = [], 0
    for s in shapes:
        n = int(np.prod(s))
        out.append(v[pos:pos + n].reshape(tuple(s)))
        pos += n
    return out


def _rows2d(a, lead=0):
    return a.reshape(a.shape[:lead] + (-1, a.shape[-1]))


def _local_step(x0, tgt0, wf, first_w, next_w, send_grads):
    seq = x0.shape[0]
    t = N_META + seq
    tp = -(-t // CHUNK) * CHUNK
    tabs = _tables(tp)
    lb_in = jnp.concatenate([wf["hgrn_lb"], jnp.zeros((8 - DEPTH, BW), f32)], axis=0)
    (lb_all,) = _seq_fwd("lb_prep", _lb_step, [_full(lb_in)], [], [tabs["lmat"]], [], [((8, BW), f32, (8, BW), lambda n: (0, 0))], 1)

    def pad128(a):
        return jnp.concatenate([a, jnp.zeros((128 - a.shape[0],), f32)]).reshape(1, 128)

    def layer_params(l, big):
        return dict(
            big,
            npm=wf["norm_pre_mix"][l].reshape(1, D_MODEL), npostmix=wf["norm_post_mix"][l].reshape(1, D_MODEL),
            npremlp=wf["norm_pre_mlp"][l].reshape(1, D_MODEL), npostmlp=wf["norm_post_mlp"][l].reshape(1, D_MODEL),
            ret_gn=wf["ret_gn_w"][l].reshape(1, BW), lam_re=wf["s5_lam_re"][l], lam_im=wf["s5_lam_im"][l],
            lstep=wf["s5_log_step"][l].reshape(S5_G, 1),
            bt_re=wf["s5_b_re"][l].transpose(0, 2, 1).reshape(S5_G * S5_J, S5_P),
            bt_im=wf["s5_b_im"][l].transpose(0, 2, 1).reshape(S5_G * S5_J, S5_P),
            c_re=wf["s5_c_re"][l], c_im=wf["s5_c_im"][l], s5_d=wf["s5_d"][l].reshape(1, BW),
            glu_b=wf["s5_glu_b"][l].reshape(1, 2 * BW), conv_w=wf["ssd_conv_w"][l],
            conv_b=wf["ssd_conv_b"][l].reshape(1, 1024), dt_bias=pad128(wf["ssd_dt_bias"][l]),
            a_log=pad128(wf["ssd_a_log"][l]), ssd_d=pad128(wf["ssd_d"][l]), ssd_nw=wf["ssd_norm_w"][l].reshape(1, BW),
            lb=lb_all[l].reshape(1, BW), hg_nw=wf["hgrn_norm_w"][l].reshape(1, BW))

    zpad = jnp.zeros((tp - t, D_MODEL), f32)
    h = jnp.concatenate([wf["meta_tokens"], x0, zpad], axis=0)
    tgt = jnp.concatenate([jnp.zeros((N_META, D_MODEL), f32), tgt0, zpad], axis=0)
    params, saved, big = [], [], first_w
    for l in range(DEPTH):
        params.append(layer_params(l, big))
        carry, arrived = next_w(l) if l + 1 < DEPTH else ({}, None)
        h, sv = _layer_fwd(h, params[l], tabs, "_l%d" % l, carry)
        saved.append(sv)
        big = arrived() if arrived else None
    dh, loss_local = _loss_call(h, tgt, N_META, t)
    g, carry = [None] * DEPTH, {}
    for l in reversed(range(DEPTH)):
        dh, g[l] = _layer_bwd(dh, params[l], saved[l], tabs, "_l%d" % l, carry)
        carry = send_grads(l, g[l]) if l > 0 else {}
    d_lb = jnp.concatenate([jnp.concatenate([gl["lb"] for gl in g], axis=0), jnp.zeros((8 - DEPTH, BW), f32)], axis=0)
    (d_hgrn_lb,), _ = _seq_bwd("lb_prep_bwd", _lb_step, [_full(lb_in)], [True], [], [tabs["lmat"]], [], [_full(d_lb)], 1)
    return loss_local, dh, g, d_hgrn_lb[:DEPTH]


def kernel(x, meta_tokens, w_in, w_branch, w_out, norm_pre_mix, norm_post_mix, norm_pre_mlp, norm_post_mlp, w_up, w_down, ret_gn_w, s5_lam_re, s5_lam_im, s5_b_re, s5_b_im, s5_c_re, s5_c_im, s5_d, s5_log_step, s5_glu_w, s5_glu_b, ssd_conv_w, ssd_conv_b, ssd_dt_bias, ssd_a_log, ssd_d, ssd_norm_w, hgrn_lb, hgrn_norm_w, loss_target, m_meta_tokens, m_w_in, m_w_branch, m_w_out, m_norm_pre_mix, m_norm_post_mix, m_norm_pre_mlp, m_norm_post_mlp, m_w_up, m_w_down, m_ret_gn_w, m_s5_lam_re, m_s5_lam_im, m_s5_b_re, m_s5_b_im, m_s5_c_re, m_s5_c_im, m_s5_d, m_s5_log_step, m_s5_glu_w, m_s5_glu_b, m_ssd_conv_w, m_ssd_conv_b, m_ssd_dt_bias, m_ssd_a_log, m_ssd_d, m_ssd_norm_w, m_hgrn_lb, m_hgrn_norm_w, v_meta_tokens, v_w_in, v_w_branch, v_w_out, v_norm_pre_mix, v_norm_post_mix, v_norm_pre_mlp, v_norm_post_mlp, v_w_up, v_w_down, v_ret_gn_w, v_s5_lam_re, v_s5_lam_im, v_s5_b_re, v_s5_b_im, v_s5_c_re, v_s5_c_im, v_s5_d, v_s5_log_step, v_s5_glu_w, v_s5_glu_b, v_ssd_conv_w, v_ssd_conv_b, v_ssd_dt_bias, v_ssd_a_log, v_ssd_d, v_ssd_norm_w, v_hgrn_lb, v_hgrn_norm_w):
    args = dict(locals())
    w = {n: args[n] for n in WEIGHTS}
    mom = {n: args["m_" + n] for n in WEIGHTS}
    var = {n: args["v_" + n] for n in WEIGHTS}
    names = [n for n, _ in SHARDED]
    big = names[:N_BF16]

    def assemble(got):
        return dict(w_in=_w_in_from_shards(got["w_in"]), w_branch=_from8(got["w_branch"], 2), w_out=_from8(got["w_out"], 0),
                    w_up=_from8(got["w_up"], 1), w_down=_from8(got["w_down"], 0), glu_w=_from8(got["s5_glu_w"], 1))

    shard = {n: [w[n][l].astype(bf16) for l in range(DEPTH)] for n in big}
    got = _all_gather("gather_first", [shard[n][0] for n in big] + [w["meta_tokens"], w["ssd_conv_w"]])
    wf = {n: w[n] for n in REPL}
    wf["meta_tokens"] = _from8(got[N_BF16], 1)
    wf["ssd_conv_w"] = _from8(got[N_BF16 + 1], 2)

    def next_w(l):
        carry = dict(s5=_Carry("gather", [shard["w_in"][l + 1]]), ssd=_Carry("gather", [shard["w_up"][l + 1]]),
                     hg=_Carry("gather", [shard["w_down"][l + 1]]),
                     ret=_Carry("gather", [shard[n][l + 1] for n in ("w_branch", "w_out", "s5_glu_w")]))
        return carry, lambda: assemble(dict(w_in=carry["s5"].result[0], w_up=carry["ssd"].result[0], w_down=carry["hg"].result[0],
                                            **dict(zip(("w_branch", "w_out", "s5_glu_w"), carry["ret"].result))))

    def to_wire(gl):
        return dict(w_in=_w_in_to_shards(gl["w_in"]).astype(WIRE), w_branch=_to8(gl["w_branch"], 2).astype(WIRE),
                    w_out=_to8(gl["w_out"], 0).astype(WIRE), w_up=_to8(gl["w_up"], 1).astype(WIRE),
                    w_down=_to8(gl["w_down"], 0).astype(WIRE), s5_glu_w=_to8(gl["glu_w"], 1).astype(WIRE))

    sent = [None] * DEPTH

    def send_grads(l, gl):
        wire = to_wire(gl)
        carry = dict(s5=_Carry("scatter", [wire["w_in"]]), mlp=_Carry("scatter", [wire["w_up"]]),
                     ssd=_Carry("scatter", [wire["w_down"], wire["w_branch"]]),
                     hg=_Carry("scatter", [wire["w_out"], wire["s5_glu_w"]]))
        sent[l] = lambda: dict(w_in=carry["s5"].result[0], w_up=carry["mlp"].result[0], w_down=carry["ssd"].result[0],
                               w_branch=carry["ssd"].result[1], w_out=carry["hg"].result[0], s5_glu_w=carry["hg"].result[1])
        return carry

    loss_local, dh0, g, d_hgrn_lb = _local_step(x[0], loss_target[0], wf, assemble(dict(zip(big, got[:N_BF16]))), next_w, send_grads)
    seq = x.shape[1]
    t = N_META + seq
    wire0 = to_wire(g[0])
    last = _exchange("scatter_last", [wire0[n] for n in big] + [_to8(dh0[:N_META], 1).astype(WIRE),
                                                                 jnp.stack([_to8(gl["conv_w"], 1) for gl in g], axis=1).astype(WIRE)])
    per_layer = [dict(zip(big, last[:N_BF16]))] + [sent[l]() for l in range(1, DEPTH)]
    parts = {n: jnp.stack([pl_[n] for pl_ in per_layer], axis=1) for n in big}
    parts["meta_tokens"], parts["ssd_conv_w"] = last[N_BF16], last[N_BF16 + 1]
    out = {k: {} for k in ("grad", "delta", "m", "v")}
    for n in names:
        res = _adam_call("adamw_" + n, _rows2d(parts[n], 1), _rows2d(w[n]), _rows2d(mom[n]), _rows2d(var[n]))
        for k, r in zip(("grad", "delta", "m", "v"), res):
            out[k][n] = r.reshape(w[n].shape)

    def stack_l(key, shape):
        return jnp.stack([gl[key].reshape(shape) for gl in g], axis=0)

    small = dict(
        norm_pre_mix=stack_l("npm", (D_MODEL,)), norm_post_mix=stack_l("npostmix", (D_MODEL,)),
        norm_pre_mlp=stack_l("npremlp", (D_MODEL,)), norm_post_mlp=stack_l("npostmlp", (D_MODEL,)),
        ret_gn_w=stack_l("ret_gn", (BW,)), s5_lam_re=stack_l("lam_re", (S5_G, S5_P)), s5_lam_im=stack_l("lam_im", (S5_G, S5_P)),
        s5_b_re=stack_l("bt_re", (S5_G, S5_J, S5_P)).transpose(0, 1, 3, 2),
        s5_b_im=stack_l("bt_im", (S5_G, S5_J, S5_P)).transpose(0, 1, 3, 2),
        s5_c_re=stack_l("c_re", (S5_G, S5_J, S5_P)), s5_c_im=stack_l("c_im", (S5_G, S5_J, S5_P)),
        s5_d=stack_l("s5_d", (BW,)), s5_log_step=stack_l("lstep", (S5_G,)), s5_glu_b=stack_l("glu_b", (2 * BW,)),
        ssd_conv_b=stack_l("conv_b", (1024,)), ssd_dt_bias=stack_l("dt_bias", (128,))[:, :SSD_HEADS],
        ssd_a_log=stack_l("a_log", (128,))[:, :SSD_HEADS], ssd_d=stack_l("ssd_d", (128,))[:, :SSD_HEADS],
        ssd_norm_w=stack_l("ssd_nw", (BW,)), hgrn_lb=d_hgrn_lb, hgrn_norm_w=stack_l("hg_nw", (BW,)))
    (parts_small,) = _all_gather("gather_small_grads", [_pack([small[n] for n in REPL], SMALL_PAD)])
    res = _adam_call("adamw_replicated", parts_small, _pack([w[n] for n in REPL], SMALL_PAD),
                     _pack([mom[n] for n in REPL], SMALL_PAD), _pack([var[n] for n in REPL], SMALL_PAD))
    for k, r in zip(("grad", "delta", "m", "v"), res):
        out[k].update(zip(REPL, _unpack(r, [w[n].shape for n in REPL])))

    loss = lax.psum(loss_local, ("x", "y", "c"))
    return (loss, dh0[N_META:t][None], *[out["grad"][n] for n in WEIGHTS], *[out["delta"][n] for n in WEIGHTS],
            *[out["m"][n] for n in WEIGHTS], *[out["v"][n] for n in WEIGHTS])
```

```python
import numpy as np
import jax
import jax.numpy as jnp
from jax import lax
from jax.experimental import pallas as pl
from jax.experimental.pallas import tpu as pltpu

f32 = jnp.float32
bf16 = jnp.bfloat16
HI = lax.Precision.HIGHEST

D_MODEL = 1024
N_META = 16
DEPTH = 4
BW = 512
D_FF = 4096
EPS = 1e-6
N_DEV = 8
RET_HEADS = 4
SSD_HEADS = 8
SSD_GROUPS = 2
HG_HEADS = 4
S5_G, S5_J, S5_P = 32, 16, 64

ADAM_LR, ADAM_B1, ADAM_B2, ADAM_EPS, ADAM_WD, ADAM_STEP = 0.001, 0.9, 0.999, 1e-08, 0.01, 10

CHUNK = 64
HG_SUB = 16
VMEM_LIMIT = 56 * 1024 * 1024

OFF = dict(gates=0, rq=4096, rk=4352, rv=4608, rg=5120, s5u=5632, sxbc=6144, sz=7168, hq=7680, hf=8192,
           hi=8704, hg=9216, dt=9728)
NP = 9856
IN_DIM = 9736


def _orig_col_slices():
    sl = [(5640, 9736)]
    for base in (0, 256):
        for half in (0, 32):
            for h in range(4):
                sl.append((base + 64 * h + half, base + 64 * h + half + 32))
    sl.append((512, 1024))
    sl.append((1024, 1536))
    sl.append((1536, 2048))
    sl.append((2560, 3584))
    sl.append((2048, 2560))
    sl.append((3592, 5640))
    return sl


def _bdot(a, b):
    return jnp.dot(a.astype(bf16), b.astype(bf16), preferred_element_type=f32)


def _bdot_nt(a, b):
    return lax.dot_general(a.astype(bf16), b.astype(bf16), (((1,), (1,)), ((), ())), preferred_element_type=f32)


def _bdot_tn(a, b):
    return lax.dot_general(a.astype(bf16), b.astype(bf16), (((0,), (0,)), ((), ())), preferred_element_type=f32)


def _hdot(a, b):
    return jnp.dot(a, b, precision=HI, preferred_element_type=f32)


def _sig(x):
    return jax.nn.sigmoid(x)


def _rms(x, w):
    return x * lax.rsqrt(jnp.mean(x * x, axis=-1, keepdims=True) + EPS) * w


def _softplus(x):
    return jnp.maximum(x, 0.0) + jnp.log(1.0 + jnp.exp(-jnp.abs(x)))


def _r(arr, rb, w, jb=0):
    return (arr, (rb, w), lambda n, jb=jb: (n, jb))


def _full(arr):
    nd = arr.ndim
    return (arr, arr.shape, lambda n, nd=nd: (0,) * nd)


def _seq_fwd(name, step, rows, consts, nds, states, outs, n_chunks, save_states=False, carry=None):
    nr, nc, nn, ns, no = len(rows), len(consts), len(nds), len(states), len(outs)
    whole = list(consts) + list(nds)

    def body(*refs):
        row_refs = refs[:nr]
        whole_hbm = refs[nr:nr + nc + nn]
        out_refs = refs[nr + nc + nn:nr + nc + nn + no]
        k = nr + nc + nn + no
        saved_refs = refs[k:k + (ns if save_states else 0)]
        k += ns if save_states else 0
        whole_vmem = refs[k:k + nc + nn]
        state_refs = refs[k + nc + nn:]
        n = pl.program_id(0)

        @pl.when(n == 0)
        def _():
            for src, dst in zip(whole_hbm, whole_vmem):
                pltpu.sync_copy(src, dst)
            for s in state_refs:
                s[...] = jnp.zeros_like(s)

        st = tuple(s[...] for s in state_refs)
        if save_states:
            for sv, v in zip(saved_refs, st):
                sv[0] = v
        o, new = step(tuple(r[...] for r in row_refs), tuple(c[...] for c in whole_vmem[:nc]),
                      tuple(c[...] for c in whole_vmem[nc:]), st, n)
        for ref, v in zip(out_refs, o):
            ref[...] = v.astype(ref.dtype)
        for s, v in zip(state_refs, new):
            s[...] = v

    in_specs = [pl.BlockSpec(bs, im) for _, bs, im in rows] + [pl.BlockSpec(memory_space=pl.ANY)] * (nc + nn)
    out_shape = [jax.ShapeDtypeStruct(s, d) for s, d, _, _ in outs]
    out_specs = [pl.BlockSpec(bs, im) for _, _, bs, im in outs]
    if save_states:
        for s in states:
            out_shape.append(jax.ShapeDtypeStruct((n_chunks,) + tuple(s), f32))
            out_specs.append(pl.BlockSpec((1,) + tuple(s), lambda n, z=len(s): (n,) + (0,) * z))
    scratch = [pltpu.VMEM(a.shape, a.dtype) for a in whole] + [pltpu.VMEM(tuple(s), f32) for s in states]
    return _pcall(body, name, (n_chunks,), in_specs, out_specs, out_shape, scratch, [a for a, _, _ in rows] + whole,
                  ("arbitrary",), carry)


def _seq_bwd(name, step, rows, row_diff, consts, nds, saved, couts, n_chunks, carry=None):
    nr, nc, nn, ns, no = len(rows), len(consts), len(nds), len(saved), len(couts)
    whole = list(consts) + list(nds)
    didx = [i for i in range(nr) if row_diff[i]]

    def rev(im):
        return lambda n: im(n_chunks - 1 - n)

    def body(*refs):
        row_refs = refs[:nr]
        whole_hbm = refs[nr:nr + nc + nn]
        k = nr + nc + nn
        saved_refs = refs[k:k + ns]
        k += ns
        cout_refs = refs[k:k + no]
        k += no
        drow_refs = refs[k:k + len(didx)]
        k += len(didx)
        dconst_hbm = refs[k:k + nc]
        k += nc
        whole_vmem = refs[k:k + nc + nn]
        k += nc + nn
        dconst_acc = refs[k:k + nc]
        k += nc
        dstate_refs = refs[k:]
        n = pl.program_id(0)

        @pl.when(n == 0)
        def _():
            for src, dst in zip(whole_hbm, whole_vmem):
                pltpu.sync_copy(src, dst)
            for a in dconst_acc:
                a[...] = jnp.zeros_like(a)
            for s in dstate_refs:
                s[...] = jnp.zeros_like(s)

        rvals = tuple(r[...] for r in row_refs)
        cvals = tuple(c[...] for c in whole_vmem[:nc])
        nvals = tuple(c[...] for c in whole_vmem[nc:])
        svals = tuple(s[0] for s in saved_refs)
        cidx = n_chunks - 1 - n

        def f(dr, cv, sv):
            full = list(rvals)
            for i, v in zip(didx, dr):
                full[i] = v
            return step(tuple(full), cv, nvals, sv, cidx)

        (o, _), vf = jax.vjp(f, tuple(rvals[i] for i in didx), cvals, svals)
        ct_o = tuple(c[...].astype(v.dtype) for c, v in zip(cout_refs, o))
        ct_s = tuple(s[...] for s in dstate_refs)
        d_rows, d_consts, d_states = vf((ct_o, ct_s))
        for ref, v in zip(drow_refs, d_rows):
            ref[...] = v.astype(ref.dtype)
        for a, v in zip(dconst_acc, d_consts):
            a[...] += v.astype(f32)
        for s, v in zip(dstate_refs, d_states):
            s[...] = v

        @pl.when(n == n_chunks - 1)
        def _():
            for a, dst in zip(dconst_acc, dconst_hbm):
                pltpu.sync_copy(a, dst)

    in_specs = ([pl.BlockSpec(bs, rev(im)) for _, bs, im in rows]
                + [pl.BlockSpec(memory_space=pl.ANY)] * (nc + nn)
                + [pl.BlockSpec((1,) + a.shape[1:], lambda n, z=a.ndim - 1: (n_chunks - 1 - n,) + (0,) * z) for a in saved]
                + [pl.BlockSpec(bs, rev(im)) for _, bs, im in couts])
    out_shape, out_specs = [], []
    for i in didx:
        a, bs, im = rows[i]
        nrows = a.shape[0]
        out_shape.append(jax.ShapeDtypeStruct((nrows,) + tuple(bs[1:]), f32))
        out_specs.append(pl.BlockSpec(bs, (lambda im: lambda n: (im(n_chunks - 1 - n)[0],) + (0,) * (len(bs) - 1))(im)))
    for c in consts:
        out_shape.append(jax.ShapeDtypeStruct(c.shape, f32))
        out_specs.append(pl.BlockSpec(memory_space=pl.ANY))
    scratch = ([pltpu.VMEM(a.shape, a.dtype) for a in whole] + [pltpu.VMEM(c.shape, f32) for c in consts]
               + [pltpu.VMEM(a.shape[1:], f32) for a in saved])
    res = _pcall(body, name, (n_chunks,), in_specs, out_specs, out_shape, scratch,
                 [a for a, _, _ in rows] + whole + list(saved) + [a for a, _, _ in couts], ("arbitrary",), carry)
    return res[:len(didx)], res[len(didx):]


def _mm(name, a, b, mode, tm, tn, tk, precision=None, carry=None):
    if mode == "nn":
        (m, kd), nn_ = a.shape, b.shape[1]
        a_spec = pl.BlockSpec((tm, tk), lambda i, j, k: (i, k))
        b_spec = pl.BlockSpec((tk, tn), lambda i, j, k: (k, j))
        dims = (((1,), (0,)), ((), ()))
    elif mode == "tn":
        (kd, m), nn_ = a.shape, b.shape[1]
        a_spec = pl.BlockSpec((tk, tm), lambda i, j, k: (k, i))
        b_spec = pl.BlockSpec((tk, tn), lambda i, j, k: (k, j))
        dims = (((0,), (0,)), ((), ()))
    else:
        (m, kd), nn_ = a.shape, b.shape[0]
        a_spec = pl.BlockSpec((tm, tk), lambda i, j, k: (i, k))
        b_spec = pl.BlockSpec((tn, tk), lambda i, j, k: (j, k))
        dims = (((1,), (1,)), ((), ()))
    assert m % tm == 0 and nn_ % tn == 0 and kd % tk == 0, (name, a.shape, b.shape, tm, tn, tk)
    nk = kd // tk

    def body(a_ref, b_ref, o_ref, acc):
        k = pl.program_id(2)

        @pl.when(k == 0)
        def _():
            acc[...] = jnp.zeros_like(acc)

        if precision is None:
            acc[...] += lax.dot_general(a_ref[...].astype(bf16), b_ref[...].astype(bf16), dims, preferred_element_type=f32)
        else:
            acc[...] += lax.dot_general(a_ref[...], b_ref[...], dims, precision=precision, preferred_element_type=f32)

        @pl.when(k == nk - 1)
        def _():
            o_ref[...] = acc[...]

    return _pcall(body, name, (m // tm, nn_ // tn, nk), [a_spec, b_spec], [pl.BlockSpec((tm, tn), lambda i, j, k: (i, j))],
                  [jax.ShapeDtypeStruct((m, nn_), f32)], [pltpu.VMEM((tm, tn), f32)], [a, b],
                  ("parallel", "parallel", "arbitrary"), carry)[0]


def _div_tile(n, want, mult):
    best = None
    for t in range(mult, min(n, want) + 1, mult):
        if n % t == 0:
            best = t
    return best if best is not None else n


class _Carry:
    def __init__(self, kind, srcs):
        self.kind, self.srcs, self.n, self.result = kind, list(srcs), len(srcs), None

    def out_shapes(self):
        if self.kind == "scatter":
            return [jax.ShapeDtypeStruct(s.shape, s.dtype) for s in self.srcs]
        return [jax.ShapeDtypeStruct((N_DEV,) + tuple(s.shape), s.dtype) for s in self.srcs]

    def sems(self):
        k = N_DEV if self.kind == "scatter" else N_DEV - 1
        return [pltpu.SemaphoreType.DMA((self.n, k)), pltpu.SemaphoreType.DMA((self.n, k)), pltpu.SemaphoreType.DMA((self.n,))]

    def _sc(self, src, dst, sems, i, j, slot):
        return pltpu.make_async_remote_copy(
            src_ref=src[i].at[j], dst_ref=dst[i].at[slot], send_sem=sems[0].at[i, j], recv_sem=sems[1].at[i, slot],
            device_id=(j // 4, (j // 2) % 2, j % 2), device_id_type=pl.DeviceIdType.MESH)

    def _where(self):
        x, y, c = lax.axis_index("x"), lax.axis_index("y"), lax.axis_index("c")
        return (x, y, c), (x, y, 1 - c), [(1 - x, y), (x, 1 - y), (1 - x, 1 - y)], c

    def _gc(self, src, dst, sems, i, k, block, to, own=False):
        slot = dst[i].at[4 * block[0] + 2 * block[1] + block[2]]
        return pltpu.make_async_remote_copy(
            src_ref=src[i] if own else slot, dst_ref=slot, send_sem=sems[0].at[i, k], recv_sem=sems[1].at[i, k],
            device_id=to, device_id_type=pl.DeviceIdType.MESH)

    def _local(self, src, dst, sems, i):
        if self.kind == "scatter":
            me = 4 * lax.axis_index("x") + 2 * lax.axis_index("y") + lax.axis_index("c")
            return pltpu.make_async_copy(src[i].at[me], dst[i].at[me], sems[2].at[i])
        x, y, c = lax.axis_index("x"), lax.axis_index("y"), lax.axis_index("c")
        return pltpu.make_async_copy(src[i], dst[i].at[4 * x + 2 * y + c], sems[2].at[i])

    def _first(self, src, dst, sems):
        me, sibling, chips, c = self._where()
        out = []
        for i in range(self.n):
            out.append(self._gc(src, dst, sems, i, 0, me, sibling, own=True))
            out += [self._gc(src, dst, sems, i, 1 + j, me, (*chip, c), own=True) for j, chip in enumerate(chips)]
        return out

    def start(self, src, dst, sems):
        for i in range(self.n):
            self._local(src, dst, sems, i).start()
        if self.kind == "scatter":
            me = 4 * lax.axis_index("x") + 2 * lax.axis_index("y") + lax.axis_index("c")
            for j in range(N_DEV):
                @pl.when(j != me)
                def _(j=j):
                    for i in range(self.n):
                        self._sc(src, dst, sems, i, j, me).start()
        else:
            for cp in self._first(src, dst, sems):
                cp.start()

    def finish(self, src, dst, sems):
        if self.kind == "scatter":
            me = 4 * lax.axis_index("x") + 2 * lax.axis_index("y") + lax.axis_index("c")
            for j in range(N_DEV):
                @pl.when(j != me)
                def _(j=j):
                    for i in range(self.n):
                        self._sc(src, dst, sems, i, j, j).wait_recv()
                        self._sc(src, dst, sems, i, j, me).wait_send()
        else:
            me, sibling, chips, c = self._where()
            passed = []
            for j, chip in enumerate(chips):
                for i in range(self.n):
                    self._gc(src, dst, sems, i, 1 + j, (*chip, c), me).wait_recv()
                    fwd = self._gc(src, dst, sems, i, 4 + j, (*chip, c), sibling)
                    fwd.start()
                    passed.append(fwd)
            for i in range(self.n):
                self._gc(src, dst, sems, i, 0, sibling, me).wait_recv()
                for j, chip in enumerate(chips):
                    self._gc(src, dst, sems, i, 4 + j, (*chip, 1 - c), me).wait_recv()
            for cp in self._first(src, dst, sems) + passed:
                cp.wait_send()
        for i in range(self.n):
            self._local(src, dst, sems, i).wait()


def _pcall(body, name, grid, in_specs, out_specs, out_shape, scratch, operands, dims, carry=None):
    n_in, n_out = len(in_specs), len(out_shape)
    if carry is not None:
        n, inner = carry.n, body

        def body(*refs):
            ins, csrc = refs[:n_in], refs[n_in:n_in + n]
            outs, cdst = refs[n_in + n:n_in + n + n_out], refs[n_in + n + n_out:n_in + 2 * n + n_out]
            rest = refs[n_in + 2 * n + n_out:]
            first = last = True
            for k, size in enumerate(grid):
                first = jnp.logical_and(first, pl.program_id(k) == 0)
                last = jnp.logical_and(last, pl.program_id(k) == size - 1)

            @pl.when(first)
            def _():
                carry.start(csrc, cdst, rest[-3:])

            inner(*ins, *outs, *rest[:-3])

            @pl.when(last)
            def _():
                carry.finish(csrc, cdst, rest[-3:])

        hbm = pl.BlockSpec(memory_space=pltpu.HBM)
        in_specs, out_specs = list(in_specs) + [hbm] * n, list(out_specs) + [hbm] * n
        out_shape, scratch = list(out_shape) + carry.out_shapes(), list(scratch) + carry.sems()
        operands = list(operands) + carry.srcs
    kw = dict(grid=grid) if grid else {}
    res = pl.pallas_call(
        body, name=name, in_specs=in_specs, out_specs=out_specs, out_shape=out_shape, scratch_shapes=scratch,
        compiler_params=pltpu.CompilerParams(dimension_semantics=dims, vmem_limit_bytes=VMEM_LIMIT) if grid else None, **kw,
    )(*operands)
    res = list(res)
    if carry is not None:
        carry.result = res[n_out:]
    return res[:n_out]


def _exchange(name, srcs):
    carry = _Carry("scatter", srcs)
    _pcall(lambda *refs: None, name, (), [], [], [], [], [], None, carry)
    return carry.result


def _all_gather(name, srcs):
    carry = _Carry("gather", srcs)
    _pcall(lambda *refs: None, name, (), [], [], [], [], [], None, carry)
    return carry.result


def _ret_tables(c):
    gam = 1.0 - 2.0 ** (-5.0 - np.arange(RET_HEADS))
    lg = np.log(gam)
    t = np.arange(c)
    dmat = np.where(t[:, None] >= t[None, :], np.exp((t[:, None] - t[None, :])[None] * lg[:, None, None]), 0.0)
    head_v = np.arange(512) // 128
    ysc = np.exp((t[:, None] + 1) * lg[head_v][None, :])
    wtab = np.exp((c - 1 - t)[:, None] * lg[head_v][None, :])
    gtab = np.exp(c * lg[head_v])[None, :]
    head_k = (np.arange(256) % 128) // 32
    mask = (head_k[:, None] == head_v[None, :]).astype(np.float32)
    hm = (head_k[None, None, :] == np.arange(4)[:, None, None]).astype(np.float32)
    return [jnp.asarray(x, f32) for x in (dmat, ysc, wtab, gtab, mask, hm)]


def _rope_tables(tp):
    inv = 10000.0 ** (-np.arange(32, dtype=np.float32) / 32)
    ang = np.arange(tp, dtype=np.float32)[:, None] * inv[None, :]
    cos = np.tile(np.cos(ang), (1, 4)).astype(np.float32)
    sin = np.tile(np.sin(ang), (1, 4)).astype(np.float32)
    return jnp.asarray(cos), jnp.asarray(sin)


def _tri(c):
    t = np.arange(c)
    return jnp.asarray((t[:, None] >= t[None, :]).astype(np.float32))


def _ssd_tables(c):
    sh = np.zeros((3 * c, c + 8), np.float32)
    for d in (3, 2, 1):
        for t in range(c):
            sh[(3 - d) * c + t, 8 + t - d] = 1.0
    e = np.zeros((128, 512), np.float32)
    for h in range(SSD_HEADS):
        e[h, 64 * h:64 * h + 64] = 1.0
    mg = ((np.arange(256) // 128)[:, None] == (np.arange(512) // 256)[None, :]).astype(np.float32)
    cm4 = ((np.arange(256) // 64)[None, None, :] == np.arange(4)[:, None, None]).astype(np.float32)
    return [jnp.asarray(x) for x in (sh, e, mg, cm4)]


def _ret_step(rows, consts, nds, states, _):
    q, k, v, g, cos, sin = rows
    (gnw,) = consts
    dmat, ysc, wtab, gtab, mask, hm = nds
    (s,) = states
    q1, q2, k1, k2 = q[:, :128], q[:, 128:], k[:, :128], k[:, 128:]
    qr = jnp.concatenate([q1 * cos - q2 * sin, q1 * sin + q2 * cos], axis=1)
    kr = jnp.concatenate([k1 * cos - k2 * sin, k1 * sin + k2 * cos], axis=1) * 0.125
    y = _bdot(qr, s) * ysc
    parts = []
    for h in range(RET_HEADS):
        a = _bdot_nt(qr * hm[h], kr) * dmat[h]
        parts.append(_bdot(a, v[:, 128 * h:128 * h + 128]))
    y = y + jnp.concatenate(parts, axis=1)
    s_new = s * gtab + _bdot_tn(kr, v * wtab) * mask
    outs = []
    for h in range(RET_HEADS):
        yh = y[:, 128 * h:128 * h + 128]
        d = yh - jnp.mean(yh, axis=-1, keepdims=True)
        outs.append(d * lax.rsqrt(jnp.mean(d * d, axis=-1, keepdims=True) + EPS))
    yn = jnp.concatenate(outs, axis=1) * gnw
    return (g * _sig(g) * yn,), (s_new,)


def _ssd_step(rows, consts, nds, states, _):
    z, xbc, dt128 = rows
    cw, cb, dtb, alog, dsk, nw = consts
    tri, sh, e, mg, cm4 = nds
    s, tail = states
    c = xbc.shape[0]
    shifted = _hdot(sh, jnp.concatenate([tail, xbc], axis=0))
    xc = (cw[0:1] * shifted[0:c] + cw[1:2] * shifted[c:2 * c] + cw[2:3] * shifted[2 * c:3 * c] + cw[3:4] * xbc + cb)
    xc = xc * _sig(xc)
    xs, bm, cm = xc[:, :512], xc[:, 512:768], xc[:, 768:]
    dt = _softplus(dt128 + dtb)
    la = dt * (-jnp.exp(alog))
    cum = _hdot(tri, la)
    cum_t = cum.T
    dtx, cumx = _hdot(dt, e), _hdot(cum, e)
    lastx = cumx[c - 1:c]
    dskx = _hdot(jnp.broadcast_to(dsk, (8, 128)), e)[0:1]
    v = xs * dtx
    y = _bdot(cm, s) * jnp.exp(cumx)
    lane = lax.broadcasted_iota(jnp.int32, cum.shape, 1)
    sub = lax.broadcasted_iota(jnp.int32, cum_t.shape, 0)
    parts = []
    for grp in range(SSD_GROUPS):
        sg = _bdot_nt(cm[:, 128 * grp:128 * grp + 128], bm[:, 128 * grp:128 * grp + 128])
        vg = v[:, 256 * grp:256 * grp + 256]
        acc = jnp.zeros((c, 256), f32)
        for hh in range(4):
            h = 4 * grp + hh
            col = jnp.sum(jnp.where(lane == h, cum, 0.0), axis=1, keepdims=True)
            row = jnp.sum(jnp.where(sub == h, cum_t, 0.0), axis=0, keepdims=True)
            dec = jnp.exp(jnp.where(tri > 0.5, col - row, -1e30))
            acc = acc + _bdot(sg * dec, vg * cm4[hh])
        parts.append(acc)
    y = y + jnp.concatenate(parts, axis=1) + dskx * xs
    s_new = s * jnp.exp(lastx) + _bdot_tn(bm, v * jnp.exp(lastx - cumx)) * mg
    y = y * (z * _sig(z))
    outs = []
    for grp in range(SSD_GROUPS):
        yg = y[:, 256 * grp:256 * grp + 256]
        outs.append(yg * lax.rsqrt(jnp.mean(yg * yg, axis=-1, keepdims=True) + EPS))
    return (jnp.concatenate(outs, axis=1) * nw,), (s_new, xbc[c - 8:c])


def _hg_step(rows, consts, nds, states, _):
    hq, hf, hi, hgate = rows
    lb, nw = consts
    (tri,) = nds
    (st,) = states
    c = hq.shape[0]
    q = hq * _sig(hq)
    f = lb + (1.0 - lb) * _sig(hf)
    lf = jnp.log(f)
    k = 1.0 - f
    v = hi
    cum = _hdot(tri, lf)
    last = jnp.sum(lf, axis=0, keepdims=True)
    qd = q * jnp.exp(cum)
    kw = k * jnp.exp(last - cum)
    heads = [slice(128 * h, 128 * h + 128) for h in range(HG_HEADS)]
    y_rows = []
    rowid = lax.broadcasted_iota(jnp.int32, (HG_SUB, 512), 0)
    for i in range(c // HG_SUB):
        r0 = HG_SUB * i
        qi, ci, ki, vi = q[r0:r0 + HG_SUB], cum[r0:r0 + HG_SUB], k[r0:r0 + HG_SUB], v[r0:r0 + HG_SUB]
        yi = [jnp.zeros((HG_SUB, 128), f32) for _ in heads]
        for s_ in range(HG_SUB):
            e = qi * ki[s_:s_ + 1] * jnp.exp(jnp.where(rowid >= s_, ci - ci[s_:s_ + 1], -1e30))
            for h, sl in enumerate(heads):
                yi[h] = yi[h] + jnp.sum(e[:, sl], axis=1, keepdims=True) * vi[s_:s_ + 1, sl]
        if i > 0:
            b = cum[r0 - 1:r0]
            qs = qi * jnp.exp(ci - b)
            ks = k[:r0] * jnp.exp(b - cum[:r0])
            for h, sl in enumerate(heads):
                yi[h] = yi[h] + _bdot(_bdot_nt(qs[:, sl], ks[:, sl]), v[:r0, sl])
        y_rows.append(jnp.concatenate(yi, axis=1))
    y_in = jnp.concatenate(y_rows, axis=0)
    y = y_in + jnp.concatenate([_bdot_nt(qd[:, sl], st[:, sl]) for sl in heads], axis=1)
    st_new = jnp.concatenate([st[:, sl] * jnp.exp(last[:, sl]) + _bdot_tn(v[:, sl], kw[:, sl]) for sl in heads], axis=1)
    outs = []
    for sl in heads:
        yh = y[:, sl]
        outs.append(yh * lax.rsqrt(jnp.mean(yh * yh, axis=-1, keepdims=True) + EPS))
    o = jnp.concatenate(outs, axis=1) * nw
    return (o * (hgate * _sig(hgate)),), (st_new,)


def _s5post_step(rows, consts, nds, states, _):
    ycore, u = rows
    dsk, gw, gb = consts
    y = jax.nn.gelu(ycore + dsk * u)
    zz = _bdot(y, gw) + gb
    return (zz[:, :512] * _sig(zz[:, 512:]),), ()


def _s5prep_step(rows, consts, nds, states, _):
    lr, li, lstep, btr, bti = rows
    (rep,) = nds
    step = jnp.exp(lstep)
    mag = jnp.exp(lr * step)
    ab_re, ab_im = mag * jnp.cos(li * step), mag * jnp.sin(li * step)
    inv = 1.0 / (lr * lr + li * li)
    co_re = ((ab_re - 1.0) * lr + ab_im * li) * inv
    co_im = (ab_im * lr - (ab_re - 1.0) * li) * inv
    cre, cim = _hdot(rep, co_re), _hdot(rep, co_im)
    return (ab_re, ab_im, cre * btr - cim * bti, cre * bti + cim * btr), ()


def _lb_step(rows, consts, nds, states, _):
    (x,) = rows
    (lmat,) = nds
    valid = lax.broadcasted_iota(jnp.int32, x.shape, 0) < DEPTH
    xm = jnp.where(valid, x, -1e30)
    ex = jnp.where(valid, jnp.exp(xm - jnp.max(xm, axis=0, keepdims=True)), 0.0)
    sm = ex / jnp.sum(ex, axis=0, keepdims=True)
    return (_hdot(lmat, sm),), ()


def _rmsn_step(rows, consts, nds, states, _):
    (h,) = rows
    (w,) = consts
    return (_rms(h, w),), ()


def _merge_step(rows, consts, nds, states, _):
    yr, ys5, yssd, yhg, gates, h = rows
    wb, wout, npost = consts
    mixed = jnp.zeros(h.shape, f32)
    for n, yb in enumerate((yr, ys5, yssd, yhg)):
        mixed = mixed + _sig(gates[:, 1024 * n:1024 * n + 1024]) * _bdot(yb, wb[n])
    return (h + _rms(_bdot(mixed, wout), npost),), ()


FF_BLK = 512


def _row_blocks(tp):
    rb = _div_tile(tp, 528, 16)
    return rb, tp // rb


def _mlp_core_fwd(tag, u, w_up, w_down):
    tp = u.shape[0]
    rb, nrb = _row_blocks(tp)
    nff = D_FF // FF_BLK

    def body(u_ref, wup_ref, wdown_ref, m_ref):
        j = pl.program_id(0)

        def rows(i, carry):
            r = pl.multiple_of(i * rb, 16)
            a = jnp.dot(u_ref[pl.ds(r, rb), :], wup_ref[...], preferred_element_type=f32)
            part = _bdot(jnp.square(jnp.maximum(a, 0.0)), wdown_ref[...])

            @pl.when(j == 0)
            def _():
                m_ref[pl.ds(r, rb), :] = part

            @pl.when(j > 0)
            def _():
                m_ref[pl.ds(r, rb), :] += part
            return carry
        lax.fori_loop(0, nrb, rows, 0)

    return pl.pallas_call(
        body, name="mlp_core_fwd" + tag, grid=(nff,),
        in_specs=[pl.BlockSpec((tp, D_MODEL), lambda j: (0, 0)), pl.BlockSpec((D_MODEL, FF_BLK), lambda j: (0, j)),
                  pl.BlockSpec((FF_BLK, D_MODEL), lambda j: (j, 0))],
        out_specs=pl.BlockSpec((tp, D_MODEL), lambda j: (0, 0)), out_shape=jax.ShapeDtypeStruct((tp, D_MODEL), f32),
        compiler_params=pltpu.CompilerParams(dimension_semantics=("arbitrary",), vmem_limit_bytes=VMEM_LIMIT),
    )(u, w_up, w_down)


def _mlp_core_bwd(tag, u, dm, w_up, w_down, carry=None):
    tp = u.shape[0]
    rb, nrb = _row_blocks(tp)
    nff = D_FF // FF_BLK

    def body(u_hbm, dm_hbm, wup_ref, wdown_ref, du_hbm, dwup_ref, dwdown_ref, u_s, dm_s, du_s):
        j = pl.program_id(0)

        @pl.when(j == 0)
        def _():
            pltpu.sync_copy(u_hbm, u_s)
            pltpu.sync_copy(dm_hbm, dm_s)

        def rows(i, carry):
            r = pl.multiple_of(i * rb, 16)
            ub = u_s[pl.ds(r, rb), :]
            dmb = dm_s[pl.ds(r, rb), :].astype(bf16)
            a = jnp.dot(ub, wup_ref[...], preferred_element_type=f32)
            ra = jnp.maximum(a, 0.0)
            da = (_bdot_nt(dmb, wdown_ref[...]) * (2.0 * ra)).astype(bf16)
            dwd = _bdot_tn(ra * ra, dmb)
            dwu = _bdot_tn(ub, da)
            dub = _bdot_nt(da, wup_ref[...])

            @pl.when(i == 0)
            def _():
                dwdown_ref[...] = dwd
                dwup_ref[...] = dwu

            @pl.when(i > 0)
            def _():
                dwdown_ref[...] += dwd
                dwup_ref[...] += dwu

            @pl.when(j == 0)
            def _():
                du_s[pl.ds(r, rb), :] = dub

            @pl.when(j > 0)
            def _():
                du_s[pl.ds(r, rb), :] += dub
            return carry
        lax.fori_loop(0, nrb, rows, 0)

        @pl.when(j == nff - 1)
        def _():
            pltpu.sync_copy(du_s, du_hbm)

    anyspec = pl.BlockSpec(memory_space=pl.ANY)
    return _pcall(
        body, "mlp_core_bwd" + tag, (nff,),
        [anyspec, anyspec, pl.BlockSpec((D_MODEL, FF_BLK), lambda j: (0, j)), pl.BlockSpec((FF_BLK, D_MODEL), lambda j: (j, 0))],
        [anyspec, pl.BlockSpec((D_MODEL, FF_BLK), lambda j: (0, j)), pl.BlockSpec((FF_BLK, D_MODEL), lambda j: (j, 0))],
        [jax.ShapeDtypeStruct((tp, D_MODEL), f32), jax.ShapeDtypeStruct((D_MODEL, D_FF), f32), jax.ShapeDtypeStruct((D_FF, D_MODEL), f32)],
        [pltpu.VMEM((tp, D_MODEL), bf16), pltpu.VMEM((tp, D_MODEL), f32), pltpu.VMEM((tp, D_MODEL), f32)],
        [u, dm, w_up, w_down], ("arbitrary",), carry)


def _resid_rms_step(rows, consts, nds, states, _):
    m, h = rows
    (w,) = consts
    return (h + _rms(m, w),), ()


SCAN_RB = 16


def _scan_inplace(xr, xi, ar0, ai0, tp, reverse, pr, pi):
    rb, cs = SCAN_RB, 64
    nb = cs // rb
    w = xr.shape[1]
    nch = tp // cs
    rowid = lax.broadcasted_iota(jnp.int32, (rb, w), 0)

    def mac(cr, ci, sr, si, ar, ai):
        return cr + ar * sr - ai * si, ci + ar * si + ai * sr

    ar, ai = ar0, ai0
    d = 1
    while d < cs:
        arb, aib = jnp.broadcast_to(ar, (rb, w)), jnp.broadcast_to(ai, (rb, w))
        edge = 0 if not reverse else nb - 1

        def level(c, carry, d=d, arb=arb, aib=aib):
            base = pl.multiple_of(c * cs, cs)
            order = range(nb - 1, -1, -1) if not reverse else range(nb)
            for b in order:
                r0 = base + rb * b
                lo = b - d // rb if not reverse else b + d // rb
                if d >= rb:
                    if lo < 0 or lo >= nb:
                        continue
                    src = base + rb * lo
                    sr, si = xr[pl.ds(src, rb), :], xi[pl.ds(src, rb), :]
                elif b == edge:
                    if not reverse:
                        sr = jnp.where(rowid >= d, pltpu.roll(xr[pl.ds(r0, rb), :], d, 0), 0.0)
                        si = jnp.where(rowid >= d, pltpu.roll(xi[pl.ds(r0, rb), :], d, 0), 0.0)
                    else:
                        sr = jnp.where(rowid < rb - d, pltpu.roll(xr[pl.ds(r0, rb), :], rb - d, 0), 0.0)
                        si = jnp.where(rowid < rb - d, pltpu.roll(xi[pl.ds(r0, rb), :], rb - d, 0), 0.0)
                elif not reverse:
                    sr = pltpu.roll(xr[pl.ds(r0 - 8, rb + 8), :], d, 0)[8:, :]
                    si = pltpu.roll(xi[pl.ds(r0 - 8, rb + 8), :], d, 0)[8:, :]
                else:
                    sr = pltpu.roll(xr[pl.ds(r0, rb + 8), :], rb + 8 - d, 0)[:rb, :]
                    si = pltpu.roll(xi[pl.ds(r0, rb + 8), :], rb + 8 - d, 0)[:rb, :]
                nr, ni = mac(xr[pl.ds(r0, rb), :], xi[pl.ds(r0, rb), :], sr, si, arb, aib)
                xr[pl.ds(r0, rb), :] = nr
                xi[pl.ds(r0, rb), :] = ni
            return carry
        lax.fori_loop(0, nch, level, 0)
        ar, ai = ar * ar - ai * ai, 2.0 * ar * ai
        d *= 2

    krow = lax.broadcasted_iota(jnp.int32, (cs, w), 0)
    expo = krow + 1 if not reverse else cs - krow
    tr, ti = jnp.ones((cs, w), f32), jnp.zeros((cs, w), f32)
    qr, qi = ar0, ai0
    for j in range(7):
        on = ((expo >> j) & 1) == 1
        fr, fi = jnp.where(on, qr, 1.0), jnp.where(on, qi, 0.0)
        tr, ti = tr * fr - ti * fi, tr * fi + ti * fr
        qr, qi = qr * qr - qi * qi, 2.0 * qr * qi
    pr[...] = tr
    pi[...] = ti

    def across(i, carry):
        c = i if not reverse else nch - 1 - i
        base = pl.multiple_of(c * cs, cs)
        cr, ci = carry
        out = carry
        for b in range(nb):
            r0 = base + rb * b
            nr, ni = mac(xr[pl.ds(r0, rb), :], xi[pl.ds(r0, rb), :], pr[rb * b:rb * b + rb, :], pi[rb * b:rb * b + rb, :], cr, ci)
            xr[pl.ds(r0, rb), :] = nr
            xi[pl.ds(r0, rb), :] = ni
            if not reverse and b == nb - 1:
                out = (nr[rb - 1:rb, :], ni[rb - 1:rb, :])
            if reverse and b == 0:
                out = (nr[0:1, :], ni[0:1, :])
        return out
    lax.fori_loop(0, nch, across, (jnp.zeros((1, w), f32), jnp.zeros((1, w), f32)))


def _s5_blocks(tp):
    rbm = tp // 8 if (tp // 8) % 8 == 0 and tp % 8 == 0 else tp
    return rbm, tp // rbm


def _s5_in_specs(proj, tp):
    ucol = OFF["s5u"] // 128
    return [
        pl.BlockSpec((tp, 128), lambda cb: (0, ucol + cb)),
        pl.BlockSpec((128, 512), lambda cb: (cb, 0)),
        pl.BlockSpec((128, 512), lambda cb: (cb, 0)),
        pl.BlockSpec((1, 512), lambda cb: (0, cb)),
        pl.BlockSpec((1, 512), lambda cb: (0, cb)),
        pl.BlockSpec((512, 128), lambda cb: (cb, 0)),
        pl.BlockSpec((512, 128), lambda cb: (cb, 0)),
    ]


def _s5_core_fwd(tag, proj, bbr, bbi, ar, ai, ccr, cci, carry=None):
    tp = proj.shape[0]
    rbm, nb = _s5_blocks(tp)

    def body(u_ref, bbr_ref, bbi_ref, ar_ref, ai_ref, ccr_ref, cci_ref, y_ref, xr, xi, pr, pi):
        def fill(i, carry):
            r = pl.multiple_of(i * rbm, 8)
            ub = u_ref[pl.ds(r, rbm), :]
            xr[pl.ds(r, rbm), :] = _bdot(ub, bbr_ref[...])
            xi[pl.ds(r, rbm), :] = _bdot(ub, bbi_ref[...])
            return carry
        lax.fori_loop(0, nb, fill, 0)
        _scan_inplace(xr, xi, ar_ref[...], ai_ref[...], tp, False, pr, pi)

        def out(i, carry):
            r = pl.multiple_of(i * rbm, 8)
            y_ref[pl.ds(r, rbm), :] = (_bdot(xr[pl.ds(r, rbm), :], ccr_ref[...]) - _bdot(xi[pl.ds(r, rbm), :], cci_ref[...]))
            return carry
        lax.fori_loop(0, nb, out, 0)

    return _pcall(body, "s5_core_fwd" + tag, (4,), _s5_in_specs(proj, tp), [pl.BlockSpec((tp, 128), lambda cb: (0, cb))],
                  [jax.ShapeDtypeStruct((tp, 512), f32)], [pltpu.VMEM((tp, 512), f32)] * 2 + [pltpu.VMEM((64, 512), f32)] * 2,
                  [proj, bbr, bbi, ar, ai, ccr, cci], ("arbitrary",), carry)[0]


def _s5_core_bwd(tag, proj, bbr, bbi, ar, ai, ccr, cci, dy, du_post, carry=None):
    tp = proj.shape[0]
    rbm, nb = _s5_blocks(tp)
    rb = SCAN_RB

    def body(u_ref, bbr_ref, bbi_ref, ar_ref, ai_ref, ccr_ref, cci_ref, dy_ref, dup_ref,
             du_ref, dbbr_ref, dbbi_ref, dar_ref, dai_ref, dccr_ref, dcci_ref, xr, xi, gr, gi, pr, pi):
        dccr_ref[...] = jnp.zeros_like(dccr_ref)
        dcci_ref[...] = jnp.zeros_like(dcci_ref)
        dbbr_ref[...] = jnp.zeros_like(dbbr_ref)
        dbbi_ref[...] = jnp.zeros_like(dbbi_ref)

        def fill(i, carry):
            r = pl.multiple_of(i * rbm, 8)
            ub = u_ref[pl.ds(r, rbm), :]
            xr[pl.ds(r, rbm), :] = _bdot(ub, bbr_ref[...])
            xi[pl.ds(r, rbm), :] = _bdot(ub, bbi_ref[...])
            return carry
        lax.fori_loop(0, nb, fill, 0)
        _scan_inplace(xr, xi, ar_ref[...], ai_ref[...], tp, False, pr, pi)

        def seed(i, carry):
            r = pl.multiple_of(i * rbm, 8)
            dyb = dy_ref[pl.ds(r, rbm), :]
            dccr_ref[...] += _bdot_tn(xr[pl.ds(r, rbm), :], dyb)
            dcci_ref[...] -= _bdot_tn(xi[pl.ds(r, rbm), :], dyb)
            gr[pl.ds(r, rbm), :] = _bdot_nt(dyb, ccr_ref[...])
            gi[pl.ds(r, rbm), :] = -_bdot_nt(dyb, cci_ref[...])
            return carry
        lax.fori_loop(0, nb, seed, 0)
        _scan_inplace(gr, gi, ar_ref[...], -ai_ref[...], tp, True, pr, pi)

        w = xr.shape[1]
        rowid = lax.broadcasted_iota(jnp.int32, (rb, w), 0)

        def prods(pr, pi, r0):
            g_r, g_i = gr[pl.ds(r0, rb), :], gi[pl.ds(r0, rb), :]
            return pr * g_r + pi * g_i, pr * g_i - pi * g_r

        def dacc(i, carry):
            r0 = pl.multiple_of((i + 1) * rb, 8)
            lo = pl.multiple_of(r0 - 8, 8)
            pr = pltpu.roll(xr[pl.ds(lo, rb + 8), :], 1, 0)[8:, :]
            pi = pltpu.roll(xi[pl.ds(lo, rb + 8), :], 1, 0)[8:, :]
            a, b = prods(pr, pi, r0)
            return carry[0] + a, carry[1] + b
        pr0 = jnp.where(rowid >= 1, pltpu.roll(xr[pl.ds(0, rb), :], 1, 0), 0.0)
        pi0 = jnp.where(rowid >= 1, pltpu.roll(xi[pl.ds(0, rb), :], 1, 0), 0.0)
        acc_r, acc_i = lax.fori_loop(0, tp // rb - 1, dacc, prods(pr0, pi0, 0))
        dar_ref[...] = jnp.sum(acc_r, axis=0, keepdims=True)
        dai_ref[...] = jnp.sum(acc_i, axis=0, keepdims=True)

        def tail(i, carry):
            r = pl.multiple_of(i * rbm, 8)
            g_r, g_i = gr[pl.ds(r, rbm), :], gi[pl.ds(r, rbm), :]
            ub = u_ref[pl.ds(r, rbm), :]
            du_ref[pl.ds(r, rbm), :] = _bdot_nt(g_r, bbr_ref[...]) + _bdot_nt(g_i, bbi_ref[...]) + dup_ref[pl.ds(r, rbm), :]
            dbbr_ref[...] += _bdot_tn(ub, g_r)
            dbbi_ref[...] += _bdot_tn(ub, g_i)
            return carry
        lax.fori_loop(0, nb, tail, 0)

    col = lambda cb: (0, cb)
    blk = lambda cb: (cb, 0)
    return _pcall(
        body, "s5_core_bwd" + tag, (4,),
        _s5_in_specs(proj, tp) + [pl.BlockSpec((tp, 128), col), pl.BlockSpec((tp, 128), col)],
        [pl.BlockSpec((tp, 128), col), pl.BlockSpec((128, 512), blk), pl.BlockSpec((128, 512), blk),
         pl.BlockSpec((1, 512), col), pl.BlockSpec((1, 512), col), pl.BlockSpec((512, 128), blk), pl.BlockSpec((512, 128), blk)],
        [jax.ShapeDtypeStruct((tp, 512), f32), jax.ShapeDtypeStruct((512, 512), f32), jax.ShapeDtypeStruct((512, 512), f32),
         jax.ShapeDtypeStruct((1, 2048), f32), jax.ShapeDtypeStruct((1, 2048), f32), jax.ShapeDtypeStruct((2048, 128), f32),
         jax.ShapeDtypeStruct((2048, 128), f32)],
        [pltpu.VMEM((tp, 512), f32)] * 4 + [pltpu.VMEM((64, 512), f32)] * 2, [proj, bbr, bbi, ar, ai, ccr, cci, dy, du_post],
        ("arbitrary",), carry)


_EYE8 = np.eye(8, dtype=np.float32)


def _bb_dense(bb):
    return jnp.einsum("cgjp,gh->cgjhp", bb.reshape(4, 8, 16, 64), _EYE8).reshape(512, 512)


def _bb_diag(d):
    return jnp.einsum("cgjhp,gh->cgjp", d.reshape(4, 8, 16, 8, 64), _EYE8).reshape(512, 64)


def _cc_dense(cmat):
    return jnp.einsum("cgjp,gh->cgphj", cmat.reshape(4, 8, 16, 64), _EYE8).reshape(2048, 128)


def _cc_diag(d):
    return jnp.einsum("cgphj,gh->cgjp", d.reshape(4, 8, 64, 8, 16), _EYE8).reshape(32, 16, 64)


def _tables(tp):
    cos, sin = _rope_tables(tp)
    rep = np.zeros((512, 32), np.float32)
    rep[np.arange(512), np.arange(512) // 16] = 1.0
    lmat = np.zeros((8, 8), np.float32)
    for l in range(DEPTH):
        lmat[l, 1:l + 1] = 1.0
    return dict(cos=cos, sin=sin, ret=_ret_tables(CHUNK), tri=_tri(CHUNK), ssd=_ssd_tables(CHUNK),
                rep=jnp.asarray(rep), lmat=jnp.asarray(lmat))


def _tiles(tp):
    return dict(tr=_div_tile(tp, 352, 16), tmg=_div_tile(tp, 192, 16))


def _mixer_rows(proj, c):
    ret = [_r(proj, c, 256, OFF["rq"] // 256), _r(proj, c, 256, OFF["rk"] // 256), _r(proj, c, 512, OFF["rv"] // 512),
           _r(proj, c, 512, OFF["rg"] // 512)]
    ssd = [_r(proj, c, 512, OFF["sz"] // 512), _r(proj, c, 1024, OFF["sxbc"] // 1024), _r(proj, c, 128, OFF["dt"] // 128)]
    hg = [_r(proj, c, 512, OFF[k] // 512) for k in ("hq", "hf", "hi", "hg")]
    return ret, ssd, hg


def _out2(rows, w, rb, dtype=f32):
    return (rows, w), dtype, (rb, w), lambda n: (n, 0)


def _s5_prep_rows(p):
    return [_full(p["lam_re"]), _full(p["lam_im"]), _full(p["lstep"]), _full(p["bt_re"]), _full(p["bt_im"])]


def _s5_consts(p, tabs, tag):
    whole = lambda s: (s, f32, s, lambda n: (0, 0))
    ab_re, ab_im, bb_re, bb_im = _seq_fwd("s5_prep" + tag, _s5prep_step, _s5_prep_rows(p), [], [tabs["rep"]], [],
                                          [whole((32, 64)), whole((32, 64)), whole((512, 64)), whole((512, 64))], 1)
    return (_bb_dense(bb_re).astype(bf16), _bb_dense(bb_im).astype(bf16), ab_re.reshape(1, 2048), ab_im.reshape(1, 2048),
            _cc_dense(p["c_re"]).astype(bf16), _cc_dense(p["c_im"]).astype(bf16))


def _rmsn_step_b(rows, consts, nds, states, n):
    (o,), _ = _rmsn_step(rows, consts, nds, states, n)
    return (o, rows[0]), ()


def _layer_fwd(h, p, tabs, tag, carry):
    tp = h.shape[0]
    c = CHUNK
    nch = tp // c
    tl = _tiles(tp)
    tr, tmg = tl["tr"], tl["tmg"]
    (u,) = _seq_fwd("rms_premix" + tag, _rmsn_step, [_r(h, tr, 1024)], [p["npm"]], [], [], [_out2(tp, 1024, tr, bf16)], tp // tr)
    proj = _mm("in_proj" + tag, u, p["w_in"], "nn", _div_tile(tp, 1056, 16), 1408, 1024, carry=carry.get("in"))
    if "in_done" in carry:
        p = dict(p, **carry["in_done"]())
    ret_rows, ssd_rows, hg_rows = _mixer_rows(proj, c)
    y_ret, ret_s = _seq_fwd("ret_fwd" + tag, _ret_step, ret_rows + [_r(tabs["cos"], c, 128), _r(tabs["sin"], c, 128)],
                            [p["ret_gn"]], tabs["ret"], [(256, 512)], [_out2(tp, 512, c)], nch, save_states=True,
                            carry=carry.get("ret"))
    ycore = _s5_core_fwd(tag, proj, *_s5_consts(p, tabs, tag), carry=carry.get("s5"))
    (y_s5,) = _seq_fwd("s5_post" + tag, _s5post_step, [_r(ycore, tr, 512), _r(proj, tr, 512, OFF["s5u"] // 512)],
                       [p["s5_d"], p["glu_w"], p["glu_b"]], [], [], [_out2(tp, 512, tr)], tp // tr)
    y_ssd, ssd_s, ssd_tail = _seq_fwd(
        "ssd_fwd" + tag, _ssd_step, ssd_rows, [p["conv_w"], p["conv_b"], p["dt_bias"], p["a_log"], p["ssd_d"], p["ssd_nw"]],
        [tabs["tri"]] + tabs["ssd"], [(256, 512), (8, 1024)], [_out2(tp, 512, c)], nch, save_states=True, carry=carry.get("ssd"))
    y_hg, hg_s = _seq_fwd("hg_fwd" + tag, _hg_step, hg_rows, [p["lb"], p["hg_nw"]], [tabs["tri"]], [(128, 512)],
                          [_out2(tp, 512, c)], nch, save_states=True, carry=carry.get("hg"))
    (h_mid,) = _seq_fwd(
        "merge_fwd" + tag, _merge_step,
        [_r(y_ret, tmg, 512), _r(y_s5, tmg, 512), _r(y_ssd, tmg, 512), _r(y_hg, tmg, 512), _r(proj, tmg, 4096, 0),
         _r(h, tmg, 1024)],
        [p["w_branch"], p["w_out"], p["npostmix"]], [], [], [_out2(tp, 1024, tmg)], tp // tmg)
    (u2,) = _seq_fwd("rms_premlp" + tag, _rmsn_step, [_r(h_mid, tr, 1024)], [p["npremlp"]], [], [],
                     [_out2(tp, 1024, tr, bf16)], tp // tr)
    m = _mlp_core_fwd(tag, u2, p["w_up"], p["w_down"])
    (h_new,) = _seq_fwd("mlp_post" + tag, _resid_rms_step, [_r(m, tr, 1024), _r(h_mid, tr, 1024)], [p["npostmlp"]], [], [],
                        [_out2(tp, 1024, tr)], tp // tr)
    saved = dict(h=h, u=u, proj=proj, ret_s=ret_s, ycore=ycore, ssd_s=ssd_s, ssd_tail=ssd_tail, hg_s=hg_s,
                 y_ret=y_ret, y_s5=y_s5, y_ssd=y_ssd, y_hg=y_hg, h_mid=h_mid, u2=u2, m=m)
    return h_new, saved, p


def _layer_bwd(dh, p, sv, tabs, tag, carry):
    tp = dh.shape[0]
    c = CHUNK
    nch = tp // c
    tl = _tiles(tp)
    tr, tmg = tl["tr"], tl["tmg"]
    proj = sv["proj"]
    g = {}
    (d_m, d_hmid), (g["npostmlp"],) = _seq_bwd(
        "mlp_post_bwd" + tag, _resid_rms_step, [_r(sv["m"], tr, 1024), _r(sv["h_mid"], tr, 1024)], [True, True],
        [p["npostmlp"]], [], [], [_r(dh, tr, 1024)], tp // tr)
    d_u2, g["w_up"], g["w_down"] = _mlp_core_bwd(tag, sv["u2"], d_m, p["w_up"], p["w_down"], carry=carry.get("mlp"))
    (d_hmid,), (g["npremlp"],) = _seq_bwd(
        "rms_premlp_bwd" + tag, _rmsn_step_b, [_r(sv["h_mid"], tr, 1024)], [True], [p["npremlp"]], [], [],
        [_r(d_u2, tr, 1024), _r(d_hmid, tr, 1024)], tp // tr)
    (dy_ret, dy_s5, dy_ssd, dy_hg, d_gates, d_h1), (g["w_branch"], g["w_out"], g["npostmix"]) = _seq_bwd(
        "merge_bwd" + tag, _merge_step,
        [_r(sv["y_ret"], tmg, 512), _r(sv["y_s5"], tmg, 512), _r(sv["y_ssd"], tmg, 512), _r(sv["y_hg"], tmg, 512),
         _r(proj, tmg, 4096, 0), _r(sv["h"], tmg, 1024)], [True] * 6,
        [p["w_branch"], p["w_out"], p["npostmix"]], [], [], [_r(d_hmid, tmg, 1024)], tp // tmg)
    ret_rows, ssd_rows, hg_rows = _mixer_rows(proj, c)
    (d_hq, d_hf, d_hi, d_hg), (g["lb"], g["hg_nw"]) = _seq_bwd(
        "hg_bwd" + tag, _hg_step, hg_rows, [True] * 4, [p["lb"], p["hg_nw"]], [tabs["tri"]], [sv["hg_s"]],
        [_r(dy_hg, c, 512)], nch, carry=carry.get("hg"))
    (d_z, d_xbc, d_dt), (g["conv_w"], g["conv_b"], g["dt_bias"], g["a_log"], g["ssd_d"], g["ssd_nw"]) = _seq_bwd(
        "ssd_bwd" + tag, _ssd_step, ssd_rows, [True] * 3,
        [p["conv_w"], p["conv_b"], p["dt_bias"], p["a_log"], p["ssd_d"], p["ssd_nw"]], [tabs["tri"]] + tabs["ssd"],
        [sv["ssd_s"], sv["ssd_tail"]], [_r(dy_ssd, c, 512)], nch, carry=carry.get("ssd"))
    (d_ycore, du_post), (g["s5_d"], g["glu_w"], g["glu_b"]) = _seq_bwd(
        "s5_post_bwd" + tag, _s5post_step, [_r(sv["ycore"], tr, 512), _r(proj, tr, 512, OFF["s5u"] // 512)], [True, True],
        [p["s5_d"], p["glu_w"], p["glu_b"]], [], [], [_r(dy_s5, tr, 512)], tp // tr)
    du_s5, dbbr, dbbi, dar, dai, dccr, dcci = _s5_core_bwd(tag, proj, *_s5_consts(p, tabs, tag + "b"), d_ycore, du_post,
                                                           carry=carry.get("s5"))
    g["c_re"], g["c_im"] = _cc_diag(dccr), _cc_diag(dcci)
    (g["lam_re"], g["lam_im"], g["lstep"], g["bt_re"], g["bt_im"]), _ = _seq_bwd(
        "s5_prep_bwd" + tag, _s5prep_step, _s5_prep_rows(p), [True] * 5, [], [tabs["rep"]], [],
        [_full(dar.reshape(32, 64)), _full(dai.reshape(32, 64)), _full(_bb_diag(dbbr)), _full(_bb_diag(dbbi))], 1)
    (d_q, d_k, d_v, d_g), (g["ret_gn"],) = _seq_bwd(
        "ret_bwd" + tag, _ret_step, ret_rows + [_r(tabs["cos"], c, 128), _r(tabs["sin"], c, 128)], [True] * 4 + [False] * 2,
        [p["ret_gn"]], tabs["ret"], [sv["ret_s"]], [_r(dy_ret, c, 512)], nch)
    dproj = jnp.concatenate([d_gates, d_q, d_k, d_v, d_g, du_s5, d_xbc, d_z, d_hq, d_hf, d_hi, d_hg, d_dt], axis=1).astype(bf16)
    if "late" in carry:
        carry = dict(carry, **carry["late"](g))
    g["w_in"] = _mm("in_proj_dw" + tag, sv["u"], dproj, "tn", 512, 1408, _div_tile(tp, 704, 16), carry=carry.get("dw"))
    du = _mm("in_proj_dx" + tag, dproj, p["w_in"], "nt", _div_tile(tp, 1056, 16), 1024, 1408, carry=carry.get("dx"))
    (dh_prev,), (g["npm"],) = _seq_bwd("rms_premix_bwd" + tag, _rmsn_step_b, [_r(sv["h"], tr, 1024)], [True], [p["npm"]], [], [],
                                       [_r(du, tr, 1024), _r(d_h1, tr, 1024)], tp // tr)
    return dh_prev, g


def _loss_call(h, tgt, lo, hi):
    tp = h.shape[0]
    tr = _div_tile(tp, 352, 16)

    def step(rows, consts, nds, states, n):
        hh, tt = rows
        row = n * tr + lax.broadcasted_iota(jnp.int32, hh.shape, 0)
        err = jnp.where((row >= lo) & (row < hi), hh - tt, 0.0)
        part = 0.5 * jnp.sum(err * err) * (1.0 / D_MODEL)
        return (err * (1.0 / D_MODEL), jnp.zeros((8, 128), f32) + part), ()

    dh, parts = _seq_fwd("loss_head", step, [_r(h, tr, 1024), _r(tgt, tr, 1024)], [], [], [],
                         [_out2(tp, 1024, tr), ((8 * (tp // tr), 128), f32, (8, 128), lambda n: (n, 0))], tp // tr)
    return dh, jnp.sum(parts[::8, 0])


def _adam_step(rows, consts, nds, states, _):
    g8, w, m, v = rows
    g = g8[0].astype(f32)
    for d in range(1, N_DEV):
        g = g + g8[d].astype(f32)
    m2 = ADAM_B1 * m + (1.0 - ADAM_B1) * g
    v2 = ADAM_B2 * v + (1.0 - ADAM_B2) * jnp.square(g)
    m_hat = m2 / (1.0 - ADAM_B1 ** ADAM_STEP)
    v_hat = v2 / (1.0 - ADAM_B2 ** ADAM_STEP)
    delta = -ADAM_LR * (m_hat / (jnp.sqrt(v_hat) + ADAM_EPS) + ADAM_WD * w)
    return (g, delta, m2, v2), ()


def _adam_call(name, g8, w, m, v):
    r, wd = w.shape
    tb = r
    for cand in range(16, r + 1, 16):
        if r % cand == 0 and cand * wd <= 256 * 1024:
            tb = cand
    o = ((r, wd), f32, (tb, wd), lambda n: (n, 0))
    return _seq_fwd(name, _adam_step, [(g8, (N_DEV, tb, wd), lambda n: (0, n, 0)), _r(w, tb, wd), _r(m, tb, wd), _r(v, tb, wd)],
                    [], [], [], [o, o, o, o], r // tb)


WEIGHTS = ['meta_tokens', 'w_in', 'w_branch', 'w_out', 'norm_pre_mix', 'norm_post_mix', 'norm_pre_mlp', 'norm_post_mlp',
           'w_up', 'w_down', 'ret_gn_w', 's5_lam_re', 's5_lam_im', 's5_b_re', 's5_b_im', 's5_c_re', 's5_c_im', 's5_d',
           's5_log_step', 's5_glu_w', 's5_glu_b', 'ssd_conv_w', 'ssd_conv_b', 'ssd_dt_bias', 'ssd_a_log', 'ssd_d',
           'ssd_norm_w', 'hgrn_lb', 'hgrn_norm_w']
SHARDED = [("w_in", 2), ("w_branch", 3), ("w_out", 1), ("w_up", 2), ("w_down", 1), ("s5_glu_w", 2), ("meta_tokens", 1),
           ("ssd_conv_w", 2)]
N_BF16 = 6
REPL = [n for n in WEIGHTS if n not in dict(SHARDED)]
SMALL_PAD = 512
WIRE = bf16
SHARD_COLS = IN_DIM // N_DEV


def _col_pieces():
    out, pos = [], 0
    for a, b in _orig_col_slices() + [(3584, 3592)]:
        if a == 3584:
            pos = OFF["dt"]
        while a < b:
            e = min(b, (a // SHARD_COLS + 1) * SHARD_COLS)
            out.append((a, e, pos))
            pos += e - a
            a = e
    return out


def _w_in_from_shards(got_l):
    parts = [got_l[a // SHARD_COLS][:, a % SHARD_COLS:a % SHARD_COLS + (b - a)] for a, b, _ in _col_pieces()]
    parts.append(jnp.zeros((got_l.shape[1], NP - IN_DIM - (OFF["dt"] - 9728)), got_l.dtype))
    return jnp.concatenate(parts, axis=1)


def _w_in_to_shards(g):
    pieces = sorted(_col_pieces())
    blocks = []
    for d in range(N_DEV):
        blocks.append(jnp.concatenate([g[:, m:m + (b - a)] for a, b, m in pieces if a // SHARD_COLS == d], axis=1))
    return jnp.stack(blocks, axis=0)


def _to8(full, axis):
    sh = full.shape
    return jnp.moveaxis(full.reshape(sh[:axis] + (N_DEV, sh[axis] // N_DEV) + sh[axis + 1:]), axis, 0)


def _from8(g8, axis):
    r = jnp.moveaxis(g8, 0, axis)
    sh = r.shape
    return r.reshape(sh[:axis] + (sh[axis] * sh[axis + 1],) + sh[axis + 2:])


def _pack(arrs, pad_rows):
    flat = jnp.concatenate([a.reshape(-1) for a in arrs])
    n = flat.shape[0]
    total = -(-n // (128 * pad_rows)) * (128 * pad_rows)
    if total != n:
        flat = jnp.concatenate([flat, jnp.zeros((total - n,), flat.dtype)])
    return flat.reshape(total // 128, 128)


def _unpack(flat, shapes):
    v = flat.reshape(-1)
    out, pos = [], 0
    for s in shapes:
        n = int(np.prod(s))
        out.append(v[pos:pos + n].reshape(tuple(s)))
        pos += n
    return out


def _rows2d(a, lead=0):
    return a.reshape(a.shape[:lead] + (-1, a.shape[-1]))


def _local_step(x0, tgt0, wf, first_w, next_w, send_grads):
    seq = x0.shape[0]
    t = N_META + seq
    tp = -(-t // CHUNK) * CHUNK
    tabs = _tables(tp)
    lb_in = jnp.concatenate([wf["hgrn_lb"], jnp.zeros((8 - DEPTH, BW), f32)], axis=0)
    (lb_all,) = _seq_fwd("lb_prep", _lb_step, [_full(lb_in)], [], [tabs["lmat"]], [], [((8, BW), f32, (8, BW), lambda n: (0, 0))], 1)

    def pad128(a):
        return jnp.concatenate([a, jnp.zeros((128 - a.shape[0],), f32)]).reshape(1, 128)

    def layer_params(l, big):
        return dict(
            big,
            npm=wf["norm_pre_mix"][l].reshape(1, D_MODEL), npostmix=wf["norm_post_mix"][l].reshape(1, D_MODEL),
            npremlp=wf["norm_pre_mlp"][l].reshape(1, D_MODEL), npostmlp=wf["norm_post_mlp"][l].reshape(1, D_MODEL),
            ret_gn=wf["ret_gn_w"][l].reshape(1, BW), lam_re=wf["s5_lam_re"][l], lam_im=wf["s5_lam_im"][l],
            lstep=wf["s5_log_step"][l].reshape(S5_G, 1),
            bt_re=wf["s5_b_re"][l].transpose(0, 2, 1).reshape(S5_G * S5_J, S5_P),
            bt_im=wf["s5_b_im"][l].transpose(0, 2, 1).reshape(S5_G * S5_J, S5_P),
            c_re=wf["s5_c_re"][l], c_im=wf["s5_c_im"][l], s5_d=wf["s5_d"][l].reshape(1, BW),
            glu_b=wf["s5_glu_b"][l].reshape(1, 2 * BW), conv_w=wf["ssd_conv_w"][l],
            conv_b=wf["ssd_conv_b"][l].reshape(1, 1024), dt_bias=pad128(wf["ssd_dt_bias"][l]),
            a_log=pad128(wf["ssd_a_log"][l]), ssd_d=pad128(wf["ssd_d"][l]), ssd_nw=wf["ssd_norm_w"][l].reshape(1, BW),
            lb=lb_all[l].reshape(1, BW), hg_nw=wf["hgrn_norm_w"][l].reshape(1, BW))

    zpad = jnp.zeros((tp - t, D_MODEL), f32)
    h = jnp.concatenate([wf["meta_tokens"], x0, zpad], axis=0)
    tgt = jnp.concatenate([jnp.zeros((N_META, D_MODEL), f32), tgt0, zpad], axis=0)
    params, saved, big = [], [], first_w
    for l in range(DEPTH):
        carry, arrived = next_w(l)
        h, sv, p = _layer_fwd(h, layer_params(l, big), tabs, "_l%d" % l, carry)
        params.append(p)
        saved.append(sv)
        big = arrived() if arrived else None
    dh, loss_local = _loss_call(h, tgt, N_META, t)
    g, carry = [None] * DEPTH, {}
    for l in reversed(range(DEPTH)):
        dh, g[l] = _layer_bwd(dh, params[l], saved[l], tabs, "_l%d" % l, carry)
        carry = send_grads(l, g[l]) if l > 0 else {}
    d_lb = jnp.concatenate([jnp.concatenate([gl["lb"] for gl in g], axis=0), jnp.zeros((8 - DEPTH, BW), f32)], axis=0)
    (d_hgrn_lb,), _ = _seq_bwd("lb_prep_bwd", _lb_step, [_full(lb_in)], [True], [], [tabs["lmat"]], [], [_full(d_lb)], 1)
    return loss_local, dh, g, d_hgrn_lb[:DEPTH]


def kernel(x, meta_tokens, w_in, w_branch, w_out, norm_pre_mix, norm_post_mix, norm_pre_mlp, norm_post_mlp, w_up, w_down, ret_gn_w, s5_lam_re, s5_lam_im, s5_b_re, s5_b_im, s5_c_re, s5_c_im, s5_d, s5_log_step, s5_glu_w, s5_glu_b, ssd_conv_w, ssd_conv_b, ssd_dt_bias, ssd_a_log, ssd_d, ssd_norm_w, hgrn_lb, hgrn_norm_w, loss_target, m_meta_tokens, m_w_in, m_w_branch, m_w_out, m_norm_pre_mix, m_norm_post_mix, m_norm_pre_mlp, m_norm_post_mlp, m_w_up, m_w_down, m_ret_gn_w, m_s5_lam_re, m_s5_lam_im, m_s5_b_re, m_s5_b_im, m_s5_c_re, m_s5_c_im, m_s5_d, m_s5_log_step, m_s5_glu_w, m_s5_glu_b, m_ssd_conv_w, m_ssd_conv_b, m_ssd_dt_bias, m_ssd_a_log, m_ssd_d, m_ssd_norm_w, m_hgrn_lb, m_hgrn_norm_w, v_meta_tokens, v_w_in, v_w_branch, v_w_out, v_norm_pre_mix, v_norm_post_mix, v_norm_pre_mlp, v_norm_post_mlp, v_w_up, v_w_down, v_ret_gn_w, v_s5_lam_re, v_s5_lam_im, v_s5_b_re, v_s5_b_im, v_s5_c_re, v_s5_c_im, v_s5_d, v_s5_log_step, v_s5_glu_w, v_s5_glu_b, v_ssd_conv_w, v_ssd_conv_b, v_ssd_dt_bias, v_ssd_a_log, v_ssd_d, v_ssd_norm_w, v_hgrn_lb, v_hgrn_norm_w):
    args = dict(locals())
    w = {n: args[n] for n in WEIGHTS}
    mom = {n: args["m_" + n] for n in WEIGHTS}
    var = {n: args["v_" + n] for n in WEIGHTS}
    names = [n for n, _ in SHARDED]
    big = names[:N_BF16]

    shard = {n: [w[n][l].astype(bf16) for l in range(DEPTH)] for n in big}
    rest = [n for n in big if n != "w_in"]
    got = _all_gather("gather_first", [shard["w_in"][0], w["meta_tokens"], w["ssd_conv_w"]])
    wf = {n: w[n] for n in REPL}
    wf["meta_tokens"] = _from8(got[1], 1)
    wf["ssd_conv_w"] = _from8(got[2], 2)

    def assemble_rest(got_rest):
        got_rest = dict(zip(rest, got_rest))
        return dict(w_branch=_from8(got_rest["w_branch"], 2), w_out=_from8(got_rest["w_out"], 0), w_up=_from8(got_rest["w_up"], 1),
                    w_down=_from8(got_rest["w_down"], 0), glu_w=_from8(got_rest["s5_glu_w"], 1))

    def next_w(l):
        carry = {}
        if l == 0:
            first = _Carry("gather", [shard[n][0] for n in rest])
            carry.update({"in": first, "in_done": lambda: assemble_rest(first.result)})
        if l + 1 == DEPTH:
            return carry, None
        carry.update(s5=_Carry("gather", [shard["w_in"][l + 1]]), ssd=_Carry("gather", [shard["w_up"][l + 1]]),
                     hg=_Carry("gather", [shard["w_down"][l + 1]]),
                     ret=_Carry("gather", [shard[n][l + 1] for n in ("w_branch", "w_out", "s5_glu_w")]))
        return carry, lambda: dict(
            assemble_rest([carry["ret"].result[0], carry["ret"].result[1], carry["ssd"].result[0], carry["hg"].result[0],
                           carry["ret"].result[2]]), w_in=_w_in_from_shards(carry["s5"].result[0]))

    def to_wire(gl):
        return dict(w_in=None if gl["w_in"] is None else _w_in_to_shards(gl["w_in"]).astype(WIRE), w_branch=_to8(gl["w_branch"], 2).astype(WIRE),
                    w_out=_to8(gl["w_out"], 0).astype(WIRE), w_up=_to8(gl["w_up"], 1).astype(WIRE),
                    w_down=_to8(gl["w_down"], 0).astype(WIRE), s5_glu_w=_to8(gl["glu_w"], 1).astype(WIRE))

    sent = [None] * DEPTH

    def send_grads(l, gl):
        wire = to_wire(gl)
        carry = dict(s5=_Carry("scatter", [wire["w_in"]]), mlp=_Carry("scatter", [wire["w_up"]]),
                     ssd=_Carry("scatter", [wire["w_down"], wire["w_branch"]]),
                     hg=_Carry("scatter", [wire["w_out"], wire["s5_glu_w"]]))
        sent[l] = lambda: dict(w_in=carry["s5"].result[0], w_up=carry["mlp"].result[0], w_down=carry["ssd"].result[0],
                               w_branch=carry["ssd"].result[1], w_out=carry["hg"].result[0], s5_glu_w=carry["hg"].result[1])
        if l == 1:
            def late(g0):
                wire0 = to_wire(dict(g0, w_in=None))
                dw = _Carry("scatter", [wire0["w_down"], wire0["w_out"], wire0["s5_glu_w"]])
                dx = _Carry("scatter", [wire0["w_up"], wire0["w_branch"]])
                sent[0] = lambda: dict(w_down=dw.result[0], w_out=dw.result[1], s5_glu_w=dw.result[2], w_up=dx.result[0],
                                       w_branch=dx.result[1])
                return dict(dw=dw, dx=dx)
            carry["late"] = late
        return carry

    loss_local, dh0, g, d_hgrn_lb = _local_step(x[0], loss_target[0], wf, dict(w_in=_w_in_from_shards(got[0])), next_w, send_grads)
    seq = x.shape[1]
    t = N_META + seq
    last = _exchange("scatter_last", [_w_in_to_shards(g[0]["w_in"]).astype(WIRE), _to8(dh0[:N_META], 1).astype(WIRE),
                                      jnp.stack([_to8(gl["conv_w"], 1) for gl in g], axis=1).astype(WIRE)])
    per_layer = [dict(sent[0](), w_in=last[0])] + [sent[l]() for l in range(1, DEPTH)]
    parts = {n: jnp.stack([pl_[n] for pl_ in per_layer], axis=1) for n in big}
    parts["meta_tokens"], parts["ssd_conv_w"] = last[1], last[2]
    out = {k: {} for k in ("grad", "delta", "m", "v")}
    for n in names:
        res = _adam_call("adamw_" + n, _rows2d(parts[n], 1), _rows2d(w[n]), _rows2d(mom[n]), _rows2d(var[n]))
        for k, r in zip(("grad", "delta", "m", "v"), res):
            out[k][n] = r.reshape(w[n].shape)

    def stack_l(key, shape):
        return jnp.stack([gl[key].reshape(shape) for gl in g], axis=0)

    small = dict(
        norm_pre_mix=stack_l("npm", (D_MODEL,)), norm_post_mix=stack_l("npostmix", (D_MODEL,)),
        norm_pre_mlp=stack_l("npremlp", (D_MODEL,)), norm_post_mlp=stack_l("npostmlp", (D_MODEL,)),
        ret_gn_w=stack_l("ret_gn", (BW,)), s5_lam_re=stack_l("lam_re", (S5_G, S5_P)), s5_lam_im=stack_l("lam_im", (S5_G, S5_P)),
        s5_b_re=stack_l("bt_re", (S5_G, S5_J, S5_P)).transpose(0, 1, 3, 2),
        s5_b_im=stack_l("bt_im", (S5_G, S5_J, S5_P)).transpose(0, 1, 3, 2),
        s5_c_re=stack_l("c_re", (S5_G, S5_J, S5_P)), s5_c_im=stack_l("c_im", (S5_G, S5_J, S5_P)),
        s5_d=stack_l("s5_d", (BW,)), s5_log_step=stack_l("lstep", (S5_G,)), s5_glu_b=stack_l("glu_b", (2 * BW,)),
        ssd_conv_b=stack_l("conv_b", (1024,)), ssd_dt_bias=stack_l("dt_bias", (128,))[:, :SSD_HEADS],
        ssd_a_log=stack_l("a_log", (128,))[:, :SSD_HEADS], ssd_d=stack_l("ssd_d", (128,))[:, :SSD_HEADS],
        ssd_norm_w=stack_l("ssd_nw", (BW,)), hgrn_lb=d_hgrn_lb, hgrn_norm_w=stack_l("hg_nw", (BW,)))
    (parts_small,) = _all_gather("gather_small_grads", [_pack([small[n] for n in REPL], SMALL_PAD)])
    res = _adam_call("adamw_replicated", parts_small, _pack([w[n] for n in REPL], SMALL_PAD),
                     _pack([mom[n] for n in REPL], SMALL_PAD), _pack([var[n] for n in REPL], SMALL_PAD))
    for k, r in zip(("grad", "delta", "m", "v"), res):
        out[k].update(zip(REPL, _unpack(r, [w[n].shape for n in REPL])))

    loss = lax.psum(loss_local, ("x", "y", "c"))
    return (loss, dh0[N_META:t][None], *[out["grad"][n] for n in WEIGHTS], *[out["delta"][n] for n in WEIGHTS],
            *[out["m"][n] for n in WEIGHTS], *[out["v"][n] for n in WEIGHTS])
```

```python
import numpy as np
import jax
import jax.numpy as jnp
from jax import lax
from jax.experimental import pallas as pl
from jax.experimental.pallas import tpu as pltpu

f32 = jnp.float32
bf16 = jnp.bfloat16
HI = lax.Precision.HIGHEST

D_MODEL = 1024
N_META = 16
DEPTH = 4
BW = 512
D_FF = 4096
EPS = 1e-6
N_DEV = 8
RET_HEADS = 4
SSD_HEADS = 8
SSD_GROUPS = 2
HG_HEADS = 4
S5_G, S5_J, S5_P = 32, 16, 64

ADAM_LR, ADAM_B1, ADAM_B2, ADAM_EPS, ADAM_WD, ADAM_STEP = 0.001, 0.9, 0.999, 1e-08, 0.01, 10

CHUNK = 64
HG_SUB = 16
VMEM_LIMIT = 56 * 1024 * 1024

OFF = dict(gates=0, rq=4096, rk=4352, rv=4608, rg=5120, s5u=5632, sxbc=6144, sz=7168, hq=7680, hf=8192,
           hi=8704, hg=9216, dt=9728)
NP = 9856
IN_DIM = 9736


def _orig_col_slices():
    sl = [(5640, 9736)]
    for base in (0, 256):
        for half in (0, 32):
            for h in range(4):
                sl.append((base + 64 * h + half, base + 64 * h + half + 32))
    sl.append((512, 1024))
    sl.append((1024, 1536))
    sl.append((1536, 2048))
    sl.append((2560, 3584))
    sl.append((2048, 2560))
    sl.append((3592, 5640))
    return sl


def _bdot(a, b):
    return jnp.dot(a.astype(bf16), b.astype(bf16), preferred_element_type=f32)


def _bdot_nt(a, b):
    return lax.dot_general(a.astype(bf16), b.astype(bf16), (((1,), (1,)), ((), ())), preferred_element_type=f32)


def _bdot_tn(a, b):
    return lax.dot_general(a.astype(bf16), b.astype(bf16), (((0,), (0,)), ((), ())), preferred_element_type=f32)


def _hdot(a, b):
    return jnp.dot(a, b, precision=HI, preferred_element_type=f32)


def _sig(x):
    return jax.nn.sigmoid(x)


def _rms(x, w):
    return x * lax.rsqrt(jnp.mean(x * x, axis=-1, keepdims=True) + EPS) * w


def _softplus(x):
    return jnp.maximum(x, 0.0) + jnp.log(1.0 + jnp.exp(-jnp.abs(x)))


def _r(arr, rb, w, jb=0):
    return (arr, (rb, w), lambda n, jb=jb: (n, jb))


def _full(arr):
    nd = arr.ndim
    return (arr, arr.shape, lambda n, nd=nd: (0,) * nd)


def _seq_fwd(name, step, rows, consts, nds, states, outs, n_chunks, save_states=False, carry=None):
    nr, nc, nn, ns, no = len(rows), len(consts), len(nds), len(states), len(outs)
    whole = list(consts) + list(nds)

    def body(*refs):
        row_refs = refs[:nr]
        whole_hbm = refs[nr:nr + nc + nn]
        out_refs = refs[nr + nc + nn:nr + nc + nn + no]
        k = nr + nc + nn + no
        saved_refs = refs[k:k + (ns if save_states else 0)]
        k += ns if save_states else 0
        whole_vmem = refs[k:k + nc + nn]
        state_refs = refs[k + nc + nn:]
        n = pl.program_id(0)

        @pl.when(n == 0)
        def _():
            for src, dst in zip(whole_hbm, whole_vmem):
                pltpu.sync_copy(src, dst)
            for s in state_refs:
                s[...] = jnp.zeros_like(s)

        st = tuple(s[...] for s in state_refs)
        if save_states:
            for sv, v in zip(saved_refs, st):
                sv[0] = v
        o, new = step(tuple(r[...] for r in row_refs), tuple(c[...] for c in whole_vmem[:nc]),
                      tuple(c[...] for c in whole_vmem[nc:]), st, n)
        for ref, v in zip(out_refs, o):
            ref[...] = v.astype(ref.dtype)
        for s, v in zip(state_refs, new):
            s[...] = v

    in_specs = [pl.BlockSpec(bs, im) for _, bs, im in rows] + [pl.BlockSpec(memory_space=pl.ANY)] * (nc + nn)
    out_shape = [jax.ShapeDtypeStruct(s, d) for s, d, _, _ in outs]
    out_specs = [pl.BlockSpec(bs, im) for _, _, bs, im in outs]
    if save_states:
        for s in states:
            out_shape.append(jax.ShapeDtypeStruct((n_chunks,) + tuple(s), f32))
            out_specs.append(pl.BlockSpec((1,) + tuple(s), lambda n, z=len(s): (n,) + (0,) * z))
    scratch = [pltpu.VMEM(a.shape, a.dtype) for a in whole] + [pltpu.VMEM(tuple(s), f32) for s in states]
    return _pcall(body, name, (n_chunks,), in_specs, out_specs, out_shape, scratch, [a for a, _, _ in rows] + whole,
                  ("arbitrary",), carry)


def _seq_bwd(name, step, rows, row_diff, consts, nds, saved, couts, n_chunks, carry=None):
    nr, nc, nn, ns, no = len(rows), len(consts), len(nds), len(saved), len(couts)
    whole = list(consts) + list(nds)
    didx = [i for i in range(nr) if row_diff[i]]

    def rev(im):
        return lambda n: im(n_chunks - 1 - n)

    def body(*refs):
        row_refs = refs[:nr]
        whole_hbm = refs[nr:nr + nc + nn]
        k = nr + nc + nn
        saved_refs = refs[k:k + ns]
        k += ns
        cout_refs = refs[k:k + no]
        k += no
        drow_refs = refs[k:k + len(didx)]
        k += len(didx)
        dconst_hbm = refs[k:k + nc]
        k += nc
        whole_vmem = refs[k:k + nc + nn]
        k += nc + nn
        dconst_acc = refs[k:k + nc]
        k += nc
        dstate_refs = refs[k:]
        n = pl.program_id(0)

        @pl.when(n == 0)
        def _():
            for src, dst in zip(whole_hbm, whole_vmem):
                pltpu.sync_copy(src, dst)
            for a in dconst_acc:
                a[...] = jnp.zeros_like(a)
            for s in dstate_refs:
                s[...] = jnp.zeros_like(s)

        rvals = tuple(r[...] for r in row_refs)
        cvals = tuple(c[...] for c in whole_vmem[:nc])
        nvals = tuple(c[...] for c in whole_vmem[nc:])
        svals = tuple(s[0] for s in saved_refs)
        cidx = n_chunks - 1 - n

        def f(dr, cv, sv):
            full = list(rvals)
            for i, v in zip(didx, dr):
                full[i] = v
            return step(tuple(full), cv, nvals, sv, cidx)

        (o, _), vf = jax.vjp(f, tuple(rvals[i] for i in didx), cvals, svals)
        ct_o = tuple(c[...].astype(v.dtype) for c, v in zip(cout_refs, o))
        ct_s = tuple(s[...] for s in dstate_refs)
        d_rows, d_consts, d_states = vf((ct_o, ct_s))
        for ref, v in zip(drow_refs, d_rows):
            ref[...] = v.astype(ref.dtype)
        for a, v in zip(dconst_acc, d_consts):
            a[...] += v.astype(f32)
        for s, v in zip(dstate_refs, d_states):
            s[...] = v

        @pl.when(n == n_chunks - 1)
        def _():
            for a, dst in zip(dconst_acc, dconst_hbm):
                pltpu.sync_copy(a, dst)

    in_specs = ([pl.BlockSpec(bs, rev(im)) for _, bs, im in rows]
                + [pl.BlockSpec(memory_space=pl.ANY)] * (nc + nn)
                + [pl.BlockSpec((1,) + a.shape[1:], lambda n, z=a.ndim - 1: (n_chunks - 1 - n,) + (0,) * z) for a in saved]
                + [pl.BlockSpec(bs, rev(im)) for _, bs, im in couts])
    out_shape, out_specs = [], []
    for i in didx:
        a, bs, im = rows[i]
        nrows = a.shape[0]
        out_shape.append(jax.ShapeDtypeStruct((nrows,) + tuple(bs[1:]), f32))
        out_specs.append(pl.BlockSpec(bs, (lambda im: lambda n: (im(n_chunks - 1 - n)[0],) + (0,) * (len(bs) - 1))(im)))
    for c in consts:
        out_shape.append(jax.ShapeDtypeStruct(c.shape, f32))
        out_specs.append(pl.BlockSpec(memory_space=pl.ANY))
    scratch = ([pltpu.VMEM(a.shape, a.dtype) for a in whole] + [pltpu.VMEM(c.shape, f32) for c in consts]
               + [pltpu.VMEM(a.shape[1:], f32) for a in saved])
    res = _pcall(body, name, (n_chunks,), in_specs, out_specs, out_shape, scratch,
                 [a for a, _, _ in rows] + whole + list(saved) + [a for a, _, _ in couts], ("arbitrary",), carry)
    return res[:len(didx)], res[len(didx):]


def _mm(name, a, b, mode, tm, tn, tk, precision=None, carry=None):
    if mode == "nn":
        (m, kd), nn_ = a.shape, b.shape[1]
        a_spec = pl.BlockSpec((tm, tk), lambda i, j, k: (i, k))
        b_spec = pl.BlockSpec((tk, tn), lambda i, j, k: (k, j))
        dims = (((1,), (0,)), ((), ()))
    elif mode == "tn":
        (kd, m), nn_ = a.shape, b.shape[1]
        a_spec = pl.BlockSpec((tk, tm), lambda i, j, k: (k, i))
        b_spec = pl.BlockSpec((tk, tn), lambda i, j, k: (k, j))
        dims = (((0,), (0,)), ((), ()))
    else:
        (m, kd), nn_ = a.shape, b.shape[0]
        a_spec = pl.BlockSpec((tm, tk), lambda i, j, k: (i, k))
        b_spec = pl.BlockSpec((tn, tk), lambda i, j, k: (j, k))
        dims = (((1,), (1,)), ((), ()))
    assert m % tm == 0 and nn_ % tn == 0 and kd % tk == 0, (name, a.shape, b.shape, tm, tn, tk)
    nk = kd // tk

    def body(a_ref, b_ref, o_ref, acc):
        k = pl.program_id(2)

        @pl.when(k == 0)
        def _():
            acc[...] = jnp.zeros_like(acc)

        if precision is None:
            acc[...] += lax.dot_general(a_ref[...].astype(bf16), b_ref[...].astype(bf16), dims, preferred_element_type=f32)
        else:
            acc[...] += lax.dot_general(a_ref[...], b_ref[...], dims, precision=precision, preferred_element_type=f32)

        @pl.when(k == nk - 1)
        def _():
            o_ref[...] = acc[...]

    return _pcall(body, name, (m // tm, nn_ // tn, nk), [a_spec, b_spec], [pl.BlockSpec((tm, tn), lambda i, j, k: (i, j))],
                  [jax.ShapeDtypeStruct((m, nn_), f32)], [pltpu.VMEM((tm, tn), f32)], [a, b],
                  ("parallel", "parallel", "arbitrary"), carry)[0]


def _div_tile(n, want, mult):
    best = None
    for t in range(mult, min(n, want) + 1, mult):
        if n % t == 0:
            best = t
    return best if best is not None else n


class _Carry:
    def __init__(self, kind, srcs):
        self.kind, self.srcs, self.n, self.result = kind, list(srcs), len(srcs), None

    def out_shapes(self):
        if self.kind == "scatter":
            return [jax.ShapeDtypeStruct(s.shape, s.dtype) for s in self.srcs]
        return [jax.ShapeDtypeStruct((N_DEV,) + tuple(s.shape), s.dtype) for s in self.srcs]

    def sems(self):
        k = N_DEV if self.kind == "scatter" else N_DEV - 1
        return [pltpu.SemaphoreType.DMA((self.n, k)), pltpu.SemaphoreType.DMA((self.n, k)), pltpu.SemaphoreType.DMA((self.n,))]

    def _sc(self, src, dst, sems, i, j, slot):
        return pltpu.make_async_remote_copy(
            src_ref=src[i].at[j], dst_ref=dst[i].at[slot], send_sem=sems[0].at[i, j], recv_sem=sems[1].at[i, slot],
            device_id=(j // 4, (j // 2) % 2, j % 2), device_id_type=pl.DeviceIdType.MESH)

    def _where(self):
        x, y, c = lax.axis_index("x"), lax.axis_index("y"), lax.axis_index("c")
        return (x, y, c), (x, y, 1 - c), [(1 - x, y), (x, 1 - y), (1 - x, 1 - y)], c

    def _gc(self, src, dst, sems, i, k, block, to, own=False):
        slot = dst[i].at[4 * block[0] + 2 * block[1] + block[2]]
        return pltpu.make_async_remote_copy(
            src_ref=src[i] if own else slot, dst_ref=slot, send_sem=sems[0].at[i, k], recv_sem=sems[1].at[i, k],
            device_id=to, device_id_type=pl.DeviceIdType.MESH)

    def _local(self, src, dst, sems, i):
        if self.kind == "scatter":
            me = 4 * lax.axis_index("x") + 2 * lax.axis_index("y") + lax.axis_index("c")
            return pltpu.make_async_copy(src[i].at[me], dst[i].at[me], sems[2].at[i])
        x, y, c = lax.axis_index("x"), lax.axis_index("y"), lax.axis_index("c")
        return pltpu.make_async_copy(src[i], dst[i].at[4 * x + 2 * y + c], sems[2].at[i])

    def _first(self, src, dst, sems):
        me, sibling, chips, c = self._where()
        out = []
        for i in range(self.n):
            out.append(self._gc(src, dst, sems, i, 0, me, sibling, own=True))
            out += [self._gc(src, dst, sems, i, 1 + j, me, (*chip, c), own=True) for j, chip in enumerate(chips)]
        return out

    def start(self, src, dst, sems):
        for i in range(self.n):
            self._local(src, dst, sems, i).start()
        if self.kind == "scatter":
            me = 4 * lax.axis_index("x") + 2 * lax.axis_index("y") + lax.axis_index("c")
            for j in range(N_DEV):
                @pl.when(j != me)
                def _(j=j):
                    for i in range(self.n):
                        self._sc(src, dst, sems, i, j, me).start()
        else:
            for cp in self._first(src, dst, sems):
                cp.start()

    def finish(self, src, dst, sems):
        if self.kind == "scatter":
            me = 4 * lax.axis_index("x") + 2 * lax.axis_index("y") + lax.axis_index("c")
            for j in range(N_DEV):
                @pl.when(j != me)
                def _(j=j):
                    for i in range(self.n):
                        self._sc(src, dst, sems, i, j, j).wait_recv()
                        self._sc(src, dst, sems, i, j, me).wait_send()
        else:
            me, sibling, chips, c = self._where()
            passed = []
            for j, chip in enumerate(chips):
                for i in range(self.n):
                    self._gc(src, dst, sems, i, 1 + j, (*chip, c), me).wait_recv()
                    fwd = self._gc(src, dst, sems, i, 4 + j, (*chip, c), sibling)
                    fwd.start()
                    passed.append(fwd)
            for i in range(self.n):
                self._gc(src, dst, sems, i, 0, sibling, me).wait_recv()
                for j, chip in enumerate(chips):
                    self._gc(src, dst, sems, i, 4 + j, (*chip, 1 - c), me).wait_recv()
            for cp in self._first(src, dst, sems) + passed:
                cp.wait_send()
        for i in range(self.n):
            self._local(src, dst, sems, i).wait()


def _pcall(body, name, grid, in_specs, out_specs, out_shape, scratch, operands, dims, carry=None):
    n_in, n_out = len(in_specs), len(out_shape)
    if carry is not None:
        n, inner = carry.n, body

        def body(*refs):
            ins, csrc = refs[:n_in], refs[n_in:n_in + n]
            outs, cdst = refs[n_in + n:n_in + n + n_out], refs[n_in + n + n_out:n_in + 2 * n + n_out]
            rest = refs[n_in + 2 * n + n_out:]
            first = last = True
            for k, size in enumerate(grid):
                first = jnp.logical_and(first, pl.program_id(k) == 0)
                last = jnp.logical_and(last, pl.program_id(k) == size - 1)

            @pl.when(first)
            def _():
                carry.start(csrc, cdst, rest[-3:])

            inner(*ins, *outs, *rest[:-3])

            @pl.when(last)
            def _():
                carry.finish(csrc, cdst, rest[-3:])

        hbm = pl.BlockSpec(memory_space=pltpu.HBM)
        in_specs, out_specs = list(in_specs) + [hbm] * n, list(out_specs) + [hbm] * n
        out_shape, scratch = list(out_shape) + carry.out_shapes(), list(scratch) + carry.sems()
        operands = list(operands) + carry.srcs
    kw = dict(grid=grid) if grid else {}
    res = pl.pallas_call(
        body, name=name, in_specs=in_specs, out_specs=out_specs, out_shape=out_shape, scratch_shapes=scratch,
        compiler_params=pltpu.CompilerParams(dimension_semantics=dims, vmem_limit_bytes=VMEM_LIMIT) if grid else None, **kw,
    )(*operands)
    res = list(res)
    if carry is not None:
        carry.result = res[n_out:]
    return res[:n_out]


def _exchange(name, srcs):
    carry = _Carry("scatter", srcs)
    _pcall(lambda *refs: None, name, (), [], [], [], [], [], None, carry)
    return carry.result


def _all_gather(name, srcs):
    carry = _Carry("gather", srcs)
    _pcall(lambda *refs: None, name, (), [], [], [], [], [], None, carry)
    return carry.result


def _ret_tables(c):
    gam = 1.0 - 2.0 ** (-5.0 - np.arange(RET_HEADS))
    lg = np.log(gam)
    t = np.arange(c)
    dmat = np.where(t[:, None] >= t[None, :], np.exp((t[:, None] - t[None, :])[None] * lg[:, None, None]), 0.0)
    head_v = np.arange(512) // 128
    ysc = np.exp((t[:, None] + 1) * lg[head_v][None, :])
    wtab = np.exp((c - 1 - t)[:, None] * lg[head_v][None, :])
    gtab = np.exp(c * lg[head_v])[None, :]
    head_k = (np.arange(256) % 128) // 32
    mask = (head_k[:, None] == head_v[None, :]).astype(np.float32)
    hm = (head_k[None, None, :] == np.arange(4)[:, None, None]).astype(np.float32)
    return [jnp.asarray(x, f32) for x in (dmat, ysc, wtab, gtab, mask, hm)]


def _rope_tables(tp):
    inv = 10000.0 ** (-np.arange(32, dtype=np.float32) / 32)
    ang = np.arange(tp, dtype=np.float32)[:, None] * inv[None, :]
    cos = np.tile(np.cos(ang), (1, 4)).astype(np.float32)
    sin = np.tile(np.sin(ang), (1, 4)).astype(np.float32)
    return jnp.asarray(cos), jnp.asarray(sin)


def _tri(c):
    t = np.arange(c)
    return jnp.asarray((t[:, None] >= t[None, :]).astype(np.float32))


def _ssd_tables(c):
    sh = np.zeros((3 * c, c + 8), np.float32)
    for d in (3, 2, 1):
        for t in range(c):
            sh[(3 - d) * c + t, 8 + t - d] = 1.0
    e = np.zeros((128, 512), np.float32)
    for h in range(SSD_HEADS):
        e[h, 64 * h:64 * h + 64] = 1.0
    mg = ((np.arange(256) // 128)[:, None] == (np.arange(512) // 256)[None, :]).astype(np.float32)
    cm4 = ((np.arange(256) // 64)[None, None, :] == np.arange(4)[:, None, None]).astype(np.float32)
    return [jnp.asarray(x) for x in (sh, e, mg, cm4)]


def _ret_step(rows, consts, nds, states, _):
    q, k, v, g, cos, sin = rows
    (gnw,) = consts
    dmat, ysc, wtab, gtab, mask, hm = nds
    (s,) = states
    q1, q2, k1, k2 = q[:, :128], q[:, 128:], k[:, :128], k[:, 128:]
    qr = jnp.concatenate([q1 * cos - q2 * sin, q1 * sin + q2 * cos], axis=1)
    kr = jnp.concatenate([k1 * cos - k2 * sin, k1 * sin + k2 * cos], axis=1) * 0.125
    y = _bdot(qr, s) * ysc
    parts = []
    for h in range(RET_HEADS):
        a = _bdot_nt(qr * hm[h], kr) * dmat[h]
        parts.append(_bdot(a, v[:, 128 * h:128 * h + 128]))
    y = y + jnp.concatenate(parts, axis=1)
    s_new = s * gtab + _bdot_tn(kr, v * wtab) * mask
    outs = []
    for h in range(RET_HEADS):
        yh = y[:, 128 * h:128 * h + 128]
        d = yh - jnp.mean(yh, axis=-1, keepdims=True)
        outs.append(d * lax.rsqrt(jnp.mean(d * d, axis=-1, keepdims=True) + EPS))
    yn = jnp.concatenate(outs, axis=1) * gnw
    return (g * _sig(g) * yn,), (s_new,)


def _ssd_step(rows, consts, nds, states, _):
    z, xbc, dt128 = rows
    cw, cb, dtb, alog, dsk, nw = consts
    tri, sh, e, mg, cm4 = nds
    s, tail = states
    c = xbc.shape[0]
    shifted = _hdot(sh, jnp.concatenate([tail, xbc], axis=0))
    xc = (cw[0:1] * shifted[0:c] + cw[1:2] * shifted[c:2 * c] + cw[2:3] * shifted[2 * c:3 * c] + cw[3:4] * xbc + cb)
    xc = xc * _sig(xc)
    xs, bm, cm = xc[:, :512], xc[:, 512:768], xc[:, 768:]
    dt = _softplus(dt128 + dtb)
    la = dt * (-jnp.exp(alog))
    cum = _hdot(tri, la)
    cum_t = cum.T
    dtx, cumx = _hdot(dt, e), _hdot(cum, e)
    lastx = cumx[c - 1:c]
    dskx = _hdot(jnp.broadcast_to(dsk, (8, 128)), e)[0:1]
    v = xs * dtx
    y = _bdot(cm, s) * jnp.exp(cumx)
    lane = lax.broadcasted_iota(jnp.int32, cum.shape, 1)
    sub = lax.broadcasted_iota(jnp.int32, cum_t.shape, 0)
    parts = []
    for grp in range(SSD_GROUPS):
        sg = _bdot_nt(cm[:, 128 * grp:128 * grp + 128], bm[:, 128 * grp:128 * grp + 128])
        vg = v[:, 256 * grp:256 * grp + 256]
        acc = jnp.zeros((c, 256), f32)
        for hh in range(4):
            h = 4 * grp + hh
            col = jnp.sum(jnp.where(lane == h, cum, 0.0), axis=1, keepdims=True)
            row = jnp.sum(jnp.where(sub == h, cum_t, 0.0), axis=0, keepdims=True)
            dec = jnp.exp(jnp.where(tri > 0.5, col - row, -1e30))
            acc = acc + _bdot(sg * dec, vg * cm4[hh])
        parts.append(acc)
    y = y + jnp.concatenate(parts, axis=1) + dskx * xs
    s_new = s * jnp.exp(lastx) + _bdot_tn(bm, v * jnp.exp(lastx - cumx)) * mg
    y = y * (z * _sig(z))
    outs = []
    for grp in range(SSD_GROUPS):
        yg = y[:, 256 * grp:256 * grp + 256]
        outs.append(yg * lax.rsqrt(jnp.mean(yg * yg, axis=-1, keepdims=True) + EPS))
    return (jnp.concatenate(outs, axis=1) * nw,), (s_new, xbc[c - 8:c])


def _hg_step(rows, consts, nds, states, _):
    hq, hf, hi, hgate = rows
    lb, nw = consts
    (tri,) = nds
    (st,) = states
    c = hq.shape[0]
    q = hq * _sig(hq)
    f = lb + (1.0 - lb) * _sig(hf)
    lf = jnp.log(f)
    k = 1.0 - f
    v = hi
    cum = _hdot(tri, lf)
    last = jnp.sum(lf, axis=0, keepdims=True)
    qd = q * jnp.exp(cum)
    kw = k * jnp.exp(last - cum)
    heads = [slice(128 * h, 128 * h + 128) for h in range(HG_HEADS)]
    y_rows = []
    rowid = lax.broadcasted_iota(jnp.int32, (HG_SUB, 512), 0)
    for i in range(c // HG_SUB):
        r0 = HG_SUB * i
        qi, ci, ki, vi = q[r0:r0 + HG_SUB], cum[r0:r0 + HG_SUB], k[r0:r0 + HG_SUB], v[r0:r0 + HG_SUB]
        yi = [jnp.zeros((HG_SUB, 128), f32) for _ in heads]
        for s_ in range(HG_SUB):
            e = qi * ki[s_:s_ + 1] * jnp.exp(jnp.where(rowid >= s_, ci - ci[s_:s_ + 1], -1e30))
            for h, sl in enumerate(heads):
                yi[h] = yi[h] + jnp.sum(e[:, sl], axis=1, keepdims=True) * vi[s_:s_ + 1, sl]
        if i > 0:
            b = cum[r0 - 1:r0]
            qs = qi * jnp.exp(ci - b)
            ks = k[:r0] * jnp.exp(b - cum[:r0])
            for h, sl in enumerate(heads):
                yi[h] = yi[h] + _bdot(_bdot_nt(qs[:, sl], ks[:, sl]), v[:r0, sl])
        y_rows.append(jnp.concatenate(yi, axis=1))
    y_in = jnp.concatenate(y_rows, axis=0)
    y = y_in + jnp.concatenate([_bdot_nt(qd[:, sl], st[:, sl]) for sl in heads], axis=1)
    st_new = jnp.concatenate([st[:, sl] * jnp.exp(last[:, sl]) + _bdot_tn(v[:, sl], kw[:, sl]) for sl in heads], axis=1)
    outs = []
    for sl in heads:
        yh = y[:, sl]
        outs.append(yh * lax.rsqrt(jnp.mean(yh * yh, axis=-1, keepdims=True) + EPS))
    o = jnp.concatenate(outs, axis=1) * nw
    return (o * (hgate * _sig(hgate)),), (st_new,)


def _s5post_step(rows, consts, nds, states, _):
    ycore, u = rows
    dsk, gw, gb = consts
    y = jax.nn.gelu(ycore + dsk * u)
    zz = _bdot(y, gw) + gb
    return (zz[:, :512] * _sig(zz[:, 512:]),), ()


def _s5prep_step(rows, consts, nds, states, _):
    lr, li, lstep, btr, bti = rows
    (rep,) = nds
    step = jnp.exp(lstep)
    mag = jnp.exp(lr * step)
    ab_re, ab_im = mag * jnp.cos(li * step), mag * jnp.sin(li * step)
    inv = 1.0 / (lr * lr + li * li)
    co_re = ((ab_re - 1.0) * lr + ab_im * li) * inv
    co_im = (ab_im * lr - (ab_re - 1.0) * li) * inv
    cre, cim = _hdot(rep, co_re), _hdot(rep, co_im)
    return (ab_re, ab_im, cre * btr - cim * bti, cre * bti + cim * btr), ()


def _lb_step(rows, consts, nds, states, _):
    (x,) = rows
    (lmat,) = nds
    valid = lax.broadcasted_iota(jnp.int32, x.shape, 0) < DEPTH
    xm = jnp.where(valid, x, -1e30)
    ex = jnp.where(valid, jnp.exp(xm - jnp.max(xm, axis=0, keepdims=True)), 0.0)
    sm = ex / jnp.sum(ex, axis=0, keepdims=True)
    return (_hdot(lmat, sm),), ()


def _rmsn_step(rows, consts, nds, states, _):
    (h,) = rows
    (w,) = consts
    return (_rms(h, w),), ()


def _merge_step(rows, consts, nds, states, _):
    yr, ys5, yssd, yhg, gates, h = rows
    wb, wout, npost = consts
    mixed = jnp.zeros(h.shape, f32)
    for n, yb in enumerate((yr, ys5, yssd, yhg)):
        mixed = mixed + _sig(gates[:, 1024 * n:1024 * n + 1024]) * _bdot(yb, wb[n])
    return (h + _rms(_bdot(mixed, wout), npost),), ()


FF_BLK = 512


def _row_blocks(tp):
    rb = _div_tile(tp, 528, 16)
    return rb, tp // rb


def _mlp_core_fwd(tag, u, w_up, w_down, carry=None):
    tp = u.shape[0]
    rb, nrb = _row_blocks(tp)
    nff = D_FF // FF_BLK

    def body(u_ref, wup_ref, wdown_ref, m_ref):
        j = pl.program_id(0)

        def rows(i, carry):
            r = pl.multiple_of(i * rb, 16)
            a = jnp.dot(u_ref[pl.ds(r, rb), :], wup_ref[...], preferred_element_type=f32)
            part = _bdot(jnp.square(jnp.maximum(a, 0.0)), wdown_ref[...])

            @pl.when(j == 0)
            def _():
                m_ref[pl.ds(r, rb), :] = part

            @pl.when(j > 0)
            def _():
                m_ref[pl.ds(r, rb), :] += part
            return carry
        lax.fori_loop(0, nrb, rows, 0)

    return _pcall(
        body, "mlp_core_fwd" + tag, (nff,),
        [pl.BlockSpec((tp, D_MODEL), lambda j: (0, 0)), pl.BlockSpec((D_MODEL, FF_BLK), lambda j: (0, j)),
         pl.BlockSpec((FF_BLK, D_MODEL), lambda j: (j, 0))],
        [pl.BlockSpec((tp, D_MODEL), lambda j: (0, 0))], [jax.ShapeDtypeStruct((tp, D_MODEL), f32)], [], [u, w_up, w_down],
        ("arbitrary",), carry)[0]


def _mlp_core_bwd(tag, u, dm, w_up, w_down, carry=None):
    tp = u.shape[0]
    rb, nrb = _row_blocks(tp)
    nff = D_FF // FF_BLK

    def body(u_hbm, dm_hbm, wup_ref, wdown_ref, du_hbm, dwup_ref, dwdown_ref, u_s, dm_s, du_s):
        j = pl.program_id(0)

        @pl.when(j == 0)
        def _():
            pltpu.sync_copy(u_hbm, u_s)
            pltpu.sync_copy(dm_hbm, dm_s)

        def rows(i, carry):
            r = pl.multiple_of(i * rb, 16)
            ub = u_s[pl.ds(r, rb), :]
            dmb = dm_s[pl.ds(r, rb), :].astype(bf16)
            a = jnp.dot(ub, wup_ref[...], preferred_element_type=f32)
            ra = jnp.maximum(a, 0.0)
            da = (_bdot_nt(dmb, wdown_ref[...]) * (2.0 * ra)).astype(bf16)
            dwd = _bdot_tn(ra * ra, dmb)
            dwu = _bdot_tn(ub, da)
            dub = _bdot_nt(da, wup_ref[...])

            @pl.when(i == 0)
            def _():
                dwdown_ref[...] = dwd
                dwup_ref[...] = dwu

            @pl.when(i > 0)
            def _():
                dwdown_ref[...] += dwd
                dwup_ref[...] += dwu

            @pl.when(j == 0)
            def _():
                du_s[pl.ds(r, rb), :] = dub

            @pl.when(j > 0)
            def _():
                du_s[pl.ds(r, rb), :] += dub
            return carry
        lax.fori_loop(0, nrb, rows, 0)

        @pl.when(j == nff - 1)
        def _():
            pltpu.sync_copy(du_s, du_hbm)

    anyspec = pl.BlockSpec(memory_space=pl.ANY)
    return _pcall(
        body, "mlp_core_bwd" + tag, (nff,),
        [anyspec, anyspec, pl.BlockSpec((D_MODEL, FF_BLK), lambda j: (0, j)), pl.BlockSpec((FF_BLK, D_MODEL), lambda j: (j, 0))],
        [anyspec, pl.BlockSpec((D_MODEL, FF_BLK), lambda j: (0, j)), pl.BlockSpec((FF_BLK, D_MODEL), lambda j: (j, 0))],
        [jax.ShapeDtypeStruct((tp, D_MODEL), f32), jax.ShapeDtypeStruct((D_MODEL, D_FF), f32), jax.ShapeDtypeStruct((D_FF, D_MODEL), f32)],
        [pltpu.VMEM((tp, D_MODEL), bf16), pltpu.VMEM((tp, D_MODEL), f32), pltpu.VMEM((tp, D_MODEL), f32)],
        [u, dm, w_up, w_down], ("arbitrary",), carry)


def _resid_rms_step(rows, consts, nds, states, _):
    m, h = rows
    (w,) = consts
    return (h + _rms(m, w),), ()


SCAN_RB = 16


def _scan_inplace(xr, xi, ar0, ai0, tp, reverse, pr, pi):
    rb, cs = SCAN_RB, 64
    nb = cs // rb
    w = xr.shape[1]
    nch = tp // cs
    rowid = lax.broadcasted_iota(jnp.int32, (rb, w), 0)

    def mac(cr, ci, sr, si, ar, ai):
        return cr + ar * sr - ai * si, ci + ar * si + ai * sr

    ar, ai = ar0, ai0
    d = 1
    while d < cs:
        arb, aib = jnp.broadcast_to(ar, (rb, w)), jnp.broadcast_to(ai, (rb, w))
        edge = 0 if not reverse else nb - 1

        def level(c, carry, d=d, arb=arb, aib=aib):
            base = pl.multiple_of(c * cs, cs)
            order = range(nb - 1, -1, -1) if not reverse else range(nb)
            for b in order:
                r0 = base + rb * b
                lo = b - d // rb if not reverse else b + d // rb
                if d >= rb:
                    if lo < 0 or lo >= nb:
                        continue
                    src = base + rb * lo
                    sr, si = xr[pl.ds(src, rb), :], xi[pl.ds(src, rb), :]
                elif b == edge:
                    if not reverse:
                        sr = jnp.where(rowid >= d, pltpu.roll(xr[pl.ds(r0, rb), :], d, 0), 0.0)
                        si = jnp.where(rowid >= d, pltpu.roll(xi[pl.ds(r0, rb), :], d, 0), 0.0)
                    else:
                        sr = jnp.where(rowid < rb - d, pltpu.roll(xr[pl.ds(r0, rb), :], rb - d, 0), 0.0)
                        si = jnp.where(rowid < rb - d, pltpu.roll(xi[pl.ds(r0, rb), :], rb - d, 0), 0.0)
                elif not reverse:
                    sr = pltpu.roll(xr[pl.ds(r0 - 8, rb + 8), :], d, 0)[8:, :]
                    si = pltpu.roll(xi[pl.ds(r0 - 8, rb + 8), :], d, 0)[8:, :]
                else:
                    sr = pltpu.roll(xr[pl.ds(r0, rb + 8), :], rb + 8 - d, 0)[:rb, :]
                    si = pltpu.roll(xi[pl.ds(r0, rb + 8), :], rb + 8 - d, 0)[:rb, :]
                nr, ni = mac(xr[pl.ds(r0, rb), :], xi[pl.ds(r0, rb), :], sr, si, arb, aib)
                xr[pl.ds(r0, rb), :] = nr
                xi[pl.ds(r0, rb), :] = ni
            return carry
        lax.fori_loop(0, nch, level, 0)
        ar, ai = ar * ar - ai * ai, 2.0 * ar * ai
        d *= 2

    krow = lax.broadcasted_iota(jnp.int32, (cs, w), 0)
    expo = krow + 1 if not reverse else cs - krow
    tr, ti = jnp.ones((cs, w), f32), jnp.zeros((cs, w), f32)
    qr, qi = ar0, ai0
    for j in range(7):
        on = ((expo >> j) & 1) == 1
        fr, fi = jnp.where(on, qr, 1.0), jnp.where(on, qi, 0.0)
        tr, ti = tr * fr - ti * fi, tr * fi + ti * fr
        qr, qi = qr * qr - qi * qi, 2.0 * qr * qi
    pr[...] = tr
    pi[...] = ti

    def across(i, carry):
        c = i if not reverse else nch - 1 - i
        base = pl.multiple_of(c * cs, cs)
        cr, ci = carry
        out = carry
        for b in range(nb):
            r0 = base + rb * b
            nr, ni = mac(xr[pl.ds(r0, rb), :], xi[pl.ds(r0, rb), :], pr[rb * b:rb * b + rb, :], pi[rb * b:rb * b + rb, :], cr, ci)
            xr[pl.ds(r0, rb), :] = nr
            xi[pl.ds(r0, rb), :] = ni
            if not reverse and b == nb - 1:
                out = (nr[rb - 1:rb, :], ni[rb - 1:rb, :])
            if reverse and b == 0:
                out = (nr[0:1, :], ni[0:1, :])
        return out
    lax.fori_loop(0, nch, across, (jnp.zeros((1, w), f32), jnp.zeros((1, w), f32)))


def _s5_blocks(tp):
    rbm = tp // 8 if (tp // 8) % 8 == 0 and tp % 8 == 0 else tp
    return rbm, tp // rbm


def _s5_in_specs(proj, tp):
    ucol = OFF["s5u"] // 128
    return [
        pl.BlockSpec((tp, 128), lambda cb: (0, ucol + cb)),
        pl.BlockSpec((128, 512), lambda cb: (cb, 0)),
        pl.BlockSpec((128, 512), lambda cb: (cb, 0)),
        pl.BlockSpec((1, 512), lambda cb: (0, cb)),
        pl.BlockSpec((1, 512), lambda cb: (0, cb)),
        pl.BlockSpec((512, 128), lambda cb: (cb, 0)),
        pl.BlockSpec((512, 128), lambda cb: (cb, 0)),
    ]


def _s5_core_fwd(tag, proj, bbr, bbi, ar, ai, ccr, cci, carry=None):
    tp = proj.shape[0]
    rbm, nb = _s5_blocks(tp)

    def body(u_ref, bbr_ref, bbi_ref, ar_ref, ai_ref, ccr_ref, cci_ref, y_ref, xr, xi, pr, pi):
        def fill(i, carry):
            r = pl.multiple_of(i * rbm, 8)
            ub = u_ref[pl.ds(r, rbm), :]
            xr[pl.ds(r, rbm), :] = _bdot(ub, bbr_ref[...])
            xi[pl.ds(r, rbm), :] = _bdot(ub, bbi_ref[...])
            return carry
        lax.fori_loop(0, nb, fill, 0)
        _scan_inplace(xr, xi, ar_ref[...], ai_ref[...], tp, False, pr, pi)

        def out(i, carry):
            r = pl.multiple_of(i * rbm, 8)
            y_ref[pl.ds(r, rbm), :] = (_bdot(xr[pl.ds(r, rbm), :], ccr_ref[...]) - _bdot(xi[pl.ds(r, rbm), :], cci_ref[...]))
            return carry
        lax.fori_loop(0, nb, out, 0)

    return _pcall(body, "s5_core_fwd" + tag, (4,), _s5_in_specs(proj, tp), [pl.BlockSpec((tp, 128), lambda cb: (0, cb))],
                  [jax.ShapeDtypeStruct((tp, 512), f32)], [pltpu.VMEM((tp, 512), f32)] * 2 + [pltpu.VMEM((64, 512), f32)] * 2,
                  [proj, bbr, bbi, ar, ai, ccr, cci], ("arbitrary",), carry)[0]


def _s5_core_bwd(tag, proj, bbr, bbi, ar, ai, ccr, cci, dy, du_post, carry=None):
    tp = proj.shape[0]
    rbm, nb = _s5_blocks(tp)
    rb = SCAN_RB

    def body(u_ref, bbr_ref, bbi_ref, ar_ref, ai_ref, ccr_ref, cci_ref, dy_ref, dup_ref,
             du_ref, dbbr_ref, dbbi_ref, dar_ref, dai_ref, dccr_ref, dcci_ref, xr, xi, gr, gi, pr, pi):
        dccr_ref[...] = jnp.zeros_like(dccr_ref)
        dcci_ref[...] = jnp.zeros_like(dcci_ref)
        dbbr_ref[...] = jnp.zeros_like(dbbr_ref)
        dbbi_ref[...] = jnp.zeros_like(dbbi_ref)

        def fill(i, carry):
            r = pl.multiple_of(i * rbm, 8)
            ub = u_ref[pl.ds(r, rbm), :]
            xr[pl.ds(r, rbm), :] = _bdot(ub, bbr_ref[...])
            xi[pl.ds(r, rbm), :] = _bdot(ub, bbi_ref[...])
            return carry
        lax.fori_loop(0, nb, fill, 0)
        _scan_inplace(xr, xi, ar_ref[...], ai_ref[...], tp, False, pr, pi)

        def seed(i, carry):
            r = pl.multiple_of(i * rbm, 8)
            dyb = dy_ref[pl.ds(r, rbm), :]
            dccr_ref[...] += _bdot_tn(xr[pl.ds(r, rbm), :], dyb)
            dcci_ref[...] -= _bdot_tn(xi[pl.ds(r, rbm), :], dyb)
            gr[pl.ds(r, rbm), :] = _bdot_nt(dyb, ccr_ref[...])
            gi[pl.ds(r, rbm), :] = -_bdot_nt(dyb, cci_ref[...])
            return carry
        lax.fori_loop(0, nb, seed, 0)
        _scan_inplace(gr, gi, ar_ref[...], -ai_ref[...], tp, True, pr, pi)

        w = xr.shape[1]
        rowid = lax.broadcasted_iota(jnp.int32, (rb, w), 0)

        def prods(pr, pi, r0):
            g_r, g_i = gr[pl.ds(r0, rb), :], gi[pl.ds(r0, rb), :]
            return pr * g_r + pi * g_i, pr * g_i - pi * g_r

        def dacc(i, carry):
            r0 = pl.multiple_of((i + 1) * rb, 8)
            lo = pl.multiple_of(r0 - 8, 8)
            pr = pltpu.roll(xr[pl.ds(lo, rb + 8), :], 1, 0)[8:, :]
            pi = pltpu.roll(xi[pl.ds(lo, rb + 8), :], 1, 0)[8:, :]
            a, b = prods(pr, pi, r0)
            return carry[0] + a, carry[1] + b
        pr0 = jnp.where(rowid >= 1, pltpu.roll(xr[pl.ds(0, rb), :], 1, 0), 0.0)
        pi0 = jnp.where(rowid >= 1, pltpu.roll(xi[pl.ds(0, rb), :], 1, 0), 0.0)
        acc_r, acc_i = lax.fori_loop(0, tp // rb - 1, dacc, prods(pr0, pi0, 0))
        dar_ref[...] = jnp.sum(acc_r, axis=0, keepdims=True)
        dai_ref[...] = jnp.sum(acc_i, axis=0, keepdims=True)

        def tail(i, carry):
            r = pl.multiple_of(i * rbm, 8)
            g_r, g_i = gr[pl.ds(r, rbm), :], gi[pl.ds(r, rbm), :]
            ub = u_ref[pl.ds(r, rbm), :]
            du_ref[pl.ds(r, rbm), :] = _bdot_nt(g_r, bbr_ref[...]) + _bdot_nt(g_i, bbi_ref[...]) + dup_ref[pl.ds(r, rbm), :]
            dbbr_ref[...] += _bdot_tn(ub, g_r)
            dbbi_ref[...] += _bdot_tn(ub, g_i)
            return carry
        lax.fori_loop(0, nb, tail, 0)

    col = lambda cb: (0, cb)
    blk = lambda cb: (cb, 0)
    return _pcall(
        body, "s5_core_bwd" + tag, (4,),
        _s5_in_specs(proj, tp) + [pl.BlockSpec((tp, 128), col), pl.BlockSpec((tp, 128), col)],
        [pl.BlockSpec((tp, 128), col), pl.BlockSpec((128, 512), blk), pl.BlockSpec((128, 512), blk),
         pl.BlockSpec((1, 512), col), pl.BlockSpec((1, 512), col), pl.BlockSpec((512, 128), blk), pl.BlockSpec((512, 128), blk)],
        [jax.ShapeDtypeStruct((tp, 512), f32), jax.ShapeDtypeStruct((512, 512), f32), jax.ShapeDtypeStruct((512, 512), f32),
         jax.ShapeDtypeStruct((1, 2048), f32), jax.ShapeDtypeStruct((1, 2048), f32), jax.ShapeDtypeStruct((2048, 128), f32),
         jax.ShapeDtypeStruct((2048, 128), f32)],
        [pltpu.VMEM((tp, 512), f32)] * 4 + [pltpu.VMEM((64, 512), f32)] * 2, [proj, bbr, bbi, ar, ai, ccr, cci, dy, du_post],
        ("arbitrary",), carry)


_EYE8 = np.eye(8, dtype=np.float32)


def _bb_dense(bb):
    return jnp.einsum("cgjp,gh->cgjhp", bb.reshape(4, 8, 16, 64), _EYE8).reshape(512, 512)


def _bb_diag(d):
    return jnp.einsum("cgjhp,gh->cgjp", d.reshape(4, 8, 16, 8, 64), _EYE8).reshape(512, 64)


def _cc_dense(cmat):
    return jnp.einsum("cgjp,gh->cgphj", cmat.reshape(4, 8, 16, 64), _EYE8).reshape(2048, 128)


def _cc_diag(d):
    return jnp.einsum("cgphj,gh->cgjp", d.reshape(4, 8, 64, 8, 16), _EYE8).reshape(32, 16, 64)


def _tables(tp):
    cos, sin = _rope_tables(tp)
    rep = np.zeros((512, 32), np.float32)
    rep[np.arange(512), np.arange(512) // 16] = 1.0
    lmat = np.zeros((8, 8), np.float32)
    for l in range(DEPTH):
        lmat[l, 1:l + 1] = 1.0
    return dict(cos=cos, sin=sin, ret=_ret_tables(CHUNK), tri=_tri(CHUNK), ssd=_ssd_tables(CHUNK),
                rep=jnp.asarray(rep), lmat=jnp.asarray(lmat))


def _tiles(tp):
    return dict(tr=_div_tile(tp, 352, 16), tmg=_div_tile(tp, 192, 16))


def _mixer_rows(proj, c):
    ret = [_r(proj, c, 256, OFF["rq"] // 256), _r(proj, c, 256, OFF["rk"] // 256), _r(proj, c, 512, OFF["rv"] // 512),
           _r(proj, c, 512, OFF["rg"] // 512)]
    ssd = [_r(proj, c, 512, OFF["sz"] // 512), _r(proj, c, 1024, OFF["sxbc"] // 1024), _r(proj, c, 128, OFF["dt"] // 128)]
    hg = [_r(proj, c, 512, OFF[k] // 512) for k in ("hq", "hf", "hi", "hg")]
    return ret, ssd, hg


def _out2(rows, w, rb, dtype=f32):
    return (rows, w), dtype, (rb, w), lambda n: (n, 0)


def _s5_prep_rows(p):
    return [_full(p["lam_re"]), _full(p["lam_im"]), _full(p["lstep"]), _full(p["bt_re"]), _full(p["bt_im"])]


def _s5_consts(p, tabs, tag):
    whole = lambda s: (s, f32, s, lambda n: (0, 0))
    ab_re, ab_im, bb_re, bb_im = _seq_fwd("s5_prep" + tag, _s5prep_step, _s5_prep_rows(p), [], [tabs["rep"]], [],
                                          [whole((32, 64)), whole((32, 64)), whole((512, 64)), whole((512, 64))], 1)
    return (_bb_dense(bb_re).astype(bf16), _bb_dense(bb_im).astype(bf16), ab_re.reshape(1, 2048), ab_im.reshape(1, 2048),
            _cc_dense(p["c_re"]).astype(bf16), _cc_dense(p["c_im"]).astype(bf16))


def _rmsn_step_b(rows, consts, nds, states, n):
    (o,), _ = _rmsn_step(rows, consts, nds, states, n)
    return (o, rows[0]), ()


def _layer_fwd(h, p, tabs, tag, carry):
    tp = h.shape[0]
    c = CHUNK
    nch = tp // c
    tl = _tiles(tp)
    tr, tmg = tl["tr"], tl["tmg"]
    (u,) = _seq_fwd("rms_premix" + tag, _rmsn_step, [_r(h, tr, 1024)], [p["npm"]], [], [], [_out2(tp, 1024, tr, bf16)], tp // tr)
    proj = _mm("in_proj" + tag, u, p["w_in"], "nn", _div_tile(tp, 1056, 16), 1408, 1024, carry=carry.get("in"))
    if "in_done" in carry:
        p = dict(p, **carry["in_done"]())
    ret_rows, ssd_rows, hg_rows = _mixer_rows(proj, c)
    y_ret, ret_s = _seq_fwd("ret_fwd" + tag, _ret_step, ret_rows + [_r(tabs["cos"], c, 128), _r(tabs["sin"], c, 128)],
                            [p["ret_gn"]], tabs["ret"], [(256, 512)], [_out2(tp, 512, c)], nch, save_states=True,
                            carry=carry.get("ret"))
    ycore = _s5_core_fwd(tag, proj, *_s5_consts(p, tabs, tag), carry=carry.get("s5"))
    (y_s5,) = _seq_fwd("s5_post" + tag, _s5post_step, [_r(ycore, tr, 512), _r(proj, tr, 512, OFF["s5u"] // 512)],
                       [p["s5_d"], p["glu_w"], p["glu_b"]], [], [], [_out2(tp, 512, tr)], tp // tr)
    y_ssd, ssd_s, ssd_tail = _seq_fwd(
        "ssd_fwd" + tag, _ssd_step, ssd_rows, [p["conv_w"], p["conv_b"], p["dt_bias"], p["a_log"], p["ssd_d"], p["ssd_nw"]],
        [tabs["tri"]] + tabs["ssd"], [(256, 512), (8, 1024)], [_out2(tp, 512, c)], nch, save_states=True, carry=carry.get("ssd"))
    y_hg, hg_s = _seq_fwd("hg_fwd" + tag, _hg_step, hg_rows, [p["lb"], p["hg_nw"]], [tabs["tri"]], [(128, 512)],
                          [_out2(tp, 512, c)], nch, save_states=True, carry=carry.get("hg"))
    (h_mid,) = _seq_fwd(
        "merge_fwd" + tag, _merge_step,
        [_r(y_ret, tmg, 512), _r(y_s5, tmg, 512), _r(y_ssd, tmg, 512), _r(y_hg, tmg, 512), _r(proj, tmg, 4096, 0),
         _r(h, tmg, 1024)],
        [p["w_branch"], p["w_out"], p["npostmix"]], [], [], [_out2(tp, 1024, tmg)], tp // tmg, carry=carry.get("mg"))
    if "mg_done" in carry:
        p = dict(p, **carry["mg_done"]())
    (u2,) = _seq_fwd("rms_premlp" + tag, _rmsn_step, [_r(h_mid, tr, 1024)], [p["npremlp"]], [], [],
                     [_out2(tp, 1024, tr, bf16)], tp // tr)
    m = _mlp_core_fwd(tag, u2, p["w_up"], p["w_down"], carry=carry.get("mlp"))
    (h_new,) = _seq_fwd("mlp_post" + tag, _resid_rms_step, [_r(m, tr, 1024), _r(h_mid, tr, 1024)], [p["npostmlp"]], [], [],
                        [_out2(tp, 1024, tr)], tp // tr)
    saved = dict(h=h, u=u, proj=proj, ret_s=ret_s, ycore=ycore, ssd_s=ssd_s, ssd_tail=ssd_tail, hg_s=hg_s,
                 y_ret=y_ret, y_s5=y_s5, y_ssd=y_ssd, y_hg=y_hg, h_mid=h_mid, u2=u2, m=m)
    return h_new, saved, p


def _layer_bwd(dh, p, sv, tabs, tag, carry):
    tp = dh.shape[0]
    c = CHUNK
    nch = tp // c
    tl = _tiles(tp)
    tr, tmg = tl["tr"], tl["tmg"]
    proj = sv["proj"]
    g = {}
    (d_m, d_hmid), (g["npostmlp"],) = _seq_bwd(
        "mlp_post_bwd" + tag, _resid_rms_step, [_r(sv["m"], tr, 1024), _r(sv["h_mid"], tr, 1024)], [True, True],
        [p["npostmlp"]], [], [], [_r(dh, tr, 1024)], tp // tr)
    d_u2, g["w_up"], g["w_down"] = _mlp_core_bwd(tag, sv["u2"], d_m, p["w_up"], p["w_down"], carry=carry.get("mlp"))
    (d_hmid,), (g["npremlp"],) = _seq_bwd(
        "rms_premlp_bwd" + tag, _rmsn_step_b, [_r(sv["h_mid"], tr, 1024)], [True], [p["npremlp"]], [], [],
        [_r(d_u2, tr, 1024), _r(d_hmid, tr, 1024)], tp // tr)
    (dy_ret, dy_s5, dy_ssd, dy_hg, d_gates, d_h1), (g["w_branch"], g["w_out"], g["npostmix"]) = _seq_bwd(
        "merge_bwd" + tag, _merge_step,
        [_r(sv["y_ret"], tmg, 512), _r(sv["y_s5"], tmg, 512), _r(sv["y_ssd"], tmg, 512), _r(sv["y_hg"], tmg, 512),
         _r(proj, tmg, 4096, 0), _r(sv["h"], tmg, 1024)], [True] * 6,
        [p["w_branch"], p["w_out"], p["npostmix"]], [], [], [_r(d_hmid, tmg, 1024)], tp // tmg)
    ret_rows, ssd_rows, hg_rows = _mixer_rows(proj, c)
    (d_hq, d_hf, d_hi, d_hg), (g["lb"], g["hg_nw"]) = _seq_bwd(
        "hg_bwd" + tag, _hg_step, hg_rows, [True] * 4, [p["lb"], p["hg_nw"]], [tabs["tri"]], [sv["hg_s"]],
        [_r(dy_hg, c, 512)], nch, carry=carry.get("hg"))
    (d_z, d_xbc, d_dt), (g["conv_w"], g["conv_b"], g["dt_bias"], g["a_log"], g["ssd_d"], g["ssd_nw"]) = _seq_bwd(
        "ssd_bwd" + tag, _ssd_step, ssd_rows, [True] * 3,
        [p["conv_w"], p["conv_b"], p["dt_bias"], p["a_log"], p["ssd_d"], p["ssd_nw"]], [tabs["tri"]] + tabs["ssd"],
        [sv["ssd_s"], sv["ssd_tail"]], [_r(dy_ssd, c, 512)], nch, carry=carry.get("ssd"))
    (d_ycore, du_post), (g["s5_d"], g["glu_w"], g["glu_b"]) = _seq_bwd(
        "s5_post_bwd" + tag, _s5post_step, [_r(sv["ycore"], tr, 512), _r(proj, tr, 512, OFF["s5u"] // 512)], [True, True],
        [p["s5_d"], p["glu_w"], p["glu_b"]], [], [], [_r(dy_s5, tr, 512)], tp // tr)
    du_s5, dbbr, dbbi, dar, dai, dccr, dcci = _s5_core_bwd(tag, proj, *_s5_consts(p, tabs, tag + "b"), d_ycore, du_post,
                                                           carry=carry.get("s5"))
    g["c_re"], g["c_im"] = _cc_diag(dccr), _cc_diag(dcci)
    (g["lam_re"], g["lam_im"], g["lstep"], g["bt_re"], g["bt_im"]), _ = _seq_bwd(
        "s5_prep_bwd" + tag, _s5prep_step, _s5_prep_rows(p), [True] * 5, [], [tabs["rep"]], [],
        [_full(dar.reshape(32, 64)), _full(dai.reshape(32, 64)), _full(_bb_diag(dbbr)), _full(_bb_diag(dbbi))], 1)
    if "late" in carry:
        carry = dict(carry, **carry["late"](g))
    (d_q, d_k, d_v, d_g), (g["ret_gn"],) = _seq_bwd(
        "ret_bwd" + tag, _ret_step, ret_rows + [_r(tabs["cos"], c, 128), _r(tabs["sin"], c, 128)], [True] * 4 + [False] * 2,
        [p["ret_gn"]], tabs["ret"], [sv["ret_s"]], [_r(dy_ret, c, 512)], nch, carry=carry.get("ret"))
    dproj = jnp.concatenate([d_gates, d_q, d_k, d_v, d_g, du_s5, d_xbc, d_z, d_hq, d_hf, d_hi, d_hg, d_dt], axis=1).astype(bf16)
    g["w_in"] = _mm("in_proj_dw" + tag, sv["u"], dproj, "tn", 512, 1408, _div_tile(tp, 704, 16), carry=carry.get("dw"))
    du = _mm("in_proj_dx" + tag, dproj, p["w_in"], "nt", _div_tile(tp, 1056, 16), 1024, 1408, carry=carry.get("dx"))
    (dh_prev,), (g["npm"],) = _seq_bwd("rms_premix_bwd" + tag, _rmsn_step_b, [_r(sv["h"], tr, 1024)], [True], [p["npm"]], [], [],
                                       [_r(du, tr, 1024), _r(d_h1, tr, 1024)], tp // tr)
    return dh_prev, g


def _loss_call(h, tgt, lo, hi):
    tp = h.shape[0]
    tr = _div_tile(tp, 352, 16)

    def step(rows, consts, nds, states, n):
        hh, tt = rows
        row = n * tr + lax.broadcasted_iota(jnp.int32, hh.shape, 0)
        err = jnp.where((row >= lo) & (row < hi), hh - tt, 0.0)
        part = 0.5 * jnp.sum(err * err) * (1.0 / D_MODEL)
        return (err * (1.0 / D_MODEL), jnp.zeros((8, 128), f32) + part), ()

    dh, parts = _seq_fwd("loss_head", step, [_r(h, tr, 1024), _r(tgt, tr, 1024)], [], [], [],
                         [_out2(tp, 1024, tr), ((8 * (tp // tr), 128), f32, (8, 128), lambda n: (n, 0))], tp // tr)
    return dh, jnp.sum(parts[::8, 0])


def _adam_step(rows, consts, nds, states, _):
    g8, w, m, v = rows
    g = g8[0].astype(f32)
    for d in range(1, N_DEV):
        g = g + g8[d].astype(f32)
    m2 = ADAM_B1 * m + (1.0 - ADAM_B1) * g
    v2 = ADAM_B2 * v + (1.0 - ADAM_B2) * jnp.square(g)
    m_hat = m2 / (1.0 - ADAM_B1 ** ADAM_STEP)
    v_hat = v2 / (1.0 - ADAM_B2 ** ADAM_STEP)
    delta = -ADAM_LR * (m_hat / (jnp.sqrt(v_hat) + ADAM_EPS) + ADAM_WD * w)
    return (g, delta, m2, v2), ()


def _adam_call(name, g8, w, m, v):
    r, wd = w.shape
    tb = r
    for cand in range(16, r + 1, 16):
        if r % cand == 0 and cand * wd <= 256 * 1024:
            tb = cand
    o = ((r, wd), f32, (tb, wd), lambda n: (n, 0))
    return _seq_fwd(name, _adam_step, [(g8, (N_DEV, tb, wd), lambda n: (0, n, 0)), _r(w, tb, wd), _r(m, tb, wd), _r(v, tb, wd)],
                    [], [], [], [o, o, o, o], r // tb)


WEIGHTS = ['meta_tokens', 'w_in', 'w_branch', 'w_out', 'norm_pre_mix', 'norm_post_mix', 'norm_pre_mlp', 'norm_post_mlp',
           'w_up', 'w_down', 'ret_gn_w', 's5_lam_re', 's5_lam_im', 's5_b_re', 's5_b_im', 's5_c_re', 's5_c_im', 's5_d',
           's5_log_step', 's5_glu_w', 's5_glu_b', 'ssd_conv_w', 'ssd_conv_b', 'ssd_dt_bias', 'ssd_a_log', 'ssd_d',
           'ssd_norm_w', 'hgrn_lb', 'hgrn_norm_w']
SHARDED = [("w_in", 2), ("w_branch", 3), ("w_out", 1), ("w_up", 2), ("w_down", 1), ("s5_glu_w", 2), ("meta_tokens", 1),
           ("ssd_conv_w", 2)]
N_BF16 = 6
REPL = [n for n in WEIGHTS if n not in dict(SHARDED)]
SMALL_PAD = 512
WIRE = bf16
SHARD_COLS = IN_DIM // N_DEV


def _col_pieces():
    out, pos = [], 0
    for a, b in _orig_col_slices() + [(3584, 3592)]:
        if a == 3584:
            pos = OFF["dt"]
        while a < b:
            e = min(b, (a // SHARD_COLS + 1) * SHARD_COLS)
            out.append((a, e, pos))
            pos += e - a
            a = e
    return out


def _w_in_from_shards(got_l):
    parts = [got_l[a // SHARD_COLS][:, a % SHARD_COLS:a % SHARD_COLS + (b - a)] for a, b, _ in _col_pieces()]
    parts.append(jnp.zeros((got_l.shape[1], NP - IN_DIM - (OFF["dt"] - 9728)), got_l.dtype))
    return jnp.concatenate(parts, axis=1)


def _w_in_to_shards(g):
    pieces = sorted(_col_pieces())
    blocks = []
    for d in range(N_DEV):
        blocks.append(jnp.concatenate([g[:, m:m + (b - a)] for a, b, m in pieces if a // SHARD_COLS == d], axis=1))
    return jnp.stack(blocks, axis=0)


def _to8(full, axis):
    sh = full.shape
    return jnp.moveaxis(full.reshape(sh[:axis] + (N_DEV, sh[axis] // N_DEV) + sh[axis + 1:]), axis, 0)


def _from8(g8, axis):
    r = jnp.moveaxis(g8, 0, axis)
    sh = r.shape
    return r.reshape(sh[:axis] + (sh[axis] * sh[axis + 1],) + sh[axis + 2:])


def _pack(arrs, pad_rows):
    flat = jnp.concatenate([a.reshape(-1) for a in arrs])
    n = flat.shape[0]
    total = -(-n // (128 * pad_rows)) * (128 * pad_rows)
    if total != n:
        flat = jnp.concatenate([flat, jnp.zeros((total - n,), flat.dtype)])
    return flat.reshape(total // 128, 128)


def _unpack(flat, shapes):
    v = flat.reshape(-1)
    out, pos = [], 0
    for s in shapes:
        n = int(np.prod(s))
        out.append(v[pos:pos + n].reshape(tuple(s)))
        pos += n
    return out


def _rows2d(a, lead=0):
    return a.reshape(a.shape[:lead] + (-1, a.shape[-1]))


def _local_step(x0, tgt0, wf, first_w, next_w, send_grads):
    seq = x0.shape[0]
    t = N_META + seq
    tp = -(-t // CHUNK) * CHUNK
    tabs = _tables(tp)
    lb_in = jnp.concatenate([wf["hgrn_lb"], jnp.zeros((8 - DEPTH, BW), f32)], axis=0)
    (lb_all,) = _seq_fwd("lb_prep", _lb_step, [_full(lb_in)], [], [tabs["lmat"]], [], [((8, BW), f32, (8, BW), lambda n: (0, 0))], 1)

    def pad128(a):
        return jnp.concatenate([a, jnp.zeros((128 - a.shape[0],), f32)]).reshape(1, 128)

    def layer_params(l, big):
        return dict(
            big,
            npm=wf["norm_pre_mix"][l].reshape(1, D_MODEL), npostmix=wf["norm_post_mix"][l].reshape(1, D_MODEL),
            npremlp=wf["norm_pre_mlp"][l].reshape(1, D_MODEL), npostmlp=wf["norm_post_mlp"][l].reshape(1, D_MODEL),
            ret_gn=wf["ret_gn_w"][l].reshape(1, BW), lam_re=wf["s5_lam_re"][l], lam_im=wf["s5_lam_im"][l],
            lstep=wf["s5_log_step"][l].reshape(S5_G, 1),
            bt_re=wf["s5_b_re"][l].transpose(0, 2, 1).reshape(S5_G * S5_J, S5_P),
            bt_im=wf["s5_b_im"][l].transpose(0, 2, 1).reshape(S5_G * S5_J, S5_P),
            c_re=wf["s5_c_re"][l], c_im=wf["s5_c_im"][l], s5_d=wf["s5_d"][l].reshape(1, BW),
            glu_b=wf["s5_glu_b"][l].reshape(1, 2 * BW), conv_w=wf["ssd_conv_w"][l],
            conv_b=wf["ssd_conv_b"][l].reshape(1, 1024), dt_bias=pad128(wf["ssd_dt_bias"][l]),
            a_log=pad128(wf["ssd_a_log"][l]), ssd_d=pad128(wf["ssd_d"][l]), ssd_nw=wf["ssd_norm_w"][l].reshape(1, BW),
            lb=lb_all[l].reshape(1, BW), hg_nw=wf["hgrn_norm_w"][l].reshape(1, BW))

    zpad = jnp.zeros((tp - t, D_MODEL), f32)
    h = jnp.concatenate([wf["meta_tokens"], x0, zpad], axis=0)
    tgt = jnp.concatenate([jnp.zeros((N_META, D_MODEL), f32), tgt0, zpad], axis=0)
    params, saved, big = [], [], first_w
    for l in range(DEPTH):
        carry, arrived = next_w(l)
        h, sv, p = _layer_fwd(h, layer_params(l, big), tabs, "_l%d" % l, carry)
        params.append(p)
        saved.append(sv)
        big = arrived() if arrived else None
    dh, loss_local = _loss_call(h, tgt, N_META, t)
    g, carry = [None] * DEPTH, {}
    for l in reversed(range(DEPTH)):
        dh, g[l] = _layer_bwd(dh, params[l], saved[l], tabs, "_l%d" % l, carry)
        carry = send_grads(l, g[l]) if l > 0 else {}
    d_lb = jnp.concatenate([jnp.concatenate([gl["lb"] for gl in g], axis=0), jnp.zeros((8 - DEPTH, BW), f32)], axis=0)
    (d_hgrn_lb,), _ = _seq_bwd("lb_prep_bwd", _lb_step, [_full(lb_in)], [True], [], [tabs["lmat"]], [], [_full(d_lb)], 1)
    return loss_local, dh, g, d_hgrn_lb[:DEPTH]


def kernel(x, meta_tokens, w_in, w_branch, w_out, norm_pre_mix, norm_post_mix, norm_pre_mlp, norm_post_mlp, w_up, w_down, ret_gn_w, s5_lam_re, s5_lam_im, s5_b_re, s5_b_im, s5_c_re, s5_c_im, s5_d, s5_log_step, s5_glu_w, s5_glu_b, ssd_conv_w, ssd_conv_b, ssd_dt_bias, ssd_a_log, ssd_d, ssd_norm_w, hgrn_lb, hgrn_norm_w, loss_target, m_meta_tokens, m_w_in, m_w_branch, m_w_out, m_norm_pre_mix, m_norm_post_mix, m_norm_pre_mlp, m_norm_post_mlp, m_w_up, m_w_down, m_ret_gn_w, m_s5_lam_re, m_s5_lam_im, m_s5_b_re, m_s5_b_im, m_s5_c_re, m_s5_c_im, m_s5_d, m_s5_log_step, m_s5_glu_w, m_s5_glu_b, m_ssd_conv_w, m_ssd_conv_b, m_ssd_dt_bias, m_ssd_a_log, m_ssd_d, m_ssd_norm_w, m_hgrn_lb, m_hgrn_norm_w, v_meta_tokens, v_w_in, v_w_branch, v_w_out, v_norm_pre_mix, v_norm_post_mix, v_norm_pre_mlp, v_norm_post_mlp, v_w_up, v_w_down, v_ret_gn_w, v_s5_lam_re, v_s5_lam_im, v_s5_b_re, v_s5_b_im, v_s5_c_re, v_s5_c_im, v_s5_d, v_s5_log_step, v_s5_glu_w, v_s5_glu_b, v_ssd_conv_w, v_ssd_conv_b, v_ssd_dt_bias, v_ssd_a_log, v_ssd_d, v_ssd_norm_w, v_hgrn_lb, v_hgrn_norm_w):
    args = dict(locals())
    w = {n: args[n] for n in WEIGHTS}
    mom = {n: args["m_" + n] for n in WEIGHTS}
    var = {n: args["v_" + n] for n in WEIGHTS}
    names = [n for n, _ in SHARDED]
    big = names[:N_BF16]

    shard = {n: [w[n][l].astype(bf16) for l in range(DEPTH)] for n in big}
    got = _all_gather("gather_first", [shard["w_in"][0], w["meta_tokens"], w["ssd_conv_w"]])
    wf = {n: w[n] for n in REPL}
    wf["meta_tokens"] = _from8(got[1], 1)
    wf["ssd_conv_w"] = _from8(got[2], 2)

    full = dict(w_branch=lambda a: _from8(a, 2), w_out=lambda a: _from8(a, 0), w_up=lambda a: _from8(a, 1),
                w_down=lambda a: _from8(a, 0), s5_glu_w=lambda a: _from8(a, 1))
    key = dict(w_branch="w_branch", w_out="w_out", w_up="w_up", w_down="w_down", s5_glu_w="glu_w")
    half = w["w_in"].shape[1] // 2

    def arrived(pairs):
        return {key[n]: full[n](a) for n, a in pairs}

    def next_w(l):
        carry = {}
        if l == 0:
            n_in = ("w_branch", "w_out", "s5_glu_w", "w_up")
            first, second = _Carry("gather", [shard[n][0] for n in n_in]), _Carry("gather", [shard["w_down"][0]])
            carry.update({"in": first, "in_done": lambda: arrived(zip(n_in, first.result)),
                          "mg": second, "mg_done": lambda: arrived([("w_down", second.result[0])])})
        if l + 1 == DEPTH:
            return carry, None
        n_ret = ("w_branch", "w_out", "s5_glu_w")
        nxt = dict(s5=_Carry("gather", [shard["w_in"][l + 1][:half]]), mlp=_Carry("gather", [shard["w_in"][l + 1][half:]]),
                   ssd=_Carry("gather", [shard["w_up"][l + 1]]), hg=_Carry("gather", [shard["w_down"][l + 1]]),
                   ret=_Carry("gather", [shard[n][l + 1] for n in n_ret]))
        carry.update(nxt)
        return carry, lambda: dict(
            arrived(list(zip(n_ret, nxt["ret"].result)) + [("w_up", nxt["ssd"].result[0]), ("w_down", nxt["hg"].result[0])]),
            w_in=_w_in_from_shards(jnp.concatenate([nxt["s5"].result[0], nxt["mlp"].result[0]], axis=1)))

    def to_wire(gl):
        return dict(w_in=None if gl["w_in"] is None else _w_in_to_shards(gl["w_in"]).astype(WIRE), w_branch=_to8(gl["w_branch"], 2).astype(WIRE),
                    w_out=_to8(gl["w_out"], 0).astype(WIRE), w_up=_to8(gl["w_up"], 1).astype(WIRE),
                    w_down=_to8(gl["w_down"], 0).astype(WIRE), s5_glu_w=_to8(gl["glu_w"], 1).astype(WIRE))

    sent = [None] * DEPTH

    def send_grads(l, gl):
        wire = to_wire(gl)
        carry = dict(s5=_Carry("scatter", [wire["w_in"][:, :half]]), mlp=_Carry("scatter", [wire["w_up"]]),
                     ssd=_Carry("scatter", [wire["w_down"], wire["w_branch"]]),
                     hg=_Carry("scatter", [wire["w_in"][:, half:], wire["w_out"], wire["s5_glu_w"]]))
        sent[l] = lambda: dict(w_in=jnp.concatenate([carry["s5"].result[0], carry["hg"].result[0]], axis=1),
                               w_up=carry["mlp"].result[0], w_down=carry["ssd"].result[0], w_branch=carry["ssd"].result[1],
                               w_out=carry["hg"].result[1], s5_glu_w=carry["hg"].result[2])
        if l == 1:
            def late(g0):
                wire0 = to_wire(dict(g0, w_in=None))
                rt = _Carry("scatter", [wire0["w_up"]])
                dw = _Carry("scatter", [wire0["w_down"], wire0["s5_glu_w"]])
                dx = _Carry("scatter", [wire0["w_branch"], wire0["w_out"]])
                sent[0] = lambda: dict(w_up=rt.result[0], w_down=dw.result[0], s5_glu_w=dw.result[1], w_branch=dx.result[0],
                                       w_out=dx.result[1])
                return dict(ret=rt, dw=dw, dx=dx)
            carry["late"] = late
        return carry

    loss_local, dh0, g, d_hgrn_lb = _local_step(x[0], loss_target[0], wf, dict(w_in=_w_in_from_shards(got[0])), next_w, send_grads)
    seq = x.shape[1]
    t = N_META + seq
    last = _exchange("scatter_last", [_w_in_to_shards(g[0]["w_in"]).astype(WIRE), _to8(dh0[:N_META], 1).astype(WIRE),
                                      jnp.stack([_to8(gl["conv_w"], 1) for gl in g], axis=1).astype(WIRE)])
    per_layer = [dict(sent[0](), w_in=last[0])] + [sent[l]() for l in range(1, DEPTH)]
    parts = {n: jnp.stack([pl_[n] for pl_ in per_layer], axis=1) for n in big}
    parts["meta_tokens"], parts["ssd_conv_w"] = last[1], last[2]
    out = {k: {} for k in ("grad", "delta", "m", "v")}
    for n in names:
        res = _adam_call("adamw_" + n, _rows2d(parts[n], 1), _rows2d(w[n]), _rows2d(mom[n]), _rows2d(var[n]))
        for k, r in zip(("grad", "delta", "m", "v"), res):
            out[k][n] = r.reshape(w[n].shape)

    def stack_l(key, shape):
        return jnp.stack([gl[key].reshape(shape) for gl in g], axis=0)

    small = dict(
        norm_pre_mix=stack_l("npm", (D_MODEL,)), norm_post_mix=stack_l("npostmix", (D_MODEL,)),
        norm_pre_mlp=stack_l("npremlp", (D_MODEL,)), norm_post_mlp=stack_l("npostmlp", (D_MODEL,)),
        ret_gn_w=stack_l("ret_gn", (BW,)), s5_lam_re=stack_l("lam_re", (S5_G, S5_P)), s5_lam_im=stack_l("lam_im", (S5_G, S5_P)),
        s5_b_re=stack_l("bt_re", (S5_G, S5_J, S5_P)).transpose(0, 1, 3, 2),
        s5_b_im=stack_l("bt_im", (S5_G, S5_J, S5_P)).transpose(0, 1, 3, 2),
        s5_c_re=stack_l("c_re", (S5_G, S5_J, S5_P)), s5_c_im=stack_l("c_im", (S5_G, S5_J, S5_P)),
        s5_d=stack_l("s5_d", (BW,)), s5_log_step=stack_l("lstep", (S5_G,)), s5_glu_b=stack_l("glu_b", (2 * BW,)),
        ssd_conv_b=stack_l("conv_b", (1024,)), ssd_dt_bias=stack_l("dt_bias", (128,))[:, :SSD_HEADS],
        ssd_a_log=stack_l("a_log", (128,))[:, :SSD_HEADS], ssd_d=stack_l("ssd_d", (128,))[:, :SSD_HEADS],
        ssd_norm_w=stack_l("ssd_nw", (BW,)), hgrn_lb=d_hgrn_lb, hgrn_norm_w=stack_l("hg_nw", (BW,)))
    (parts_small,) = _all_gather("gather_small_grads", [_pack([small[n] for n in REPL], SMALL_PAD)])
    res = _adam_call("adamw_replicated", parts_small, _pack([w[n] for n in REPL], SMALL_PAD),
                     _pack([mom[n] for n in REPL], SMALL_PAD), _pack([var[n] for n in REPL], SMALL_PAD))
    for k, r in zip(("grad", "delta", "m", "v"), res):
        out[k].update(zip(REPL, _unpack(r, [w[n].shape for n in REPL])))

    loss = lax.psum(loss_local, ("x", "y", "c"))
    return (loss, dh0[N_META:t][None], *[out["grad"][n] for n in WEIGHTS], *[out["delta"][n] for n in WEIGHTS],
            *[out["m"][n] for n in WEIGHTS], *[out["v"][n] for n in WEIGHTS])
```

```python
import numpy as np
import jax
import jax.numpy as jnp
from jax import lax
from jax.experimental import pallas as pl
from jax.experimental.pallas import tpu as pltpu

f32 = jnp.float32
bf16 = jnp.bfloat16
HI = lax.Precision.HIGHEST

D_MODEL = 1024
N_META = 16
DEPTH = 4
BW = 512
D_FF = 4096
EPS = 1e-6
N_DEV = 8
RET_HEADS = 4
SSD_HEADS = 8
SSD_GROUPS = 2
HG_HEADS = 4
S5_G, S5_J, S5_P = 32, 16, 64

ADAM_LR, ADAM_B1, ADAM_B2, ADAM_EPS, ADAM_WD, ADAM_STEP = 0.001, 0.9, 0.999, 1e-08, 0.01, 10

CHUNK = 64
HG_SUB = 16
VMEM_LIMIT = 56 * 1024 * 1024

OFF = dict(gates=0, rq=4096, rk=4352, rv=4608, rg=5120, s5u=5632, sxbc=6144, sz=7168, hq=7680, hf=8192,
           hi=8704, hg=9216, dt=9728)
NP = 9856
IN_DIM = 9736


def _orig_col_slices():
    sl = [(5640, 9736)]
    for base in (0, 256):
        for half in (0, 32):
            for h in range(4):
                sl.append((base + 64 * h + half, base + 64 * h + half + 32))
    sl.append((512, 1024))
    sl.append((1024, 1536))
    sl.append((1536, 2048))
    sl.append((2560, 3584))
    sl.append((2048, 2560))
    sl.append((3592, 5640))
    return sl


def _bdot(a, b):
    return jnp.dot(a.astype(bf16), b.astype(bf16), preferred_element_type=f32)


def _bdot_nt(a, b):
    return lax.dot_general(a.astype(bf16), b.astype(bf16), (((1,), (1,)), ((), ())), preferred_element_type=f32)


def _bdot_tn(a, b):
    return lax.dot_general(a.astype(bf16), b.astype(bf16), (((0,), (0,)), ((), ())), preferred_element_type=f32)


def _hdot(a, b):
    return jnp.dot(a, b, precision=HI, preferred_element_type=f32)


def _sig(x):
    return jax.nn.sigmoid(x)


def _rms(x, w):
    return x * lax.rsqrt(jnp.mean(x * x, axis=-1, keepdims=True) + EPS) * w


def _softplus(x):
    return jnp.maximum(x, 0.0) + jnp.log(1.0 + jnp.exp(-jnp.abs(x)))


def _r(arr, rb, w, jb=0):
    return (arr, (rb, w), lambda n, jb=jb: (n, jb))


def _full(arr):
    nd = arr.ndim
    return (arr, arr.shape, lambda n, nd=nd: (0,) * nd)


def _seq_fwd(name, step, rows, consts, nds, states, outs, n_chunks, save_states=False, carry=None):
    nr, nc, nn, ns, no = len(rows), len(consts), len(nds), len(states), len(outs)
    whole = list(consts) + list(nds)

    def body(*refs):
        row_refs = refs[:nr]
        whole_hbm = refs[nr:nr + nc + nn]
        out_refs = refs[nr + nc + nn:nr + nc + nn + no]
        k = nr + nc + nn + no
        saved_refs = refs[k:k + (ns if save_states else 0)]
        k += ns if save_states else 0
        whole_vmem = refs[k:k + nc + nn]
        state_refs = refs[k + nc + nn:]
        n = pl.program_id(0)

        @pl.when(n == 0)
        def _():
            for src, dst in zip(whole_hbm, whole_vmem):
                pltpu.sync_copy(src, dst)
            for s in state_refs:
                s[...] = jnp.zeros_like(s)

        st = tuple(s[...] for s in state_refs)
        if save_states:
            for sv, v in zip(saved_refs, st):
                sv[0] = v
        o, new = step(tuple(r[...] for r in row_refs), tuple(c[...] for c in whole_vmem[:nc]),
                      tuple(c[...] for c in whole_vmem[nc:]), st, n)
        for ref, v in zip(out_refs, o):
            ref[...] = v.astype(ref.dtype)
        for s, v in zip(state_refs, new):
            s[...] = v

    in_specs = [pl.BlockSpec(bs, im) for _, bs, im in rows] + [pl.BlockSpec(memory_space=pl.ANY)] * (nc + nn)
    out_shape = [jax.ShapeDtypeStruct(s, d) for s, d, _, _ in outs]
    out_specs = [pl.BlockSpec(bs, im) for _, _, bs, im in outs]
    if save_states:
        for s in states:
            out_shape.append(jax.ShapeDtypeStruct((n_chunks,) + tuple(s), f32))
            out_specs.append(pl.BlockSpec((1,) + tuple(s), lambda n, z=len(s): (n,) + (0,) * z))
    scratch = [pltpu.VMEM(a.shape, a.dtype) for a in whole] + [pltpu.VMEM(tuple(s), f32) for s in states]
    return _pcall(body, name, (n_chunks,), in_specs, out_specs, out_shape, scratch, [a for a, _, _ in rows] + whole,
                  ("arbitrary",), carry)


def _seq_bwd(name, step, rows, row_diff, consts, nds, saved, couts, n_chunks, carry=None):
    nr, nc, nn, ns, no = len(rows), len(consts), len(nds), len(saved), len(couts)
    whole = list(consts) + list(nds)
    didx = [i for i in range(nr) if row_diff[i]]

    def rev(im):
        return lambda n: im(n_chunks - 1 - n)

    def body(*refs):
        row_refs = refs[:nr]
        whole_hbm = refs[nr:nr + nc + nn]
        k = nr + nc + nn
        saved_refs = refs[k:k + ns]
        k += ns
        cout_refs = refs[k:k + no]
        k += no
        drow_refs = refs[k:k + len(didx)]
        k += len(didx)
        dconst_hbm = refs[k:k + nc]
        k += nc
        whole_vmem = refs[k:k + nc + nn]
        k += nc + nn
        dconst_acc = refs[k:k + nc]
        k += nc
        dstate_refs = refs[k:]
        n = pl.program_id(0)

        @pl.when(n == 0)
        def _():
            for src, dst in zip(whole_hbm, whole_vmem):
                pltpu.sync_copy(src, dst)
            for a in dconst_acc:
                a[...] = jnp.zeros_like(a)
            for s in dstate_refs:
                s[...] = jnp.zeros_like(s)

        rvals = tuple(r[...] for r in row_refs)
        cvals = tuple(c[...] for c in whole_vmem[:nc])
        nvals = tuple(c[...] for c in whole_vmem[nc:])
        svals = tuple(s[0] for s in saved_refs)
        cidx = n_chunks - 1 - n

        def f(dr, cv, sv):
            full = list(rvals)
            for i, v in zip(didx, dr):
                full[i] = v
            return step(tuple(full), cv, nvals, sv, cidx)

        (o, _), vf = jax.vjp(f, tuple(rvals[i] for i in didx), cvals, svals)
        ct_o = tuple(c[...].astype(v.dtype) for c, v in zip(cout_refs, o))
        ct_s = tuple(s[...] for s in dstate_refs)
        d_rows, d_consts, d_states = vf((ct_o, ct_s))
        for ref, v in zip(drow_refs, d_rows):
            ref[...] = v.astype(ref.dtype)
        for a, v in zip(dconst_acc, d_consts):
            a[...] += v.astype(f32)
        for s, v in zip(dstate_refs, d_states):
            s[...] = v

        @pl.when(n == n_chunks - 1)
        def _():
            for a, dst in zip(dconst_acc, dconst_hbm):
                pltpu.sync_copy(a, dst)

    in_specs = ([pl.BlockSpec(bs, rev(im)) for _, bs, im in rows]
                + [pl.BlockSpec(memory_space=pl.ANY)] * (nc + nn)
                + [pl.BlockSpec((1,) + a.shape[1:], lambda n, z=a.ndim - 1: (n_chunks - 1 - n,) + (0,) * z) for a in saved]
                + [pl.BlockSpec(bs, rev(im)) for _, bs, im in couts])
    out_shape, out_specs = [], []
    for i in didx:
        a, bs, im = rows[i]
        nrows = a.shape[0]
        out_shape.append(jax.ShapeDtypeStruct((nrows,) + tuple(bs[1:]), f32))
        out_specs.append(pl.BlockSpec(bs, (lambda im: lambda n: (im(n_chunks - 1 - n)[0],) + (0,) * (len(bs) - 1))(im)))
    for c in consts:
        out_shape.append(jax.ShapeDtypeStruct(c.shape, f32))
        out_specs.append(pl.BlockSpec(memory_space=pl.ANY))
    scratch = ([pltpu.VMEM(a.shape, a.dtype) for a in whole] + [pltpu.VMEM(c.shape, f32) for c in consts]
               + [pltpu.VMEM(a.shape[1:], f32) for a in saved])
    res = _pcall(body, name, (n_chunks,), in_specs, out_specs, out_shape, scratch,
                 [a for a, _, _ in rows] + whole + list(saved) + [a for a, _, _ in couts], ("arbitrary",), carry)
    return res[:len(didx)], res[len(didx):]


def _mm(name, a, b, mode, tm, tn, tk, precision=None, carry=None):
    if mode == "nn":
        (m, kd), nn_ = a.shape, b.shape[1]
        a_spec = pl.BlockSpec((tm, tk), lambda i, j, k: (i, k))
        b_spec = pl.BlockSpec((tk, tn), lambda i, j, k: (k, j))
        dims = (((1,), (0,)), ((), ()))
    elif mode == "tn":
        (kd, m), nn_ = a.shape, b.shape[1]
        a_spec = pl.BlockSpec((tk, tm), lambda i, j, k: (k, i))
        b_spec = pl.BlockSpec((tk, tn), lambda i, j, k: (k, j))
        dims = (((0,), (0,)), ((), ()))
    else:
        (m, kd), nn_ = a.shape, b.shape[0]
        a_spec = pl.BlockSpec((tm, tk), lambda i, j, k: (i, k))
        b_spec = pl.BlockSpec((tn, tk), lambda i, j, k: (j, k))
        dims = (((1,), (1,)), ((), ()))
    assert m % tm == 0 and nn_ % tn == 0 and kd % tk == 0, (name, a.shape, b.shape, tm, tn, tk)
    nk = kd // tk

    def body(a_ref, b_ref, o_ref, acc):
        k = pl.program_id(2)

        @pl.when(k == 0)
        def _():
            acc[...] = jnp.zeros_like(acc)

        if precision is None:
            acc[...] += lax.dot_general(a_ref[...].astype(bf16), b_ref[...].astype(bf16), dims, preferred_element_type=f32)
        else:
            acc[...] += lax.dot_general(a_ref[...], b_ref[...], dims, precision=precision, preferred_element_type=f32)

        @pl.when(k == nk - 1)
        def _():
            o_ref[...] = acc[...]

    return _pcall(body, name, (m // tm, nn_ // tn, nk), [a_spec, b_spec], [pl.BlockSpec((tm, tn), lambda i, j, k: (i, j))],
                  [jax.ShapeDtypeStruct((m, nn_), f32)], [pltpu.VMEM((tm, tn), f32)], [a, b],
                  ("parallel", "parallel", "arbitrary"), carry)[0]


def _div_tile(n, want, mult):
    best = None
    for t in range(mult, min(n, want) + 1, mult):
        if n % t == 0:
            best = t
    return best if best is not None else n


class _Carry:
    def __init__(self, kind, srcs):
        self.kind, self.srcs, self.n, self.result = kind, list(srcs), len(srcs), None

    def out_shapes(self):
        if self.kind == "scatter":
            return [jax.ShapeDtypeStruct(s.shape, s.dtype) for s in self.srcs]
        return [jax.ShapeDtypeStruct((N_DEV,) + tuple(s.shape), s.dtype) for s in self.srcs]

    def sems(self):
        k = N_DEV if self.kind == "scatter" else N_DEV - 1
        return [pltpu.SemaphoreType.DMA((self.n, k)), pltpu.SemaphoreType.DMA((self.n, k)), pltpu.SemaphoreType.DMA((self.n,))]

    def _sc(self, src, dst, sems, i, j, slot):
        return pltpu.make_async_remote_copy(
            src_ref=src[i].at[j], dst_ref=dst[i].at[slot], send_sem=sems[0].at[i, j], recv_sem=sems[1].at[i, slot],
            device_id=(j // 4, (j // 2) % 2, j % 2), device_id_type=pl.DeviceIdType.MESH)

    def _where(self):
        x, y, c = lax.axis_index("x"), lax.axis_index("y"), lax.axis_index("c")
        return (x, y, c), (x, y, 1 - c), [(1 - x, y), (x, 1 - y), (1 - x, 1 - y)], c

    def _gc(self, src, dst, sems, i, k, block, to, own=False):
        slot = dst[i].at[4 * block[0] + 2 * block[1] + block[2]]
        return pltpu.make_async_remote_copy(
            src_ref=src[i] if own else slot, dst_ref=slot, send_sem=sems[0].at[i, k], recv_sem=sems[1].at[i, k],
            device_id=to, device_id_type=pl.DeviceIdType.MESH)

    def _local(self, src, dst, sems, i):
        if self.kind == "scatter":
            me = 4 * lax.axis_index("x") + 2 * lax.axis_index("y") + lax.axis_index("c")
            return pltpu.make_async_copy(src[i].at[me], dst[i].at[me], sems[2].at[i])
        x, y, c = lax.axis_index("x"), lax.axis_index("y"), lax.axis_index("c")
        return pltpu.make_async_copy(src[i], dst[i].at[4 * x + 2 * y + c], sems[2].at[i])

    def _first(self, src, dst, sems):
        me, sibling, chips, c = self._where()
        out = []
        for i in range(self.n):
            out.append(self._gc(src, dst, sems, i, 0, me, sibling, own=True))
            out += [self._gc(src, dst, sems, i, 1 + j, me, (*chip, c), own=True) for j, chip in enumerate(chips)]
        return out

    def start(self, src, dst, sems):
        for i in range(self.n):
            self._local(src, dst, sems, i).start()
        if self.kind == "scatter":
            me = 4 * lax.axis_index("x") + 2 * lax.axis_index("y") + lax.axis_index("c")
            for j in range(N_DEV):
                @pl.when(j != me)
                def _(j=j):
                    for i in range(self.n):
                        self._sc(src, dst, sems, i, j, me).start()
        else:
            for cp in self._first(src, dst, sems):
                cp.start()

    def finish(self, src, dst, sems):
        if self.kind == "scatter":
            me = 4 * lax.axis_index("x") + 2 * lax.axis_index("y") + lax.axis_index("c")
            for j in range(N_DEV):
                @pl.when(j != me)
                def _(j=j):
                    for i in range(self.n):
                        self._sc(src, dst, sems, i, j, j).wait_recv()
                        self._sc(src, dst, sems, i, j, me).wait_send()
        else:
            me, sibling, chips, c = self._where()
            passed = []
            for j, chip in enumerate(chips):
                for i in range(self.n):
                    self._gc(src, dst, sems, i, 1 + j, (*chip, c), me).wait_recv()
                    fwd = self._gc(src, dst, sems, i, 4 + j, (*chip, c), sibling)
                    fwd.start()
                    passed.append(fwd)
            for i in range(self.n):
                self._gc(src, dst, sems, i, 0, sibling, me).wait_recv()
                for j, chip in enumerate(chips):
                    self._gc(src, dst, sems, i, 4 + j, (*chip, 1 - c), me).wait_recv()
            for cp in self._first(src, dst, sems) + passed:
                cp.wait_send()
        for i in range(self.n):
            self._local(src, dst, sems, i).wait()


def _pcall(body, name, grid, in_specs, out_specs, out_shape, scratch, operands, dims, carry=None):
    n_in, n_out = len(in_specs), len(out_shape)
    if carry is not None:
        n, inner = carry.n, body

        def body(*refs):
            ins, csrc = refs[:n_in], refs[n_in:n_in + n]
            outs, cdst = refs[n_in + n:n_in + n + n_out], refs[n_in + n + n_out:n_in + 2 * n + n_out]
            rest = refs[n_in + 2 * n + n_out:]
            first = last = True
            for k, size in enumerate(grid):
                first = jnp.logical_and(first, pl.program_id(k) == 0)
                last = jnp.logical_and(last, pl.program_id(k) == size - 1)

            @pl.when(first)
            def _():
                carry.start(csrc, cdst, rest[-3:])

            inner(*ins, *outs, *rest[:-3])

            @pl.when(last)
            def _():
                carry.finish(csrc, cdst, rest[-3:])

        hbm = pl.BlockSpec(memory_space=pltpu.HBM)
        in_specs, out_specs = list(in_specs) + [hbm] * n, list(out_specs) + [hbm] * n
        out_shape, scratch = list(out_shape) + carry.out_shapes(), list(scratch) + carry.sems()
        operands = list(operands) + carry.srcs
    kw = dict(grid=grid) if grid else {}
    res = pl.pallas_call(
        body, name=name, in_specs=in_specs, out_specs=out_specs, out_shape=out_shape, scratch_shapes=scratch,
        compiler_params=pltpu.CompilerParams(dimension_semantics=dims, vmem_limit_bytes=VMEM_LIMIT) if grid else None, **kw,
    )(*operands)
    res = list(res)
    if carry is not None:
        carry.result = res[n_out:]
    return res[:n_out]


def _exchange(name, srcs):
    carry = _Carry("scatter", srcs)
    _pcall(lambda *refs: None, name, (), [], [], [], [], [], None, carry)
    return carry.result


def _all_gather(name, srcs):
    carry = _Carry("gather", srcs)
    _pcall(lambda *refs: None, name, (), [], [], [], [], [], None, carry)
    return carry.result


def _ret_tables(c):
    gam = 1.0 - 2.0 ** (-5.0 - np.arange(RET_HEADS))
    lg = np.log(gam)
    t = np.arange(c)
    dmat = np.where(t[:, None] >= t[None, :], np.exp((t[:, None] - t[None, :])[None] * lg[:, None, None]), 0.0)
    head_v = np.arange(512) // 128
    ysc = np.exp((t[:, None] + 1) * lg[head_v][None, :])
    wtab = np.exp((c - 1 - t)[:, None] * lg[head_v][None, :])
    gtab = np.exp(c * lg[head_v])[None, :]
    head_k = (np.arange(256) % 128) // 32
    mask = (head_k[:, None] == head_v[None, :]).astype(np.float32)
    hm = (head_k[None, None, :] == np.arange(4)[:, None, None]).astype(np.float32)
    return [jnp.asarray(x, f32) for x in (dmat, ysc, wtab, gtab, mask, hm)]


def _rope_tables(tp):
    inv = 10000.0 ** (-np.arange(32, dtype=np.float32) / 32)
    ang = np.arange(tp, dtype=np.float32)[:, None] * inv[None, :]
    cos = np.tile(np.cos(ang), (1, 4)).astype(np.float32)
    sin = np.tile(np.sin(ang), (1, 4)).astype(np.float32)
    return jnp.asarray(cos), jnp.asarray(sin)


def _tri(c):
    t = np.arange(c)
    return jnp.asarray((t[:, None] >= t[None, :]).astype(np.float32))


def _ssd_tables(c):
    sh = np.zeros((3 * c, c + 8), np.float32)
    for d in (3, 2, 1):
        for t in range(c):
            sh[(3 - d) * c + t, 8 + t - d] = 1.0
    e = np.zeros((128, 512), np.float32)
    for h in range(SSD_HEADS):
        e[h, 64 * h:64 * h + 64] = 1.0
    mg = ((np.arange(256) // 128)[:, None] == (np.arange(512) // 256)[None, :]).astype(np.float32)
    cm4 = ((np.arange(256) // 64)[None, None, :] == np.arange(4)[:, None, None]).astype(np.float32)
    return [jnp.asarray(x) for x in (sh, e, mg, cm4)]


def _ret_step(rows, consts, nds, states, _):
    q, k, v, g, cos, sin = rows
    (gnw,) = consts
    dmat, ysc, wtab, gtab, mask, hm = nds
    (s,) = states
    q1, q2, k1, k2 = q[:, :128], q[:, 128:], k[:, :128], k[:, 128:]
    qr = jnp.concatenate([q1 * cos - q2 * sin, q1 * sin + q2 * cos], axis=1)
    kr = jnp.concatenate([k1 * cos - k2 * sin, k1 * sin + k2 * cos], axis=1) * 0.125
    y = _bdot(qr, s) * ysc
    parts = []
    for h in range(RET_HEADS):
        a = _bdot_nt(qr * hm[h], kr) * dmat[h]
        parts.append(_bdot(a, v[:, 128 * h:128 * h + 128]))
    y = y + jnp.concatenate(parts, axis=1)
    s_new = s * gtab + _bdot_tn(kr, v * wtab) * mask
    outs = []
    for h in range(RET_HEADS):
        yh = y[:, 128 * h:128 * h + 128]
        d = yh - jnp.mean(yh, axis=-1, keepdims=True)
        outs.append(d * lax.rsqrt(jnp.mean(d * d, axis=-1, keepdims=True) + EPS))
    yn = jnp.concatenate(outs, axis=1) * gnw
    return (g * _sig(g) * yn,), (s_new,)


def _ssd_step(rows, consts, nds, states, _):
    z, xbc, dt128 = rows
    cw, cb, dtb, alog, dsk, nw = consts
    tri, sh, e, mg, cm4 = nds
    s, tail = states
    c = xbc.shape[0]
    shifted = _hdot(sh, jnp.concatenate([tail, xbc], axis=0))
    xc = (cw[0:1] * shifted[0:c] + cw[1:2] * shifted[c:2 * c] + cw[2:3] * shifted[2 * c:3 * c] + cw[3:4] * xbc + cb)
    xc = xc * _sig(xc)
    xs, bm, cm = xc[:, :512], xc[:, 512:768], xc[:, 768:]
    dt = _softplus(dt128 + dtb)
    la = dt * (-jnp.exp(alog))
    cum = _hdot(tri, la)
    cum_t = cum.T
    dtx, cumx = _hdot(dt, e), _hdot(cum, e)
    lastx = cumx[c - 1:c]
    dskx = _hdot(jnp.broadcast_to(dsk, (8, 128)), e)[0:1]
    v = xs * dtx
    y = _bdot(cm, s) * jnp.exp(cumx)
    lane = lax.broadcasted_iota(jnp.int32, cum.shape, 1)
    sub = lax.broadcasted_iota(jnp.int32, cum_t.shape, 0)
    parts = []
    for grp in range(SSD_GROUPS):
        sg = _bdot_nt(cm[:, 128 * grp:128 * grp + 128], bm[:, 128 * grp:128 * grp + 128])
        vg = v[:, 256 * grp:256 * grp + 256]
        acc = jnp.zeros((c, 256), f32)
        for hh in range(4):
            h = 4 * grp + hh
            col = jnp.sum(jnp.where(lane == h, cum, 0.0), axis=1, keepdims=True)
            row = jnp.sum(jnp.where(sub == h, cum_t, 0.0), axis=0, keepdims=True)
            dec = jnp.exp(jnp.where(tri > 0.5, col - row, -1e30))
            acc = acc + _bdot(sg * dec, vg * cm4[hh])
        parts.append(acc)
    y = y + jnp.concatenate(parts, axis=1) + dskx * xs
    s_new = s * jnp.exp(lastx) + _bdot_tn(bm, v * jnp.exp(lastx - cumx)) * mg
    y = y * (z * _sig(z))
    outs = []
    for grp in range(SSD_GROUPS):
        yg = y[:, 256 * grp:256 * grp + 256]
        outs.append(yg * lax.rsqrt(jnp.mean(yg * yg, axis=-1, keepdims=True) + EPS))
    return (jnp.concatenate(outs, axis=1) * nw,), (s_new, xbc[c - 8:c])


def _hg_step(rows, consts, nds, states, _):
    hq, hf, hi, hgate = rows
    lb, nw = consts
    (tri,) = nds
    (st,) = states
    c = hq.shape[0]
    q = hq * _sig(hq)
    f = lb + (1.0 - lb) * _sig(hf)
    lf = jnp.log(f)
    k = 1.0 - f
    v = hi
    cum = _hdot(tri, lf)
    last = jnp.sum(lf, axis=0, keepdims=True)
    qd = q * jnp.exp(cum)
    kw = k * jnp.exp(last - cum)
    heads = [slice(128 * h, 128 * h + 128) for h in range(HG_HEADS)]
    y_rows = []
    rowid = lax.broadcasted_iota(jnp.int32, (HG_SUB, 512), 0)
    for i in range(c // HG_SUB):
        r0 = HG_SUB * i
        qi, ci, ki, vi = q[r0:r0 + HG_SUB], cum[r0:r0 + HG_SUB], k[r0:r0 + HG_SUB], v[r0:r0 + HG_SUB]
        yi = [jnp.zeros((HG_SUB, 128), f32) for _ in heads]
        for s_ in range(HG_SUB):
            e = qi * ki[s_:s_ + 1] * jnp.exp(jnp.where(rowid >= s_, ci - ci[s_:s_ + 1], -1e30))
            for h, sl in enumerate(heads):
                yi[h] = yi[h] + jnp.sum(e[:, sl], axis=1, keepdims=True) * vi[s_:s_ + 1, sl]
        if i > 0:
            b = cum[r0 - 1:r0]
            qs = qi * jnp.exp(ci - b)
            ks = k[:r0] * jnp.exp(b - cum[:r0])
            for h, sl in enumerate(heads):
                yi[h] = yi[h] + _bdot(_bdot_nt(qs[:, sl], ks[:, sl]), v[:r0, sl])
        y_rows.append(jnp.concatenate(yi, axis=1))
    y_in = jnp.concatenate(y_rows, axis=0)
    y = y_in + jnp.concatenate([_bdot_nt(qd[:, sl], st[:, sl]) for sl in heads], axis=1)
    st_new = jnp.concatenate([st[:, sl] * jnp.exp(last[:, sl]) + _bdot_tn(v[:, sl], kw[:, sl]) for sl in heads], axis=1)
    outs = []
    for sl in heads:
        yh = y[:, sl]
        outs.append(yh * lax.rsqrt(jnp.mean(yh * yh, axis=-1, keepdims=True) + EPS))
    o = jnp.concatenate(outs, axis=1) * nw
    return (o * (hgate * _sig(hgate)),), (st_new,)


def _s5post_step(rows, consts, nds, states, _):
    ycore, u = rows
    dsk, gw, gb = consts
    y = jax.nn.gelu(ycore + dsk * u)
    zz = _bdot(y, gw) + gb
    return (zz[:, :512] * _sig(zz[:, 512:]),), ()


def _s5prep_step(rows, consts, nds, states, _):
    lr, li, lstep, btr, bti = rows
    (rep,) = nds
    step = jnp.exp(lstep)
    mag = jnp.exp(lr * step)
    ab_re, ab_im = mag * jnp.cos(li * step), mag * jnp.sin(li * step)
    inv = 1.0 / (lr * lr + li * li)
    co_re = ((ab_re - 1.0) * lr + ab_im * li) * inv
    co_im = (ab_im * lr - (ab_re - 1.0) * li) * inv
    cre, cim = _hdot(rep, co_re), _hdot(rep, co_im)
    return (ab_re, ab_im, cre * btr - cim * bti, cre * bti + cim * btr), ()


def _lb_step(rows, consts, nds, states, _):
    (x,) = rows
    (lmat,) = nds
    valid = lax.broadcasted_iota(jnp.int32, x.shape, 0) < DEPTH
    xm = jnp.where(valid, x, -1e30)
    ex = jnp.where(valid, jnp.exp(xm - jnp.max(xm, axis=0, keepdims=True)), 0.0)
    sm = ex / jnp.sum(ex, axis=0, keepdims=True)
    return (_hdot(lmat, sm),), ()


def _rmsn_step(rows, consts, nds, states, _):
    (h,) = rows
    (w,) = consts
    return (_rms(h, w),), ()


def _merge_step(rows, consts, nds, states, _):
    yr, ys5, yssd, yhg, gates, h = rows
    wb, wout, npost = consts
    mixed = jnp.zeros(h.shape, f32)
    for n, yb in enumerate((yr, ys5, yssd, yhg)):
        mixed = mixed + _sig(gates[:, 1024 * n:1024 * n + 1024]) * _bdot(yb, wb[n])
    return (h + _rms(_bdot(mixed, wout), npost),), ()


FF_BLK = 512


def _row_blocks(tp):
    rb = _div_tile(tp, 528, 16)
    return rb, tp // rb


def _mlp_core_fwd(tag, u, w_up, w_down, carry=None):
    tp = u.shape[0]
    rb, nrb = _row_blocks(tp)
    nff = D_FF // FF_BLK

    def body(u_ref, wup_ref, wdown_ref, m_ref):
        j = pl.program_id(0)

        def rows(i, carry):
            r = pl.multiple_of(i * rb, 16)
            a = jnp.dot(u_ref[pl.ds(r, rb), :], wup_ref[...], preferred_element_type=f32)
            part = _bdot(jnp.square(jnp.maximum(a, 0.0)), wdown_ref[...])

            @pl.when(j == 0)
            def _():
                m_ref[pl.ds(r, rb), :] = part

            @pl.when(j > 0)
            def _():
                m_ref[pl.ds(r, rb), :] += part
            return carry
        lax.fori_loop(0, nrb, rows, 0)

    return _pcall(
        body, "mlp_core_fwd" + tag, (nff,),
        [pl.BlockSpec((tp, D_MODEL), lambda j: (0, 0)), pl.BlockSpec((D_MODEL, FF_BLK), lambda j: (0, j)),
         pl.BlockSpec((FF_BLK, D_MODEL), lambda j: (j, 0))],
        [pl.BlockSpec((tp, D_MODEL), lambda j: (0, 0))], [jax.ShapeDtypeStruct((tp, D_MODEL), f32)], [], [u, w_up, w_down],
        ("arbitrary",), carry)[0]


def _mlp_core_bwd(tag, u, dm, w_up, w_down, carry=None):
    tp = u.shape[0]
    rb, nrb = _row_blocks(tp)
    nff = D_FF // FF_BLK

    def body(u_hbm, dm_hbm, wup_ref, wdown_ref, du_hbm, dwup_ref, dwdown_ref, u_s, dm_s, du_s):
        j = pl.program_id(0)

        @pl.when(j == 0)
        def _():
            pltpu.sync_copy(u_hbm, u_s)
            pltpu.sync_copy(dm_hbm, dm_s)

        def rows(i, carry):
            r = pl.multiple_of(i * rb, 16)
            ub = u_s[pl.ds(r, rb), :]
            dmb = dm_s[pl.ds(r, rb), :].astype(bf16)
            a = jnp.dot(ub, wup_ref[...], preferred_element_type=f32)
            ra = jnp.maximum(a, 0.0)
            da = (_bdot_nt(dmb, wdown_ref[...]) * (2.0 * ra)).astype(bf16)
            dwd = _bdot_tn(ra * ra, dmb)
            dwu = _bdot_tn(ub, da)
            dub = _bdot_nt(da, wup_ref[...])

            @pl.when(i == 0)
            def _():
                dwdown_ref[...] = dwd
                dwup_ref[...] = dwu

            @pl.when(i > 0)
            def _():
                dwdown_ref[...] += dwd
                dwup_ref[...] += dwu

            @pl.when(j == 0)
            def _():
                du_s[pl.ds(r, rb), :] = dub

            @pl.when(j > 0)
            def _():
                du_s[pl.ds(r, rb), :] += dub
            return carry
        lax.fori_loop(0, nrb, rows, 0)

        @pl.when(j == nff - 1)
        def _():
            pltpu.sync_copy(du_s, du_hbm)

    anyspec = pl.BlockSpec(memory_space=pl.ANY)
    return _pcall(
        body, "mlp_core_bwd" + tag, (nff,),
        [anyspec, anyspec, pl.BlockSpec((D_MODEL, FF_BLK), lambda j: (0, j)), pl.BlockSpec((FF_BLK, D_MODEL), lambda j: (j, 0))],
        [anyspec, pl.BlockSpec((D_MODEL, FF_BLK), lambda j: (0, j)), pl.BlockSpec((FF_BLK, D_MODEL), lambda j: (j, 0))],
        [jax.ShapeDtypeStruct((tp, D_MODEL), f32), jax.ShapeDtypeStruct((D_MODEL, D_FF), f32), jax.ShapeDtypeStruct((D_FF, D_MODEL), f32)],
        [pltpu.VMEM((tp, D_MODEL), bf16), pltpu.VMEM((tp, D_MODEL), f32), pltpu.VMEM((tp, D_MODEL), f32)],
        [u, dm, w_up, w_down], ("arbitrary",), carry)


def _resid_rms_step(rows, consts, nds, states, _):
    m, h = rows
    (w,) = consts
    return (h + _rms(m, w),), ()


SCAN_RB = 16


def _scan_inplace(xr, xi, ar0, ai0, tp, reverse, pr, pi):
    rb, cs = SCAN_RB, 64
    nb = cs // rb
    w = xr.shape[1]
    nch = tp // cs
    rowid = lax.broadcasted_iota(jnp.int32, (rb, w), 0)

    def mac(cr, ci, sr, si, ar, ai):
        return cr + ar * sr - ai * si, ci + ar * si + ai * sr

    ar, ai = ar0, ai0
    d = 1
    while d < cs:
        arb, aib = jnp.broadcast_to(ar, (rb, w)), jnp.broadcast_to(ai, (rb, w))
        edge = 0 if not reverse else nb - 1

        def level(c, carry, d=d, arb=arb, aib=aib):
            base = pl.multiple_of(c * cs, cs)
            order = range(nb - 1, -1, -1) if not reverse else range(nb)
            for b in order:
                r0 = base + rb * b
                lo = b - d // rb if not reverse else b + d // rb
                if d >= rb:
                    if lo < 0 or lo >= nb:
                        continue
                    src = base + rb * lo
                    sr, si = xr[pl.ds(src, rb), :], xi[pl.ds(src, rb), :]
                elif b == edge:
                    if not reverse:
                        sr = jnp.where(rowid >= d, pltpu.roll(xr[pl.ds(r0, rb), :], d, 0), 0.0)
                        si = jnp.where(rowid >= d, pltpu.roll(xi[pl.ds(r0, rb), :], d, 0), 0.0)
                    else:
                        sr = jnp.where(rowid < rb - d, pltpu.roll(xr[pl.ds(r0, rb), :], rb - d, 0), 0.0)
                        si = jnp.where(rowid < rb - d, pltpu.roll(xi[pl.ds(r0, rb), :], rb - d, 0), 0.0)
                elif not reverse:
                    sr = pltpu.roll(xr[pl.ds(r0 - 8, rb + 8), :], d, 0)[8:, :]
                    si = pltpu.roll(xi[pl.ds(r0 - 8, rb + 8), :], d, 0)[8:, :]
                else:
                    sr = pltpu.roll(xr[pl.ds(r0, rb + 8), :], rb + 8 - d, 0)[:rb, :]
                    si = pltpu.roll(xi[pl.ds(r0, rb + 8), :], rb + 8 - d, 0)[:rb, :]
                nr, ni = mac(xr[pl.ds(r0, rb), :], xi[pl.ds(r0, rb), :], sr, si, arb, aib)
                xr[pl.ds(r0, rb), :] = nr
                xi[pl.ds(r0, rb), :] = ni
            return carry
        lax.fori_loop(0, nch, level, 0)
        ar, ai = ar * ar - ai * ai, 2.0 * ar * ai
        d *= 2

    krow = lax.broadcasted_iota(jnp.int32, (cs, w), 0)
    expo = krow + 1 if not reverse else cs - krow
    tr, ti = jnp.ones((cs, w), f32), jnp.zeros((cs, w), f32)
    qr, qi = ar0, ai0
    for j in range(7):
        on = ((expo >> j) & 1) == 1
        fr, fi = jnp.where(on, qr, 1.0), jnp.where(on, qi, 0.0)
        tr, ti = tr * fr - ti * fi, tr * fi + ti * fr
        qr, qi = qr * qr - qi * qi, 2.0 * qr * qi
    pr[...] = tr
    pi[...] = ti

    def across(i, carry):
        c = i if not reverse else nch - 1 - i
        base = pl.multiple_of(c * cs, cs)
        cr, ci = carry
        out = carry
        for b in range(nb):
            r0 = base + rb * b
            nr, ni = mac(xr[pl.ds(r0, rb), :], xi[pl.ds(r0, rb), :], pr[rb * b:rb * b + rb, :], pi[rb * b:rb * b + rb, :], cr, ci)
            xr[pl.ds(r0, rb), :] = nr
            xi[pl.ds(r0, rb), :] = ni
            if not reverse and b == nb - 1:
                out = (nr[rb - 1:rb, :], ni[rb - 1:rb, :])
            if reverse and b == 0:
                out = (nr[0:1, :], ni[0:1, :])
        return out
    lax.fori_loop(0, nch, across, (jnp.zeros((1, w), f32), jnp.zeros((1, w), f32)))


def _s5_blocks(tp):
    rbm = tp // 8 if (tp // 8) % 8 == 0 and tp % 8 == 0 else tp
    return rbm, tp // rbm


def _s5_in_specs(proj, tp):
    ucol = OFF["s5u"] // 128
    return [
        pl.BlockSpec((tp, 128), lambda cb: (0, ucol + cb)),
        pl.BlockSpec((128, 512), lambda cb: (cb, 0)),
        pl.BlockSpec((128, 512), lambda cb: (cb, 0)),
        pl.BlockSpec((1, 512), lambda cb: (0, cb)),
        pl.BlockSpec((1, 512), lambda cb: (0, cb)),
        pl.BlockSpec((512, 128), lambda cb: (cb, 0)),
        pl.BlockSpec((512, 128), lambda cb: (cb, 0)),
    ]


def _s5_core_fwd(tag, proj, bbr, bbi, ar, ai, ccr, cci, carry=None):
    tp = proj.shape[0]
    rbm, nb = _s5_blocks(tp)

    def body(u_ref, bbr_ref, bbi_ref, ar_ref, ai_ref, ccr_ref, cci_ref, y_ref, xr, xi, pr, pi):
        def fill(i, carry):
            r = pl.multiple_of(i * rbm, 8)
            ub = u_ref[pl.ds(r, rbm), :]
            xr[pl.ds(r, rbm), :] = _bdot(ub, bbr_ref[...])
            xi[pl.ds(r, rbm), :] = _bdot(ub, bbi_ref[...])
            return carry
        lax.fori_loop(0, nb, fill, 0)
        _scan_inplace(xr, xi, ar_ref[...], ai_ref[...], tp, False, pr, pi)

        def out(i, carry):
            r = pl.multiple_of(i * rbm, 8)
            y_ref[pl.ds(r, rbm), :] = (_bdot(xr[pl.ds(r, rbm), :], ccr_ref[...]) - _bdot(xi[pl.ds(r, rbm), :], cci_ref[...]))
            return carry
        lax.fori_loop(0, nb, out, 0)

    return _pcall(body, "s5_core_fwd" + tag, (4,), _s5_in_specs(proj, tp), [pl.BlockSpec((tp, 128), lambda cb: (0, cb))],
                  [jax.ShapeDtypeStruct((tp, 512), f32)], [pltpu.VMEM((tp, 512), f32)] * 2 + [pltpu.VMEM((64, 512), f32)] * 2,
                  [proj, bbr, bbi, ar, ai, ccr, cci], ("arbitrary",), carry)[0]


def _s5_core_bwd(tag, proj, bbr, bbi, ar, ai, ccr, cci, dy, du_post, carry=None):
    tp = proj.shape[0]
    rbm, nb = _s5_blocks(tp)
    rb = SCAN_RB

    def body(u_ref, bbr_ref, bbi_ref, ar_ref, ai_ref, ccr_ref, cci_ref, dy_ref, dup_ref,
             du_ref, dbbr_ref, dbbi_ref, dar_ref, dai_ref, dccr_ref, dcci_ref, xr, xi, gr, gi, pr, pi):
        dccr_ref[...] = jnp.zeros_like(dccr_ref)
        dcci_ref[...] = jnp.zeros_like(dcci_ref)
        dbbr_ref[...] = jnp.zeros_like(dbbr_ref)
        dbbi_ref[...] = jnp.zeros_like(dbbi_ref)

        def fill(i, carry):
            r = pl.multiple_of(i * rbm, 8)
            ub = u_ref[pl.ds(r, rbm), :]
            xr[pl.ds(r, rbm), :] = _bdot(ub, bbr_ref[...])
            xi[pl.ds(r, rbm), :] = _bdot(ub, bbi_ref[...])
            return carry
        lax.fori_loop(0, nb, fill, 0)
        _scan_inplace(xr, xi, ar_ref[...], ai_ref[...], tp, False, pr, pi)

        def seed(i, carry):
            r = pl.multiple_of(i * rbm, 8)
            dyb = dy_ref[pl.ds(r, rbm), :]
            dccr_ref[...] += _bdot_tn(xr[pl.ds(r, rbm), :], dyb)
            dcci_ref[...] -= _bdot_tn(xi[pl.ds(r, rbm), :], dyb)
            gr[pl.ds(r, rbm), :] = _bdot_nt(dyb, ccr_ref[...])
            gi[pl.ds(r, rbm), :] = -_bdot_nt(dyb, cci_ref[...])
            return carry
        lax.fori_loop(0, nb, seed, 0)
        _scan_inplace(gr, gi, ar_ref[...], -ai_ref[...], tp, True, pr, pi)

        w = xr.shape[1]
        rowid = lax.broadcasted_iota(jnp.int32, (rb, w), 0)

        def prods(pr, pi, r0):
            g_r, g_i = gr[pl.ds(r0, rb), :], gi[pl.ds(r0, rb), :]
            return pr * g_r + pi * g_i, pr * g_i - pi * g_r

        def dacc(i, carry):
            r0 = pl.multiple_of((i + 1) * rb, 8)
            lo = pl.multiple_of(r0 - 8, 8)
            pr = pltpu.roll(xr[pl.ds(lo, rb + 8), :], 1, 0)[8:, :]
            pi = pltpu.roll(xi[pl.ds(lo, rb + 8), :], 1, 0)[8:, :]
            a, b = prods(pr, pi, r0)
            return carry[0] + a, carry[1] + b
        pr0 = jnp.where(rowid >= 1, pltpu.roll(xr[pl.ds(0, rb), :], 1, 0), 0.0)
        pi0 = jnp.where(rowid >= 1, pltpu.roll(xi[pl.ds(0, rb), :], 1, 0), 0.0)
        acc_r, acc_i = lax.fori_loop(0, tp // rb - 1, dacc, prods(pr0, pi0, 0))
        dar_ref[...] = jnp.sum(acc_r, axis=0, keepdims=True)
        dai_ref[...] = jnp.sum(acc_i, axis=0, keepdims=True)

        def tail(i, carry):
            r = pl.multiple_of(i * rbm, 8)
            g_r, g_i = gr[pl.ds(r, rbm), :], gi[pl.ds(r, rbm), :]
            ub = u_ref[pl.ds(r, rbm), :]
            du_ref[pl.ds(r, rbm), :] = _bdot_nt(g_r, bbr_ref[...]) + _bdot_nt(g_i, bbi_ref[...]) + dup_ref[pl.ds(r, rbm), :]
            dbbr_ref[...] += _bdot_tn(ub, g_r)
            dbbi_ref[...] += _bdot_tn(ub, g_i)
            return carry
        lax.fori_loop(0, nb, tail, 0)

    col = lambda cb: (0, cb)
    blk = lambda cb: (cb, 0)
    return _pcall(
        body, "s5_core_bwd" + tag, (4,),
        _s5_in_specs(proj, tp) + [pl.BlockSpec((tp, 128), col), pl.BlockSpec((tp, 128), col)],
        [pl.BlockSpec((tp, 128), col), pl.BlockSpec((128, 512), blk), pl.BlockSpec((128, 512), blk),
         pl.BlockSpec((1, 512), col), pl.BlockSpec((1, 512), col), pl.BlockSpec((512, 128), blk), pl.BlockSpec((512, 128), blk)],
        [jax.ShapeDtypeStruct((tp, 512), f32), jax.ShapeDtypeStruct((512, 512), f32), jax.ShapeDtypeStruct((512, 512), f32),
         jax.ShapeDtypeStruct((1, 2048), f32), jax.ShapeDtypeStruct((1, 2048), f32), jax.ShapeDtypeStruct((2048, 128), f32),
         jax.ShapeDtypeStruct((2048, 128), f32)],
        [pltpu.VMEM((tp, 512), f32)] * 4 + [pltpu.VMEM((64, 512), f32)] * 2, [proj, bbr, bbi, ar, ai, ccr, cci, dy, du_post],
        ("arbitrary",), carry)


_EYE8 = np.eye(8, dtype=np.float32)


def _bb_dense(bb):
    return jnp.einsum("cgjp,gh->cgjhp", bb.reshape(4, 8, 16, 64), _EYE8).reshape(512, 512)


def _bb_diag(d):
    return jnp.einsum("cgjhp,gh->cgjp", d.reshape(4, 8, 16, 8, 64), _EYE8).reshape(512, 64)


def _cc_dense(cmat):
    return jnp.einsum("cgjp,gh->cgphj", cmat.reshape(4, 8, 16, 64), _EYE8).reshape(2048, 128)


def _cc_diag(d):
    return jnp.einsum("cgphj,gh->cgjp", d.reshape(4, 8, 64, 8, 16), _EYE8).reshape(32, 16, 64)


def _tables(tp):
    cos, sin = _rope_tables(tp)
    rep = np.zeros((512, 32), np.float32)
    rep[np.arange(512), np.arange(512) // 16] = 1.0
    lmat = np.zeros((8, 8), np.float32)
    for l in range(DEPTH):
        lmat[l, 1:l + 1] = 1.0
    return dict(cos=cos, sin=sin, ret=_ret_tables(CHUNK), tri=_tri(CHUNK), ssd=_ssd_tables(CHUNK),
                rep=jnp.asarray(rep), lmat=jnp.asarray(lmat))


def _tiles(tp):
    return dict(tr=_div_tile(tp, 352, 16), tmg=_div_tile(tp, 192, 16))


def _mixer_rows(proj, c):
    ret = [_r(proj, c, 256, OFF["rq"] // 256), _r(proj, c, 256, OFF["rk"] // 256), _r(proj, c, 512, OFF["rv"] // 512),
           _r(proj, c, 512, OFF["rg"] // 512)]
    ssd = [_r(proj, c, 512, OFF["sz"] // 512), _r(proj, c, 1024, OFF["sxbc"] // 1024), _r(proj, c, 128, OFF["dt"] // 128)]
    hg = [_r(proj, c, 512, OFF[k] // 512) for k in ("hq", "hf", "hi", "hg")]
    return ret, ssd, hg


def _out2(rows, w, rb, dtype=f32):
    return (rows, w), dtype, (rb, w), lambda n: (n, 0)


def _s5_prep_rows(p):
    return [_full(p["lam_re"]), _full(p["lam_im"]), _full(p["lstep"]), _full(p["bt_re"]), _full(p["bt_im"])]


def _s5_consts(p, tabs, tag):
    whole = lambda s: (s, f32, s, lambda n: (0, 0))
    ab_re, ab_im, bb_re, bb_im = _seq_fwd("s5_prep" + tag, _s5prep_step, _s5_prep_rows(p), [], [tabs["rep"]], [],
                                          [whole((32, 64)), whole((32, 64)), whole((512, 64)), whole((512, 64))], 1)
    return (_bb_dense(bb_re).astype(bf16), _bb_dense(bb_im).astype(bf16), ab_re.reshape(1, 2048), ab_im.reshape(1, 2048),
            _cc_dense(p["c_re"]).astype(bf16), _cc_dense(p["c_im"]).astype(bf16))


def _rmsn_step_b(rows, consts, nds, states, n):
    (o,), _ = _rmsn_step(rows, consts, nds, states, n)
    return (o, rows[0]), ()


def _layer_fwd(h, p, tabs, tag, carry):
    tp = h.shape[0]
    c = CHUNK
    nch = tp // c
    tl = _tiles(tp)
    tr, tmg = tl["tr"], tl["tmg"]
    (u,) = _seq_fwd("rms_premix" + tag, _rmsn_step, [_r(h, tr, 1024)], [p["npm"]], [], [], [_out2(tp, 1024, tr, bf16)], tp // tr)
    proj = _mm("in_proj" + tag, u, p["w_in"], "nn", _div_tile(tp, 1056, 16), 1408, 1024, carry=carry.get("in"))
    if "in_done" in carry:
        p = dict(p, **carry["in_done"]())
    ret_rows, ssd_rows, hg_rows = _mixer_rows(proj, c)
    y_ret, ret_s = _seq_fwd("ret_fwd" + tag, _ret_step, ret_rows + [_r(tabs["cos"], c, 128), _r(tabs["sin"], c, 128)],
                            [p["ret_gn"]], tabs["ret"], [(256, 512)], [_out2(tp, 512, c)], nch, save_states=True,
                            carry=carry.get("ret"))
    ycore = _s5_core_fwd(tag, proj, *_s5_consts(p, tabs, tag), carry=carry.get("s5"))
    (y_s5,) = _seq_fwd("s5_post" + tag, _s5post_step, [_r(ycore, tr, 512), _r(proj, tr, 512, OFF["s5u"] // 512)],
                       [p["s5_d"], p["glu_w"], p["glu_b"]], [], [], [_out2(tp, 512, tr)], tp // tr)
    y_ssd, ssd_s, ssd_tail = _seq_fwd(
        "ssd_fwd" + tag, _ssd_step, ssd_rows, [p["conv_w"], p["conv_b"], p["dt_bias"], p["a_log"], p["ssd_d"], p["ssd_nw"]],
        [tabs["tri"]] + tabs["ssd"], [(256, 512), (8, 1024)], [_out2(tp, 512, c)], nch, save_states=True, carry=carry.get("ssd"))
    y_hg, hg_s = _seq_fwd("hg_fwd" + tag, _hg_step, hg_rows, [p["lb"], p["hg_nw"]], [tabs["tri"]], [(128, 512)],
                          [_out2(tp, 512, c)], nch, save_states=True, carry=carry.get("hg"))
    (h_mid,) = _seq_fwd(
        "merge_fwd" + tag, _merge_step,
        [_r(y_ret, tmg, 512), _r(y_s5, tmg, 512), _r(y_ssd, tmg, 512), _r(y_hg, tmg, 512), _r(proj, tmg, 4096, 0),
         _r(h, tmg, 1024)],
        [p["w_branch"], p["w_out"], p["npostmix"]], [], [], [_out2(tp, 1024, tmg)], tp // tmg, carry=carry.get("mg"))
    if "mg_done" in carry:
        p = dict(p, **carry["mg_done"]())
    (u2,) = _seq_fwd("rms_premlp" + tag, _rmsn_step, [_r(h_mid, tr, 1024)], [p["npremlp"]], [], [],
                     [_out2(tp, 1024, tr, bf16)], tp // tr)
    m = _mlp_core_fwd(tag, u2, p["w_up"], p["w_down"], carry=carry.get("mlp"))
    (h_new,) = _seq_fwd("mlp_post" + tag, _resid_rms_step, [_r(m, tr, 1024), _r(h_mid, tr, 1024)], [p["npostmlp"]], [], [],
                        [_out2(tp, 1024, tr)], tp // tr)
    saved = dict(h=h, u=u, proj=proj, ret_s=ret_s, ycore=ycore, ssd_s=ssd_s, ssd_tail=ssd_tail, hg_s=hg_s,
                 y_ret=y_ret, y_s5=y_s5, y_ssd=y_ssd, y_hg=y_hg, h_mid=h_mid, u2=u2, m=m)
    return h_new, saved, p


def _layer_bwd(dh, p, sv, tabs, tag, carry):
    tp = dh.shape[0]
    c = CHUNK
    nch = tp // c
    tl = _tiles(tp)
    tr, tmg = tl["tr"], tl["tmg"]
    proj = sv["proj"]
    g = {}
    (d_m, d_hmid), (g["npostmlp"],) = _seq_bwd(
        "mlp_post_bwd" + tag, _resid_rms_step, [_r(sv["m"], tr, 1024), _r(sv["h_mid"], tr, 1024)], [True, True],
        [p["npostmlp"]], [], [], [_r(dh, tr, 1024)], tp // tr)
    d_u2, g["w_up"], g["w_down"] = _mlp_core_bwd(tag, sv["u2"], d_m, p["w_up"], p["w_down"], carry=carry.get("mlp"))
    (d_hmid,), (g["npremlp"],) = _seq_bwd(
        "rms_premlp_bwd" + tag, _rmsn_step_b, [_r(sv["h_mid"], tr, 1024)], [True], [p["npremlp"]], [], [],
        [_r(d_u2, tr, 1024), _r(d_hmid, tr, 1024)], tp // tr)
    (dy_ret, dy_s5, dy_ssd, dy_hg, d_gates, d_h1), (g["w_branch"], g["w_out"], g["npostmix"]) = _seq_bwd(
        "merge_bwd" + tag, _merge_step,
        [_r(sv["y_ret"], tmg, 512), _r(sv["y_s5"], tmg, 512), _r(sv["y_ssd"], tmg, 512), _r(sv["y_hg"], tmg, 512),
         _r(proj, tmg, 4096, 0), _r(sv["h"], tmg, 1024)], [True] * 6,
        [p["w_branch"], p["w_out"], p["npostmix"]], [], [], [_r(d_hmid, tmg, 1024)], tp // tmg)
    ret_rows, ssd_rows, hg_rows = _mixer_rows(proj, c)
    (d_hq, d_hf, d_hi, d_hg), (g["lb"], g["hg_nw"]) = _seq_bwd(
        "hg_bwd" + tag, _hg_step, hg_rows, [True] * 4, [p["lb"], p["hg_nw"]], [tabs["tri"]], [sv["hg_s"]],
        [_r(dy_hg, c, 512)], nch, carry=carry.get("hg"))
    (d_z, d_xbc, d_dt), (g["conv_w"], g["conv_b"], g["dt_bias"], g["a_log"], g["ssd_d"], g["ssd_nw"]) = _seq_bwd(
        "ssd_bwd" + tag, _ssd_step, ssd_rows, [True] * 3,
        [p["conv_w"], p["conv_b"], p["dt_bias"], p["a_log"], p["ssd_d"], p["ssd_nw"]], [tabs["tri"]] + tabs["ssd"],
        [sv["ssd_s"], sv["ssd_tail"]], [_r(dy_ssd, c, 512)], nch, carry=carry.get("ssd"))
    (d_ycore, du_post), (g["s5_d"], g["glu_w"], g["glu_b"]) = _seq_bwd(
        "s5_post_bwd" + tag, _s5post_step, [_r(sv["ycore"], tr, 512), _r(proj, tr, 512, OFF["s5u"] // 512)], [True, True],
        [p["s5_d"], p["glu_w"], p["glu_b"]], [], [], [_r(dy_s5, tr, 512)], tp // tr)
    du_s5, dbbr, dbbi, dar, dai, dccr, dcci = _s5_core_bwd(tag, proj, *_s5_consts(p, tabs, tag + "b"), d_ycore, du_post,
                                                           carry=carry.get("s5"))
    g["c_re"], g["c_im"] = _cc_diag(dccr), _cc_diag(dcci)
    (g["lam_re"], g["lam_im"], g["lstep"], g["bt_re"], g["bt_im"]), _ = _seq_bwd(
        "s5_prep_bwd" + tag, _s5prep_step, _s5_prep_rows(p), [True] * 5, [], [tabs["rep"]], [],
        [_full(dar.reshape(32, 64)), _full(dai.reshape(32, 64)), _full(_bb_diag(dbbr)), _full(_bb_diag(dbbi))], 1)
    if "late" in carry:
        carry = dict(carry, **carry["late"](g))
    (d_q, d_k, d_v, d_g), (g["ret_gn"],) = _seq_bwd(
        "ret_bwd" + tag, _ret_step, ret_rows + [_r(tabs["cos"], c, 128), _r(tabs["sin"], c, 128)], [True] * 4 + [False] * 2,
        [p["ret_gn"]], tabs["ret"], [sv["ret_s"]], [_r(dy_ret, c, 512)], nch, carry=carry.get("ret"))
    dproj = jnp.concatenate([d_gates, d_q, d_k, d_v, d_g, du_s5, d_xbc, d_z, d_hq, d_hf, d_hi, d_hg, d_dt], axis=1).astype(bf16)
    g["w_in"] = _mm("in_proj_dw" + tag, sv["u"], dproj, "tn", 512, 1408, _div_tile(tp, 704, 16), carry=carry.get("dw"))
    du = _mm("in_proj_dx" + tag, dproj, p["w_in"], "nt", _div_tile(tp, 1056, 16), 1024, 1408, carry=carry.get("dx"))
    (dh_prev,), (g["npm"],) = _seq_bwd("rms_premix_bwd" + tag, _rmsn_step_b, [_r(sv["h"], tr, 1024)], [True], [p["npm"]], [], [],
                                       [_r(du, tr, 1024), _r(d_h1, tr, 1024)], tp // tr)
    return dh_prev, g


def _loss_call(h, tgt, lo, hi):
    tp = h.shape[0]
    tr = _div_tile(tp, 352, 16)

    def step(rows, consts, nds, states, n):
        hh, tt = rows
        row = n * tr + lax.broadcasted_iota(jnp.int32, hh.shape, 0)
        err = jnp.where((row >= lo) & (row < hi), hh - tt, 0.0)
        part = 0.5 * jnp.sum(err * err) * (1.0 / D_MODEL)
        return (err * (1.0 / D_MODEL), jnp.zeros((8, 128), f32) + part), ()

    dh, parts = _seq_fwd("loss_head", step, [_r(h, tr, 1024), _r(tgt, tr, 1024)], [], [], [],
                         [_out2(tp, 1024, tr), ((8 * (tp // tr), 128), f32, (8, 128), lambda n: (n, 0))], tp // tr)
    return dh, jnp.sum(parts[::8, 0])


def _adam_step(rows, consts, nds, states, _):
    g8, w, m, v = rows
    g = g8[0].astype(f32)
    for d in range(1, N_DEV):
        g = g + g8[d].astype(f32)
    m2 = ADAM_B1 * m + (1.0 - ADAM_B1) * g
    v2 = ADAM_B2 * v + (1.0 - ADAM_B2) * jnp.square(g)
    m_hat = m2 / (1.0 - ADAM_B1 ** ADAM_STEP)
    v_hat = v2 / (1.0 - ADAM_B2 ** ADAM_STEP)
    delta = -ADAM_LR * (m_hat / (jnp.sqrt(v_hat) + ADAM_EPS) + ADAM_WD * w)
    return (g, delta, m2, v2), ()


def _adam_call(name, g8, w, m, v, carry=None):
    r, wd = w.shape
    tb = r
    for cand in range(16, r + 1, 16):
        if r % cand == 0 and cand * wd <= 256 * 1024:
            tb = cand
    o = ((r, wd), f32, (tb, wd), lambda n: (n, 0))
    return _seq_fwd(name, _adam_step, [(g8, (N_DEV, tb, wd), lambda n: (0, n, 0)), _r(w, tb, wd), _r(m, tb, wd), _r(v, tb, wd)],
                    [], [], [], [o, o, o, o], r // tb, carry=carry)


WEIGHTS = ['meta_tokens', 'w_in', 'w_branch', 'w_out', 'norm_pre_mix', 'norm_post_mix', 'norm_pre_mlp', 'norm_post_mlp',
           'w_up', 'w_down', 'ret_gn_w', 's5_lam_re', 's5_lam_im', 's5_b_re', 's5_b_im', 's5_c_re', 's5_c_im', 's5_d',
           's5_log_step', 's5_glu_w', 's5_glu_b', 'ssd_conv_w', 'ssd_conv_b', 'ssd_dt_bias', 'ssd_a_log', 'ssd_d',
           'ssd_norm_w', 'hgrn_lb', 'hgrn_norm_w']
SHARDED = [("w_in", 2), ("w_branch", 3), ("w_out", 1), ("w_up", 2), ("w_down", 1), ("s5_glu_w", 2), ("meta_tokens", 1),
           ("ssd_conv_w", 2)]
N_BF16 = 6
REPL = [n for n in WEIGHTS if n not in dict(SHARDED)]
SMALL_PAD = 512
WIRE = bf16
SHARD_COLS = IN_DIM // N_DEV


def _col_pieces():
    out, pos = [], 0
    for a, b in _orig_col_slices() + [(3584, 3592)]:
        if a == 3584:
            pos = OFF["dt"]
        while a < b:
            e = min(b, (a // SHARD_COLS + 1) * SHARD_COLS)
            out.append((a, e, pos))
            pos += e - a
            a = e
    return out


def _w_in_from_shards(got_l):
    parts = [got_l[a // SHARD_COLS][:, a % SHARD_COLS:a % SHARD_COLS + (b - a)] for a, b, _ in _col_pieces()]
    parts.append(jnp.zeros((got_l.shape[1], NP - IN_DIM - (OFF["dt"] - 9728)), got_l.dtype))
    return jnp.concatenate(parts, axis=1)


def _w_in_to_shards(g):
    pieces = sorted(_col_pieces())
    blocks = []
    for d in range(N_DEV):
        blocks.append(jnp.concatenate([g[:, m:m + (b - a)] for a, b, m in pieces if a // SHARD_COLS == d], axis=1))
    return jnp.stack(blocks, axis=0)


def _to8(full, axis):
    sh = full.shape
    return jnp.moveaxis(full.reshape(sh[:axis] + (N_DEV, sh[axis] // N_DEV) + sh[axis + 1:]), axis, 0)


def _from8(g8, axis):
    r = jnp.moveaxis(g8, 0, axis)
    sh = r.shape
    return r.reshape(sh[:axis] + (sh[axis] * sh[axis + 1],) + sh[axis + 2:])


def _pack(arrs, pad_rows):
    flat = jnp.concatenate([a.reshape(-1) for a in arrs])
    n = flat.shape[0]
    total = -(-n // (128 * pad_rows)) * (128 * pad_rows)
    if total != n:
        flat = jnp.concatenate([flat, jnp.zeros((total - n,), flat.dtype)])
    return flat.reshape(total // 128, 128)


def _unpack(flat, shapes):
    v = flat.reshape(-1)
    out, pos = [], 0
    for s in shapes:
        n = int(np.prod(s))
        out.append(v[pos:pos + n].reshape(tuple(s)))
        pos += n
    return out


def _rows2d(a, lead=0):
    return a.reshape(a.shape[:lead] + (-1, a.shape[-1]))


def _local_step(x0, tgt0, wf, first_w, next_w, send_grads):
    seq = x0.shape[0]
    t = N_META + seq
    tp = -(-t // CHUNK) * CHUNK
    tabs = _tables(tp)
    lb_in = jnp.concatenate([wf["hgrn_lb"], jnp.zeros((8 - DEPTH, BW), f32)], axis=0)
    (lb_all,) = _seq_fwd("lb_prep", _lb_step, [_full(lb_in)], [], [tabs["lmat"]], [], [((8, BW), f32, (8, BW), lambda n: (0, 0))], 1)

    def pad128(a):
        return jnp.concatenate([a, jnp.zeros((128 - a.shape[0],), f32)]).reshape(1, 128)

    def layer_params(l, big):
        return dict(
            big,
            npm=wf["norm_pre_mix"][l].reshape(1, D_MODEL), npostmix=wf["norm_post_mix"][l].reshape(1, D_MODEL),
            npremlp=wf["norm_pre_mlp"][l].reshape(1, D_MODEL), npostmlp=wf["norm_post_mlp"][l].reshape(1, D_MODEL),
            ret_gn=wf["ret_gn_w"][l].reshape(1, BW), lam_re=wf["s5_lam_re"][l], lam_im=wf["s5_lam_im"][l],
            lstep=wf["s5_log_step"][l].reshape(S5_G, 1),
            bt_re=wf["s5_b_re"][l].transpose(0, 2, 1).reshape(S5_G * S5_J, S5_P),
            bt_im=wf["s5_b_im"][l].transpose(0, 2, 1).reshape(S5_G * S5_J, S5_P),
            c_re=wf["s5_c_re"][l], c_im=wf["s5_c_im"][l], s5_d=wf["s5_d"][l].reshape(1, BW),
            glu_b=wf["s5_glu_b"][l].reshape(1, 2 * BW), conv_w=wf["ssd_conv_w"][l],
            conv_b=wf["ssd_conv_b"][l].reshape(1, 1024), dt_bias=pad128(wf["ssd_dt_bias"][l]),
            a_log=pad128(wf["ssd_a_log"][l]), ssd_d=pad128(wf["ssd_d"][l]), ssd_nw=wf["ssd_norm_w"][l].reshape(1, BW),
            lb=lb_all[l].reshape(1, BW), hg_nw=wf["hgrn_norm_w"][l].reshape(1, BW))

    zpad = jnp.zeros((tp - t, D_MODEL), f32)
    h = jnp.concatenate([wf["meta_tokens"], x0, zpad], axis=0)
    tgt = jnp.concatenate([jnp.zeros((N_META, D_MODEL), f32), tgt0, zpad], axis=0)
    params, saved, big = [], [], first_w
    for l in range(DEPTH):
        carry, arrived = next_w(l)
        h, sv, p = _layer_fwd(h, layer_params(l, big), tabs, "_l%d" % l, carry)
        params.append(p)
        saved.append(sv)
        big = arrived() if arrived else None
    dh, loss_local = _loss_call(h, tgt, N_META, t)
    g, carry = [None] * DEPTH, {}
    for l in reversed(range(DEPTH)):
        dh, g[l] = _layer_bwd(dh, params[l], saved[l], tabs, "_l%d" % l, carry)
        carry = send_grads(l, g[l]) if l > 0 else {}
    d_lb = jnp.concatenate([jnp.concatenate([gl["lb"] for gl in g], axis=0), jnp.zeros((8 - DEPTH, BW), f32)], axis=0)
    (d_hgrn_lb,), _ = _seq_bwd("lb_prep_bwd", _lb_step, [_full(lb_in)], [True], [], [tabs["lmat"]], [], [_full(d_lb)], 1)
    return loss_local, dh, g, d_hgrn_lb[:DEPTH]


def kernel(x, meta_tokens, w_in, w_branch, w_out, norm_pre_mix, norm_post_mix, norm_pre_mlp, norm_post_mlp, w_up, w_down, ret_gn_w, s5_lam_re, s5_lam_im, s5_b_re, s5_b_im, s5_c_re, s5_c_im, s5_d, s5_log_step, s5_glu_w, s5_glu_b, ssd_conv_w, ssd_conv_b, ssd_dt_bias, ssd_a_log, ssd_d, ssd_norm_w, hgrn_lb, hgrn_norm_w, loss_target, m_meta_tokens, m_w_in, m_w_branch, m_w_out, m_norm_pre_mix, m_norm_post_mix, m_norm_pre_mlp, m_norm_post_mlp, m_w_up, m_w_down, m_ret_gn_w, m_s5_lam_re, m_s5_lam_im, m_s5_b_re, m_s5_b_im, m_s5_c_re, m_s5_c_im, m_s5_d, m_s5_log_step, m_s5_glu_w, m_s5_glu_b, m_ssd_conv_w, m_ssd_conv_b, m_ssd_dt_bias, m_ssd_a_log, m_ssd_d, m_ssd_norm_w, m_hgrn_lb, m_hgrn_norm_w, v_meta_tokens, v_w_in, v_w_branch, v_w_out, v_norm_pre_mix, v_norm_post_mix, v_norm_pre_mlp, v_norm_post_mlp, v_w_up, v_w_down, v_ret_gn_w, v_s5_lam_re, v_s5_lam_im, v_s5_b_re, v_s5_b_im, v_s5_c_re, v_s5_c_im, v_s5_d, v_s5_log_step, v_s5_glu_w, v_s5_glu_b, v_ssd_conv_w, v_ssd_conv_b, v_ssd_dt_bias, v_ssd_a_log, v_ssd_d, v_ssd_norm_w, v_hgrn_lb, v_hgrn_norm_w):
    args = dict(locals())
    w = {n: args[n] for n in WEIGHTS}
    mom = {n: args["m_" + n] for n in WEIGHTS}
    var = {n: args["v_" + n] for n in WEIGHTS}
    names = [n for n, _ in SHARDED]
    big = names[:N_BF16]

    shard = {n: [w[n][l].astype(bf16) for l in range(DEPTH)] for n in big}
    got = _all_gather("gather_first", [shard["w_in"][0], w["meta_tokens"], w["ssd_conv_w"]])
    wf = {n: w[n] for n in REPL}
    wf["meta_tokens"] = _from8(got[1], 1)
    wf["ssd_conv_w"] = _from8(got[2], 2)

    full = dict(w_branch=lambda a: _from8(a, 2), w_out=lambda a: _from8(a, 0), w_up=lambda a: _from8(a, 1),
                w_down=lambda a: _from8(a, 0), s5_glu_w=lambda a: _from8(a, 1))
    key = dict(w_branch="w_branch", w_out="w_out", w_up="w_up", w_down="w_down", s5_glu_w="glu_w")
    half = w["w_in"].shape[1] // 2

    def arrived(pairs):
        return {key[n]: full[n](a) for n, a in pairs}

    def next_w(l):
        carry = {}
        if l == 0:
            n_in = ("w_branch", "w_out", "s5_glu_w", "w_up")
            first, second = _Carry("gather", [shard[n][0] for n in n_in]), _Carry("gather", [shard["w_down"][0]])
            carry.update({"in": first, "in_done": lambda: arrived(zip(n_in, first.result)),
                          "mg": second, "mg_done": lambda: arrived([("w_down", second.result[0])])})
        if l + 1 == DEPTH:
            return carry, None
        n_ret = ("w_branch", "w_out", "s5_glu_w")
        nxt = dict(s5=_Carry("gather", [shard["w_in"][l + 1][:half]]), mlp=_Carry("gather", [shard["w_in"][l + 1][half:]]),
                   ssd=_Carry("gather", [shard["w_up"][l + 1]]), hg=_Carry("gather", [shard["w_down"][l + 1]]),
                   ret=_Carry("gather", [shard[n][l + 1] for n in n_ret]))
        carry.update(nxt)
        return carry, lambda: dict(
            arrived(list(zip(n_ret, nxt["ret"].result)) + [("w_up", nxt["ssd"].result[0]), ("w_down", nxt["hg"].result[0])]),
            w_in=_w_in_from_shards(jnp.concatenate([nxt["s5"].result[0], nxt["mlp"].result[0]], axis=1)))

    def to_wire(gl):
        return dict(w_in=None if gl["w_in"] is None else _w_in_to_shards(gl["w_in"]).astype(WIRE), w_branch=_to8(gl["w_branch"], 2).astype(WIRE),
                    w_out=_to8(gl["w_out"], 0).astype(WIRE), w_up=_to8(gl["w_up"], 1).astype(WIRE),
                    w_down=_to8(gl["w_down"], 0).astype(WIRE), s5_glu_w=_to8(gl["glu_w"], 1).astype(WIRE))

    sent = [None] * DEPTH

    def send_grads(l, gl):
        wire = to_wire(gl)
        carry = dict(s5=_Carry("scatter", [wire["w_in"][:, :half]]), mlp=_Carry("scatter", [wire["w_up"]]),
                     ssd=_Carry("scatter", [wire["w_down"], wire["w_branch"]]),
                     hg=_Carry("scatter", [wire["w_in"][:, half:], wire["w_out"], wire["s5_glu_w"]]))
        sent[l] = lambda: dict(w_in=jnp.concatenate([carry["s5"].result[0], carry["hg"].result[0]], axis=1),
                               w_up=carry["mlp"].result[0], w_down=carry["ssd"].result[0], w_branch=carry["ssd"].result[1],
                               w_out=carry["hg"].result[1], s5_glu_w=carry["hg"].result[2])
        if l == 1:
            def late(g0):
                wire0 = to_wire(dict(g0, w_in=None))
                rt = _Carry("scatter", [wire0["w_up"]])
                dw = _Carry("scatter", [wire0["w_down"], wire0["s5_glu_w"]])
                dx = _Carry("scatter", [wire0["w_branch"], wire0["w_out"]])
                sent[0] = lambda: dict(w_up=rt.result[0], w_down=dw.result[0], s5_glu_w=dw.result[1], w_branch=dx.result[0],
                                       w_out=dx.result[1])
                return dict(ret=rt, dw=dw, dx=dx)
            carry["late"] = late
        return carry

    loss_local, dh0, g, d_hgrn_lb = _local_step(x[0], loss_target[0], wf, dict(w_in=_w_in_from_shards(got[0])), next_w, send_grads)
    seq = x.shape[1]
    t = N_META + seq
    tiny = _exchange("scatter_small", [_to8(dh0[:N_META], 1).astype(WIRE),
                                       jnp.stack([_to8(gl["conv_w"], 1) for gl in g], axis=1).astype(WIRE)])
    w_in0 = _w_in_to_shards(g[0]["w_in"]).astype(WIRE)
    riders = {n: _Carry("scatter", [w_in0[:, a:b]]) for n, a, b in (("w_up", 0, 384), ("w_down", 384, 768), ("w_branch", 768, 1024))}
    per_layer = [sent[l]() for l in range(DEPTH)]
    parts = {n: jnp.stack([pl_[n] for pl_ in per_layer], axis=1) for n in big if n != "w_in"}
    parts["meta_tokens"], parts["ssd_conv_w"] = tiny

    def stack_l(key, shape):
        return jnp.stack([gl[key].reshape(shape) for gl in g], axis=0)

    small = dict(
        norm_pre_mix=stack_l("npm", (D_MODEL,)), norm_post_mix=stack_l("npostmix", (D_MODEL,)),
        norm_pre_mlp=stack_l("npremlp", (D_MODEL,)), norm_post_mlp=stack_l("npostmlp", (D_MODEL,)),
        ret_gn_w=stack_l("ret_gn", (BW,)), s5_lam_re=stack_l("lam_re", (S5_G, S5_P)), s5_lam_im=stack_l("lam_im", (S5_G, S5_P)),
        s5_b_re=stack_l("bt_re", (S5_G, S5_J, S5_P)).transpose(0, 1, 3, 2),
        s5_b_im=stack_l("bt_im", (S5_G, S5_J, S5_P)).transpose(0, 1, 3, 2),
        s5_c_re=stack_l("c_re", (S5_G, S5_J, S5_P)), s5_c_im=stack_l("c_im", (S5_G, S5_J, S5_P)),
        s5_d=stack_l("s5_d", (BW,)), s5_log_step=stack_l("lstep", (S5_G,)), s5_glu_b=stack_l("glu_b", (2 * BW,)),
        ssd_conv_b=stack_l("conv_b", (1024,)), ssd_dt_bias=stack_l("dt_bias", (128,))[:, :SSD_HEADS],
        ssd_a_log=stack_l("a_log", (128,))[:, :SSD_HEADS], ssd_d=stack_l("ssd_d", (128,))[:, :SSD_HEADS],
        ssd_norm_w=stack_l("ssd_nw", (BW,)), hgrn_lb=d_hgrn_lb, hgrn_norm_w=stack_l("hg_nw", (BW,)))
    small_ride = _Carry("gather", [_pack([small[n] for n in REPL], SMALL_PAD)])
    out = {k: {} for k in ("grad", "delta", "m", "v")}
    for n in [x for x in names if x != "w_in"] + ["w_in"]:
        if n == "w_in":
            w_in_l0 = jnp.concatenate([riders[x].result[0] for x in ("w_up", "w_down", "w_branch")], axis=1)
            parts[n] = jnp.stack([w_in_l0] + [pl_[n] for pl_ in per_layer[1:]], axis=1)
        res = _adam_call("adamw_" + n, _rows2d(parts[n], 1), _rows2d(w[n]), _rows2d(mom[n]), _rows2d(var[n]),
                         carry=small_ride if n == "w_in" else riders.get(n))
        for k, r in zip(("grad", "delta", "m", "v"), res):
            out[k][n] = r.reshape(w[n].shape)
    res = _adam_call("adamw_replicated", small_ride.result[0], _pack([w[n] for n in REPL], SMALL_PAD),
                     _pack([mom[n] for n in REPL], SMALL_PAD), _pack([var[n] for n in REPL], SMALL_PAD))
    for k, r in zip(("grad", "delta", "m", "v"), res):
        out[k].update(zip(REPL, _unpack(r, [w[n].shape for n in REPL])))

    loss = lax.psum(loss_local, ("x", "y", "c"))
    return (loss, dh0[N_META:t][None], *[out["grad"][n] for n in WEIGHTS], *[out["delta"][n] for n in WEIGHTS],
            *[out["m"][n] for n in WEIGHTS], *[out["v"][n] for n in WEIGHTS])
```

```python
import numpy as np
import jax
import jax.numpy as jnp
from jax import lax
from jax.experimental import pallas as pl
from jax.experimental.pallas import tpu as pltpu

f32 = jnp.float32
bf16 = jnp.bfloat16
HI = lax.Precision.HIGHEST

D_MODEL = 1024
N_META = 16
DEPTH = 4
BW = 512
D_FF = 4096
EPS = 1e-6
N_DEV = 8
RET_HEADS = 4
SSD_HEADS = 8
SSD_GROUPS = 2
HG_HEADS = 4
S5_G, S5_J, S5_P = 32, 16, 64

ADAM_LR, ADAM_B1, ADAM_B2, ADAM_EPS, ADAM_WD, ADAM_STEP = 0.001, 0.9, 0.999, 1e-08, 0.01, 10

CHUNK = 64
HG_SUB = 16
VMEM_LIMIT = 56 * 1024 * 1024

OFF = dict(gates=0, rq=4096, rk=4352, rv=4608, rg=5120, s5u=5632, sxbc=6144, sz=7168, hq=7680, hf=8192,
           hi=8704, hg=9216, dt=9728)
NP = 9856
IN_DIM = 9736


def _orig_col_slices():
    sl = [(5640, 9736)]
    for base in (0, 256):
        for half in (0, 32):
            for h in range(4):
                sl.append((base + 64 * h + half, base + 64 * h + half + 32))
    sl.append((512, 1024))
    sl.append((1024, 1536))
    sl.append((1536, 2048))
    sl.append((2560, 3584))
    sl.append((2048, 2560))
    sl.append((3592, 5640))
    return sl


def _bdot(a, b):
    return jnp.dot(a.astype(bf16), b.astype(bf16), preferred_element_type=f32)


def _bdot_nt(a, b):
    return lax.dot_general(a.astype(bf16), b.astype(bf16), (((1,), (1,)), ((), ())), preferred_element_type=f32)


def _bdot_tn(a, b):
    return lax.dot_general(a.astype(bf16), b.astype(bf16), (((0,), (0,)), ((), ())), preferred_element_type=f32)


def _hdot(a, b):
    return jnp.dot(a, b, precision=HI, preferred_element_type=f32)


def _sig(x):
    return jax.nn.sigmoid(x)


def _rms(x, w):
    return x * lax.rsqrt(jnp.mean(x * x, axis=-1, keepdims=True) + EPS) * w


def _softplus(x):
    return jnp.maximum(x, 0.0) + jnp.log(1.0 + jnp.exp(-jnp.abs(x)))


def _r(arr, rb, w, jb=0):
    return (arr, (rb, w), lambda n, jb=jb: (n, jb))


def _full(arr):
    nd = arr.ndim
    return (arr, arr.shape, lambda n, nd=nd: (0,) * nd)


def _seq_fwd(name, step, rows, consts, nds, states, outs, n_chunks, save_states=False, carry=None):
    nr, nc, nn, ns, no = len(rows), len(consts), len(nds), len(states), len(outs)
    whole = list(consts) + list(nds)

    def body(*refs):
        row_refs = refs[:nr]
        whole_hbm = refs[nr:nr + nc + nn]
        out_refs = refs[nr + nc + nn:nr + nc + nn + no]
        k = nr + nc + nn + no
        saved_refs = refs[k:k + (ns if save_states else 0)]
        k += ns if save_states else 0
        whole_vmem = refs[k:k + nc + nn]
        state_refs = refs[k + nc + nn:]
        n = pl.program_id(0)

        @pl.when(n == 0)
        def _():
            for src, dst in zip(whole_hbm, whole_vmem):
                pltpu.sync_copy(src, dst)
            for s in state_refs:
                s[...] = jnp.zeros_like(s)

        st = tuple(s[...] for s in state_refs)
        if save_states:
            for sv, v in zip(saved_refs, st):
                sv[0] = v
        o, new = step(tuple(r[...] for r in row_refs), tuple(c[...] for c in whole_vmem[:nc]),
                      tuple(c[...] for c in whole_vmem[nc:]), st, n)
        for ref, v in zip(out_refs, o):
            ref[...] = v.astype(ref.dtype)
        for s, v in zip(state_refs, new):
            s[...] = v

    in_specs = [pl.BlockSpec(bs, im) for _, bs, im in rows] + [pl.BlockSpec(memory_space=pl.ANY)] * (nc + nn)
    out_shape = [jax.ShapeDtypeStruct(s, d) for s, d, _, _ in outs]
    out_specs = [pl.BlockSpec(bs, im) for _, _, bs, im in outs]
    if save_states:
        for s in states:
            out_shape.append(jax.ShapeDtypeStruct((n_chunks,) + tuple(s), f32))
            out_specs.append(pl.BlockSpec((1,) + tuple(s), lambda n, z=len(s): (n,) + (0,) * z))
    scratch = [pltpu.VMEM(a.shape, a.dtype) for a in whole] + [pltpu.VMEM(tuple(s), f32) for s in states]
    return _pcall(body, name, (n_chunks,), in_specs, out_specs, out_shape, scratch, [a for a, _, _ in rows] + whole,
                  ("arbitrary",), carry)


def _seq_bwd(name, step, rows, row_diff, consts, nds, saved, couts, n_chunks, carry=None):
    nr, nc, nn, ns, no = len(rows), len(consts), len(nds), len(saved), len(couts)
    whole = list(consts) + list(nds)
    didx = [i for i in range(nr) if row_diff[i]]

    def rev(im):
        return lambda n: im(n_chunks - 1 - n)

    def body(*refs):
        row_refs = refs[:nr]
        whole_hbm = refs[nr:nr + nc + nn]
        k = nr + nc + nn
        saved_refs = refs[k:k + ns]
        k += ns
        cout_refs = refs[k:k + no]
        k += no
        drow_refs = refs[k:k + len(didx)]
        k += len(didx)
        dconst_hbm = refs[k:k + nc]
        k += nc
        whole_vmem = refs[k:k + nc + nn]
        k += nc + nn
        dconst_acc = refs[k:k + nc]
        k += nc
        dstate_refs = refs[k:]
        n = pl.program_id(0)

        @pl.when(n == 0)
        def _():
            for src, dst in zip(whole_hbm, whole_vmem):
                pltpu.sync_copy(src, dst)
            for a in dconst_acc:
                a[...] = jnp.zeros_like(a)
            for s in dstate_refs:
                s[...] = jnp.zeros_like(s)

        rvals = tuple(r[...] for r in row_refs)
        cvals = tuple(c[...] for c in whole_vmem[:nc])
        nvals = tuple(c[...] for c in whole_vmem[nc:])
        svals = tuple(s[0] for s in saved_refs)
        cidx = n_chunks - 1 - n

        def f(dr, cv, sv):
            full = list(rvals)
            for i, v in zip(didx, dr):
                full[i] = v
            return step(tuple(full), cv, nvals, sv, cidx)

        (o, _), vf = jax.vjp(f, tuple(rvals[i] for i in didx), cvals, svals)
        ct_o = tuple(c[...].astype(v.dtype) for c, v in zip(cout_refs, o))
        ct_s = tuple(s[...] for s in dstate_refs)
        d_rows, d_consts, d_states = vf((ct_o, ct_s))
        for ref, v in zip(drow_refs, d_rows):
            ref[...] = v.astype(ref.dtype)
        for a, v in zip(dconst_acc, d_consts):
            a[...] += v.astype(f32)
        for s, v in zip(dstate_refs, d_states):
            s[...] = v

        @pl.when(n == n_chunks - 1)
        def _():
            for a, dst in zip(dconst_acc, dconst_hbm):
                pltpu.sync_copy(a, dst)

    in_specs = ([pl.BlockSpec(bs, rev(im)) for _, bs, im in rows]
                + [pl.BlockSpec(memory_space=pl.ANY)] * (nc + nn)
                + [pl.BlockSpec((1,) + a.shape[1:], lambda n, z=a.ndim - 1: (n_chunks - 1 - n,) + (0,) * z) for a in saved]
                + [pl.BlockSpec(bs, rev(im)) for _, bs, im in couts])
    out_shape, out_specs = [], []
    for i in didx:
        a, bs, im = rows[i]
        nrows = a.shape[0]
        out_shape.append(jax.ShapeDtypeStruct((nrows,) + tuple(bs[1:]), f32))
        out_specs.append(pl.BlockSpec(bs, (lambda im: lambda n: (im(n_chunks - 1 - n)[0],) + (0,) * (len(bs) - 1))(im)))
    for c in consts:
        out_shape.append(jax.ShapeDtypeStruct(c.shape, f32))
        out_specs.append(pl.BlockSpec(memory_space=pl.ANY))
    scratch = ([pltpu.VMEM(a.shape, a.dtype) for a in whole] + [pltpu.VMEM(c.shape, f32) for c in consts]
               + [pltpu.VMEM(a.shape[1:], f32) for a in saved])
    res = _pcall(body, name, (n_chunks,), in_specs, out_specs, out_shape, scratch,
                 [a for a, _, _ in rows] + whole + list(saved) + [a for a, _, _ in couts], ("arbitrary",), carry)
    return res[:len(didx)], res[len(didx):]


def _mm(name, a, b, mode, tm, tn, tk, precision=None, carry=None):
    if mode == "nn":
        (m, kd), nn_ = a.shape, b.shape[1]
        a_spec = pl.BlockSpec((tm, tk), lambda i, j, k: (i, k))
        b_spec = pl.BlockSpec((tk, tn), lambda i, j, k: (k, j))
        dims = (((1,), (0,)), ((), ()))
    elif mode == "tn":
        (kd, m), nn_ = a.shape, b.shape[1]
        a_spec = pl.BlockSpec((tk, tm), lambda i, j, k: (k, i))
        b_spec = pl.BlockSpec((tk, tn), lambda i, j, k: (k, j))
        dims = (((0,), (0,)), ((), ()))
    else:
        (m, kd), nn_ = a.shape, b.shape[0]
        a_spec = pl.BlockSpec((tm, tk), lambda i, j, k: (i, k))
        b_spec = pl.BlockSpec((tn, tk), lambda i, j, k: (j, k))
        dims = (((1,), (1,)), ((), ()))
    assert m % tm == 0 and nn_ % tn == 0 and kd % tk == 0, (name, a.shape, b.shape, tm, tn, tk)
    nk = kd // tk

    def body(a_ref, b_ref, o_ref, acc):
        k = pl.program_id(2)

        @pl.when(k == 0)
        def _():
            acc[...] = jnp.zeros_like(acc)

        if precision is None:
            acc[...] += lax.dot_general(a_ref[...].astype(bf16), b_ref[...].astype(bf16), dims, preferred_element_type=f32)
        else:
            acc[...] += lax.dot_general(a_ref[...], b_ref[...], dims, precision=precision, preferred_element_type=f32)

        @pl.when(k == nk - 1)
        def _():
            o_ref[...] = acc[...]

    return _pcall(body, name, (m // tm, nn_ // tn, nk), [a_spec, b_spec], [pl.BlockSpec((tm, tn), lambda i, j, k: (i, j))],
                  [jax.ShapeDtypeStruct((m, nn_), f32)], [pltpu.VMEM((tm, tn), f32)], [a, b],
                  ("parallel", "parallel", "arbitrary"), carry)[0]


def _div_tile(n, want, mult):
    best = None
    for t in range(mult, min(n, want) + 1, mult):
        if n % t == 0:
            best = t
    return best if best is not None else n


class _Carry:
    def __init__(self, kind, srcs):
        self.kind, self.srcs, self.n, self.result = kind, list(srcs), len(srcs), None

    def out_shapes(self):
        if self.kind == "scatter":
            return [jax.ShapeDtypeStruct(s.shape, s.dtype) for s in self.srcs]
        return [jax.ShapeDtypeStruct((N_DEV,) + tuple(s.shape), s.dtype) for s in self.srcs]

    def sems(self):
        k = N_DEV if self.kind == "scatter" else N_DEV - 1
        return [pltpu.SemaphoreType.DMA((self.n, k)), pltpu.SemaphoreType.DMA((self.n, k)), pltpu.SemaphoreType.DMA((self.n,))]

    def _sc(self, src, dst, sems, i, j, slot):
        return pltpu.make_async_remote_copy(
            src_ref=src[i].at[j], dst_ref=dst[i].at[slot], send_sem=sems[0].at[i, j], recv_sem=sems[1].at[i, slot],
            device_id=(j // 4, (j // 2) % 2, j % 2), device_id_type=pl.DeviceIdType.MESH)

    def _where(self):
        x, y, c = lax.axis_index("x"), lax.axis_index("y"), lax.axis_index("c")
        return (x, y, c), (x, y, 1 - c), [(1 - x, y), (x, 1 - y), (1 - x, 1 - y)], c

    def _gc(self, src, dst, sems, i, k, block, to, own=False):
        slot = dst[i].at[4 * block[0] + 2 * block[1] + block[2]]
        return pltpu.make_async_remote_copy(
            src_ref=src[i] if own else slot, dst_ref=slot, send_sem=sems[0].at[i, k], recv_sem=sems[1].at[i, k],
            device_id=to, device_id_type=pl.DeviceIdType.MESH)

    def _local(self, src, dst, sems, i):
        if self.kind == "scatter":
            me = 4 * lax.axis_index("x") + 2 * lax.axis_index("y") + lax.axis_index("c")
            return pltpu.make_async_copy(src[i].at[me], dst[i].at[me], sems[2].at[i])
        x, y, c = lax.axis_index("x"), lax.axis_index("y"), lax.axis_index("c")
        return pltpu.make_async_copy(src[i], dst[i].at[4 * x + 2 * y + c], sems[2].at[i])

    def _first(self, src, dst, sems):
        me, sibling, chips, c = self._where()
        out = []
        for i in range(self.n):
            out.append(self._gc(src, dst, sems, i, 0, me, sibling, own=True))
            out += [self._gc(src, dst, sems, i, 1 + j, me, (*chip, c), own=True) for j, chip in enumerate(chips)]
        return out

    def start(self, src, dst, sems):
        for i in range(self.n):
            self._local(src, dst, sems, i).start()
        if self.kind == "scatter":
            me = 4 * lax.axis_index("x") + 2 * lax.axis_index("y") + lax.axis_index("c")
            for j in range(N_DEV):
                @pl.when(j != me)
                def _(j=j):
                    for i in range(self.n):
                        self._sc(src, dst, sems, i, j, me).start()
        else:
            for cp in self._first(src, dst, sems):
                cp.start()

    def finish(self, src, dst, sems):
        if self.kind == "scatter":
            me = 4 * lax.axis_index("x") + 2 * lax.axis_index("y") + lax.axis_index("c")
            for j in range(N_DEV):
                @pl.when(j != me)
                def _(j=j):
                    for i in range(self.n):
                        self._sc(src, dst, sems, i, j, j).wait_recv()
                        self._sc(src, dst, sems, i, j, me).wait_send()
        else:
            me, sibling, chips, c = self._where()
            passed = []
            for j, chip in enumerate(chips):
                for i in range(self.n):
                    self._gc(src, dst, sems, i, 1 + j, (*chip, c), me).wait_recv()
                    fwd = self._gc(src, dst, sems, i, 4 + j, (*chip, c), sibling)
                    fwd.start()
                    passed.append(fwd)
            for i in range(self.n):
                self._gc(src, dst, sems, i, 0, sibling, me).wait_recv()
                for j, chip in enumerate(chips):
                    self._gc(src, dst, sems, i, 4 + j, (*chip, 1 - c), me).wait_recv()
            for cp in self._first(src, dst, sems) + passed:
                cp.wait_send()
        for i in range(self.n):
            self._local(src, dst, sems, i).wait()


def _pcall(body, name, grid, in_specs, out_specs, out_shape, scratch, operands, dims, carry=None):
    n_in, n_out = len(in_specs), len(out_shape)
    if carry is not None:
        n, inner = carry.n, body

        def body(*refs):
            ins, csrc = refs[:n_in], refs[n_in:n_in + n]
            outs, cdst = refs[n_in + n:n_in + n + n_out], refs[n_in + n + n_out:n_in + 2 * n + n_out]
            rest = refs[n_in + 2 * n + n_out:]
            first = last = True
            for k, size in enumerate(grid):
                first = jnp.logical_and(first, pl.program_id(k) == 0)
                last = jnp.logical_and(last, pl.program_id(k) == size - 1)

            @pl.when(first)
            def _():
                carry.start(csrc, cdst, rest[-3:])

            inner(*ins, *outs, *rest[:-3])

            @pl.when(last)
            def _():
                carry.finish(csrc, cdst, rest[-3:])

        hbm = pl.BlockSpec(memory_space=pltpu.HBM)
        in_specs, out_specs = list(in_specs) + [hbm] * n, list(out_specs) + [hbm] * n
        out_shape, scratch = list(out_shape) + carry.out_shapes(), list(scratch) + carry.sems()
        operands = list(operands) + carry.srcs
    kw = dict(grid=grid) if grid else {}
    res = pl.pallas_call(
        body, name=name, in_specs=in_specs, out_specs=out_specs, out_shape=out_shape, scratch_shapes=scratch,
        compiler_params=pltpu.CompilerParams(dimension_semantics=dims, vmem_limit_bytes=VMEM_LIMIT) if grid else None, **kw,
    )(*operands)
    res = list(res)
    if carry is not None:
        carry.result = res[n_out:]
    return res[:n_out]


def _exchange(name, srcs):
    carry = _Carry("scatter", srcs)
    _pcall(lambda *refs: None, name, (), [], [], [], [], [], None, carry)
    return carry.result


def _all_gather(name, srcs):
    carry = _Carry("gather", srcs)
    _pcall(lambda *refs: None, name, (), [], [], [], [], [], None, carry)
    return carry.result


def _ret_tables(c):
    gam = 1.0 - 2.0 ** (-5.0 - np.arange(RET_HEADS))
    lg = np.log(gam)
    t = np.arange(c)
    dmat = np.where(t[:, None] >= t[None, :], np.exp((t[:, None] - t[None, :])[None] * lg[:, None, None]), 0.0)
    head_v = np.arange(512) // 128
    ysc = np.exp((t[:, None] + 1) * lg[head_v][None, :])
    wtab = np.exp((c - 1 - t)[:, None] * lg[head_v][None, :])
    gtab = np.exp(c * lg[head_v])[None, :]
    head_k = (np.arange(256) % 128) // 32
    mask = (head_k[:, None] == head_v[None, :]).astype(np.float32)
    hm = (head_k[None, None, :] == np.arange(4)[:, None, None]).astype(np.float32)
    return [jnp.asarray(x, f32) for x in (dmat, ysc, wtab, gtab, mask, hm)]


def _rope_tables(tp):
    inv = 10000.0 ** (-np.arange(32, dtype=np.float32) / 32)
    ang = np.arange(tp, dtype=np.float32)[:, None] * inv[None, :]
    cos = np.tile(np.cos(ang), (1, 4)).astype(np.float32)
    sin = np.tile(np.sin(ang), (1, 4)).astype(np.float32)
    return jnp.asarray(cos), jnp.asarray(sin)


def _tri(c):
    t = np.arange(c)
    return jnp.asarray((t[:, None] >= t[None, :]).astype(np.float32))


def _ssd_tables(c):
    sh = np.zeros((3 * c, c + 8), np.float32)
    for d in (3, 2, 1):
        for t in range(c):
            sh[(3 - d) * c + t, 8 + t - d] = 1.0
    e = np.zeros((128, 512), np.float32)
    for h in range(SSD_HEADS):
        e[h, 64 * h:64 * h + 64] = 1.0
    mg = ((np.arange(256) // 128)[:, None] == (np.arange(512) // 256)[None, :]).astype(np.float32)
    cm4 = ((np.arange(256) // 64)[None, None, :] == np.arange(4)[:, None, None]).astype(np.float32)
    return [jnp.asarray(x) for x in (sh, e, mg, cm4)]


def _ret_step(rows, consts, nds, states, _):
    q, k, v, g, cos, sin = rows
    (gnw,) = consts
    dmat, ysc, wtab, gtab, mask, hm = nds
    (s,) = states
    q1, q2, k1, k2 = q[:, :128], q[:, 128:], k[:, :128], k[:, 128:]
    qr = jnp.concatenate([q1 * cos - q2 * sin, q1 * sin + q2 * cos], axis=1)
    kr = jnp.concatenate([k1 * cos - k2 * sin, k1 * sin + k2 * cos], axis=1) * 0.125
    y = _bdot(qr, s) * ysc
    parts = []
    for h in range(RET_HEADS):
        a = _bdot_nt(qr * hm[h], kr) * dmat[h]
        parts.append(_bdot(a, v[:, 128 * h:128 * h + 128]))
    y = y + jnp.concatenate(parts, axis=1)
    s_new = s * gtab + _bdot_tn(kr, v * wtab) * mask
    outs = []
    for h in range(RET_HEADS):
        yh = y[:, 128 * h:128 * h + 128]
        d = yh - jnp.mean(yh, axis=-1, keepdims=True)
        outs.append(d * lax.rsqrt(jnp.mean(d * d, axis=-1, keepdims=True) + EPS))
    yn = jnp.concatenate(outs, axis=1) * gnw
    return (g * _sig(g) * yn,), (s_new,)


def _ssd_step(rows, consts, nds, states, _):
    z, xbc, dt128 = rows
    cw, cb, dtb, alog, dsk, nw = consts
    tri, sh, e, mg, cm4 = nds
    s, tail = states
    c = xbc.shape[0]
    shifted = _hdot(sh, jnp.concatenate([tail, xbc], axis=0))
    xc = (cw[0:1] * shifted[0:c] + cw[1:2] * shifted[c:2 * c] + cw[2:3] * shifted[2 * c:3 * c] + cw[3:4] * xbc + cb)
    xc = xc * _sig(xc)
    xs, bm, cm = xc[:, :512], xc[:, 512:768], xc[:, 768:]
    dt = _softplus(dt128 + dtb)
    la = dt * (-jnp.exp(alog))
    cum = _hdot(tri, la)
    cum_t = cum.T
    dtx, cumx = _hdot(dt, e), _hdot(cum, e)
    lastx = cumx[c - 1:c]
    dskx = _hdot(jnp.broadcast_to(dsk, (8, 128)), e)[0:1]
    v = xs * dtx
    y = _bdot(cm, s) * jnp.exp(cumx)
    lane = lax.broadcasted_iota(jnp.int32, cum.shape, 1)
    sub = lax.broadcasted_iota(jnp.int32, cum_t.shape, 0)
    parts = []
    for grp in range(SSD_GROUPS):
        sg = _bdot_nt(cm[:, 128 * grp:128 * grp + 128], bm[:, 128 * grp:128 * grp + 128])
        vg = v[:, 256 * grp:256 * grp + 256]
        acc = jnp.zeros((c, 256), f32)
        for hh in range(4):
            h = 4 * grp + hh
            col = jnp.sum(jnp.where(lane == h, cum, 0.0), axis=1, keepdims=True)
            row = jnp.sum(jnp.where(sub == h, cum_t, 0.0), axis=0, keepdims=True)
            dec = jnp.exp(jnp.where(tri > 0.5, col - row, -1e30))
            acc = acc + _bdot(sg * dec, vg * cm4[hh])
        parts.append(acc)
    y = y + jnp.concatenate(parts, axis=1) + dskx * xs
    s_new = s * jnp.exp(lastx) + _bdot_tn(bm, v * jnp.exp(lastx - cumx)) * mg
    y = y * (z * _sig(z))
    outs = []
    for grp in range(SSD_GROUPS):
        yg = y[:, 256 * grp:256 * grp + 256]
        outs.append(yg * lax.rsqrt(jnp.mean(yg * yg, axis=-1, keepdims=True) + EPS))
    return (jnp.concatenate(outs, axis=1) * nw,), (s_new, xbc[c - 8:c])


def _hg_step(rows, consts, nds, states, _):
    hq, hf, hi, hgate = rows
    lb, nw = consts
    (tri,) = nds
    (st,) = states
    c = hq.shape[0]
    q = hq * _sig(hq)
    f = lb + (1.0 - lb) * _sig(hf)
    lf = jnp.log(f)
    k = 1.0 - f
    v = hi
    cum = _hdot(tri, lf)
    last = jnp.sum(lf, axis=0, keepdims=True)
    qd = q * jnp.exp(cum)
    kw = k * jnp.exp(last - cum)
    heads = [slice(128 * h, 128 * h + 128) for h in range(HG_HEADS)]
    y_rows = []
    rowid = lax.broadcasted_iota(jnp.int32, (HG_SUB, 512), 0)
    for i in range(c // HG_SUB):
        r0 = HG_SUB * i
        qi, ci, ki, vi = q[r0:r0 + HG_SUB], cum[r0:r0 + HG_SUB], k[r0:r0 + HG_SUB], v[r0:r0 + HG_SUB]
        yi = [jnp.zeros((HG_SUB, 128), f32) for _ in heads]
        for s_ in range(HG_SUB):
            e = qi * ki[s_:s_ + 1] * jnp.exp(jnp.where(rowid >= s_, ci - ci[s_:s_ + 1], -1e30))
            for h, sl in enumerate(heads):
                yi[h] = yi[h] + jnp.sum(e[:, sl], axis=1, keepdims=True) * vi[s_:s_ + 1, sl]
        if i > 0:
            b = cum[r0 - 1:r0]
            qs = qi * jnp.exp(ci - b)
            ks = k[:r0] * jnp.exp(b - cum[:r0])
            for h, sl in enumerate(heads):
                yi[h] = yi[h] + _bdot(_bdot_nt(qs[:, sl], ks[:, sl]), v[:r0, sl])
        y_rows.append(jnp.concatenate(yi, axis=1))
    y_in = jnp.concatenate(y_rows, axis=0)
    y = y_in + jnp.concatenate([_bdot_nt(qd[:, sl], st[:, sl]) for sl in heads], axis=1)
    st_new = jnp.concatenate([st[:, sl] * jnp.exp(last[:, sl]) + _bdot_tn(v[:, sl], kw[:, sl]) for sl in heads], axis=1)
    outs = []
    for sl in heads:
        yh = y[:, sl]
        outs.append(yh * lax.rsqrt(jnp.mean(yh * yh, axis=-1, keepdims=True) + EPS))
    o = jnp.concatenate(outs, axis=1) * nw
    return (o * (hgate * _sig(hgate)),), (st_new,)


def _s5post_step(rows, consts, nds, states, _):
    ycore, u = rows
    dsk, gw, gb = consts
    y = jax.nn.gelu(ycore + dsk * u)
    zz = _bdot(y, gw) + gb
    return (zz[:, :512] * _sig(zz[:, 512:]),), ()


def _s5prep_step(rows, consts, nds, states, _):
    lr, li, lstep, btr, bti = rows
    (rep,) = nds
    step = jnp.exp(lstep)
    mag = jnp.exp(lr * step)
    ab_re, ab_im = mag * jnp.cos(li * step), mag * jnp.sin(li * step)
    inv = 1.0 / (lr * lr + li * li)
    co_re = ((ab_re - 1.0) * lr + ab_im * li) * inv
    co_im = (ab_im * lr - (ab_re - 1.0) * li) * inv
    cre, cim = _hdot(rep, co_re), _hdot(rep, co_im)
    return (ab_re, ab_im, cre * btr - cim * bti, cre * bti + cim * btr), ()


def _lb_step(rows, consts, nds, states, _):
    (x,) = rows
    (lmat,) = nds
    valid = lax.broadcasted_iota(jnp.int32, x.shape, 0) < DEPTH
    xm = jnp.where(valid, x, -1e30)
    ex = jnp.where(valid, jnp.exp(xm - jnp.max(xm, axis=0, keepdims=True)), 0.0)
    sm = ex / jnp.sum(ex, axis=0, keepdims=True)
    return (_hdot(lmat, sm),), ()


def _rmsn_step(rows, consts, nds, states, _):
    (h,) = rows
    (w,) = consts
    return (_rms(h, w),), ()


def _merge_step(rows, consts, nds, states, _):
    yr, ys5, yssd, yhg, gates, h = rows
    wb, wout, npost = consts
    mixed = jnp.zeros(h.shape, f32)
    for n, yb in enumerate((yr, ys5, yssd, yhg)):
        mixed = mixed + _sig(gates[:, 1024 * n:1024 * n + 1024]) * _bdot(yb, wb[n])
    return (h + _rms(_bdot(mixed, wout), npost),), ()


FF_BLK = 512


def _row_blocks(tp):
    rb = _div_tile(tp, 528, 16)
    return rb, tp // rb


def _mlp_core_fwd(tag, u, w_up, w_down, carry=None):
    tp = u.shape[0]
    rb, nrb = _row_blocks(tp)
    nff = D_FF // FF_BLK

    def body(u_ref, wup_ref, wdown_ref, m_ref):
        j = pl.program_id(0)

        def rows(i, carry):
            r = pl.multiple_of(i * rb, 16)
            a = jnp.dot(u_ref[pl.ds(r, rb), :], wup_ref[...], preferred_element_type=f32)
            part = _bdot(jnp.square(jnp.maximum(a, 0.0)), wdown_ref[...])

            @pl.when(j == 0)
            def _():
                m_ref[pl.ds(r, rb), :] = part

            @pl.when(j > 0)
            def _():
                m_ref[pl.ds(r, rb), :] += part
            return carry
        lax.fori_loop(0, nrb, rows, 0)

    return _pcall(
        body, "mlp_core_fwd" + tag, (nff,),
        [pl.BlockSpec((tp, D_MODEL), lambda j: (0, 0)), pl.BlockSpec((D_MODEL, FF_BLK), lambda j: (0, j)),
         pl.BlockSpec((FF_BLK, D_MODEL), lambda j: (j, 0))],
        [pl.BlockSpec((tp, D_MODEL), lambda j: (0, 0))], [jax.ShapeDtypeStruct((tp, D_MODEL), f32)], [], [u, w_up, w_down],
        ("arbitrary",), carry)[0]


def _mlp_core_bwd(tag, u, dm, w_up, w_down, carry=None):
    tp = u.shape[0]
    rb, nrb = _row_blocks(tp)
    nff = D_FF // FF_BLK

    def body(u_hbm, dm_hbm, wup_ref, wdown_ref, du_hbm, dwup_ref, dwdown_ref, u_s, dm_s, du_s):
        j = pl.program_id(0)

        @pl.when(j == 0)
        def _():
            pltpu.sync_copy(u_hbm, u_s)
            pltpu.sync_copy(dm_hbm, dm_s)

        def rows(i, carry):
            r = pl.multiple_of(i * rb, 16)
            ub = u_s[pl.ds(r, rb), :]
            dmb = dm_s[pl.ds(r, rb), :].astype(bf16)
            a = jnp.dot(ub, wup_ref[...], preferred_element_type=f32)
            ra = jnp.maximum(a, 0.0)
            da = (_bdot_nt(dmb, wdown_ref[...]) * (2.0 * ra)).astype(bf16)
            dwd = _bdot_tn(ra * ra, dmb)
            dwu = _bdot_tn(ub, da)
            dub = _bdot_nt(da, wup_ref[...])

            @pl.when(i == 0)
            def _():
                dwdown_ref[...] = dwd
                dwup_ref[...] = dwu

            @pl.when(i > 0)
            def _():
                dwdown_ref[...] += dwd
                dwup_ref[...] += dwu

            @pl.when(j == 0)
            def _():
                du_s[pl.ds(r, rb), :] = dub

            @pl.when(j > 0)
            def _():
                du_s[pl.ds(r, rb), :] += dub
            return carry
        lax.fori_loop(0, nrb, rows, 0)

        @pl.when(j == nff - 1)
        def _():
            pltpu.sync_copy(du_s, du_hbm)

    anyspec = pl.BlockSpec(memory_space=pl.ANY)
    return _pcall(
        body, "mlp_core_bwd" + tag, (nff,),
        [anyspec, anyspec, pl.BlockSpec((D_MODEL, FF_BLK), lambda j: (0, j)), pl.BlockSpec((FF_BLK, D_MODEL), lambda j: (j, 0))],
        [anyspec, pl.BlockSpec((D_MODEL, FF_BLK), lambda j: (0, j)), pl.BlockSpec((FF_BLK, D_MODEL), lambda j: (j, 0))],
        [jax.ShapeDtypeStruct((tp, D_MODEL), f32), jax.ShapeDtypeStruct((D_MODEL, D_FF), f32), jax.ShapeDtypeStruct((D_FF, D_MODEL), f32)],
        [pltpu.VMEM((tp, D_MODEL), bf16), pltpu.VMEM((tp, D_MODEL), f32), pltpu.VMEM((tp, D_MODEL), f32)],
        [u, dm, w_up, w_down], ("arbitrary",), carry)


def _resid_rms_step(rows, consts, nds, states, _):
    m, h = rows
    (w,) = consts
    return (h + _rms(m, w),), ()


SCAN_RB = 16


def _scan_inplace(xr, xi, ar0, ai0, tp, reverse, pr, pi):
    rb, cs = SCAN_RB, 64
    nb = cs // rb
    w = xr.shape[1]
    nch = tp // cs
    rowid = lax.broadcasted_iota(jnp.int32, (rb, w), 0)

    def mac(cr, ci, sr, si, ar, ai):
        return cr + ar * sr - ai * si, ci + ar * si + ai * sr

    ar, ai = ar0, ai0
    d = 1
    while d < cs:
        arb, aib = jnp.broadcast_to(ar, (rb, w)), jnp.broadcast_to(ai, (rb, w))
        edge = 0 if not reverse else nb - 1

        def level(c, carry, d=d, arb=arb, aib=aib):
            base = pl.multiple_of(c * cs, cs)
            order = range(nb - 1, -1, -1) if not reverse else range(nb)
            for b in order:
                r0 = base + rb * b
                lo = b - d // rb if not reverse else b + d // rb
                if d >= rb:
                    if lo < 0 or lo >= nb:
                        continue
                    src = base + rb * lo
                    sr, si = xr[pl.ds(src, rb), :], xi[pl.ds(src, rb), :]
                elif b == edge:
                    if not reverse:
                        sr = jnp.where(rowid >= d, pltpu.roll(xr[pl.ds(r0, rb), :], d, 0), 0.0)
                        si = jnp.where(rowid >= d, pltpu.roll(xi[pl.ds(r0, rb), :], d, 0), 0.0)
                    else:
                        sr = jnp.where(rowid < rb - d, pltpu.roll(xr[pl.ds(r0, rb), :], rb - d, 0), 0.0)
                        si = jnp.where(rowid < rb - d, pltpu.roll(xi[pl.ds(r0, rb), :], rb - d, 0), 0.0)
                elif not reverse:
                    sr = pltpu.roll(xr[pl.ds(r0 - 8, rb + 8), :], d, 0)[8:, :]
                    si = pltpu.roll(xi[pl.ds(r0 - 8, rb + 8), :], d, 0)[8:, :]
                else:
                    sr = pltpu.roll(xr[pl.ds(r0, rb + 8), :], rb + 8 - d, 0)[:rb, :]
                    si = pltpu.roll(xi[pl.ds(r0, rb + 8), :], rb + 8 - d, 0)[:rb, :]
                nr, ni = mac(xr[pl.ds(r0, rb), :], xi[pl.ds(r0, rb), :], sr, si, arb, aib)
                xr[pl.ds(r0, rb), :] = nr
                xi[pl.ds(r0, rb), :] = ni
            return carry
        lax.fori_loop(0, nch, level, 0)
        ar, ai = ar * ar - ai * ai, 2.0 * ar * ai
        d *= 2

    krow = lax.broadcasted_iota(jnp.int32, (cs, w), 0)
    expo = krow + 1 if not reverse else cs - krow
    tr, ti = jnp.ones((cs, w), f32), jnp.zeros((cs, w), f32)
    qr, qi = ar0, ai0
    for j in range(7):
        on = ((expo >> j) & 1) == 1
        fr, fi = jnp.where(on, qr, 1.0), jnp.where(on, qi, 0.0)
        tr, ti = tr * fr - ti * fi, tr * fi + ti * fr
        qr, qi = qr * qr - qi * qi, 2.0 * qr * qi
    pr[...] = tr
    pi[...] = ti

    def across(i, carry):
        c = i if not reverse else nch - 1 - i
        base = pl.multiple_of(c * cs, cs)
        cr, ci = carry
        out = carry
        for b in range(nb):
            r0 = base + rb * b
            nr, ni = mac(xr[pl.ds(r0, rb), :], xi[pl.ds(r0, rb), :], pr[rb * b:rb * b + rb, :], pi[rb * b:rb * b + rb, :], cr, ci)
            xr[pl.ds(r0, rb), :] = nr
            xi[pl.ds(r0, rb), :] = ni
            if not reverse and b == nb - 1:
                out = (nr[rb - 1:rb, :], ni[rb - 1:rb, :])
            if reverse and b == 0:
                out = (nr[0:1, :], ni[0:1, :])
        return out
    lax.fori_loop(0, nch, across, (jnp.zeros((1, w), f32), jnp.zeros((1, w), f32)))


def _s5_blocks(tp):
    rbm = tp // 8 if (tp // 8) % 8 == 0 and tp % 8 == 0 else tp
    return rbm, tp // rbm


def _s5_in_specs(proj, tp):
    ucol = OFF["s5u"] // 128
    return [
        pl.BlockSpec((tp, 128), lambda cb: (0, ucol + cb)),
        pl.BlockSpec((128, 512), lambda cb: (cb, 0)),
        pl.BlockSpec((128, 512), lambda cb: (cb, 0)),
        pl.BlockSpec((1, 512), lambda cb: (0, cb)),
        pl.BlockSpec((1, 512), lambda cb: (0, cb)),
        pl.BlockSpec((512, 128), lambda cb: (cb, 0)),
        pl.BlockSpec((512, 128), lambda cb: (cb, 0)),
    ]


def _s5_core_fwd(tag, proj, bbr, bbi, ar, ai, ccr, cci, carry=None):
    tp = proj.shape[0]
    rbm, nb = _s5_blocks(tp)

    def body(u_ref, bbr_ref, bbi_ref, ar_ref, ai_ref, ccr_ref, cci_ref, y_ref, xr, xi, pr, pi):
        def fill(i, carry):
            r = pl.multiple_of(i * rbm, 8)
            ub = u_ref[pl.ds(r, rbm), :]
            xr[pl.ds(r, rbm), :] = _bdot(ub, bbr_ref[...])
            xi[pl.ds(r, rbm), :] = _bdot(ub, bbi_ref[...])
            return carry
        lax.fori_loop(0, nb, fill, 0)
        _scan_inplace(xr, xi, ar_ref[...], ai_ref[...], tp, False, pr, pi)

        def out(i, carry):
            r = pl.multiple_of(i * rbm, 8)
            y_ref[pl.ds(r, rbm), :] = (_bdot(xr[pl.ds(r, rbm), :], ccr_ref[...]) - _bdot(xi[pl.ds(r, rbm), :], cci_ref[...]))
            return carry
        lax.fori_loop(0, nb, out, 0)

    return _pcall(body, "s5_core_fwd" + tag, (4,), _s5_in_specs(proj, tp), [pl.BlockSpec((tp, 128), lambda cb: (0, cb))],
                  [jax.ShapeDtypeStruct((tp, 512), f32)], [pltpu.VMEM((tp, 512), f32)] * 2 + [pltpu.VMEM((64, 512), f32)] * 2,
                  [proj, bbr, bbi, ar, ai, ccr, cci], ("arbitrary",), carry)[0]


def _s5_core_bwd(tag, proj, bbr, bbi, ar, ai, ccr, cci, dy, du_post, carry=None):
    tp = proj.shape[0]
    rbm, nb = _s5_blocks(tp)
    rb = SCAN_RB

    def body(u_ref, bbr_ref, bbi_ref, ar_ref, ai_ref, ccr_ref, cci_ref, dy_ref, dup_ref,
             du_ref, dbbr_ref, dbbi_ref, dar_ref, dai_ref, dccr_ref, dcci_ref, xr, xi, gr, gi, pr, pi):
        dccr_ref[...] = jnp.zeros_like(dccr_ref)
        dcci_ref[...] = jnp.zeros_like(dcci_ref)
        dbbr_ref[...] = jnp.zeros_like(dbbr_ref)
        dbbi_ref[...] = jnp.zeros_like(dbbi_ref)

        def fill(i, carry):
            r = pl.multiple_of(i * rbm, 8)
            ub = u_ref[pl.ds(r, rbm), :]
            xr[pl.ds(r, rbm), :] = _bdot(ub, bbr_ref[...])
            xi[pl.ds(r, rbm), :] = _bdot(ub, bbi_ref[...])
            return carry
        lax.fori_loop(0, nb, fill, 0)
        _scan_inplace(xr, xi, ar_ref[...], ai_ref[...], tp, False, pr, pi)

        def seed(i, carry):
            r = pl.multiple_of(i * rbm, 8)
            dyb = dy_ref[pl.ds(r, rbm), :]
            dccr_ref[...] += _bdot_tn(xr[pl.ds(r, rbm), :], dyb)
            dcci_ref[...] -= _bdot_tn(xi[pl.ds(r, rbm), :], dyb)
            gr[pl.ds(r, rbm), :] = _bdot_nt(dyb, ccr_ref[...])
            gi[pl.ds(r, rbm), :] = -_bdot_nt(dyb, cci_ref[...])
            return carry
        lax.fori_loop(0, nb, seed, 0)
        _scan_inplace(gr, gi, ar_ref[...], -ai_ref[...], tp, True, pr, pi)

        w = xr.shape[1]
        rowid = lax.broadcasted_iota(jnp.int32, (rb, w), 0)

        def prods(pr, pi, r0):
            g_r, g_i = gr[pl.ds(r0, rb), :], gi[pl.ds(r0, rb), :]
            return pr * g_r + pi * g_i, pr * g_i - pi * g_r

        def dacc(i, carry):
            r0 = pl.multiple_of((i + 1) * rb, 8)
            lo = pl.multiple_of(r0 - 8, 8)
            pr = pltpu.roll(xr[pl.ds(lo, rb + 8), :], 1, 0)[8:, :]
            pi = pltpu.roll(xi[pl.ds(lo, rb + 8), :], 1, 0)[8:, :]
            a, b = prods(pr, pi, r0)
            return carry[0] + a, carry[1] + b
        pr0 = jnp.where(rowid >= 1, pltpu.roll(xr[pl.ds(0, rb), :], 1, 0), 0.0)
        pi0 = jnp.where(rowid >= 1, pltpu.roll(xi[pl.ds(0, rb), :], 1, 0), 0.0)
        acc_r, acc_i = lax.fori_loop(0, tp // rb - 1, dacc, prods(pr0, pi0, 0))
        dar_ref[...] = jnp.sum(acc_r, axis=0, keepdims=True)
        dai_ref[...] = jnp.sum(acc_i, axis=0, keepdims=True)

        def tail(i, carry):
            r = pl.multiple_of(i * rbm, 8)
            g_r, g_i = gr[pl.ds(r, rbm), :], gi[pl.ds(r, rbm), :]
            ub = u_ref[pl.ds(r, rbm), :]
            du_ref[pl.ds(r, rbm), :] = _bdot_nt(g_r, bbr_ref[...]) + _bdot_nt(g_i, bbi_ref[...]) + dup_ref[pl.ds(r, rbm), :]
            dbbr_ref[...] += _bdot_tn(ub, g_r)
            dbbi_ref[...] += _bdot_tn(ub, g_i)
            return carry
        lax.fori_loop(0, nb, tail, 0)

    col = lambda cb: (0, cb)
    blk = lambda cb: (cb, 0)
    return _pcall(
        body, "s5_core_bwd" + tag, (4,),
        _s5_in_specs(proj, tp) + [pl.BlockSpec((tp, 128), col), pl.BlockSpec((tp, 128), col)],
        [pl.BlockSpec((tp, 128), col), pl.BlockSpec((128, 512), blk), pl.BlockSpec((128, 512), blk),
         pl.BlockSpec((1, 512), col), pl.BlockSpec((1, 512), col), pl.BlockSpec((512, 128), blk), pl.BlockSpec((512, 128), blk)],
        [jax.ShapeDtypeStruct((tp, 512), f32), jax.ShapeDtypeStruct((512, 512), f32), jax.ShapeDtypeStruct((512, 512), f32),
         jax.ShapeDtypeStruct((1, 2048), f32), jax.ShapeDtypeStruct((1, 2048), f32), jax.ShapeDtypeStruct((2048, 128), f32),
         jax.ShapeDtypeStruct((2048, 128), f32)],
        [pltpu.VMEM((tp, 512), f32)] * 4 + [pltpu.VMEM((64, 512), f32)] * 2, [proj, bbr, bbi, ar, ai, ccr, cci, dy, du_post],
        ("arbitrary",), carry)


_EYE8 = np.eye(8, dtype=np.float32)


def _bb_dense(bb):
    return jnp.einsum("cgjp,gh->cgjhp", bb.reshape(4, 8, 16, 64), _EYE8).reshape(512, 512)


def _bb_diag(d):
    return jnp.einsum("cgjhp,gh->cgjp", d.reshape(4, 8, 16, 8, 64), _EYE8).reshape(512, 64)


def _cc_dense(cmat):
    return jnp.einsum("cgjp,gh->cgphj", cmat.reshape(4, 8, 16, 64), _EYE8).reshape(2048, 128)


def _cc_diag(d):
    return jnp.einsum("cgphj,gh->cgjp", d.reshape(4, 8, 64, 8, 16), _EYE8).reshape(32, 16, 64)


def _tables(tp):
    cos, sin = _rope_tables(tp)
    rep = np.zeros((512, 32), np.float32)
    rep[np.arange(512), np.arange(512) // 16] = 1.0
    lmat = np.zeros((8, 8), np.float32)
    for l in range(DEPTH):
        lmat[l, 1:l + 1] = 1.0
    return dict(cos=cos, sin=sin, ret=_ret_tables(CHUNK), tri=_tri(CHUNK), ssd=_ssd_tables(CHUNK),
                rep=jnp.asarray(rep), lmat=jnp.asarray(lmat))


def _tiles(tp):
    return dict(tr=_div_tile(tp, 352, 16), tmg=_div_tile(tp, 192, 16))


def _mixer_rows(proj, c):
    ret = [_r(proj, c, 256, OFF["rq"] // 256), _r(proj, c, 256, OFF["rk"] // 256), _r(proj, c, 512, OFF["rv"] // 512),
           _r(proj, c, 512, OFF["rg"] // 512)]
    ssd = [_r(proj, c, 512, OFF["sz"] // 512), _r(proj, c, 1024, OFF["sxbc"] // 1024), _r(proj, c, 128, OFF["dt"] // 128)]
    hg = [_r(proj, c, 512, OFF[k] // 512) for k in ("hq", "hf", "hi", "hg")]
    return ret, ssd, hg


def _out2(rows, w, rb, dtype=f32):
    return (rows, w), dtype, (rb, w), lambda n: (n, 0)


def _s5_prep_rows(p):
    return [_full(p["lam_re"]), _full(p["lam_im"]), _full(p["lstep"]), _full(p["bt_re"]), _full(p["bt_im"])]


def _s5_consts(p, tabs, tag):
    whole = lambda s: (s, f32, s, lambda n: (0, 0))
    ab_re, ab_im, bb_re, bb_im = _seq_fwd("s5_prep" + tag, _s5prep_step, _s5_prep_rows(p), [], [tabs["rep"]], [],
                                          [whole((32, 64)), whole((32, 64)), whole((512, 64)), whole((512, 64))], 1)
    return (_bb_dense(bb_re).astype(bf16), _bb_dense(bb_im).astype(bf16), ab_re.reshape(1, 2048), ab_im.reshape(1, 2048),
            _cc_dense(p["c_re"]).astype(bf16), _cc_dense(p["c_im"]).astype(bf16))


def _rmsn_step_b(rows, consts, nds, states, n):
    (o,), _ = _rmsn_step(rows, consts, nds, states, n)
    return (o, rows[0]), ()


def _layer_fwd(h, p, tabs, tag, carry):
    tp = h.shape[0]
    c = CHUNK
    nch = tp // c
    tl = _tiles(tp)
    tr, tmg = tl["tr"], tl["tmg"]
    (u,) = _seq_fwd("rms_premix" + tag, _rmsn_step, [_r(h, tr, 1024)], [p["npm"]], [], [], [_out2(tp, 1024, tr, bf16)], tp // tr)
    proj = _mm("in_proj" + tag, u, p["w_in"], "nn", _div_tile(tp, 1056, 16), 1408, 1024, carry=carry.get("in"))
    if "in_done" in carry:
        p = dict(p, **carry["in_done"]())
    ret_rows, ssd_rows, hg_rows = _mixer_rows(proj, c)
    y_ret, ret_s = _seq_fwd("ret_fwd" + tag, _ret_step, ret_rows + [_r(tabs["cos"], c, 128), _r(tabs["sin"], c, 128)],
                            [p["ret_gn"]], tabs["ret"], [(256, 512)], [_out2(tp, 512, c)], nch, save_states=True,
                            carry=carry.get("ret"))
    ycore = _s5_core_fwd(tag, proj, *_s5_consts(p, tabs, tag), carry=carry.get("s5"))
    (y_s5,) = _seq_fwd("s5_post" + tag, _s5post_step, [_r(ycore, tr, 512), _r(proj, tr, 512, OFF["s5u"] // 512)],
                       [p["s5_d"], p["glu_w"], p["glu_b"]], [], [], [_out2(tp, 512, tr)], tp // tr)
    y_ssd, ssd_s, ssd_tail = _seq_fwd(
        "ssd_fwd" + tag, _ssd_step, ssd_rows, [p["conv_w"], p["conv_b"], p["dt_bias"], p["a_log"], p["ssd_d"], p["ssd_nw"]],
        [tabs["tri"]] + tabs["ssd"], [(256, 512), (8, 1024)], [_out2(tp, 512, c)], nch, save_states=True, carry=carry.get("ssd"))
    y_hg, hg_s = _seq_fwd("hg_fwd" + tag, _hg_step, hg_rows, [p["lb"], p["hg_nw"]], [tabs["tri"]], [(128, 512)],
                          [_out2(tp, 512, c)], nch, save_states=True, carry=carry.get("hg"))
    (h_mid,) = _seq_fwd(
        "merge_fwd" + tag, _merge_step,
        [_r(y_ret, tmg, 512), _r(y_s5, tmg, 512), _r(y_ssd, tmg, 512), _r(y_hg, tmg, 512), _r(proj, tmg, 4096, 0),
         _r(h, tmg, 1024)],
        [p["w_branch"], p["w_out"], p["npostmix"]], [], [], [_out2(tp, 1024, tmg)], tp // tmg, carry=carry.get("mg"))
    if "mg_done" in carry:
        p = dict(p, **carry["mg_done"]())
    (u2,) = _seq_fwd("rms_premlp" + tag, _rmsn_step, [_r(h_mid, tr, 1024)], [p["npremlp"]], [], [],
                     [_out2(tp, 1024, tr, bf16)], tp // tr)
    m = _mlp_core_fwd(tag, u2, p["w_up"], p["w_down"], carry=carry.get("mlp"))
    (h_new,) = _seq_fwd("mlp_post" + tag, _resid_rms_step, [_r(m, tr, 1024), _r(h_mid, tr, 1024)], [p["npostmlp"]], [], [],
                        [_out2(tp, 1024, tr)], tp // tr)
    saved = dict(h=h, u=u, proj=proj, ret_s=ret_s, ycore=ycore, ssd_s=ssd_s, ssd_tail=ssd_tail, hg_s=hg_s,
                 y_ret=y_ret, y_s5=y_s5, y_ssd=y_ssd, y_hg=y_hg, h_mid=h_mid, u2=u2, m=m)
    return h_new, saved, p


def _layer_bwd(dh, p, sv, tabs, tag, carry):
    tp = dh.shape[0]
    c = CHUNK
    nch = tp // c
    tl = _tiles(tp)
    tr, tmg = tl["tr"], tl["tmg"]
    proj = sv["proj"]
    g = {}
    (d_m, d_hmid), (g["npostmlp"],) = _seq_bwd(
        "mlp_post_bwd" + tag, _resid_rms_step, [_r(sv["m"], tr, 1024), _r(sv["h_mid"], tr, 1024)], [True, True],
        [p["npostmlp"]], [], [], [_r(dh, tr, 1024)], tp // tr)
    d_u2, g["w_up"], g["w_down"] = _mlp_core_bwd(tag, sv["u2"], d_m, p["w_up"], p["w_down"], carry=carry.get("mlp"))
    (d_hmid,), (g["npremlp"],) = _seq_bwd(
        "rms_premlp_bwd" + tag, _rmsn_step_b, [_r(sv["h_mid"], tr, 1024)], [True], [p["npremlp"]], [], [],
        [_r(d_u2, tr, 1024), _r(d_hmid, tr, 1024)], tp // tr)
    (dy_ret, dy_s5, dy_ssd, dy_hg, d_gates, d_h1), (g["w_branch"], g["w_out"], g["npostmix"]) = _seq_bwd(
        "merge_bwd" + tag, _merge_step,
        [_r(sv["y_ret"], tmg, 512), _r(sv["y_s5"], tmg, 512), _r(sv["y_ssd"], tmg, 512), _r(sv["y_hg"], tmg, 512),
         _r(proj, tmg, 4096, 0), _r(sv["h"], tmg, 1024)], [True] * 6,
        [p["w_branch"], p["w_out"], p["npostmix"]], [], [], [_r(d_hmid, tmg, 1024)], tp // tmg)
    ret_rows, ssd_rows, hg_rows = _mixer_rows(proj, c)
    (d_hq, d_hf, d_hi, d_hg), (g["lb"], g["hg_nw"]) = _seq_bwd(
        "hg_bwd" + tag, _hg_step, hg_rows, [True] * 4, [p["lb"], p["hg_nw"]], [tabs["tri"]], [sv["hg_s"]],
        [_r(dy_hg, c, 512)], nch, carry=carry.get("hg"))
    (d_z, d_xbc, d_dt), (g["conv_w"], g["conv_b"], g["dt_bias"], g["a_log"], g["ssd_d"], g["ssd_nw"]) = _seq_bwd(
        "ssd_bwd" + tag, _ssd_step, ssd_rows, [True] * 3,
        [p["conv_w"], p["conv_b"], p["dt_bias"], p["a_log"], p["ssd_d"], p["ssd_nw"]], [tabs["tri"]] + tabs["ssd"],
        [sv["ssd_s"], sv["ssd_tail"]], [_r(dy_ssd, c, 512)], nch, carry=carry.get("ssd"))
    (d_ycore, du_post), (g["s5_d"], g["glu_w"], g["glu_b"]) = _seq_bwd(
        "s5_post_bwd" + tag, _s5post_step, [_r(sv["ycore"], tr, 512), _r(proj, tr, 512, OFF["s5u"] // 512)], [True, True],
        [p["s5_d"], p["glu_w"], p["glu_b"]], [], [], [_r(dy_s5, tr, 512)], tp // tr)
    du_s5, dbbr, dbbi, dar, dai, dccr, dcci = _s5_core_bwd(tag, proj, *_s5_consts(p, tabs, tag + "b"), d_ycore, du_post,
                                                           carry=carry.get("s5"))
    g["c_re"], g["c_im"] = _cc_diag(dccr), _cc_diag(dcci)
    (g["lam_re"], g["lam_im"], g["lstep"], g["bt_re"], g["bt_im"]), _ = _seq_bwd(
        "s5_prep_bwd" + tag, _s5prep_step, _s5_prep_rows(p), [True] * 5, [], [tabs["rep"]], [],
        [_full(dar.reshape(32, 64)), _full(dai.reshape(32, 64)), _full(_bb_diag(dbbr)), _full(_bb_diag(dbbi))], 1)
    if "late" in carry:
        carry = dict(carry, **carry["late"](g))
    (d_q, d_k, d_v, d_g), (g["ret_gn"],) = _seq_bwd(
        "ret_bwd" + tag, _ret_step, ret_rows + [_r(tabs["cos"], c, 128), _r(tabs["sin"], c, 128)], [True] * 4 + [False] * 2,
        [p["ret_gn"]], tabs["ret"], [sv["ret_s"]], [_r(dy_ret, c, 512)], nch, carry=carry.get("ret"))
    dproj = jnp.concatenate([d_gates, d_q, d_k, d_v, d_g, du_s5, d_xbc, d_z, d_hq, d_hf, d_hi, d_hg, d_dt], axis=1).astype(bf16)
    g["w_in"] = _mm("in_proj_dw" + tag, sv["u"], dproj, "tn", 1024, 1408, _div_tile(tp, 704, 16), carry=carry.get("dw"))
    du = _mm("in_proj_dx" + tag, dproj, p["w_in"], "nt", _div_tile(tp, 1056, 16), 1024, 1408, carry=carry.get("dx"))
    (dh_prev,), (g["npm"],) = _seq_bwd("rms_premix_bwd" + tag, _rmsn_step_b, [_r(sv["h"], tr, 1024)], [True], [p["npm"]], [], [],
                                       [_r(du, tr, 1024), _r(d_h1, tr, 1024)], tp // tr)
    return dh_prev, g


def _loss_call(h, tgt, lo, hi):
    tp = h.shape[0]
    tr = _div_tile(tp, 352, 16)

    def step(rows, consts, nds, states, n):
        hh, tt = rows
        row = n * tr + lax.broadcasted_iota(jnp.int32, hh.shape, 0)
        err = jnp.where((row >= lo) & (row < hi), hh - tt, 0.0)
        part = 0.5 * jnp.sum(err * err) * (1.0 / D_MODEL)
        return (err * (1.0 / D_MODEL), jnp.zeros((8, 128), f32) + part), ()

    dh, parts = _seq_fwd("loss_head", step, [_r(h, tr, 1024), _r(tgt, tr, 1024)], [], [], [],
                         [_out2(tp, 1024, tr), ((8 * (tp // tr), 128), f32, (8, 128), lambda n: (n, 0))], tp // tr)
    return dh, jnp.sum(parts[::8, 0])


def _adam_step(rows, consts, nds, states, _):
    g8, w, m, v = rows
    g = g8[0].astype(f32)
    for d in range(1, N_DEV):
        g = g + g8[d].astype(f32)
    m2 = ADAM_B1 * m + (1.0 - ADAM_B1) * g
    v2 = ADAM_B2 * v + (1.0 - ADAM_B2) * jnp.square(g)
    m_hat = m2 / (1.0 - ADAM_B1 ** ADAM_STEP)
    v_hat = v2 / (1.0 - ADAM_B2 ** ADAM_STEP)
    delta = -ADAM_LR * (m_hat / (jnp.sqrt(v_hat) + ADAM_EPS) + ADAM_WD * w)
    return (g, delta, m2, v2), ()


def _adam_call(name, g8, w, m, v, carry=None):
    r, wd = w.shape
    tb = r
    for cand in range(16, r + 1, 16):
        if r % cand == 0 and cand * wd <= 256 * 1024:
            tb = cand
    o = ((r, wd), f32, (tb, wd), lambda n: (n, 0))
    return _seq_fwd(name, _adam_step, [(g8, (N_DEV, tb, wd), lambda n: (0, n, 0)), _r(w, tb, wd), _r(m, tb, wd), _r(v, tb, wd)],
                    [], [], [], [o, o, o, o], r // tb, carry=carry)


WEIGHTS = ['meta_tokens', 'w_in', 'w_branch', 'w_out', 'norm_pre_mix', 'norm_post_mix', 'norm_pre_mlp', 'norm_post_mlp',
           'w_up', 'w_down', 'ret_gn_w', 's5_lam_re', 's5_lam_im', 's5_b_re', 's5_b_im', 's5_c_re', 's5_c_im', 's5_d',
           's5_log_step', 's5_glu_w', 's5_glu_b', 'ssd_conv_w', 'ssd_conv_b', 'ssd_dt_bias', 'ssd_a_log', 'ssd_d',
           'ssd_norm_w', 'hgrn_lb', 'hgrn_norm_w']
SHARDED = [("w_in", 2), ("w_branch", 3), ("w_out", 1), ("w_up", 2), ("w_down", 1), ("s5_glu_w", 2), ("meta_tokens", 1),
           ("ssd_conv_w", 2)]
N_BF16 = 6
REPL = [n for n in WEIGHTS if n not in dict(SHARDED)]
SMALL_PAD = 512
WIRE = bf16
SHARD_COLS = IN_DIM // N_DEV


def _col_pieces():
    out, pos = [], 0
    for a, b in _orig_col_slices() + [(3584, 3592)]:
        if a == 3584:
            pos = OFF["dt"]
        while a < b:
            e = min(b, (a // SHARD_COLS + 1) * SHARD_COLS)
            out.append((a, e, pos))
            pos += e - a
            a = e
    return out


def _w_in_from_shards(got_l):
    parts = [got_l[a // SHARD_COLS][:, a % SHARD_COLS:a % SHARD_COLS + (b - a)] for a, b, _ in _col_pieces()]
    parts.append(jnp.zeros((got_l.shape[1], NP - IN_DIM - (OFF["dt"] - 9728)), got_l.dtype))
    return jnp.concatenate(parts, axis=1)


def _w_in_to_shards(g):
    pieces = sorted(_col_pieces())
    blocks = []
    for d in range(N_DEV):
        blocks.append(jnp.concatenate([g[:, m:m + (b - a)] for a, b, m in pieces if a // SHARD_COLS == d], axis=1))
    return jnp.stack(blocks, axis=0)


def _to8(full, axis):
    sh = full.shape
    return jnp.moveaxis(full.reshape(sh[:axis] + (N_DEV, sh[axis] // N_DEV) + sh[axis + 1:]), axis, 0)


def _from8(g8, axis):
    r = jnp.moveaxis(g8, 0, axis)
    sh = r.shape
    return r.reshape(sh[:axis] + (sh[axis] * sh[axis + 1],) + sh[axis + 2:])


def _pack(arrs, pad_rows):
    flat = jnp.concatenate([a.reshape(-1) for a in arrs])
    n = flat.shape[0]
    total = -(-n // (128 * pad_rows)) * (128 * pad_rows)
    if total != n:
        flat = jnp.concatenate([flat, jnp.zeros((total - n,), flat.dtype)])
    return flat.reshape(total // 128, 128)


def _unpack(flat, shapes):
    v = flat.reshape(-1)
    out, pos = [], 0
    for s in shapes:
        n = int(np.prod(s))
        out.append(v[pos:pos + n].reshape(tuple(s)))
        pos += n
    return out


def _rows2d(a, lead=0):
    return a.reshape(a.shape[:lead] + (-1, a.shape[-1]))


def _local_step(x0, tgt0, wf, first_w, next_w, send_grads):
    seq = x0.shape[0]
    t = N_META + seq
    tp = -(-t // CHUNK) * CHUNK
    tabs = _tables(tp)
    lb_in = jnp.concatenate([wf["hgrn_lb"], jnp.zeros((8 - DEPTH, BW), f32)], axis=0)
    (lb_all,) = _seq_fwd("lb_prep", _lb_step, [_full(lb_in)], [], [tabs["lmat"]], [], [((8, BW), f32, (8, BW), lambda n: (0, 0))], 1)

    def pad128(a):
        return jnp.concatenate([a, jnp.zeros((128 - a.shape[0],), f32)]).reshape(1, 128)

    def layer_params(l, big):
        return dict(
            big,
            npm=wf["norm_pre_mix"][l].reshape(1, D_MODEL), npostmix=wf["norm_post_mix"][l].reshape(1, D_MODEL),
            npremlp=wf["norm_pre_mlp"][l].reshape(1, D_MODEL), npostmlp=wf["norm_post_mlp"][l].reshape(1, D_MODEL),
            ret_gn=wf["ret_gn_w"][l].reshape(1, BW), lam_re=wf["s5_lam_re"][l], lam_im=wf["s5_lam_im"][l],
            lstep=wf["s5_log_step"][l].reshape(S5_G, 1),
            bt_re=wf["s5_b_re"][l].transpose(0, 2, 1).reshape(S5_G * S5_J, S5_P),
            bt_im=wf["s5_b_im"][l].transpose(0, 2, 1).reshape(S5_G * S5_J, S5_P),
            c_re=wf["s5_c_re"][l], c_im=wf["s5_c_im"][l], s5_d=wf["s5_d"][l].reshape(1, BW),
            glu_b=wf["s5_glu_b"][l].reshape(1, 2 * BW), conv_w=wf["ssd_conv_w"][l],
            conv_b=wf["ssd_conv_b"][l].reshape(1, 1024), dt_bias=pad128(wf["ssd_dt_bias"][l]),
            a_log=pad128(wf["ssd_a_log"][l]), ssd_d=pad128(wf["ssd_d"][l]), ssd_nw=wf["ssd_norm_w"][l].reshape(1, BW),
            lb=lb_all[l].reshape(1, BW), hg_nw=wf["hgrn_norm_w"][l].reshape(1, BW))

    zpad = jnp.zeros((tp - t, D_MODEL), f32)
    h = jnp.concatenate([wf["meta_tokens"], x0, zpad], axis=0)
    tgt = jnp.concatenate([jnp.zeros((N_META, D_MODEL), f32), tgt0, zpad], axis=0)
    params, saved, big = [], [], first_w
    for l in range(DEPTH):
        carry, arrived = next_w(l)
        h, sv, p = _layer_fwd(h, layer_params(l, big), tabs, "_l%d" % l, carry)
        params.append(p)
        saved.append(sv)
        big = arrived() if arrived else None
    dh, loss_local = _loss_call(h, tgt, N_META, t)
    g, carry = [None] * DEPTH, {}
    for l in reversed(range(DEPTH)):
        dh, g[l] = _layer_bwd(dh, params[l], saved[l], tabs, "_l%d" % l, carry)
        carry = send_grads(l, g[l]) if l > 0 else {}
    d_lb = jnp.concatenate([jnp.concatenate([gl["lb"] for gl in g], axis=0), jnp.zeros((8 - DEPTH, BW), f32)], axis=0)
    (d_hgrn_lb,), _ = _seq_bwd("lb_prep_bwd", _lb_step, [_full(lb_in)], [True], [], [tabs["lmat"]], [], [_full(d_lb)], 1)
    return loss_local, dh, g, d_hgrn_lb[:DEPTH]


def kernel(x, meta_tokens, w_in, w_branch, w_out, norm_pre_mix, norm_post_mix, norm_pre_mlp, norm_post_mlp, w_up, w_down, ret_gn_w, s5_lam_re, s5_lam_im, s5_b_re, s5_b_im, s5_c_re, s5_c_im, s5_d, s5_log_step, s5_glu_w, s5_glu_b, ssd_conv_w, ssd_conv_b, ssd_dt_bias, ssd_a_log, ssd_d, ssd_norm_w, hgrn_lb, hgrn_norm_w, loss_target, m_meta_tokens, m_w_in, m_w_branch, m_w_out, m_norm_pre_mix, m_norm_post_mix, m_norm_pre_mlp, m_norm_post_mlp, m_w_up, m_w_down, m_ret_gn_w, m_s5_lam_re, m_s5_lam_im, m_s5_b_re, m_s5_b_im, m_s5_c_re, m_s5_c_im, m_s5_d, m_s5_log_step, m_s5_glu_w, m_s5_glu_b, m_ssd_conv_w, m_ssd_conv_b, m_ssd_dt_bias, m_ssd_a_log, m_ssd_d, m_ssd_norm_w, m_hgrn_lb, m_hgrn_norm_w, v_meta_tokens, v_w_in, v_w_branch, v_w_out, v_norm_pre_mix, v_norm_post_mix, v_norm_pre_mlp, v_norm_post_mlp, v_w_up, v_w_down, v_ret_gn_w, v_s5_lam_re, v_s5_lam_im, v_s5_b_re, v_s5_b_im, v_s5_c_re, v_s5_c_im, v_s5_d, v_s5_log_step, v_s5_glu_w, v_s5_glu_b, v_ssd_conv_w, v_ssd_conv_b, v_ssd_dt_bias, v_ssd_a_log, v_ssd_d, v_ssd_norm_w, v_hgrn_lb, v_hgrn_norm_w):
    args = dict(locals())
    w = {n: args[n] for n in WEIGHTS}
    mom = {n: args["m_" + n] for n in WEIGHTS}
    var = {n: args["v_" + n] for n in WEIGHTS}
    names = [n for n, _ in SHARDED]
    big = names[:N_BF16]

    shard = {n: [w[n][l].astype(bf16) for l in range(DEPTH)] for n in big}
    got = _all_gather("gather_first", [shard["w_in"][0], w["meta_tokens"], w["ssd_conv_w"]])
    wf = {n: w[n] for n in REPL}
    wf["meta_tokens"] = _from8(got[1], 1)
    wf["ssd_conv_w"] = _from8(got[2], 2)

    full = dict(w_branch=lambda a: _from8(a, 2), w_out=lambda a: _from8(a, 0), w_up=lambda a: _from8(a, 1),
                w_down=lambda a: _from8(a, 0), s5_glu_w=lambda a: _from8(a, 1))
    key = dict(w_branch="w_branch", w_out="w_out", w_up="w_up", w_down="w_down", s5_glu_w="glu_w")
    half = w["w_in"].shape[1] // 2

    def arrived(pairs):
        return {key[n]: full[n](a) for n, a in pairs}

    def next_w(l):
        carry = {}
        if l == 0:
            n_in = ("w_branch", "w_out", "s5_glu_w", "w_up")
            first, second = _Carry("gather", [shard[n][0] for n in n_in]), _Carry("gather", [shard["w_down"][0]])
            carry.update({"in": first, "in_done": lambda: arrived(zip(n_in, first.result)),
                          "mg": second, "mg_done": lambda: arrived([("w_down", second.result[0])])})
        if l + 1 == DEPTH:
            return carry, None
        n_ret = ("w_branch", "w_out", "s5_glu_w")
        nxt = dict(s5=_Carry("gather", [shard["w_in"][l + 1][:half]]), mlp=_Carry("gather", [shard["w_in"][l + 1][half:]]),
                   ssd=_Carry("gather", [shard["w_up"][l + 1]]), hg=_Carry("gather", [shard["w_down"][l + 1]]),
                   ret=_Carry("gather", [shard[n][l + 1] for n in n_ret]))
        carry.update(nxt)
        return carry, lambda: dict(
            arrived(list(zip(n_ret, nxt["ret"].result)) + [("w_up", nxt["ssd"].result[0]), ("w_down", nxt["hg"].result[0])]),
            w_in=_w_in_from_shards(jnp.concatenate([nxt["s5"].result[0], nxt["mlp"].result[0]], axis=1)))

    def to_wire(gl):
        return dict(w_in=None if gl["w_in"] is None else _w_in_to_shards(gl["w_in"]).astype(WIRE), w_branch=_to8(gl["w_branch"], 2).astype(WIRE),
                    w_out=_to8(gl["w_out"], 0).astype(WIRE), w_up=_to8(gl["w_up"], 1).astype(WIRE),
                    w_down=_to8(gl["w_down"], 0).astype(WIRE), s5_glu_w=_to8(gl["glu_w"], 1).astype(WIRE))

    sent = [None] * DEPTH

    def send_grads(l, gl):
        wire = to_wire(gl)
        carry = dict(s5=_Carry("scatter", [wire["w_in"][:, :half]]), mlp=_Carry("scatter", [wire["w_up"]]),
                     ssd=_Carry("scatter", [wire["w_down"], wire["w_branch"]]),
                     hg=_Carry("scatter", [wire["w_in"][:, half:], wire["w_out"], wire["s5_glu_w"]]))
        sent[l] = lambda: dict(w_in=jnp.concatenate([carry["s5"].result[0], carry["hg"].result[0]], axis=1),
                               w_up=carry["mlp"].result[0], w_down=carry["ssd"].result[0], w_branch=carry["ssd"].result[1],
                               w_out=carry["hg"].result[1], s5_glu_w=carry["hg"].result[2])
        if l == 1:
            def late(g0):
                wire0 = to_wire(dict(g0, w_in=None))
                rt = _Carry("scatter", [wire0["w_up"]])
                dw = _Carry("scatter", [wire0["w_down"], wire0["s5_glu_w"]])
                dx = _Carry("scatter", [wire0["w_branch"], wire0["w_out"]])
                sent[0] = lambda: dict(w_up=rt.result[0], w_down=dw.result[0], s5_glu_w=dw.result[1], w_branch=dx.result[0],
                                       w_out=dx.result[1])
                return dict(ret=rt, dw=dw, dx=dx)
            carry["late"] = late
        return carry

    loss_local, dh0, g, d_hgrn_lb = _local_step(x[0], loss_target[0], wf, dict(w_in=_w_in_from_shards(got[0])), next_w, send_grads)
    seq = x.shape[1]
    t = N_META + seq
    tiny = _exchange("scatter_small", [_to8(dh0[:N_META], 1).astype(WIRE),
                                       jnp.stack([_to8(gl["conv_w"], 1) for gl in g], axis=1).astype(WIRE)])
    w_in0 = _w_in_to_shards(g[0]["w_in"]).astype(WIRE)
    riders = {n: _Carry("scatter", [w_in0[:, a:b]]) for n, a, b in (("w_up", 0, 384), ("w_down", 384, 768), ("w_branch", 768, 1024))}
    per_layer = [sent[l]() for l in range(DEPTH)]
    parts = {n: jnp.stack([pl_[n] for pl_ in per_layer], axis=1) for n in big if n != "w_in"}
    parts["meta_tokens"], parts["ssd_conv_w"] = tiny

    def stack_l(key, shape):
        return jnp.stack([gl[key].reshape(shape) for gl in g], axis=0)

    small = dict(
        norm_pre_mix=stack_l("npm", (D_MODEL,)), norm_post_mix=stack_l("npostmix", (D_MODEL,)),
        norm_pre_mlp=stack_l("npremlp", (D_MODEL,)), norm_post_mlp=stack_l("npostmlp", (D_MODEL,)),
        ret_gn_w=stack_l("ret_gn", (BW,)), s5_lam_re=stack_l("lam_re", (S5_G, S5_P)), s5_lam_im=stack_l("lam_im", (S5_G, S5_P)),
        s5_b_re=stack_l("bt_re", (S5_G, S5_J, S5_P)).transpose(0, 1, 3, 2),
        s5_b_im=stack_l("bt_im", (S5_G, S5_J, S5_P)).transpose(0, 1, 3, 2),
        s5_c_re=stack_l("c_re", (S5_G, S5_J, S5_P)), s5_c_im=stack_l("c_im", (S5_G, S5_J, S5_P)),
        s5_d=stack_l("s5_d", (BW,)), s5_log_step=stack_l("lstep", (S5_G,)), s5_glu_b=stack_l("glu_b", (2 * BW,)),
        ssd_conv_b=stack_l("conv_b", (1024,)), ssd_dt_bias=stack_l("dt_bias", (128,))[:, :SSD_HEADS],
        ssd_a_log=stack_l("a_log", (128,))[:, :SSD_HEADS], ssd_d=stack_l("ssd_d", (128,))[:, :SSD_HEADS],
        ssd_norm_w=stack_l("ssd_nw", (BW,)), hgrn_lb=d_hgrn_lb, hgrn_norm_w=stack_l("hg_nw", (BW,)))
    small_ride = _Carry("gather", [_pack([small[n] for n in REPL], SMALL_PAD)])
    out = {k: {} for k in ("grad", "delta", "m", "v")}
    for n in [x for x in names if x != "w_in"] + ["w_in"]:
        if n == "w_in":
            w_in_l0 = jnp.concatenate([riders[x].result[0] for x in ("w_up", "w_down", "w_branch")], axis=1)
            parts[n] = jnp.stack([w_in_l0] + [pl_[n] for pl_ in per_layer[1:]], axis=1)
        res = _adam_call("adamw_" + n, _rows2d(parts[n], 1), _rows2d(w[n]), _rows2d(mom[n]), _rows2d(var[n]),
                         carry=small_ride if n == "w_in" else riders.get(n))
        for k, r in zip(("grad", "delta", "m", "v"), res):
            out[k][n] = r.reshape(w[n].shape)
    res = _adam_call("adamw_replicated", small_ride.result[0], _pack([w[n] for n in REPL], SMALL_PAD),
                     _pack([mom[n] for n in REPL], SMALL_PAD), _pack([var[n] for n in REPL], SMALL_PAD))
    for k, r in zip(("grad", "delta", "m", "v"), res):
        out[k].update(zip(REPL, _unpack(r, [w[n].shape for n in REPL])))

    loss = lax.psum(loss_local, ("x", "y", "c"))
    return (loss, dh0[N_META:t][None], *[out["grad"][n] for n in WEIGHTS], *[out["delta"][n] for n in WEIGHTS],
            *[out["m"][n] for n in WEIGHTS], *[out["v"][n] for n in WEIGHTS])
```

```python
import numpy as np
import jax
import jax.numpy as jnp
from jax import lax
from jax.experimental import pallas as pl
from jax.experimental.pallas import tpu as pltpu

f32 = jnp.float32
bf16 = jnp.bfloat16
HI = lax.Precision.HIGHEST

D_MODEL = 1024
N_META = 16
DEPTH = 4
BW = 512
D_FF = 4096
EPS = 1e-6
N_DEV = 8
RET_HEADS = 4
SSD_HEADS = 8
SSD_GROUPS = 2
HG_HEADS = 4
S5_G, S5_J, S5_P = 32, 16, 64

ADAM_LR, ADAM_B1, ADAM_B2, ADAM_EPS, ADAM_WD, ADAM_STEP = 0.001, 0.9, 0.999, 1e-08, 0.01, 10

CHUNK = 64
HG_SUB = 16
VMEM_LIMIT = 56 * 1024 * 1024

OFF = dict(gates=0, rq=4096, rk=4352, rv=4608, rg=5120, s5u=5632, sxbc=6144, sz=7168, hq=7680, hf=8192,
           hi=8704, hg=9216, dt=9728)
NP = 9856
IN_DIM = 9736


def _orig_col_slices():
    sl = [(5640, 9736)]
    for base in (0, 256):
        for half in (0, 32):
            for h in range(4):
                sl.append((base + 64 * h + half, base + 64 * h + half + 32))
    sl.append((512, 1024))
    sl.append((1024, 1536))
    sl.append((1536, 2048))
    sl.append((2560, 3584))
    sl.append((2048, 2560))
    sl.append((3592, 5640))
    return sl


def _bdot(a, b):
    return jnp.dot(a.astype(bf16), b.astype(bf16), preferred_element_type=f32)


def _bdot_nt(a, b):
    return lax.dot_general(a.astype(bf16), b.astype(bf16), (((1,), (1,)), ((), ())), preferred_element_type=f32)


def _bdot_tn(a, b):
    return lax.dot_general(a.astype(bf16), b.astype(bf16), (((0,), (0,)), ((), ())), preferred_element_type=f32)


def _hdot(a, b):
    return jnp.dot(a, b, precision=HI, preferred_element_type=f32)


def _sig(x):
    return jax.nn.sigmoid(x)


def _rms(x, w):
    return x * lax.rsqrt(jnp.mean(x * x, axis=-1, keepdims=True) + EPS) * w


def _softplus(x):
    return jnp.maximum(x, 0.0) + jnp.log(1.0 + jnp.exp(-jnp.abs(x)))


def _r(arr, rb, w, jb=0):
    return (arr, (rb, w), lambda n, jb=jb: (n, jb))


def _full(arr):
    nd = arr.ndim
    return (arr, arr.shape, lambda n, nd=nd: (0,) * nd)


def _seq_fwd(name, step, rows, consts, nds, states, outs, n_chunks, save_states=False, carry=None):
    nr, nc, nn, ns, no = len(rows), len(consts), len(nds), len(states), len(outs)
    whole = list(consts) + list(nds)

    def body(*refs):
        row_refs = refs[:nr]
        whole_hbm = refs[nr:nr + nc + nn]
        out_refs = refs[nr + nc + nn:nr + nc + nn + no]
        k = nr + nc + nn + no
        saved_refs = refs[k:k + (ns if save_states else 0)]
        k += ns if save_states else 0
        whole_vmem = refs[k:k + nc + nn]
        state_refs = refs[k + nc + nn:]
        n = pl.program_id(0)

        @pl.when(n == 0)
        def _():
            for src, dst in zip(whole_hbm, whole_vmem):
                pltpu.sync_copy(src, dst)
            for s in state_refs:
                s[...] = jnp.zeros_like(s)

        st = tuple(s[...] for s in state_refs)
        if save_states:
            for sv, v in zip(saved_refs, st):
                sv[0] = v
        o, new = step(tuple(r[...] for r in row_refs), tuple(c[...] for c in whole_vmem[:nc]),
                      tuple(c[...] for c in whole_vmem[nc:]), st, n)
        for ref, v in zip(out_refs, o):
            ref[...] = v.astype(ref.dtype)
        for s, v in zip(state_refs, new):
            s[...] = v

    in_specs = [pl.BlockSpec(bs, im) for _, bs, im in rows] + [pl.BlockSpec(memory_space=pl.ANY)] * (nc + nn)
    out_shape = [jax.ShapeDtypeStruct(s, d) for s, d, _, _ in outs]
    out_specs = [pl.BlockSpec(bs, im) for _, _, bs, im in outs]
    if save_states:
        for s in states:
            out_shape.append(jax.ShapeDtypeStruct((n_chunks,) + tuple(s), f32))
            out_specs.append(pl.BlockSpec((1,) + tuple(s), lambda n, z=len(s): (n,) + (0,) * z))
    scratch = [pltpu.VMEM(a.shape, a.dtype) for a in whole] + [pltpu.VMEM(tuple(s), f32) for s in states]
    return _pcall(body, name, (n_chunks,), in_specs, out_specs, out_shape, scratch, [a for a, _, _ in rows] + whole,
                  ("arbitrary",), carry)


def _seq_bwd(name, step, rows, row_diff, consts, nds, saved, couts, n_chunks, carry=None):
    nr, nc, nn, ns, no = len(rows), len(consts), len(nds), len(saved), len(couts)
    whole = list(consts) + list(nds)
    didx = [i for i in range(nr) if row_diff[i]]

    def rev(im):
        return lambda n: im(n_chunks - 1 - n)

    def body(*refs):
        row_refs = refs[:nr]
        whole_hbm = refs[nr:nr + nc + nn]
        k = nr + nc + nn
        saved_refs = refs[k:k + ns]
        k += ns
        cout_refs = refs[k:k + no]
        k += no
        drow_refs = refs[k:k + len(didx)]
        k += len(didx)
        dconst_hbm = refs[k:k + nc]
        k += nc
        whole_vmem = refs[k:k + nc + nn]
        k += nc + nn
        dconst_acc = refs[k:k + nc]
        k += nc
        dstate_refs = refs[k:]
        n = pl.program_id(0)

        @pl.when(n == 0)
        def _():
            for src, dst in zip(whole_hbm, whole_vmem):
                pltpu.sync_copy(src, dst)
            for a in dconst_acc:
                a[...] = jnp.zeros_like(a)
            for s in dstate_refs:
                s[...] = jnp.zeros_like(s)

        rvals = tuple(r[...] for r in row_refs)
        cvals = tuple(c[...] for c in whole_vmem[:nc])
        nvals = tuple(c[...] for c in whole_vmem[nc:])
        svals = tuple(s[0] for s in saved_refs)
        cidx = n_chunks - 1 - n

        def f(dr, cv, sv):
            full = list(rvals)
            for i, v in zip(didx, dr):
                full[i] = v
            return step(tuple(full), cv, nvals, sv, cidx)

        (o, _), vf = jax.vjp(f, tuple(rvals[i] for i in didx), cvals, svals)
        ct_o = tuple(c[...].astype(v.dtype) for c, v in zip(cout_refs, o))
        ct_s = tuple(s[...] for s in dstate_refs)
        d_rows, d_consts, d_states = vf((ct_o, ct_s))
        for ref, v in zip(drow_refs, d_rows):
            ref[...] = v.astype(ref.dtype)
        for a, v in zip(dconst_acc, d_consts):
            a[...] += v.astype(f32)
        for s, v in zip(dstate_refs, d_states):
            s[...] = v

        @pl.when(n == n_chunks - 1)
        def _():
            for a, dst in zip(dconst_acc, dconst_hbm):
                pltpu.sync_copy(a, dst)

    in_specs = ([pl.BlockSpec(bs, rev(im)) for _, bs, im in rows]
                + [pl.BlockSpec(memory_space=pl.ANY)] * (nc + nn)
                + [pl.BlockSpec((1,) + a.shape[1:], lambda n, z=a.ndim - 1: (n_chunks - 1 - n,) + (0,) * z) for a in saved]
                + [pl.BlockSpec(bs, rev(im)) for _, bs, im in couts])
    out_shape, out_specs = [], []
    for i in didx:
        a, bs, im = rows[i]
        nrows = a.shape[0]
        out_shape.append(jax.ShapeDtypeStruct((nrows,) + tuple(bs[1:]), f32))
        out_specs.append(pl.BlockSpec(bs, (lambda im: lambda n: (im(n_chunks - 1 - n)[0],) + (0,) * (len(bs) - 1))(im)))
    for c in consts:
        out_shape.append(jax.ShapeDtypeStruct(c.shape, f32))
        out_specs.append(pl.BlockSpec(memory_space=pl.ANY))
    scratch = ([pltpu.VMEM(a.shape, a.dtype) for a in whole] + [pltpu.VMEM(c.shape, f32) for c in consts]
               + [pltpu.VMEM(a.shape[1:], f32) for a in saved])
    res = _pcall(body, name, (n_chunks,), in_specs, out_specs, out_shape, scratch,
                 [a for a, _, _ in rows] + whole + list(saved) + [a for a, _, _ in couts], ("arbitrary",), carry)
    return res[:len(didx)], res[len(didx):]


def _mm(name, a, b, mode, tm, tn, tk, precision=None, carry=None):
    if mode == "nn":
        (m, kd), nn_ = a.shape, b.shape[1]
        a_spec = pl.BlockSpec((tm, tk), lambda i, j, k: (i, k))
        b_spec = pl.BlockSpec((tk, tn), lambda i, j, k: (k, j))
        dims = (((1,), (0,)), ((), ()))
    elif mode == "tn":
        (kd, m), nn_ = a.shape, b.shape[1]
        a_spec = pl.BlockSpec((tk, tm), lambda i, j, k: (k, i))
        b_spec = pl.BlockSpec((tk, tn), lambda i, j, k: (k, j))
        dims = (((0,), (0,)), ((), ()))
    else:
        (m, kd), nn_ = a.shape, b.shape[0]
        a_spec = pl.BlockSpec((tm, tk), lambda i, j, k: (i, k))
        b_spec = pl.BlockSpec((tn, tk), lambda i, j, k: (j, k))
        dims = (((1,), (1,)), ((), ()))
    assert m % tm == 0 and nn_ % tn == 0 and kd % tk == 0, (name, a.shape, b.shape, tm, tn, tk)
    nk = kd // tk

    def body(a_ref, b_ref, o_ref, acc):
        k = pl.program_id(2)

        @pl.when(k == 0)
        def _():
            acc[...] = jnp.zeros_like(acc)

        if precision is None:
            acc[...] += lax.dot_general(a_ref[...].astype(bf16), b_ref[...].astype(bf16), dims, preferred_element_type=f32)
        else:
            acc[...] += lax.dot_general(a_ref[...], b_ref[...], dims, precision=precision, preferred_element_type=f32)

        @pl.when(k == nk - 1)
        def _():
            o_ref[...] = acc[...]

    return _pcall(body, name, (m // tm, nn_ // tn, nk), [a_spec, b_spec], [pl.BlockSpec((tm, tn), lambda i, j, k: (i, j))],
                  [jax.ShapeDtypeStruct((m, nn_), f32)], [pltpu.VMEM((tm, tn), f32)], [a, b],
                  ("parallel", "parallel", "arbitrary"), carry)[0]


def _div_tile(n, want, mult):
    best = None
    for t in range(mult, min(n, want) + 1, mult):
        if n % t == 0:
            best = t
    return best if best is not None else n


class _Carry:
    def __init__(self, kind, srcs):
        self.kind, self.srcs, self.n, self.result = kind, list(srcs), len(srcs), None

    def out_shapes(self):
        if self.kind == "scatter":
            return [jax.ShapeDtypeStruct(s.shape, s.dtype) for s in self.srcs]
        return [jax.ShapeDtypeStruct((N_DEV,) + tuple(s.shape), s.dtype) for s in self.srcs]

    def sems(self):
        k = N_DEV if self.kind == "scatter" else N_DEV - 1
        return [pltpu.SemaphoreType.DMA((self.n, k)), pltpu.SemaphoreType.DMA((self.n, k)), pltpu.SemaphoreType.DMA((self.n,))]

    def _sc(self, src, dst, sems, i, j, slot):
        return pltpu.make_async_remote_copy(
            src_ref=src[i].at[j], dst_ref=dst[i].at[slot], send_sem=sems[0].at[i, j], recv_sem=sems[1].at[i, slot],
            device_id=(j // 4, (j // 2) % 2, j % 2), device_id_type=pl.DeviceIdType.MESH)

    def _where(self):
        x, y, c = lax.axis_index("x"), lax.axis_index("y"), lax.axis_index("c")
        return (x, y, c), (x, y, 1 - c), [(1 - x, y), (x, 1 - y), (1 - x, 1 - y)], c

    def _gc(self, src, dst, sems, i, k, block, to, own=False):
        slot = dst[i].at[4 * block[0] + 2 * block[1] + block[2]]
        return pltpu.make_async_remote_copy(
            src_ref=src[i] if own else slot, dst_ref=slot, send_sem=sems[0].at[i, k], recv_sem=sems[1].at[i, k],
            device_id=to, device_id_type=pl.DeviceIdType.MESH)

    def _local(self, src, dst, sems, i):
        if self.kind == "scatter":
            me = 4 * lax.axis_index("x") + 2 * lax.axis_index("y") + lax.axis_index("c")
            return pltpu.make_async_copy(src[i].at[me], dst[i].at[me], sems[2].at[i])
        x, y, c = lax.axis_index("x"), lax.axis_index("y"), lax.axis_index("c")
        return pltpu.make_async_copy(src[i], dst[i].at[4 * x + 2 * y + c], sems[2].at[i])

    def _first(self, src, dst, sems):
        me, sibling, chips, c = self._where()
        out = []
        for i in range(self.n):
            out.append(self._gc(src, dst, sems, i, 0, me, sibling, own=True))
            out += [self._gc(src, dst, sems, i, 1 + j, me, (*chip, c), own=True) for j, chip in enumerate(chips)]
        return out

    def start(self, src, dst, sems):
        for i in range(self.n):
            self._local(src, dst, sems, i).start()
        if self.kind == "scatter":
            me = 4 * lax.axis_index("x") + 2 * lax.axis_index("y") + lax.axis_index("c")
            for j in range(N_DEV):
                @pl.when(j != me)
                def _(j=j):
                    for i in range(self.n):
                        self._sc(src, dst, sems, i, j, me).start()
        else:
            for cp in self._first(src, dst, sems):
                cp.start()

    def finish(self, src, dst, sems):
        if self.kind == "scatter":
            me = 4 * lax.axis_index("x") + 2 * lax.axis_index("y") + lax.axis_index("c")
            for j in range(N_DEV):
                @pl.when(j != me)
                def _(j=j):
                    for i in range(self.n):
                        self._sc(src, dst, sems, i, j, j).wait_recv()
                        self._sc(src, dst, sems, i, j, me).wait_send()
        else:
            me, sibling, chips, c = self._where()
            passed = []
            for j, chip in enumerate(chips):
                for i in range(self.n):
                    self._gc(src, dst, sems, i, 1 + j, (*chip, c), me).wait_recv()
                    fwd = self._gc(src, dst, sems, i, 4 + j, (*chip, c), sibling)
                    fwd.start()
                    passed.append(fwd)
            for i in range(self.n):
                self._gc(src, dst, sems, i, 0, sibling, me).wait_recv()
                for j, chip in enumerate(chips):
                    self._gc(src, dst, sems, i, 4 + j, (*chip, 1 - c), me).wait_recv()
            for cp in self._first(src, dst, sems) + passed:
                cp.wait_send()
        for i in range(self.n):
            self._local(src, dst, sems, i).wait()


def _pcall(body, name, grid, in_specs, out_specs, out_shape, scratch, operands, dims, carry=None):
    n_in, n_out = len(in_specs), len(out_shape)
    if carry is not None:
        n, inner = carry.n, body

        def body(*refs):
            ins, csrc = refs[:n_in], refs[n_in:n_in + n]
            outs, cdst = refs[n_in + n:n_in + n + n_out], refs[n_in + n + n_out:n_in + 2 * n + n_out]
            rest = refs[n_in + 2 * n + n_out:]
            first = last = True
            for k, size in enumerate(grid):
                first = jnp.logical_and(first, pl.program_id(k) == 0)
                last = jnp.logical_and(last, pl.program_id(k) == size - 1)

            @pl.when(first)
            def _():
                carry.start(csrc, cdst, rest[-3:])

            inner(*ins, *outs, *rest[:-3])

            @pl.when(last)
            def _():
                carry.finish(csrc, cdst, rest[-3:])

        hbm = pl.BlockSpec(memory_space=pltpu.HBM)
        in_specs, out_specs = list(in_specs) + [hbm] * n, list(out_specs) + [hbm] * n
        out_shape, scratch = list(out_shape) + carry.out_shapes(), list(scratch) + carry.sems()
        operands = list(operands) + carry.srcs
    kw = dict(grid=grid) if grid else {}
    res = pl.pallas_call(
        body, name=name, in_specs=in_specs, out_specs=out_specs, out_shape=out_shape, scratch_shapes=scratch,
        compiler_params=pltpu.CompilerParams(dimension_semantics=dims, vmem_limit_bytes=VMEM_LIMIT) if grid else None, **kw,
    )(*operands)
    res = list(res)
    if carry is not None:
        carry.result = res[n_out:]
    return res[:n_out]


def _exchange(name, srcs):
    carry = _Carry("scatter", srcs)
    _pcall(lambda *refs: None, name, (), [], [], [], [], [], None, carry)
    return carry.result


def _all_gather(name, srcs):
    carry = _Carry("gather", srcs)
    _pcall(lambda *refs: None, name, (), [], [], [], [], [], None, carry)
    return carry.result


def _ret_tables(c):
    gam = 1.0 - 2.0 ** (-5.0 - np.arange(RET_HEADS))
    lg = np.log(gam)
    t = np.arange(c)
    dmat = np.where(t[:, None] >= t[None, :], np.exp((t[:, None] - t[None, :])[None] * lg[:, None, None]), 0.0)
    head_v = np.arange(512) // 128
    ysc = np.exp((t[:, None] + 1) * lg[head_v][None, :])
    wtab = np.exp((c - 1 - t)[:, None] * lg[head_v][None, :])
    gtab = np.exp(c * lg[head_v])[None, :]
    head_k = (np.arange(256) % 128) // 32
    mask = (head_k[:, None] == head_v[None, :]).astype(np.float32)
    hm = (head_k[None, None, :] == np.arange(4)[:, None, None]).astype(np.float32)
    return [jnp.asarray(x, f32) for x in (dmat, ysc, wtab, gtab, mask, hm)]


def _rope_tables(tp):
    inv = 10000.0 ** (-np.arange(32, dtype=np.float32) / 32)
    ang = np.arange(tp, dtype=np.float32)[:, None] * inv[None, :]
    cos = np.tile(np.cos(ang), (1, 4)).astype(np.float32)
    sin = np.tile(np.sin(ang), (1, 4)).astype(np.float32)
    return jnp.asarray(cos), jnp.asarray(sin)


def _tri(c):
    t = np.arange(c)
    return jnp.asarray((t[:, None] >= t[None, :]).astype(np.float32))


def _ssd_tables(c):
    sh = np.zeros((3 * c, c + 8), np.float32)
    for d in (3, 2, 1):
        for t in range(c):
            sh[(3 - d) * c + t, 8 + t - d] = 1.0
    e = np.zeros((128, 512), np.float32)
    for h in range(SSD_HEADS):
        e[h, 64 * h:64 * h + 64] = 1.0
    mg = ((np.arange(256) // 128)[:, None] == (np.arange(512) // 256)[None, :]).astype(np.float32)
    cm4 = ((np.arange(256) // 64)[None, None, :] == np.arange(4)[:, None, None]).astype(np.float32)
    return [jnp.asarray(x) for x in (sh, e, mg, cm4)]


def _ret_step(rows, consts, nds, states, _):
    q, k, v, g, cos, sin = rows
    (gnw,) = consts
    dmat, ysc, wtab, gtab, mask, hm = nds
    (s,) = states
    q1, q2, k1, k2 = q[:, :128], q[:, 128:], k[:, :128], k[:, 128:]
    qr = jnp.concatenate([q1 * cos - q2 * sin, q1 * sin + q2 * cos], axis=1)
    kr = jnp.concatenate([k1 * cos - k2 * sin, k1 * sin + k2 * cos], axis=1) * 0.125
    y = _bdot(qr, s) * ysc
    parts = []
    for h in range(RET_HEADS):
        a = _bdot_nt(qr * hm[h], kr) * dmat[h]
        parts.append(_bdot(a, v[:, 128 * h:128 * h + 128]))
    y = y + jnp.concatenate(parts, axis=1)
    s_new = s * gtab + _bdot_tn(kr, v * wtab) * mask
    outs = []
    for h in range(RET_HEADS):
        yh = y[:, 128 * h:128 * h + 128]
        d = yh - jnp.mean(yh, axis=-1, keepdims=True)
        outs.append(d * lax.rsqrt(jnp.mean(d * d, axis=-1, keepdims=True) + EPS))
    yn = jnp.concatenate(outs, axis=1) * gnw
    return (g * _sig(g) * yn,), (s_new,)


def _ssd_step(rows, consts, nds, states, _):
    z, xbc, dt128 = rows
    cw, cb, dtb, alog, dsk, nw = consts
    tri, sh, e, mg, cm4 = nds
    s, tail = states
    c = xbc.shape[0]
    shifted = _hdot(sh, jnp.concatenate([tail, xbc], axis=0))
    xc = (cw[0:1] * shifted[0:c] + cw[1:2] * shifted[c:2 * c] + cw[2:3] * shifted[2 * c:3 * c] + cw[3:4] * xbc + cb)
    xc = xc * _sig(xc)
    xs, bm, cm = xc[:, :512], xc[:, 512:768], xc[:, 768:]
    dt = _softplus(dt128 + dtb)
    la = dt * (-jnp.exp(alog))
    cum = _hdot(tri, la)
    cum_t = cum.T
    dtx, cumx = _hdot(dt, e), _hdot(cum, e)
    lastx = cumx[c - 1:c]
    dskx = _hdot(jnp.broadcast_to(dsk, (8, 128)), e)[0:1]
    v = xs * dtx
    y = _bdot(cm, s) * jnp.exp(cumx)
    lane = lax.broadcasted_iota(jnp.int32, cum.shape, 1)
    sub = lax.broadcasted_iota(jnp.int32, cum_t.shape, 0)
    parts = []
    for grp in range(SSD_GROUPS):
        sg = _bdot_nt(cm[:, 128 * grp:128 * grp + 128], bm[:, 128 * grp:128 * grp + 128])
        vg = v[:, 256 * grp:256 * grp + 256]
        acc = jnp.zeros((c, 256), f32)
        for hh in range(4):
            h = 4 * grp + hh
            col = jnp.sum(jnp.where(lane == h, cum, 0.0), axis=1, keepdims=True)
            row = jnp.sum(jnp.where(sub == h, cum_t, 0.0), axis=0, keepdims=True)
            dec = jnp.exp(jnp.where(tri > 0.5, col - row, -1e30))
            acc = acc + _bdot(sg * dec, vg * cm4[hh])
        parts.append(acc)
    y = y + jnp.concatenate(parts, axis=1) + dskx * xs
    s_new = s * jnp.exp(lastx) + _bdot_tn(bm, v * jnp.exp(lastx - cumx)) * mg
    y = y * (z * _sig(z))
    outs = []
    for grp in range(SSD_GROUPS):
        yg = y[:, 256 * grp:256 * grp + 256]
        outs.append(yg * lax.rsqrt(jnp.mean(yg * yg, axis=-1, keepdims=True) + EPS))
    return (jnp.concatenate(outs, axis=1) * nw,), (s_new, xbc[c - 8:c])


def _hg_step(rows, consts, nds, states, _):
    hq, hf, hi, hgate = rows
    lb, nw = consts
    (tri,) = nds
    (st,) = states
    c = hq.shape[0]
    q = hq * _sig(hq)
    f = lb + (1.0 - lb) * _sig(hf)
    lf = jnp.log(f)
    k = 1.0 - f
    v = hi
    cum = _hdot(tri, lf)
    last = jnp.sum(lf, axis=0, keepdims=True)
    qd = q * jnp.exp(cum)
    kw = k * jnp.exp(last - cum)
    heads = [slice(128 * h, 128 * h + 128) for h in range(HG_HEADS)]
    y_rows = []
    rowid = lax.broadcasted_iota(jnp.int32, (HG_SUB, 512), 0)
    for i in range(c // HG_SUB):
        r0 = HG_SUB * i
        qi, ci, ki, vi = q[r0:r0 + HG_SUB], cum[r0:r0 + HG_SUB], k[r0:r0 + HG_SUB], v[r0:r0 + HG_SUB]
        yi = [jnp.zeros((HG_SUB, 128), f32) for _ in heads]
        for s_ in range(HG_SUB):
            e = qi * ki[s_:s_ + 1] * jnp.exp(jnp.where(rowid >= s_, ci - ci[s_:s_ + 1], -1e30))
            for h, sl in enumerate(heads):
                yi[h] = yi[h] + jnp.sum(e[:, sl], axis=1, keepdims=True) * vi[s_:s_ + 1, sl]
        if i > 0:
            b = cum[r0 - 1:r0]
            qs = qi * jnp.exp(ci - b)
            ks = k[:r0] * jnp.exp(b - cum[:r0])
            for h, sl in enumerate(heads):
                yi[h] = yi[h] + _bdot(_bdot_nt(qs[:, sl], ks[:, sl]), v[:r0, sl])
        y_rows.append(jnp.concatenate(yi, axis=1))
    y_in = jnp.concatenate(y_rows, axis=0)
    y = y_in + jnp.concatenate([_bdot_nt(qd[:, sl], st[:, sl]) for sl in heads], axis=1)
    st_new = jnp.concatenate([st[:, sl] * jnp.exp(last[:, sl]) + _bdot_tn(v[:, sl], kw[:, sl]) for sl in heads], axis=1)
    outs = []
    for sl in heads:
        yh = y[:, sl]
        outs.append(yh * lax.rsqrt(jnp.mean(yh * yh, axis=-1, keepdims=True) + EPS))
    o = jnp.concatenate(outs, axis=1) * nw
    return (o * (hgate * _sig(hgate)),), (st_new,)


def _s5post_step(rows, consts, nds, states, _):
    ycore, u = rows
    dsk, gw, gb = consts
    y = jax.nn.gelu(ycore + dsk * u)
    zz = _bdot(y, gw) + gb
    return (zz[:, :512] * _sig(zz[:, 512:]),), ()


def _s5prep_step(rows, consts, nds, states, _):
    lr, li, lstep, btr, bti = rows
    (rep,) = nds
    step = jnp.exp(lstep)
    mag = jnp.exp(lr * step)
    ab_re, ab_im = mag * jnp.cos(li * step), mag * jnp.sin(li * step)
    inv = 1.0 / (lr * lr + li * li)
    co_re = ((ab_re - 1.0) * lr + ab_im * li) * inv
    co_im = (ab_im * lr - (ab_re - 1.0) * li) * inv
    cre, cim = _hdot(rep, co_re), _hdot(rep, co_im)
    return (ab_re, ab_im, cre * btr - cim * bti, cre * bti + cim * btr), ()


def _lb_step(rows, consts, nds, states, _):
    (x,) = rows
    (lmat,) = nds
    valid = lax.broadcasted_iota(jnp.int32, x.shape, 0) < DEPTH
    xm = jnp.where(valid, x, -1e30)
    ex = jnp.where(valid, jnp.exp(xm - jnp.max(xm, axis=0, keepdims=True)), 0.0)
    sm = ex / jnp.sum(ex, axis=0, keepdims=True)
    return (_hdot(lmat, sm),), ()


def _rmsn_step(rows, consts, nds, states, _):
    (h,) = rows
    (w,) = consts
    return (_rms(h, w),), ()


def _merge_step(rows, consts, nds, states, _):
    yr, ys5, yssd, yhg, gates, h = rows
    wb, wout, npost = consts
    mixed = jnp.zeros(h.shape, f32)
    for n, yb in enumerate((yr, ys5, yssd, yhg)):
        mixed = mixed + _sig(gates[:, 1024 * n:1024 * n + 1024]) * _bdot(yb, wb[n])
    return (h + _rms(_bdot(mixed, wout), npost),), ()


FF_BLK = 512


def _row_blocks(tp):
    rb = _div_tile(tp, 528, 16)
    return rb, tp // rb


def _mlp_core_fwd(tag, u, w_up, w_down, carry=None):
    tp = u.shape[0]
    rb, nrb = _row_blocks(tp)
    nff = D_FF // FF_BLK

    def body(u_ref, wup_ref, wdown_ref, m_ref):
        j = pl.program_id(0)

        def rows(i, carry):
            r = pl.multiple_of(i * rb, 16)
            a = jnp.dot(u_ref[pl.ds(r, rb), :], wup_ref[...], preferred_element_type=f32)
            part = _bdot(jnp.square(jnp.maximum(a, 0.0)), wdown_ref[...])

            @pl.when(j == 0)
            def _():
                m_ref[pl.ds(r, rb), :] = part

            @pl.when(j > 0)
            def _():
                m_ref[pl.ds(r, rb), :] += part
            return carry
        lax.fori_loop(0, nrb, rows, 0)

    return _pcall(
        body, "mlp_core_fwd" + tag, (nff,),
        [pl.BlockSpec((tp, D_MODEL), lambda j: (0, 0)), pl.BlockSpec((D_MODEL, FF_BLK), lambda j: (0, j)),
         pl.BlockSpec((FF_BLK, D_MODEL), lambda j: (j, 0))],
        [pl.BlockSpec((tp, D_MODEL), lambda j: (0, 0))], [jax.ShapeDtypeStruct((tp, D_MODEL), f32)], [], [u, w_up, w_down],
        ("arbitrary",), carry)[0]


def _mlp_core_bwd(tag, u, dm, w_up, w_down, carry=None):
    tp = u.shape[0]
    rb, nrb = _row_blocks(tp)
    nff = D_FF // FF_BLK

    def body(u_hbm, dm_hbm, wup_ref, wdown_ref, du_hbm, dwup_ref, dwdown_ref, u_s, dm_s, du_s):
        j = pl.program_id(0)

        @pl.when(j == 0)
        def _():
            pltpu.sync_copy(u_hbm, u_s)
            pltpu.sync_copy(dm_hbm, dm_s)

        def rows(i, carry):
            r = pl.multiple_of(i * rb, 16)
            ub = u_s[pl.ds(r, rb), :]
            dmb = dm_s[pl.ds(r, rb), :].astype(bf16)
            a = jnp.dot(ub, wup_ref[...], preferred_element_type=f32)
            ra = jnp.maximum(a, 0.0)
            da = (_bdot_nt(dmb, wdown_ref[...]) * (2.0 * ra)).astype(bf16)
            dwd = _bdot_tn(ra * ra, dmb)
            dwu = _bdot_tn(ub, da)
            dub = _bdot_nt(da, wup_ref[...])

            @pl.when(i == 0)
            def _():
                dwdown_ref[...] = dwd
                dwup_ref[...] = dwu

            @pl.when(i > 0)
            def _():
                dwdown_ref[...] += dwd
                dwup_ref[...] += dwu

            @pl.when(j == 0)
            def _():
                du_s[pl.ds(r, rb), :] = dub

            @pl.when(j > 0)
            def _():
                du_s[pl.ds(r, rb), :] += dub
            return carry
        lax.fori_loop(0, nrb, rows, 0)

        @pl.when(j == nff - 1)
        def _():
            pltpu.sync_copy(du_s, du_hbm)

    anyspec = pl.BlockSpec(memory_space=pl.ANY)
    return _pcall(
        body, "mlp_core_bwd" + tag, (nff,),
        [anyspec, anyspec, pl.BlockSpec((D_MODEL, FF_BLK), lambda j: (0, j)), pl.BlockSpec((FF_BLK, D_MODEL), lambda j: (j, 0))],
        [anyspec, pl.BlockSpec((D_MODEL, FF_BLK), lambda j: (0, j)), pl.BlockSpec((FF_BLK, D_MODEL), lambda j: (j, 0))],
        [jax.ShapeDtypeStruct((tp, D_MODEL), f32), jax.ShapeDtypeStruct((D_MODEL, D_FF), f32), jax.ShapeDtypeStruct((D_FF, D_MODEL), f32)],
        [pltpu.VMEM((tp, D_MODEL), bf16), pltpu.VMEM((tp, D_MODEL), f32), pltpu.VMEM((tp, D_MODEL), f32)],
        [u, dm, w_up, w_down], ("arbitrary",), carry)


def _resid_rms_step(rows, consts, nds, states, _):
    m, h = rows
    (w,) = consts
    return (h + _rms(m, w),), ()


SCAN_RB = 16


def _scan_inplace(xr, xi, ar0, ai0, tp, reverse, pr, pi):
    rb, cs = SCAN_RB, 64
    nb = cs // rb
    w = xr.shape[1]
    nch = tp // cs
    rowid = lax.broadcasted_iota(jnp.int32, (rb, w), 0)

    def mac(cr, ci, sr, si, ar, ai):
        return cr + ar * sr - ai * si, ci + ar * si + ai * sr

    ar, ai = ar0, ai0
    d = 1
    while d < cs:
        arb, aib = jnp.broadcast_to(ar, (rb, w)), jnp.broadcast_to(ai, (rb, w))
        edge = 0 if not reverse else nb - 1

        def level(c, carry, d=d, arb=arb, aib=aib):
            base = pl.multiple_of(c * cs, cs)
            order = range(nb - 1, -1, -1) if not reverse else range(nb)
            for b in order:
                r0 = base + rb * b
                lo = b - d // rb if not reverse else b + d // rb
                if d >= rb:
                    if lo < 0 or lo >= nb:
                        continue
                    src = base + rb * lo
                    sr, si = xr[pl.ds(src, rb), :], xi[pl.ds(src, rb), :]
                elif b == edge:
                    if not reverse:
                        sr = jnp.where(rowid >= d, pltpu.roll(xr[pl.ds(r0, rb), :], d, 0), 0.0)
                        si = jnp.where(rowid >= d, pltpu.roll(xi[pl.ds(r0, rb), :], d, 0), 0.0)
                    else:
                        sr = jnp.where(rowid < rb - d, pltpu.roll(xr[pl.ds(r0, rb), :], rb - d, 0), 0.0)
                        si = jnp.where(rowid < rb - d, pltpu.roll(xi[pl.ds(r0, rb), :], rb - d, 0), 0.0)
                elif not reverse:
                    sr = pltpu.roll(xr[pl.ds(r0 - 8, rb + 8), :], d, 0)[8:, :]
                    si = pltpu.roll(xi[pl.ds(r0 - 8, rb + 8), :], d, 0)[8:, :]
                else:
                    sr = pltpu.roll(xr[pl.ds(r0, rb + 8), :], rb + 8 - d, 0)[:rb, :]
                    si = pltpu.roll(xi[pl.ds(r0, rb + 8), :], rb + 8 - d, 0)[:rb, :]
                nr, ni = mac(xr[pl.ds(r0, rb), :], xi[pl.ds(r0, rb), :], sr, si, arb, aib)
                xr[pl.ds(r0, rb), :] = nr
                xi[pl.ds(r0, rb), :] = ni
            return carry
        lax.fori_loop(0, nch, level, 0)
        ar, ai = ar * ar - ai * ai, 2.0 * ar * ai
        d *= 2

    krow = lax.broadcasted_iota(jnp.int32, (cs, w), 0)
    expo = krow + 1 if not reverse else cs - krow
    tr, ti = jnp.ones((cs, w), f32), jnp.zeros((cs, w), f32)
    qr, qi = ar0, ai0
    for j in range(7):
        on = ((expo >> j) & 1) == 1
        fr, fi = jnp.where(on, qr, 1.0), jnp.where(on, qi, 0.0)
        tr, ti = tr * fr - ti * fi, tr * fi + ti * fr
        qr, qi = qr * qr - qi * qi, 2.0 * qr * qi
    pr[...] = tr
    pi[...] = ti

    def across(i, carry):
        c = i if not reverse else nch - 1 - i
        base = pl.multiple_of(c * cs, cs)
        cr, ci = carry
        out = carry
        for b in range(nb):
            r0 = base + rb * b
            nr, ni = mac(xr[pl.ds(r0, rb), :], xi[pl.ds(r0, rb), :], pr[rb * b:rb * b + rb, :], pi[rb * b:rb * b + rb, :], cr, ci)
            xr[pl.ds(r0, rb), :] = nr
            xi[pl.ds(r0, rb), :] = ni
            if not reverse and b == nb - 1:
                out = (nr[rb - 1:rb, :], ni[rb - 1:rb, :])
            if reverse and b == 0:
                out = (nr[0:1, :], ni[0:1, :])
        return out
    lax.fori_loop(0, nch, across, (jnp.zeros((1, w), f32), jnp.zeros((1, w), f32)))


def _s5_blocks(tp):
    rbm = tp // 8 if (tp // 8) % 8 == 0 and tp % 8 == 0 else tp
    return rbm, tp // rbm


def _s5_in_specs(proj, tp):
    ucol = OFF["s5u"] // 128
    return [
        pl.BlockSpec((tp, 128), lambda cb: (0, ucol + cb)),
        pl.BlockSpec((128, 512), lambda cb: (cb, 0)),
        pl.BlockSpec((128, 512), lambda cb: (cb, 0)),
        pl.BlockSpec((1, 512), lambda cb: (0, cb)),
        pl.BlockSpec((1, 512), lambda cb: (0, cb)),
        pl.BlockSpec((512, 128), lambda cb: (cb, 0)),
        pl.BlockSpec((512, 128), lambda cb: (cb, 0)),
    ]


def _s5_core_fwd(tag, proj, bbr, bbi, ar, ai, ccr, cci, carry=None):
    tp = proj.shape[0]
    rbm, nb = _s5_blocks(tp)

    def body(u_ref, bbr_ref, bbi_ref, ar_ref, ai_ref, ccr_ref, cci_ref, y_ref, xr, xi, pr, pi):
        def fill(i, carry):
            r = pl.multiple_of(i * rbm, 8)
            ub = u_ref[pl.ds(r, rbm), :]
            xr[pl.ds(r, rbm), :] = _bdot(ub, bbr_ref[...])
            xi[pl.ds(r, rbm), :] = _bdot(ub, bbi_ref[...])
            return carry
        lax.fori_loop(0, nb, fill, 0)
        _scan_inplace(xr, xi, ar_ref[...], ai_ref[...], tp, False, pr, pi)

        def out(i, carry):
            r = pl.multiple_of(i * rbm, 8)
            y_ref[pl.ds(r, rbm), :] = (_bdot(xr[pl.ds(r, rbm), :], ccr_ref[...]) - _bdot(xi[pl.ds(r, rbm), :], cci_ref[...]))
            return carry
        lax.fori_loop(0, nb, out, 0)

    return _pcall(body, "s5_core_fwd" + tag, (4,), _s5_in_specs(proj, tp), [pl.BlockSpec((tp, 128), lambda cb: (0, cb))],
                  [jax.ShapeDtypeStruct((tp, 512), f32)], [pltpu.VMEM((tp, 512), f32)] * 2 + [pltpu.VMEM((64, 512), f32)] * 2,
                  [proj, bbr, bbi, ar, ai, ccr, cci], ("arbitrary",), carry)[0]


def _s5_core_bwd(tag, proj, bbr, bbi, ar, ai, ccr, cci, dy, du_post, carry=None):
    tp = proj.shape[0]
    rbm, nb = _s5_blocks(tp)
    rb = SCAN_RB

    def body(u_ref, bbr_ref, bbi_ref, ar_ref, ai_ref, ccr_ref, cci_ref, dy_ref, dup_ref,
             du_ref, dbbr_ref, dbbi_ref, dar_ref, dai_ref, dccr_ref, dcci_ref, xr, xi, gr, gi, pr, pi):
        dccr_ref[...] = jnp.zeros_like(dccr_ref)
        dcci_ref[...] = jnp.zeros_like(dcci_ref)
        dbbr_ref[...] = jnp.zeros_like(dbbr_ref)
        dbbi_ref[...] = jnp.zeros_like(dbbi_ref)

        def fill(i, carry):
            r = pl.multiple_of(i * rbm, 8)
            ub = u_ref[pl.ds(r, rbm), :]
            xr[pl.ds(r, rbm), :] = _bdot(ub, bbr_ref[...])
            xi[pl.ds(r, rbm), :] = _bdot(ub, bbi_ref[...])
            return carry
        lax.fori_loop(0, nb, fill, 0)
        _scan_inplace(xr, xi, ar_ref[...], ai_ref[...], tp, False, pr, pi)

        def seed(i, carry):
            r = pl.multiple_of(i * rbm, 8)
            dyb = dy_ref[pl.ds(r, rbm), :]
            dccr_ref[...] += _bdot_tn(xr[pl.ds(r, rbm), :], dyb)
            dcci_ref[...] -= _bdot_tn(xi[pl.ds(r, rbm), :], dyb)
            gr[pl.ds(r, rbm), :] = _bdot_nt(dyb, ccr_ref[...])
            gi[pl.ds(r, rbm), :] = -_bdot_nt(dyb, cci_ref[...])
            return carry
        lax.fori_loop(0, nb, seed, 0)
        _scan_inplace(gr, gi, ar_ref[...], -ai_ref[...], tp, True, pr, pi)

        w = xr.shape[1]
        rowid = lax.broadcasted_iota(jnp.int32, (rb, w), 0)

        def prods(pr, pi, r0):
            g_r, g_i = gr[pl.ds(r0, rb), :], gi[pl.ds(r0, rb), :]
            return pr * g_r + pi * g_i, pr * g_i - pi * g_r

        def dacc(i, carry):
            r0 = pl.multiple_of((i + 1) * rb, 8)
            lo = pl.multiple_of(r0 - 8, 8)
            pr = pltpu.roll(xr[pl.ds(lo, rb + 8), :], 1, 0)[8:, :]
            pi = pltpu.roll(xi[pl.ds(lo, rb + 8), :], 1, 0)[8:, :]
            a, b = prods(pr, pi, r0)
            return carry[0] + a, carry[1] + b
        pr0 = jnp.where(rowid >= 1, pltpu.roll(xr[pl.ds(0, rb), :], 1, 0), 0.0)
        pi0 = jnp.where(rowid >= 1, pltpu.roll(xi[pl.ds(0, rb), :], 1, 0), 0.0)
        acc_r, acc_i = lax.fori_loop(0, tp // rb - 1, dacc, prods(pr0, pi0, 0))
        dar_ref[...] = jnp.sum(acc_r, axis=0, keepdims=True)
        dai_ref[...] = jnp.sum(acc_i, axis=0, keepdims=True)

        def tail(i, carry):
            r = pl.multiple_of(i * rbm, 8)
            g_r, g_i = gr[pl.ds(r, rbm), :], gi[pl.ds(r, rbm), :]
            ub = u_ref[pl.ds(r, rbm), :]
            du_ref[pl.ds(r, rbm), :] = _bdot_nt(g_r, bbr_ref[...]) + _bdot_nt(g_i, bbi_ref[...]) + dup_ref[pl.ds(r, rbm), :]
            dbbr_ref[...] += _bdot_tn(ub, g_r)
            dbbi_ref[...] += _bdot_tn(ub, g_i)
            return carry
        lax.fori_loop(0, nb, tail, 0)

    col = lambda cb: (0, cb)
    blk = lambda cb: (cb, 0)
    return _pcall(
        body, "s5_core_bwd" + tag, (4,),
        _s5_in_specs(proj, tp) + [pl.BlockSpec((tp, 128), col), pl.BlockSpec((tp, 128), col)],
        [pl.BlockSpec((tp, 128), col), pl.BlockSpec((128, 512), blk), pl.BlockSpec((128, 512), blk),
         pl.BlockSpec((1, 512), col), pl.BlockSpec((1, 512), col), pl.BlockSpec((512, 128), blk), pl.BlockSpec((512, 128), blk)],
        [jax.ShapeDtypeStruct((tp, 512), f32), jax.ShapeDtypeStruct((512, 512), f32), jax.ShapeDtypeStruct((512, 512), f32),
         jax.ShapeDtypeStruct((1, 2048), f32), jax.ShapeDtypeStruct((1, 2048), f32), jax.ShapeDtypeStruct((2048, 128), f32),
         jax.ShapeDtypeStruct((2048, 128), f32)],
        [pltpu.VMEM((tp, 512), f32)] * 4 + [pltpu.VMEM((64, 512), f32)] * 2, [proj, bbr, bbi, ar, ai, ccr, cci, dy, du_post],
        ("arbitrary",), carry)


_EYE8 = np.eye(8, dtype=np.float32)


def _bb_dense(bb):
    return jnp.einsum("cgjp,gh->cgjhp", bb.reshape(4, 8, 16, 64), _EYE8).reshape(512, 512)


def _bb_diag(d):
    return jnp.einsum("cgjhp,gh->cgjp", d.reshape(4, 8, 16, 8, 64), _EYE8).reshape(512, 64)


def _cc_dense(cmat):
    return jnp.einsum("cgjp,gh->cgphj", cmat.reshape(4, 8, 16, 64), _EYE8).reshape(2048, 128)


def _cc_diag(d):
    return jnp.einsum("cgphj,gh->cgjp", d.reshape(4, 8, 64, 8, 16), _EYE8).reshape(32, 16, 64)


def _tables(tp):
    cos, sin = _rope_tables(tp)
    rep = np.zeros((512, 32), np.float32)
    rep[np.arange(512), np.arange(512) // 16] = 1.0
    lmat = np.zeros((8, 8), np.float32)
    for l in range(DEPTH):
        lmat[l, 1:l + 1] = 1.0
    return dict(cos=cos, sin=sin, ret=_ret_tables(_ret_chunk(tp)), tri=_tri(CHUNK), ssd=_ssd_tables(CHUNK),
                rep=jnp.asarray(rep), lmat=jnp.asarray(lmat))


RET_CHUNK = 192


def _ret_chunk(tp):
    return RET_CHUNK if tp % RET_CHUNK == 0 else CHUNK


def _tiles(tp):
    return dict(tr=_div_tile(tp, 352, 16), tmg=_div_tile(tp, 192, 16))


def _mixer_rows(proj, c):
    ret = [_r(proj, c, 256, OFF["rq"] // 256), _r(proj, c, 256, OFF["rk"] // 256), _r(proj, c, 512, OFF["rv"] // 512),
           _r(proj, c, 512, OFF["rg"] // 512)]
    ssd = [_r(proj, c, 512, OFF["sz"] // 512), _r(proj, c, 1024, OFF["sxbc"] // 1024), _r(proj, c, 128, OFF["dt"] // 128)]
    hg = [_r(proj, c, 512, OFF[k] // 512) for k in ("hq", "hf", "hi", "hg")]
    return ret, ssd, hg


def _out2(rows, w, rb, dtype=f32):
    return (rows, w), dtype, (rb, w), lambda n: (n, 0)


def _s5_prep_rows(p):
    return [_full(p["lam_re"]), _full(p["lam_im"]), _full(p["lstep"]), _full(p["bt_re"]), _full(p["bt_im"])]


def _s5_consts(p, tabs, tag):
    whole = lambda s: (s, f32, s, lambda n: (0, 0))
    ab_re, ab_im, bb_re, bb_im = _seq_fwd("s5_prep" + tag, _s5prep_step, _s5_prep_rows(p), [], [tabs["rep"]], [],
                                          [whole((32, 64)), whole((32, 64)), whole((512, 64)), whole((512, 64))], 1)
    return (_bb_dense(bb_re).astype(bf16), _bb_dense(bb_im).astype(bf16), ab_re.reshape(1, 2048), ab_im.reshape(1, 2048),
            _cc_dense(p["c_re"]).astype(bf16), _cc_dense(p["c_im"]).astype(bf16))


def _rmsn_step_b(rows, consts, nds, states, n):
    (o,), _ = _rmsn_step(rows, consts, nds, states, n)
    return (o, rows[0]), ()


def _layer_fwd(h, p, tabs, tag, carry):
    tp = h.shape[0]
    c = CHUNK
    nch = tp // c
    tl = _tiles(tp)
    tr, tmg = tl["tr"], tl["tmg"]
    (u,) = _seq_fwd("rms_premix" + tag, _rmsn_step, [_r(h, tr, 1024)], [p["npm"]], [], [], [_out2(tp, 1024, tr, bf16)], tp // tr)
    proj = _mm("in_proj" + tag, u, p["w_in"], "nn", _div_tile(tp, 1056, 16), 1408, 1024, carry=carry.get("in"))
    if "in_done" in carry:
        p = dict(p, **carry["in_done"]())
    ret_rows, ssd_rows, hg_rows = _mixer_rows(proj, c)
    cr = _ret_chunk(tp)
    y_ret, ret_s = _seq_fwd("ret_fwd" + tag, _ret_step,
                            _mixer_rows(proj, cr)[0] + [_r(tabs["cos"], cr, 128), _r(tabs["sin"], cr, 128)],
                            [p["ret_gn"]], tabs["ret"], [(256, 512)], [_out2(tp, 512, cr)], tp // cr, save_states=True,
                            carry=carry.get("ret"))
    ycore = _s5_core_fwd(tag, proj, *_s5_consts(p, tabs, tag), carry=carry.get("s5"))
    (y_s5,) = _seq_fwd("s5_post" + tag, _s5post_step, [_r(ycore, tr, 512), _r(proj, tr, 512, OFF["s5u"] // 512)],
                       [p["s5_d"], p["glu_w"], p["glu_b"]], [], [], [_out2(tp, 512, tr)], tp // tr)
    y_ssd, ssd_s, ssd_tail = _seq_fwd(
        "ssd_fwd" + tag, _ssd_step, ssd_rows, [p["conv_w"], p["conv_b"], p["dt_bias"], p["a_log"], p["ssd_d"], p["ssd_nw"]],
        [tabs["tri"]] + tabs["ssd"], [(256, 512), (8, 1024)], [_out2(tp, 512, c)], nch, save_states=True, carry=carry.get("ssd"))
    y_hg, hg_s = _seq_fwd("hg_fwd" + tag, _hg_step, hg_rows, [p["lb"], p["hg_nw"]], [tabs["tri"]], [(128, 512)],
                          [_out2(tp, 512, c)], nch, save_states=True, carry=carry.get("hg"))
    (h_mid,) = _seq_fwd(
        "merge_fwd" + tag, _merge_step,
        [_r(y_ret, tmg, 512), _r(y_s5, tmg, 512), _r(y_ssd, tmg, 512), _r(y_hg, tmg, 512), _r(proj, tmg, 4096, 0),
         _r(h, tmg, 1024)],
        [p["w_branch"], p["w_out"], p["npostmix"]], [], [], [_out2(tp, 1024, tmg)], tp // tmg, carry=carry.get("mg"))
    if "mg_done" in carry:
        p = dict(p, **carry["mg_done"]())
    (u2,) = _seq_fwd("rms_premlp" + tag, _rmsn_step, [_r(h_mid, tr, 1024)], [p["npremlp"]], [], [],
                     [_out2(tp, 1024, tr, bf16)], tp // tr)
    m = _mlp_core_fwd(tag, u2, p["w_up"], p["w_down"], carry=carry.get("mlp"))
    (h_new,) = _seq_fwd("mlp_post" + tag, _resid_rms_step, [_r(m, tr, 1024), _r(h_mid, tr, 1024)], [p["npostmlp"]], [], [],
                        [_out2(tp, 1024, tr)], tp // tr)
    saved = dict(h=h, u=u, proj=proj, ret_s=ret_s, ycore=ycore, ssd_s=ssd_s, ssd_tail=ssd_tail, hg_s=hg_s,
                 y_ret=y_ret, y_s5=y_s5, y_ssd=y_ssd, y_hg=y_hg, h_mid=h_mid, u2=u2, m=m)
    return h_new, saved, p


def _layer_bwd(dh, p, sv, tabs, tag, carry):
    tp = dh.shape[0]
    c = CHUNK
    nch = tp // c
    tl = _tiles(tp)
    tr, tmg = tl["tr"], tl["tmg"]
    proj = sv["proj"]
    g = {}
    (d_m, d_hmid), (g["npostmlp"],) = _seq_bwd(
        "mlp_post_bwd" + tag, _resid_rms_step, [_r(sv["m"], tr, 1024), _r(sv["h_mid"], tr, 1024)], [True, True],
        [p["npostmlp"]], [], [], [_r(dh, tr, 1024)], tp // tr)
    d_u2, g["w_up"], g["w_down"] = _mlp_core_bwd(tag, sv["u2"], d_m, p["w_up"], p["w_down"], carry=carry.get("mlp"))
    (d_hmid,), (g["npremlp"],) = _seq_bwd(
        "rms_premlp_bwd" + tag, _rmsn_step_b, [_r(sv["h_mid"], tr, 1024)], [True], [p["npremlp"]], [], [],
        [_r(d_u2, tr, 1024), _r(d_hmid, tr, 1024)], tp // tr)
    (dy_ret, dy_s5, dy_ssd, dy_hg, d_gates, d_h1), (g["w_branch"], g["w_out"], g["npostmix"]) = _seq_bwd(
        "merge_bwd" + tag, _merge_step,
        [_r(sv["y_ret"], tmg, 512), _r(sv["y_s5"], tmg, 512), _r(sv["y_ssd"], tmg, 512), _r(sv["y_hg"], tmg, 512),
         _r(proj, tmg, 4096, 0), _r(sv["h"], tmg, 1024)], [True] * 6,
        [p["w_branch"], p["w_out"], p["npostmix"]], [], [], [_r(d_hmid, tmg, 1024)], tp // tmg)
    ret_rows, ssd_rows, hg_rows = _mixer_rows(proj, c)
    (d_hq, d_hf, d_hi, d_hg), (g["lb"], g["hg_nw"]) = _seq_bwd(
        "hg_bwd" + tag, _hg_step, hg_rows, [True] * 4, [p["lb"], p["hg_nw"]], [tabs["tri"]], [sv["hg_s"]],
        [_r(dy_hg, c, 512)], nch, carry=carry.get("hg"))
    (d_z, d_xbc, d_dt), (g["conv_w"], g["conv_b"], g["dt_bias"], g["a_log"], g["ssd_d"], g["ssd_nw"]) = _seq_bwd(
        "ssd_bwd" + tag, _ssd_step, ssd_rows, [True] * 3,
        [p["conv_w"], p["conv_b"], p["dt_bias"], p["a_log"], p["ssd_d"], p["ssd_nw"]], [tabs["tri"]] + tabs["ssd"],
        [sv["ssd_s"], sv["ssd_tail"]], [_r(dy_ssd, c, 512)], nch, carry=carry.get("ssd"))
    (d_ycore, du_post), (g["s5_d"], g["glu_w"], g["glu_b"]) = _seq_bwd(
        "s5_post_bwd" + tag, _s5post_step, [_r(sv["ycore"], tr, 512), _r(proj, tr, 512, OFF["s5u"] // 512)], [True, True],
        [p["s5_d"], p["glu_w"], p["glu_b"]], [], [], [_r(dy_s5, tr, 512)], tp // tr)
    du_s5, dbbr, dbbi, dar, dai, dccr, dcci = _s5_core_bwd(tag, proj, *_s5_consts(p, tabs, tag + "b"), d_ycore, du_post,
                                                           carry=carry.get("s5"))
    g["c_re"], g["c_im"] = _cc_diag(dccr), _cc_diag(dcci)
    (g["lam_re"], g["lam_im"], g["lstep"], g["bt_re"], g["bt_im"]), _ = _seq_bwd(
        "s5_prep_bwd" + tag, _s5prep_step, _s5_prep_rows(p), [True] * 5, [], [tabs["rep"]], [],
        [_full(dar.reshape(32, 64)), _full(dai.reshape(32, 64)), _full(_bb_diag(dbbr)), _full(_bb_diag(dbbi))], 1)
    if "late" in carry:
        carry = dict(carry, **carry["late"](g))
    (d_q, d_k, d_v, d_g), (g["ret_gn"],) = _seq_bwd(
        "ret_bwd" + tag, _ret_step,
        _mixer_rows(proj, _ret_chunk(tp))[0] + [_r(tabs["cos"], _ret_chunk(tp), 128), _r(tabs["sin"], _ret_chunk(tp), 128)],
        [True] * 4 + [False] * 2, [p["ret_gn"]], tabs["ret"], [sv["ret_s"]], [_r(dy_ret, _ret_chunk(tp), 512)],
        tp // _ret_chunk(tp), carry=carry.get("ret"))
    dproj = jnp.concatenate([d_gates, d_q, d_k, d_v, d_g, du_s5, d_xbc, d_z, d_hq, d_hf, d_hi, d_hg, d_dt], axis=1).astype(bf16)
    g["w_in"] = _mm("in_proj_dw" + tag, sv["u"], dproj, "tn", 1024, 1408, _div_tile(tp, 704, 16), carry=carry.get("dw"))
    du = _mm("in_proj_dx" + tag, dproj, p["w_in"], "nt", _div_tile(tp, 1056, 16), 1024, 1408, carry=carry.get("dx"))
    (dh_prev,), (g["npm"],) = _seq_bwd("rms_premix_bwd" + tag, _rmsn_step_b, [_r(sv["h"], tr, 1024)], [True], [p["npm"]], [], [],
                                       [_r(du, tr, 1024), _r(d_h1, tr, 1024)], tp // tr)
    return dh_prev, g


def _loss_call(h, tgt, lo, hi):
    tp = h.shape[0]
    tr = _div_tile(tp, 352, 16)

    def step(rows, consts, nds, states, n):
        hh, tt = rows
        row = n * tr + lax.broadcasted_iota(jnp.int32, hh.shape, 0)
        err = jnp.where((row >= lo) & (row < hi), hh - tt, 0.0)
        part = 0.5 * jnp.sum(err * err) * (1.0 / D_MODEL)
        return (err * (1.0 / D_MODEL), jnp.zeros((8, 128), f32) + part), ()

    dh, parts = _seq_fwd("loss_head", step, [_r(h, tr, 1024), _r(tgt, tr, 1024)], [], [], [],
                         [_out2(tp, 1024, tr), ((8 * (tp // tr), 128), f32, (8, 128), lambda n: (n, 0))], tp // tr)
    return dh, jnp.sum(parts[::8, 0])


def _adam_step(rows, consts, nds, states, _):
    g8, w, m, v = rows
    g = g8[0].astype(f32)
    for d in range(1, N_DEV):
        g = g + g8[d].astype(f32)
    m2 = ADAM_B1 * m + (1.0 - ADAM_B1) * g
    v2 = ADAM_B2 * v + (1.0 - ADAM_B2) * jnp.square(g)
    m_hat = m2 / (1.0 - ADAM_B1 ** ADAM_STEP)
    v_hat = v2 / (1.0 - ADAM_B2 ** ADAM_STEP)
    delta = -ADAM_LR * (m_hat / (jnp.sqrt(v_hat) + ADAM_EPS) + ADAM_WD * w)
    return (g, delta, m2, v2), ()


def _adam_call(name, g8, w, m, v, carry=None):
    r, wd = w.shape
    tb = r
    for cand in range(16, r + 1, 16):
        if r % cand == 0 and cand * wd <= 256 * 1024:
            tb = cand
    o = ((r, wd), f32, (tb, wd), lambda n: (n, 0))
    return _seq_fwd(name, _adam_step, [(g8, (N_DEV, tb, wd), lambda n: (0, n, 0)), _r(w, tb, wd), _r(m, tb, wd), _r(v, tb, wd)],
                    [], [], [], [o, o, o, o], r // tb, carry=carry)


WEIGHTS = ['meta_tokens', 'w_in', 'w_branch', 'w_out', 'norm_pre_mix', 'norm_post_mix', 'norm_pre_mlp', 'norm_post_mlp',
           'w_up', 'w_down', 'ret_gn_w', 's5_lam_re', 's5_lam_im', 's5_b_re', 's5_b_im', 's5_c_re', 's5_c_im', 's5_d',
           's5_log_step', 's5_glu_w', 's5_glu_b', 'ssd_conv_w', 'ssd_conv_b', 'ssd_dt_bias', 'ssd_a_log', 'ssd_d',
           'ssd_norm_w', 'hgrn_lb', 'hgrn_norm_w']
SHARDED = [("w_in", 2), ("w_branch", 3), ("w_out", 1), ("w_up", 2), ("w_down", 1), ("s5_glu_w", 2), ("meta_tokens", 1),
           ("ssd_conv_w", 2)]
N_BF16 = 6
REPL = [n for n in WEIGHTS if n not in dict(SHARDED)]
SMALL_PAD = 512
WIRE = bf16
SHARD_COLS = IN_DIM // N_DEV


def _col_pieces():
    out, pos = [], 0
    for a, b in _orig_col_slices() + [(3584, 3592)]:
        if a == 3584:
            pos = OFF["dt"]
        while a < b:
            e = min(b, (a // SHARD_COLS + 1) * SHARD_COLS)
            out.append((a, e, pos))
            pos += e - a
            a = e
    return out


def _w_in_from_shards(got_l):
    parts = [got_l[a // SHARD_COLS][:, a % SHARD_COLS:a % SHARD_COLS + (b - a)] for a, b, _ in _col_pieces()]
    parts.append(jnp.zeros((got_l.shape[1], NP - IN_DIM - (OFF["dt"] - 9728)), got_l.dtype))
    return jnp.concatenate(parts, axis=1)


def _w_in_to_shards(g):
    pieces = sorted(_col_pieces())
    blocks = []
    for d in range(N_DEV):
        blocks.append(jnp.concatenate([g[:, m:m + (b - a)] for a, b, m in pieces if a // SHARD_COLS == d], axis=1))
    return jnp.stack(blocks, axis=0)


def _to8(full, axis):
    sh = full.shape
    return jnp.moveaxis(full.reshape(sh[:axis] + (N_DEV, sh[axis] // N_DEV) + sh[axis + 1:]), axis, 0)


def _from8(g8, axis):
    r = jnp.moveaxis(g8, 0, axis)
    sh = r.shape
    return r.reshape(sh[:axis] + (sh[axis] * sh[axis + 1],) + sh[axis + 2:])


def _pack(arrs, pad_rows):
    flat = jnp.concatenate([a.reshape(-1) for a in arrs])
    n = flat.shape[0]
    total = -(-n // (128 * pad_rows)) * (128 * pad_rows)
    if total != n:
        flat = jnp.concatenate([flat, jnp.zeros((total - n,), flat.dtype)])
    return flat.reshape(total // 128, 128)


def _unpack(flat, shapes):
    v = flat.reshape(-1)
    out, pos = [], 0
    for s in shapes:
        n = int(np.prod(s))
        out.append(v[pos:pos + n].reshape(tuple(s)))
        pos += n
    return out


def _rows2d(a, lead=0):
    return a.reshape(a.shape[:lead] + (-1, a.shape[-1]))


def _local_step(x0, tgt0, wf, first_w, next_w, send_grads):
    seq = x0.shape[0]
    t = N_META + seq
    tp = -(-t // CHUNK) * CHUNK
    tabs = _tables(tp)
    lb_in = jnp.concatenate([wf["hgrn_lb"], jnp.zeros((8 - DEPTH, BW), f32)], axis=0)
    (lb_all,) = _seq_fwd("lb_prep", _lb_step, [_full(lb_in)], [], [tabs["lmat"]], [], [((8, BW), f32, (8, BW), lambda n: (0, 0))], 1)

    def pad128(a):
        return jnp.concatenate([a, jnp.zeros((128 - a.shape[0],), f32)]).reshape(1, 128)

    def layer_params(l, big):
        return dict(
            big,
            npm=wf["norm_pre_mix"][l].reshape(1, D_MODEL), npostmix=wf["norm_post_mix"][l].reshape(1, D_MODEL),
            npremlp=wf["norm_pre_mlp"][l].reshape(1, D_MODEL), npostmlp=wf["norm_post_mlp"][l].reshape(1, D_MODEL),
            ret_gn=wf["ret_gn_w"][l].reshape(1, BW), lam_re=wf["s5_lam_re"][l], lam_im=wf["s5_lam_im"][l],
            lstep=wf["s5_log_step"][l].reshape(S5_G, 1),
            bt_re=wf["s5_b_re"][l].transpose(0, 2, 1).reshape(S5_G * S5_J, S5_P),
            bt_im=wf["s5_b_im"][l].transpose(0, 2, 1).reshape(S5_G * S5_J, S5_P),
            c_re=wf["s5_c_re"][l], c_im=wf["s5_c_im"][l], s5_d=wf["s5_d"][l].reshape(1, BW),
            glu_b=wf["s5_glu_b"][l].reshape(1, 2 * BW), conv_w=wf["ssd_conv_w"][l],
            conv_b=wf["ssd_conv_b"][l].reshape(1, 1024), dt_bias=pad128(wf["ssd_dt_bias"][l]),
            a_log=pad128(wf["ssd_a_log"][l]), ssd_d=pad128(wf["ssd_d"][l]), ssd_nw=wf["ssd_norm_w"][l].reshape(1, BW),
            lb=lb_all[l].reshape(1, BW), hg_nw=wf["hgrn_norm_w"][l].reshape(1, BW))

    zpad = jnp.zeros((tp - t, D_MODEL), f32)
    h = jnp.concatenate([wf["meta_tokens"], x0, zpad], axis=0)
    tgt = jnp.concatenate([jnp.zeros((N_META, D_MODEL), f32), tgt0, zpad], axis=0)
    params, saved, big = [], [], first_w
    for l in range(DEPTH):
        carry, arrived = next_w(l)
        h, sv, p = _layer_fwd(h, layer_params(l, big), tabs, "_l%d" % l, carry)
        params.append(p)
        saved.append(sv)
        big = arrived() if arrived else None
    dh, loss_local = _loss_call(h, tgt, N_META, t)
    g, carry = [None] * DEPTH, {}
    for l in reversed(range(DEPTH)):
        dh, g[l] = _layer_bwd(dh, params[l], saved[l], tabs, "_l%d" % l, carry)
        carry = send_grads(l, g[l]) if l > 0 else {}
    d_lb = jnp.concatenate([jnp.concatenate([gl["lb"] for gl in g], axis=0), jnp.zeros((8 - DEPTH, BW), f32)], axis=0)
    (d_hgrn_lb,), _ = _seq_bwd("lb_prep_bwd", _lb_step, [_full(lb_in)], [True], [], [tabs["lmat"]], [], [_full(d_lb)], 1)
    return loss_local, dh, g, d_hgrn_lb[:DEPTH]


def kernel(x, meta_tokens, w_in, w_branch, w_out, norm_pre_mix, norm_post_mix, norm_pre_mlp, norm_post_mlp, w_up, w_down, ret_gn_w, s5_lam_re, s5_lam_im, s5_b_re, s5_b_im, s5_c_re, s5_c_im, s5_d, s5_log_step, s5_glu_w, s5_glu_b, ssd_conv_w, ssd_conv_b, ssd_dt_bias, ssd_a_log, ssd_d, ssd_norm_w, hgrn_lb, hgrn_norm_w, loss_target, m_meta_tokens, m_w_in, m_w_branch, m_w_out, m_norm_pre_mix, m_norm_post_mix, m_norm_pre_mlp, m_norm_post_mlp, m_w_up, m_w_down, m_ret_gn_w, m_s5_lam_re, m_s5_lam_im, m_s5_b_re, m_s5_b_im, m_s5_c_re, m_s5_c_im, m_s5_d, m_s5_log_step, m_s5_glu_w, m_s5_glu_b, m_ssd_conv_w, m_ssd_conv_b, m_ssd_dt_bias, m_ssd_a_log, m_ssd_d, m_ssd_norm_w, m_hgrn_lb, m_hgrn_norm_w, v_meta_tokens, v_w_in, v_w_branch, v_w_out, v_norm_pre_mix, v_norm_post_mix, v_norm_pre_mlp, v_norm_post_mlp, v_w_up, v_w_down, v_ret_gn_w, v_s5_lam_re, v_s5_lam_im, v_s5_b_re, v_s5_b_im, v_s5_c_re, v_s5_c_im, v_s5_d, v_s5_log_step, v_s5_glu_w, v_s5_glu_b, v_ssd_conv_w, v_ssd_conv_b, v_ssd_dt_bias, v_ssd_a_log, v_ssd_d, v_ssd_norm_w, v_hgrn_lb, v_hgrn_norm_w):
    args = dict(locals())
    w = {n: args[n] for n in WEIGHTS}
    mom = {n: args["m_" + n] for n in WEIGHTS}
    var = {n: args["v_" + n] for n in WEIGHTS}
    names = [n for n, _ in SHARDED]
    big = names[:N_BF16]

    shard = {n: [w[n][l].astype(bf16) for l in range(DEPTH)] for n in big}
    got = _all_gather("gather_first", [shard["w_in"][0], w["meta_tokens"], w["ssd_conv_w"]])
    wf = {n: w[n] for n in REPL}
    wf["meta_tokens"] = _from8(got[1], 1)
    wf["ssd_conv_w"] = _from8(got[2], 2)

    full = dict(w_branch=lambda a: _from8(a, 2), w_out=lambda a: _from8(a, 0), w_up=lambda a: _from8(a, 1),
                w_down=lambda a: _from8(a, 0), s5_glu_w=lambda a: _from8(a, 1))
    key = dict(w_branch="w_branch", w_out="w_out", w_up="w_up", w_down="w_down", s5_glu_w="glu_w")
    half = w["w_in"].shape[1] // 2

    def arrived(pairs):
        return {key[n]: full[n](a) for n, a in pairs}

    def next_w(l):
        carry = {}
        if l == 0:
            n_in = ("w_branch", "w_out", "s5_glu_w", "w_up")
            first, second = _Carry("gather", [shard[n][0] for n in n_in]), _Carry("gather", [shard["w_down"][0]])
            carry.update({"in": first, "in_done": lambda: arrived(zip(n_in, first.result)),
                          "mg": second, "mg_done": lambda: arrived([("w_down", second.result[0])])})
        if l + 1 == DEPTH:
            return carry, None
        n_ret = ("w_branch", "w_out", "s5_glu_w")
        nxt = dict(s5=_Carry("gather", [shard["w_in"][l + 1][:half]]), mlp=_Carry("gather", [shard["w_in"][l + 1][half:]]),
                   ssd=_Carry("gather", [shard["w_up"][l + 1]]), hg=_Carry("gather", [shard["w_down"][l + 1]]),
                   ret=_Carry("gather", [shard[n][l + 1] for n in n_ret]))
        carry.update(nxt)
        return carry, lambda: dict(
            arrived(list(zip(n_ret, nxt["ret"].result)) + [("w_up", nxt["ssd"].result[0]), ("w_down", nxt["hg"].result[0])]),
            w_in=_w_in_from_shards(jnp.concatenate([nxt["s5"].result[0], nxt["mlp"].result[0]], axis=1)))

    def to_wire(gl):
        return dict(w_in=None if gl["w_in"] is None else _w_in_to_shards(gl["w_in"]).astype(WIRE), w_branch=_to8(gl["w_branch"], 2).astype(WIRE),
                    w_out=_to8(gl["w_out"], 0).astype(WIRE), w_up=_to8(gl["w_up"], 1).astype(WIRE),
                    w_down=_to8(gl["w_down"], 0).astype(WIRE), s5_glu_w=_to8(gl["glu_w"], 1).astype(WIRE))

    sent = [None] * DEPTH

    def send_grads(l, gl):
        wire = to_wire(gl)
        carry = dict(s5=_Carry("scatter", [wire["w_in"][:, :half]]), mlp=_Carry("scatter", [wire["w_up"]]),
                     ssd=_Carry("scatter", [wire["w_down"], wire["w_branch"]]),
                     hg=_Carry("scatter", [wire["w_in"][:, half:], wire["w_out"], wire["s5_glu_w"]]))
        sent[l] = lambda: dict(w_in=jnp.concatenate([carry["s5"].result[0], carry["hg"].result[0]], axis=1),
                               w_up=carry["mlp"].result[0], w_down=carry["ssd"].result[0], w_branch=carry["ssd"].result[1],
                               w_out=carry["hg"].result[1], s5_glu_w=carry["hg"].result[2])
        if l == 1:
            def late(g0):
                wire0 = to_wire(dict(g0, w_in=None))
                rt = _Carry("scatter", [wire0["w_up"]])
                dw = _Carry("scatter", [wire0["w_down"], wire0["s5_glu_w"]])
                dx = _Carry("scatter", [wire0["w_branch"], wire0["w_out"]])
                sent[0] = lambda: dict(w_up=rt.result[0], w_down=dw.result[0], s5_glu_w=dw.result[1], w_branch=dx.result[0],
                                       w_out=dx.result[1])
                return dict(ret=rt, dw=dw, dx=dx)
            carry["late"] = late
        return carry

    loss_local, dh0, g, d_hgrn_lb = _local_step(x[0], loss_target[0], wf, dict(w_in=_w_in_from_shards(got[0])), next_w, send_grads)
    seq = x.shape[1]
    t = N_META + seq
    tiny = _exchange("scatter_small", [_to8(dh0[:N_META], 1).astype(WIRE),
                                       jnp.stack([_to8(gl["conv_w"], 1) for gl in g], axis=1).astype(WIRE)])
    w_in0 = _w_in_to_shards(g[0]["w_in"]).astype(WIRE)
    riders = {n: _Carry("scatter", [w_in0[:, a:b]]) for n, a, b in (("w_up", 0, 384), ("w_down", 384, 768), ("w_branch", 768, 1024))}
    per_layer = [sent[l]() for l in range(DEPTH)]
    parts = {n: jnp.stack([pl_[n] for pl_ in per_layer], axis=1) for n in big if n != "w_in"}
    parts["meta_tokens"], parts["ssd_conv_w"] = tiny

    def stack_l(key, shape):
        return jnp.stack([gl[key].reshape(shape) for gl in g], axis=0)

    small = dict(
        norm_pre_mix=stack_l("npm", (D_MODEL,)), norm_post_mix=stack_l("npostmix", (D_MODEL,)),
        norm_pre_mlp=stack_l("npremlp", (D_MODEL,)), norm_post_mlp=stack_l("npostmlp", (D_MODEL,)),
        ret_gn_w=stack_l("ret_gn", (BW,)), s5_lam_re=stack_l("lam_re", (S5_G, S5_P)), s5_lam_im=stack_l("lam_im", (S5_G, S5_P)),
        s5_b_re=stack_l("bt_re", (S5_G, S5_J, S5_P)).transpose(0, 1, 3, 2),
        s5_b_im=stack_l("bt_im", (S5_G, S5_J, S5_P)).transpose(0, 1, 3, 2),
        s5_c_re=stack_l("c_re", (S5_G, S5_J, S5_P)), s5_c_im=stack_l("c_im", (S5_G, S5_J, S5_P)),
        s5_d=stack_l("s5_d", (BW,)), s5_log_step=stack_l("lstep", (S5_G,)), s5_glu_b=stack_l("glu_b", (2 * BW,)),
        ssd_conv_b=stack_l("conv_b", (1024,)), ssd_dt_bias=stack_l("dt_bias", (128,))[:, :SSD_HEADS],
        ssd_a_log=stack_l("a_log", (128,))[:, :SSD_HEADS], ssd_d=stack_l("ssd_d", (128,))[:, :SSD_HEADS],
        ssd_norm_w=stack_l("ssd_nw", (BW,)), hgrn_lb=d_hgrn_lb, hgrn_norm_w=stack_l("hg_nw", (BW,)))
    small_ride = _Carry("gather", [_pack([small[n] for n in REPL], SMALL_PAD)])
    out = {k: {} for k in ("grad", "delta", "m", "v")}
    for n in [x for x in names if x != "w_in"] + ["w_in"]:
        if n == "w_in":
            w_in_l0 = jnp.concatenate([riders[x].result[0] for x in ("w_up", "w_down", "w_branch")], axis=1)
            parts[n] = jnp.stack([w_in_l0] + [pl_[n] for pl_ in per_layer[1:]], axis=1)
        res = _adam_call("adamw_" + n, _rows2d(parts[n], 1), _rows2d(w[n]), _rows2d(mom[n]), _rows2d(var[n]),
                         carry=small_ride if n == "w_in" else riders.get(n))
        for k, r in zip(("grad", "delta", "m", "v"), res):
            out[k][n] = r.reshape(w[n].shape)
    res = _adam_call("adamw_replicated", small_ride.result[0], _pack([w[n] for n in REPL], SMALL_PAD),
                     _pack([mom[n] for n in REPL], SMALL_PAD), _pack([var[n] for n in REPL], SMALL_PAD))
    for k, r in zip(("grad", "delta", "m", "v"), res):
        out[k].update(zip(REPL, _unpack(r, [w[n].shape for n in REPL])))

    loss = lax.psum(loss_local, ("x", "y", "c"))
    return (loss, dh0[N_META:t][None], *[out["grad"][n] for n in WEIGHTS], *[out["delta"][n] for n in WEIGHTS],
            *[out["m"][n] for n in WEIGHTS], *[out["v"][n] for n in WEIGHTS])
```
